```python
import jax, jax.numpy as jnp
from jax import lax
import numpy as np

D_MODEL = 2048
BATCH = 8
SEQ = 2048
DEPTH = 1

GRID_W = 64
CTX_LEN = 256
D_MIX = D_MODEL
GLA_WIDTH = D_MIX // 2
GLA_HEADS = 4
GLA_DK = GLA_WIDTH // 2 // GLA_HEADS
GLA_DV = GLA_WIDTH // GLA_HEADS
GLA_KEY_WIDTH = GLA_HEADS * GLA_DK
GLA_RANK = 16
GLA_TAU = 16.0
GLA_CHUNK = 64
POOL_WIDTH = D_MIX - GLA_WIDTH
POOL_WINDOWS = (2, 4, 8, 16)
POOL_GROUP = POOL_WIDTH // len(POOL_WINDOWS)
D_IN = 2 * GLA_KEY_WIDTH + 2 * GLA_WIDTH + 2 * GLA_RANK + POOL_WIDTH
SPLITS = (
    GLA_KEY_WIDTH,
    2 * GLA_KEY_WIDTH,
    2 * GLA_KEY_WIDTH + GLA_WIDTH,
    2 * GLA_KEY_WIDTH + 2 * GLA_WIDTH,
    2 * GLA_KEY_WIDTH + 2 * GLA_WIDTH + GLA_RANK,
    2 * GLA_KEY_WIDTH + 2 * GLA_WIDTH + 2 * GLA_RANK,
)
N_EXPERTS = 64
TOP_K = 8
N_GROUPS = 8
TOPK_GROUPS = 4
D_EXPERT = D_MODEL // 4
D_SHARED = D_MODEL // 4
ROUTED_SCALE = 2.5
MOE_BLOCK = 256
EPS = 1e-6

kernel_name = "hybrid_gla_pool_moe_dit_prefix"


def _rmsnorm(x, g):
    xf = x.astype(jnp.float32)
    y = xf * lax.rsqrt(jnp.mean(xf * xf, axis=-1, keepdims=True) + EPS)
    return (y * g.astype(jnp.float32)).astype(x.dtype)


def _log_decay(a_lr, w_a2, b_a):
    B, L, _ = a_lr.shape
    z = (a_lr @ w_a2 + b_a).astype(jnp.float32)
    return (jax.nn.log_sigmoid(z) / GLA_TAU).reshape(B, L, GLA_HEADS, GLA_DK)


def _gla_scan(q, k, v, g, s0, with_output):
    B, L, H, DK = q.shape
    DV = v.shape[-1]
    C = GLA_CHUNK
    n = L // C
    qc = q.reshape(B, n, C, H, DK)
    kc = k.reshape(B, n, C, H, DK)
    vc = v.reshape(B, n, C, H, DV)
    G = jnp.cumsum(g.reshape(B, n, C, H, DK), axis=2)
    G_end = G[:, :, -1:]
    u = jnp.einsum("bnchk,bnchv->nbhkv", kc * jnp.exp(G_end - G), vc)
    decay = jnp.moveaxis(jnp.exp(G_end[:, :, 0]), 1, 0)

    def step(s, inp):
        d, du = inp
        return d[..., None] * s + du, s

    s_final, s_prev = lax.scan(step, s0, (decay, u))
    if not with_output:
        return None, s_final
    o_inter = jnp.einsum("bnchk,nbhkv->bnchv", qc * jnp.exp(G), s_prev)
    G_mid = G[:, :, C // 2 : C // 2 + 1]
    a = jnp.einsum("bnihk,bnjhk->bnhij", qc * jnp.exp(G - G_mid), kc * jnp.exp(G_mid - G))
    lower = jnp.tril(jnp.ones((C, C), dtype=bool))
    a = jnp.where(lower, a, 0.0)
    o_intra = jnp.einsum("bnhij,bnjhv->bnihv", a, vc)
    return (o_inter + o_intra).reshape(B, L, H, DV), s_final


def _window_mean(x, w, axis):
    n = x.shape[axis]
    xf = jnp.moveaxis(x.astype(jnp.float32), axis, 0)
    cs = jnp.concatenate([jnp.zeros_like(xf[:1]), jnp.cumsum(xf, axis=0)], axis=0)
    pos = jnp.arange(n)
    lo = jnp.maximum(pos - w // 2, 0)
    hi = jnp.minimum(pos + w // 2, n)
    cnt = (hi - lo).astype(jnp.float32).reshape((n,) + (1,) * (xf.ndim - 1))
    return jnp.moveaxis((cs[hi] - cs[lo]) / cnt, 0, axis)


def _pool_mixer(p, w_pool, pool_scale, grid):
    B, L, _ = p.shape
    pg = p.reshape(B, L, len(POOL_WINDOWS), POOL_GROUP)
    outs = []
    for gi, w in enumerate(POOL_WINDOWS):
        xg = pg[:, :, gi]
        if grid:
            rows = L // GRID_W
            x2 = xg.reshape(B, rows, GRID_W, POOL_GROUP)
            m = _window_mean(_window_mean(x2, w, 1), w, 2).reshape(B, L, POOL_GROUP)
        else:
            m = _window_mean(xg, w, 1)
        outs.append(m.astype(p.dtype) - xg)
    d = jnp.stack(outs, axis=2)
    y = jnp.einsum("blgc,gcd->blgd", d, w_pool).reshape(B, L, POOL_WIDTH)
    return y * pool_scale


def _token_mixer(u, s_f0, s_b0, w_a2_f, b_a_f, w_a2_b, b_a_b, gla_norm, w_pool, pool_scale,
                 w_out, grid, with_output):
    B, L, _ = u.shape
    f32 = jnp.float32
    q, k, v, r, a_f, a_b, p = jnp.split(u, SPLITS, axis=-1)
    q = q.astype(f32).reshape(B, L, GLA_HEADS, GLA_DK) * (GLA_DK ** -0.5)
    k = k.astype(f32).reshape(B, L, GLA_HEADS, GLA_DK)
    v = v.astype(f32).reshape(B, L, GLA_HEADS, GLA_DV)
    g_f = _log_decay(a_f, w_a2_f, b_a_f)
    g_b = _log_decay(a_b, w_a2_b, b_a_b)
    o_f, s_f = _gla_scan(q, k, v, g_f, s_f0, with_output)
    o_b, s_b = _gla_scan(q[:, ::-1], k[:, ::-1], v[:, ::-1], g_b[:, ::-1], s_b0, with_output)
    if not with_output:
        return None, s_f, s_b
    o = o_f + o_b[:, ::-1]
    o = o * lax.rsqrt(jnp.mean(o * o, axis=-1, keepdims=True) + EPS)
    o = o * gla_norm.astype(f32).reshape(GLA_HEADS, GLA_DV)
    y_gla = o.reshape(B, L, GLA_WIDTH).astype(u.dtype) * jax.nn.silu(r)
    y_pool = _pool_mixer(p, w_pool, pool_scale, grid)
    y = jnp.concatenate([y_gla, y_pool], axis=-1) @ w_out
    return y, s_f, s_b


def _route(h, w_router, router_bias):
    scores = jax.nn.sigmoid((h @ w_router).astype(jnp.float32))
    sel = scores + router_bias.astype(jnp.float32)
    grp = sel.reshape(-1, N_GROUPS, N_EXPERTS // N_GROUPS)
    grp_score = lax.top_k(grp, 2)[0].sum(-1)
    _, gidx = lax.top_k(grp_score, TOPK_GROUPS)
    gmask = jax.nn.one_hot(gidx, N_GROUPS, dtype=jnp.float32).sum(1) > 0
    emask = jnp.repeat(gmask, N_EXPERTS // N_GROUPS, axis=1)
    _, eidx = lax.top_k(jnp.where(emask, sel, -jnp.inf), TOP_K)
    wts = jnp.take_along_axis(scores, eidx, axis=1)
    wts = wts / jnp.sum(wts, axis=-1, keepdims=True) * ROUTED_SCALE
    return eidx, wts


def _moe(h, w_router, router_bias, w_eg, w_eu, w_ed, w_sg, w_su, w_sd):
    T, D = h.shape
    eidx, wts = _route(h, w_router, router_bias)
    tk = T * TOP_K
    e_flat = eidx.reshape(tk)
    tok_flat = jnp.repeat(jnp.arange(T, dtype=jnp.int32), TOP_K)
    w_flat = wts.reshape(tk)
    order = jnp.argsort(e_flat)
    e_sorted = e_flat[order]
    counts = jnp.zeros((N_EXPERTS,), jnp.int32).at[e_flat].add(1)
    starts = jnp.cumsum(counts) - counts
    padded = (counts + MOE_BLOCK - 1) // MOE_BLOCK * MOE_BLOCK
    pends = jnp.cumsum(padded)
    pstarts = pends - padded
    dest = pstarts[e_sorted] + jnp.arange(tk, dtype=jnp.int32) - starts[e_sorted]
    n_blocks = -(-tk // MOE_BLOCK) + N_EXPERTS
    n_rows = n_blocks * MOE_BLOCK
    tok_buf = jnp.full((n_rows,), T, jnp.int32).at[dest].set(tok_flat[order])
    gate_buf = jnp.zeros((n_rows,), h.dtype).at[dest].set(w_flat[order].astype(h.dtype))
    block_start = jnp.arange(n_blocks, dtype=jnp.int32) * MOE_BLOCK
    block_e = jnp.minimum(jnp.searchsorted(pends, block_start, side="right"), N_EXPERTS - 1)
    h_pad = jnp.concatenate([h, jnp.zeros((1, D), h.dtype)], axis=0)

    def body(acc, blk):
        tok, gate, e = blk
        rows = h_pad[tok]
        a = jax.nn.silu(rows @ w_eg[e]) * (rows @ w_eu[e])
        y = (a @ w_ed[e]) * gate[:, None]
        return acc.at[tok].add(y), None

    acc, _ = lax.scan(body, jnp.zeros((T + 1, D), h.dtype),
                      (tok_buf.reshape(n_blocks, MOE_BLOCK), gate_buf.reshape(n_blocks, MOE_BLOCK), block_e))
    shared = (jax.nn.silu(h @ w_sg) * (h @ w_su)) @ w_sd
    return acc[:T] + shared


def setup_inputs(seed: int = 0) -> dict:
    key = jax.random.key(seed)
    ks = jax.random.split(key, 27)
    f32 = jnp.float32
    L = DEPTH
    D = D_MODEL

    def nrm(k, shape, scale):
        return jax.random.normal(k, shape, f32) * scale

    def gain(k, n):
        return 1.0 + 0.05 * jax.random.normal(k, (L, n), f32)

    return {
        "x": nrm(ks[0], (BATCH, SEQ, D), 1.0),
        "c": nrm(ks[1], (BATCH, D), 1.0),
        "ctx": nrm(ks[2], (BATCH, CTX_LEN, D), 1.0),
        "c_ctx": nrm(ks[3], (D,), 1.0),
        "w_mod": nrm(ks[4], (L, D, 6 * D), 0.5 * D ** -0.5),
        "b_mod": nrm(ks[5], (L, 6 * D), 0.02),
        "norm_mix_pre": gain(ks[6], D),
        "norm_mix_post": gain(ks[7], D),
        "norm_ffn_pre": gain(ks[8], D),
        "norm_ffn_post": gain(ks[9], D),
        "w_in": nrm(ks[10], (L, D, D_IN), D ** -0.5),
        "w_a2_fwd": nrm(ks[11], (L, GLA_RANK, GLA_KEY_WIDTH), GLA_RANK ** -0.5),
        "b_a_fwd": nrm(ks[12], (L, GLA_KEY_WIDTH), 0.5),
        "w_a2_bwd": nrm(ks[13], (L, GLA_RANK, GLA_KEY_WIDTH), GLA_RANK ** -0.5),
        "b_a_bwd": nrm(ks[14], (L, GLA_KEY_WIDTH), 0.5),
        "gla_norm": gain(ks[15], GLA_WIDTH),
        "w_pool": nrm(ks[16], (L, len(POOL_WINDOWS), POOL_GROUP, POOL_GROUP), POOL_GROUP ** -0.5),
        "pool_scale": gain(ks[17], POOL_WIDTH),
        "w_out": nrm(ks[18], (L, D_MIX, D), D_MIX ** -0.5),
        "w_router": nrm(ks[19], (L, D, N_EXPERTS), D ** -0.5),
        "router_bias": nrm(ks[20], (L, N_EXPERTS), 0.01),
        "w_exp_gate": nrm(ks[21], (L, N_EXPERTS, D, D_EXPERT), D ** -0.5),
        "w_exp_up": nrm(ks[22], (L, N_EXPERTS, D, D_EXPERT), D ** -0.5),
        "w_exp_down": nrm(ks[23], (L, N_EXPERTS, D_EXPERT, D), D_EXPERT ** -0.5),
        "w_sh_gate": nrm(ks[24], (L, D, D_SHARED), D ** -0.5),
        "w_sh_up": nrm(ks[25], (L, D, D_SHARED), D ** -0.5),
        "w_sh_down": nrm(ks[26], (L, D_SHARED, D), D_SHARED ** -0.5),
    }


def reference(x, c, ctx, c_ctx, w_mod, b_mod, norm_mix_pre, norm_mix_post, norm_ffn_pre,
              norm_ffn_post, w_in, w_a2_fwd, b_a_fwd, w_a2_bwd, b_a_bwd, gla_norm, w_pool,
              pool_scale, w_out, w_router, router_bias, w_exp_gate, w_exp_up, w_exp_down,
              w_sh_gate, w_sh_up, w_sh_down):
    B = x.shape[0]
    zero_state = jnp.zeros((B, GLA_HEADS, GLA_DK, GLA_DV), jnp.float32)
    for i in range(DEPTH):
        last = i == DEPTH - 1
        mod = jax.nn.silu(c) @ w_mod[i] + b_mod[i]
        mod_c = jax.nn.silu(c_ctx) @ w_mod[i] + b_mod[i]
        sh1, sc1, gt1, sh2, sc2, gt2 = jnp.split(mod[:, None, :], 6, axis=-1)
        csh1, csc1, cgt1, csh2, csc2, cgt2 = jnp.split(mod_c, 6, axis=-1)
        mix_w = (w_a2_fwd[i], b_a_fwd[i], w_a2_bwd[i], b_a_bwd[i], gla_norm[i], w_pool[i],
                 pool_scale[i], w_out[i])
        ffn_w = (w_router[i], router_bias[i], w_exp_gate[i], w_exp_up[i], w_exp_down[i],
                 w_sh_gate[i], w_sh_up[i], w_sh_down[i])
        hc = _rmsnorm(ctx, norm_mix_pre[i]) * (1 + csc1) + csh1
        y_c, s_f, s_b = _token_mixer(hc @ w_in[i], zero_state, zero_state, *mix_w,
                                     grid=False, with_output=not last)
        h = _rmsnorm(x, norm_mix_pre[i]) * (1 + sc1) + sh1
        y, _, _ = _token_mixer(h @ w_in[i], s_f, s_b, *mix_w, grid=True, with_output=True)
        x = x + gt1 * _rmsnorm(y, norm_mix_post[i])
        h = _rmsnorm(x, norm_ffn_pre[i]) * (1 + sc2) + sh2
        y = _moe(h.reshape(-1, D_MODEL), *ffn_w).reshape(x.shape)
        x = x + gt2 * _rmsnorm(y, norm_ffn_post[i])
        if not last:
            ctx = ctx + cgt1 * _rmsnorm(y_c, norm_mix_post[i])
            hc = _rmsnorm(ctx, norm_ffn_pre[i]) * (1 + csc2) + csh2
            y_c = _moe(hc.reshape(-1, D_MODEL), *ffn_w).reshape(ctx.shape)
            ctx = ctx + cgt2 * _rmsnorm(y_c, norm_ffn_post[i])
    return x
```

```python
import functools

import numpy as np
import jax
import jax.numpy as jnp
from jax import lax
from jax.experimental import pallas as pl
from jax.experimental.pallas import tpu as pltpu

F32 = jnp.float32
BF16 = jnp.bfloat16
I32 = jnp.int32
U32 = jnp.uint32
HIGHEST = lax.Precision.HIGHEST

D_MODEL = 2048
GRID_W = 64
GLA_HEADS = 4
GLA_DK = 128
GLA_DV = 256
GLA_KEY_WIDTH = GLA_HEADS * GLA_DK
GLA_WIDTH = GLA_HEADS * GLA_DV
GLA_RANK = 16
GLA_TAU = 16.0
GLA_CHUNK = 64
POOL_WIDTH = 1024
POOL_WINDOWS = (2, 4, 8, 16)
POOL_GROUP = 256
N_EXPERTS = 64
TOP_K = 8
N_GROUPS = 8
GROUP_SIZE = N_EXPERTS // N_GROUPS
TOPK_GROUPS = 4
D_EXPERT = 512
D_SHARED = 512
ROUTED_SCALE = 2.5
EPS = 1e-6

HALF = D_MODEL // 2
SUPER = 4 * GLA_CHUNK
POOL_PAD = 8 * GRID_W
VMEM_LIMIT = 56 * 1024 * 1024

MOD_TN = 1024
PROJ_TM = 512
PROJ_TN = 512
MIX_TM = 256
MOE_TILE = 256
DISP_TT = 256
COMB_TT = 128


def _cparams(sem):
    return pltpu.CompilerParams(dimension_semantics=sem, vmem_limit_bytes=VMEM_LIMIT)


def _resident(shape):
    nd = len(shape)
    return pl.BlockSpec(shape, lambda *_: (0,) * nd, pipeline_mode=pl.Buffered(1))


def _silu(v):
    return v * jax.nn.sigmoid(v)


def _pack_halves(lo, hi):
    lo_b = lax.bitcast_convert_type(lo.astype(BF16).astype(F32), U32)
    hi_b = lax.bitcast_convert_type(hi.astype(BF16).astype(F32), U32)
    return (hi_b & jnp.uint32(0xFFFF0000)) | (lo_b >> 16)


def _unpack_halves(p):
    lo = lax.bitcast_convert_type(p << 16, F32)
    hi = lax.bitcast_convert_type(p & jnp.uint32(0xFFFF0000), F32)
    return lo, hi


def _mod_body(c_ref, w_ref, b_ref, o_ref):
    s = _silu(c_ref[...])
    o_ref[...] = jnp.dot(s, w_ref[...], preferred_element_type=F32, precision=HIGHEST) + b_ref[...]


def _modulation(c_all, w_mod, b_mod):
    rows, d = c_all.shape
    n = w_mod.shape[1]
    return pl.pallas_call(
        _mod_body,
        grid=(n // MOD_TN,),
        in_specs=[
            pl.BlockSpec((rows, d), lambda j: (0, 0)),
            pl.BlockSpec((d, MOD_TN), lambda j: (0, j)),
            pl.BlockSpec((1, MOD_TN), lambda j: (0, j)),
        ],
        out_specs=pl.BlockSpec((rows, MOD_TN), lambda j: (0, j)),
        out_shape=jax.ShapeDtypeStruct((rows, n), F32),
        compiler_params=_cparams(("arbitrary",)),
        name="modulation",
    )(c_all, w_mod, b_mod)


def _rms_scale(x):
    return lax.rsqrt(jnp.mean(x * x, axis=-1, keepdims=True) + EPS)


def _inproj_body(x_ref, g_ref, sc_ref, sh_ref, w_ref, wa_ref, o_ref, a_ref, *, n_main):
    x = x_ref[...]
    h = x * _rms_scale(x) * g_ref[...]
    h = h * (1.0 + sc_ref[0]) + sh_ref[0]
    hb = h.astype(BF16)
    for n in range(n_main // PROJ_TN):
        cols = slice(n * PROJ_TN, (n + 1) * PROJ_TN)
        o_ref[:, cols] = jnp.dot(hb, w_ref[:, cols], preferred_element_type=F32).astype(BF16)
    a_ref[...] = jnp.dot(hb, wa_ref[...], preferred_element_type=F32)


def _in_projection(x2d, gain, sc, sh, w_main, w_a, rows_per_mod):
    rows, d = x2d.shape
    n_main = w_main.shape[1]
    tiles_per_mod = rows_per_mod // PROJ_TM
    mod_map = lambda i: (i // tiles_per_mod, 0, 0)
    return pl.pallas_call(
        functools.partial(_inproj_body, n_main=n_main),
        grid=(rows // PROJ_TM,),
        in_specs=[
            pl.BlockSpec((PROJ_TM, d), lambda i: (i, 0)),
            _resident((1, d)),
            pl.BlockSpec((1, 1, d), mod_map),
            pl.BlockSpec((1, 1, d), mod_map),
            _resident((d, n_main)),
            _resident((d, 128)),
        ],
        out_specs=[
            pl.BlockSpec((PROJ_TM, n_main), lambda i: (i, 0)),
            pl.BlockSpec((PROJ_TM, 128), lambda i: (i, 0)),
        ],
        out_shape=[
            jax.ShapeDtypeStruct((rows, n_main), BF16),
            jax.ShapeDtypeStruct((rows, 128), F32),
        ],
        compiler_params=_cparams(("arbitrary",)),
        name="in_projection",
    )(x2d, gain, sc, sh, w_main, w_a)


def _log_sigmoid(z):
    return jnp.minimum(z, 0.0) - jnp.log1p(jnp.exp(-jnp.abs(z)))


def _gla_super(q, k, v, a, w2, ba, tri, mask, st_ref, reverse):
    nc = SUPER // GLA_CHUNK
    z = jnp.dot(a, w2, preferred_element_type=F32, precision=HIGHEST) + ba
    g = _log_sigmoid(z) * (1.0 / GLA_TAU)
    G = jnp.dot(tri, g, preferred_element_type=F32, precision=HIGHEST).reshape(nc, GLA_CHUNK, GLA_DK)
    end_row = 0 if reverse else GLA_CHUNK - 1
    mid_row = GLA_CHUNK - 1 - GLA_CHUNK // 2 if reverse else GLA_CHUNK // 2
    g_end = G[:, end_row:end_row + 1, :]
    g_mid = G[:, mid_row:mid_row + 1, :]
    k4 = k.astype(F32).reshape(nc, GLA_CHUNK, GLA_DK)
    kd = (k4 * jnp.exp(g_end - G)).astype(BF16)
    dec = jnp.exp(g_end)
    o = None
    if q is not None:
        q4 = q.astype(F32).reshape(nc, GLA_CHUNK, GLA_DK) * (GLA_DK ** -0.5)
        qg = (q4 * jnp.exp(G - g_mid)).reshape(SUPER, GLA_DK).astype(BF16)
        kg = (k4 * jnp.exp(g_mid - G)).reshape(SUPER, GLA_DK).astype(BF16)
        qe = (q4 * jnp.exp(G)).astype(BF16)
        att = lax.dot_general(qg, kg, (((1,), (1,)), ((), ())), preferred_element_type=F32)
        att = jnp.where(mask, att, 0.0).astype(BF16)
        o = jnp.dot(att, v, preferred_element_type=F32)
    outs = [None] * nc
    order = range(nc - 1, -1, -1) if reverse else range(nc)
    for c in order:
        rows = slice(c * GLA_CHUNK, (c + 1) * GLA_CHUNK)
        st = st_ref[...]
        if q is not None:
            inter = lax.dot_general(qe[c], st.astype(BF16), (((1,), (1,)), ((), ())),
                                    preferred_element_type=F32)
            outs[c] = o[rows] + inter
        upd = lax.dot_general(v[rows], kd[c], (((0,), (0,)), ((), ())), preferred_element_type=F32)
        st_ref[...] = st * dec[c] + upd
    if q is None:
        return None
    return jnp.concatenate(outs, axis=0)


def _gla_body(q_ref, k_ref, v_ref, r_ref, a_ref, kc_ref, vc_ref, ac_ref,
              w2f_ref, baf_ref, w2b_ref, bab_ref, gn_ref, y_ref, o_acc, st_f, st_b, *, n_ctx):
    n_sup = q_ref.shape[0] // SUPER
    row = lax.broadcasted_iota(I32, (SUPER, SUPER), 0)
    col = lax.broadcasted_iota(I32, (SUPER, SUPER), 1)
    same_chunk = (row >> 6) == (col >> 6)
    mask_f = same_chunk & (col <= row)
    mask_b = same_chunk & (col >= row)
    tri_f = mask_f.astype(F32)
    tri_b = mask_b.astype(F32)
    w2f, baf, w2b, bab = w2f_ref[...], baf_ref[...], w2b_ref[...], bab_ref[...]

    st_f[...] = jnp.zeros_like(st_f)
    st_b[...] = jnp.zeros_like(st_b)
    for s in range(n_ctx // SUPER):
        rf = slice(s * SUPER, (s + 1) * SUPER)
        _gla_super(None, kc_ref[rf, :], vc_ref[rf, :], ac_ref[rf, :], w2f, baf, tri_f, mask_f, st_f, False)
        sb = n_ctx // SUPER - 1 - s
        rb = slice(sb * SUPER, (sb + 1) * SUPER)
        _gla_super(None, kc_ref[rb, :], vc_ref[rb, :], ac_ref[rb, :], w2b, bab, tri_b, mask_b, st_b, True)

    o_acc[...] = jnp.zeros_like(o_acc)

    def step(i, carry):
        rf = pl.ds(pl.multiple_of(i * SUPER, SUPER), SUPER)
        of = _gla_super(q_ref[rf, :], k_ref[rf, :], v_ref[rf, :], a_ref[rf, :],
                        w2f, baf, tri_f, mask_f, st_f, False)
        o_acc[rf, :] += of
        rb = pl.ds(pl.multiple_of((n_sup - 1 - i) * SUPER, SUPER), SUPER)
        ob = _gla_super(q_ref[rb, :], k_ref[rb, :], v_ref[rb, :], a_ref[rb, :],
                        w2b, bab, tri_b, mask_b, st_b, True)
        o_acc[rb, :] += ob
        return carry

    lax.fori_loop(0, n_sup, step, 0)

    o = o_acc[...]
    o = o * _rms_scale(o) * gn_ref[...]
    y_ref[...] = (o * _silu(r_ref[...].astype(F32))).astype(BF16)


def _gla(u_lat, a_lat, u_ctx, a_ctx, w2f, baf, w2b, bab, gla_norm, batch, seq, n_ctx):
    kb = GLA_KEY_WIDTH // GLA_DK
    vb = 2 * GLA_KEY_WIDTH // GLA_DV
    rb = vb + GLA_HEADS
    cvb = GLA_KEY_WIDTH // GLA_DV
    return pl.pallas_call(
        functools.partial(_gla_body, n_ctx=n_ctx),
        grid=(batch, GLA_HEADS),
        in_specs=[
            pl.BlockSpec((seq, GLA_DK), lambda b, h: (b, h)),
            pl.BlockSpec((seq, GLA_DK), lambda b, h: (b, kb + h)),
            pl.BlockSpec((seq, GLA_DV), lambda b, h: (b, vb + h)),
            pl.BlockSpec((seq, GLA_DV), lambda b, h: (b, rb + h)),
            pl.BlockSpec((seq, 128), lambda b, h: (b, 0)),
            pl.BlockSpec((n_ctx, GLA_DK), lambda b, h: (b, h)),
            pl.BlockSpec((n_ctx, GLA_DV), lambda b, h: (b, cvb + h)),
            pl.BlockSpec((n_ctx, 128), lambda b, h: (b, 0)),
            pl.BlockSpec((128, GLA_DK), lambda b, h: (0, h)),
            pl.BlockSpec((1, GLA_DK), lambda b, h: (0, h)),
            pl.BlockSpec((128, GLA_DK), lambda b, h: (0, h)),
            pl.BlockSpec((1, GLA_DK), lambda b, h: (0, h)),
            pl.BlockSpec((1, GLA_DV), lambda b, h: (0, h)),
        ],
        out_specs=pl.BlockSpec((seq, GLA_DV), lambda b, h: (b, h)),
        out_shape=jax.ShapeDtypeStruct((batch * seq, GLA_WIDTH), BF16),
        scratch_shapes=[
            pltpu.VMEM((seq, GLA_DV), F32),
            pltpu.VMEM((GLA_DV, GLA_DK), F32),
            pltpu.VMEM((GLA_DV, GLA_DK), F32),
        ],
        compiler_params=_cparams(("arbitrary", "arbitrary")),
        name="gla",
    )(u_lat, u_lat, u_lat, u_lat, a_lat, u_ctx, u_ctx, a_ctx, w2f, baf, w2b, bab, gla_norm)


def _col_window_matrices():
    t = np.arange(SUPER)
    r, c = t // GRID_W, t % GRID_W
    mats = []
    for w in POOL_WINDOWS:
        lo = np.maximum(c - w // 2, 0)[:, None]
        hi = np.minimum(c + w // 2, GRID_W)[:, None]
        m = (r[:, None] == r[None, :]) & (c[None, :] >= lo) & (c[None, :] < hi)
        mats.append(m.astype(np.float32))
    return jnp.asarray(np.stack(mats), dtype=BF16)


def _pool_body(p_ref, cw_ref, wp_ref, ps_ref, y_ref, pad_ref):
    seq = p_ref.shape[0]
    n_rows = seq // GRID_W
    zeros = jnp.zeros((POOL_PAD, POOL_GROUP), F32)
    pad_ref[0:POOL_PAD, :] = zeros
    pad_ref[POOL_PAD + seq:POOL_PAD + seq + POOL_PAD, :] = zeros
    t = lax.broadcasted_iota(I32, (seq, POOL_GROUP), 0)
    r = t >> 6
    c = t & (GRID_W - 1)
    for gi, w in enumerate(POOL_WINDOWS):
        cols = slice(gi * POOL_GROUP, (gi + 1) * POOL_GROUP)
        cw = cw_ref[gi]
        for j in range(seq // SUPER):
            rows = slice(j * SUPER, (j + 1) * SUPER)
            pad_ref[POOL_PAD + j * SUPER:POOL_PAD + (j + 1) * SUPER, :] = jnp.dot(
                cw, p_ref[rows, cols], preferred_element_type=F32)
        total = None
        for d in range(-(w // 2), w // 2):
            start = POOL_PAD + d * GRID_W
            part = pad_ref[start:start + seq, :]
            total = part if total is None else total + part
        cnt_r = jnp.minimum(r + w // 2, n_rows) - jnp.maximum(r - w // 2, 0)
        cnt_c = jnp.minimum(c + w // 2, GRID_W) - jnp.maximum(c - w // 2, 0)
        mean = total / (cnt_r * cnt_c).astype(F32)
        diff = (mean - p_ref[:, cols].astype(F32)).astype(BF16)
        y = jnp.dot(diff, wp_ref[gi], preferred_element_type=F32) * ps_ref[:, cols]
        y_ref[:, cols] = y.astype(BF16)


def _pool_mixer(u_lat, col_mats, w_pool, pool_scale, batch, seq):
    pb = (u_lat.shape[1] - POOL_WIDTH) // POOL_WIDTH
    ng = len(POOL_WINDOWS)
    return pl.pallas_call(
        _pool_body,
        grid=(batch,),
        in_specs=[
            pl.BlockSpec((seq, POOL_WIDTH), lambda b: (b, pb)),
            _resident((ng, SUPER, SUPER)),
            _resident((ng, POOL_GROUP, POOL_GROUP)),
            _resident((1, POOL_WIDTH)),
        ],
        out_specs=pl.BlockSpec((seq, POOL_WIDTH), lambda b: (b, 0)),
        out_shape=jax.ShapeDtypeStruct((batch * seq, POOL_WIDTH), BF16),
        scratch_shapes=[pltpu.VMEM((seq + 2 * POOL_PAD, POOL_GROUP), F32)],
        compiler_params=_cparams(("arbitrary",)),
        name="pool_mixer",
    )(u_lat, col_mats, w_pool, pool_scale)


def _first_index(hit, iota, size, axis):
    return jnp.min(jnp.where(hit, iota, size), axis=axis, keepdims=True)


def _mix_body(yg_ref, yp_ref, x_ref, gt1_ref, sc2_ref, sh2_ref, gpost_ref, gpre_ref, wout_ref,
              wr_ref, rb_ref, wsg_ref, wsu_ref, wsd_ref, upper_ref,
              x1_ref, hp_ref, shr_ref, eidx_ref, pos_ref, wts_ref, cnt_ref, run_ref):
    tm = x_ref.shape[0]
    neg_inf = jnp.float32(-jnp.inf)

    @pl.when(pl.program_id(0) == 0)
    def _():
        run_ref[...] = jnp.zeros_like(run_ref)

    y = jnp.dot(yg_ref[...], wout_ref[0:GLA_WIDTH, :], preferred_element_type=F32)
    y = y + jnp.dot(yp_ref[...], wout_ref[GLA_WIDTH:, :], preferred_element_type=F32)
    x1 = x_ref[...] + gt1_ref[0] * (y * _rms_scale(y) * gpost_ref[...])
    x1_ref[...] = x1
    h = x1 * _rms_scale(x1) * gpre_ref[...]
    h = h * (1.0 + sc2_ref[0]) + sh2_ref[0]
    hp_ref[...] = _pack_halves(h[:, :HALF], h[:, HALF:])
    hb = h.astype(BF16)

    sg = jnp.dot(hb, wsg_ref[...], preferred_element_type=F32)
    su = jnp.dot(hb, wsu_ref[...], preferred_element_type=F32)
    act = (_silu(sg) * su).astype(BF16)
    shr_ref[...] = jnp.dot(act, wsd_ref[...], preferred_element_type=F32).astype(BF16)

    logits = lax.dot_general(wr_ref[...], h, (((1,), (1,)), ((), ())),
                             preferred_element_type=F32, precision=HIGHEST)
    scores = jax.nn.sigmoid(logits)
    sel = scores + rb_ref[...]
    shape3 = (N_GROUPS, GROUP_SIZE, tm)
    sel3 = sel.reshape(shape3)
    i_in = lax.broadcasted_iota(I32, shape3, 1).astype(F32)
    m1 = jnp.max(sel3, axis=1, keepdims=True)
    f1 = _first_index(sel3 == m1, i_in, float(GROUP_SIZE), 1)
    m2 = jnp.max(jnp.where(i_in == f1, neg_inf, sel3), axis=1, keepdims=True)
    grp = jnp.broadcast_to(m1 + m2, shape3).reshape(N_EXPERTS, tm)
    i_e = lax.broadcasted_iota(I32, (N_EXPERTS, tm), 0)
    i_grp = (i_e >> 3).astype(F32)
    i_e = i_e.astype(F32)
    allowed = jnp.zeros((N_EXPERTS, tm), F32)
    for _ in range(TOPK_GROUPS):
        m = jnp.max(grp, axis=0, keepdims=True)
        pick = i_grp == _first_index(grp == m, i_grp, float(N_GROUPS), 0)
        allowed = jnp.where(pick, 1.0, allowed)
        grp = jnp.where(pick, neg_inf, grp)
    cand = jnp.where(allowed > 0.0, sel, neg_inf)
    onehot = jnp.zeros((N_EXPERTS, tm), F32)
    picks, wts = [], []
    for k in range(TOP_K):
        m = jnp.max(cand, axis=0, keepdims=True)
        f = _first_index(cand == m, i_e, float(N_EXPERTS), 0)
        pick = i_e == f
        picks.append(pick)
        eidx_ref[k:k + 1, :] = f.astype(I32)
        wts.append(jnp.sum(jnp.where(pick, scores, 0.0), axis=0, keepdims=True))
        onehot = jnp.where(pick, 1.0, onehot)
        cand = jnp.where(pick, neg_inf, cand)
    w_sum = wts[0]
    for k in range(1, TOP_K):
        w_sum = w_sum + wts[k]
    for k in range(TOP_K):
        wts_ref[k:k + 1, :] = wts[k] / w_sum * ROUTED_SCALE

    before = jnp.dot(onehot.astype(BF16), upper_ref[...], preferred_element_type=F32)
    before = before + run_ref[:, 0:1]
    for k in range(TOP_K):
        pos_ref[k:k + 1, :] = jnp.sum(jnp.where(picks[k], before, 0.0), axis=0, keepdims=True).astype(I32)
    run_ref[...] = run_ref[...] + jnp.sum(onehot, axis=1, keepdims=True)
    cnt_ref[...] = run_ref[...].astype(I32)


def _mix_and_route(y_gla, y_pool, x2d, gt1, sc2, sh2, g_post, g_pre, w_out, w_router_t, router_bias,
                   w_sg, w_su, w_sd, seq):
    rows, d = x2d.shape
    tiles_per_b = seq // MIX_TM
    bmap = lambda i: (i // tiles_per_b, 0, 0)
    rmap = lambda i: (i, 0)
    tmap = lambda i: (0, i)
    upper = jnp.asarray(np.triu(np.ones((MIX_TM, MIX_TM), np.float32), 1), dtype=BF16)
    return pl.pallas_call(
        _mix_body,
        grid=(rows // MIX_TM,),
        in_specs=[
            pl.BlockSpec((MIX_TM, GLA_WIDTH), rmap),
            pl.BlockSpec((MIX_TM, POOL_WIDTH), rmap),
            pl.BlockSpec((MIX_TM, d), rmap),
            pl.BlockSpec((1, 1, d), bmap),
            pl.BlockSpec((1, 1, d), bmap),
            pl.BlockSpec((1, 1, d), bmap),
            _resident((1, d)),
            _resident((1, d)),
            _resident((d, d)),
            _resident((N_EXPERTS, d)),
            _resident((N_EXPERTS, 1)),
            _resident((d, D_SHARED)),
            _resident((d, D_SHARED)),
            _resident((D_SHARED, d)),
            _resident((MIX_TM, MIX_TM)),
        ],
        out_specs=[
            pl.BlockSpec((MIX_TM, d), rmap),
            pl.BlockSpec((MIX_TM, HALF), rmap),
            pl.BlockSpec((MIX_TM, d), rmap),
            pl.BlockSpec((TOP_K, MIX_TM), tmap),
            pl.BlockSpec((TOP_K, MIX_TM), tmap),
            pl.BlockSpec((TOP_K, MIX_TM), tmap),
            pl.BlockSpec((N_EXPERTS, 128), lambda i: (0, 0)),
        ],
        out_shape=[
            jax.ShapeDtypeStruct((rows, d), F32),
            jax.ShapeDtypeStruct((rows, HALF), U32),
            jax.ShapeDtypeStruct((rows, d), BF16),
            jax.ShapeDtypeStruct((TOP_K, rows), I32),
            jax.ShapeDtypeStruct((TOP_K, rows), I32),
            jax.ShapeDtypeStruct((TOP_K, rows), F32),
            jax.ShapeDtypeStruct((N_EXPERTS, 128), I32),
        ],
        scratch_shapes=[pltpu.VMEM((N_EXPERTS, 128), F32)],
        compiler_params=_cparams(("arbitrary",)),
        name="mix_and_route",
    )(y_gla, y_pool, x2d, gt1, sc2, sh2, g_post, g_pre, w_out, w_router_t, router_bias,
      w_sg, w_su, w_sd, upper)


def _row_copy(src_ref, src_row, dst_ref, dst_row, sem):
    return pltpu.make_async_copy(src_ref.at[pl.ds(src_row, 1)], dst_ref.at[pl.ds(dst_row, 1)], sem)


def _dispatch_body(pend_ref, padded_ref, dest_ref, h_ref, xs_ref, zero_ref, zsem, sem):
    tt = h_ref.shape[0]

    def zero_copy(e):
        start = pl.multiple_of(pend_ref[e] - MOE_TILE, MOE_TILE)
        return pltpu.make_async_copy(zero_ref, xs_ref.at[pl.ds(start, MOE_TILE)], zsem)

    @pl.when(pl.program_id(0) == 0)
    def _():
        zero_ref[...] = jnp.zeros_like(zero_ref)

        def zstart(e, c):
            @pl.when(padded_ref[e] > 0)
            def _():
                zero_copy(e).start()
            return c

        def zwait(e, c):
            @pl.when(padded_ref[e] > 0)
            def _():
                zero_copy(e).wait()
            return c

        lax.fori_loop(0, N_EXPERTS, zstart, 0)
        lax.fori_loop(0, N_EXPERTS, zwait, 0)

    def start(t, c):
        for k in range(TOP_K):
            _row_copy(h_ref, t, xs_ref, dest_ref[k, t], sem).start()
        return c

    def wait(t, c):
        for k in range(TOP_K):
            _row_copy(h_ref, t, xs_ref, dest_ref[k, t], sem).wait()
        return c

    lax.fori_loop(0, tt, start, 0)
    lax.fori_loop(0, tt, wait, 0)


def _dispatch(h_packed, dest_t, pends, padded, n_rows):
    rows, half = h_packed.shape
    grid_spec = pltpu.PrefetchScalarGridSpec(
        num_scalar_prefetch=2,
        grid=(rows // DISP_TT,),
        in_specs=[
            pl.BlockSpec((TOP_K, DISP_TT), lambda i, *_: (0, i), memory_space=pltpu.SMEM),
            pl.BlockSpec((DISP_TT, half), lambda i, *_: (i, 0)),
        ],
        out_specs=pl.BlockSpec(memory_space=pl.ANY),
        scratch_shapes=[
            pltpu.VMEM((MOE_TILE, half), U32),
            pltpu.SemaphoreType.DMA,
            pltpu.SemaphoreType.DMA,
        ],
    )
    return pl.pallas_call(
        _dispatch_body,
        grid_spec=grid_spec,
        out_shape=jax.ShapeDtypeStruct((n_rows, half), U32),
        compiler_params=_cparams(("arbitrary",)),
        name="dispatch",
    )(pends, padded, dest_t, h_packed)


def _expert_body(be_ref, nu_ref, x_ref, wg_ref, wu_ref, wd_ref, y_ref):
    @pl.when(pl.program_id(0) < nu_ref[0])
    def _():
        lo, hi = _unpack_halves(x_ref[...])
        lo = lo.astype(BF16)
        hi = hi.astype(BF16)
        wg = wg_ref[0].astype(BF16)
        wu = wu_ref[0].astype(BF16)
        g = jnp.dot(lo, wg[:HALF], preferred_element_type=F32) + jnp.dot(hi, wg[HALF:], preferred_element_type=F32)
        u = jnp.dot(lo, wu[:HALF], preferred_element_type=F32) + jnp.dot(hi, wu[HALF:], preferred_element_type=F32)
        act = (_silu(g) * u).astype(BF16)
        y = jnp.dot(act, wd_ref[0].astype(BF16), preferred_element_type=F32)
        y_ref[...] = _pack_halves(y[:, :HALF], y[:, HALF:])


def _experts(xs, block_expert, n_used, w_eg, w_eu, w_ed):
    n_rows, half = xs.shape
    n_blocks = n_rows // MOE_TILE
    d, de = w_eg.shape[1], w_eg.shape[2]
    row_map = lambda i, be, nu: (jnp.minimum(i, nu[0] - 1), 0)
    grid_spec = pltpu.PrefetchScalarGridSpec(
        num_scalar_prefetch=2,
        grid=(n_blocks,),
        in_specs=[
            pl.BlockSpec((MOE_TILE, half), row_map),
            pl.BlockSpec((1, d, de), lambda i, be, nu: (be[i], 0, 0)),
            pl.BlockSpec((1, d, de), lambda i, be, nu: (be[i], 0, 0)),
            pl.BlockSpec((1, de, d), lambda i, be, nu: (be[i], 0, 0)),
        ],
        out_specs=pl.BlockSpec((MOE_TILE, half), row_map),
    )
    return pl.pallas_call(
        _expert_body,
        grid_spec=grid_spec,
        out_shape=jax.ShapeDtypeStruct((n_rows, half), U32),
        compiler_params=_cparams(("arbitrary",)),
        name="experts",
    )(block_expert, n_used, xs, w_eg, w_eu, w_ed)


def _combine_body(dest_ref, ys_ref, w_ref, shr_ref, x1_ref, gt2_ref, gpost_ref, o_ref, buf, sem):
    tt = x1_ref.shape[0]

    def start(t, c):
        for k in range(TOP_K):
            _row_copy(ys_ref, dest_ref[k, t], buf.at[k], t, sem).start()
        return c

    def wait(t, c):
        for k in range(TOP_K):
            _row_copy(ys_ref, dest_ref[k, t], buf.at[k], t, sem).wait()
        return c

    lax.fori_loop(0, tt, start, 0)
    lax.fori_loop(0, tt, wait, 0)

    w = w_ref[...]
    y_lo = shr_ref[:, :HALF].astype(F32)
    y_hi = shr_ref[:, HALF:].astype(F32)
    for k in range(TOP_K):
        lo, hi = _unpack_halves(buf[k])
        y_lo = y_lo + w[:, k:k + 1] * lo
        y_hi = y_hi + w[:, k:k + 1] * hi
    ms = (jnp.sum(y_lo * y_lo, axis=-1, keepdims=True) + jnp.sum(y_hi * y_hi, axis=-1, keepdims=True)) / D_MODEL
    scale = lax.rsqrt(ms + EPS)
    gt2 = gt2_ref[0]
    gpost = gpost_ref[...]
    o_ref[:, :HALF] = x1_ref[:, :HALF] + gt2[:, :HALF] * (y_lo * scale * gpost[:, :HALF])
    o_ref[:, HALF:] = x1_ref[:, HALF:] + gt2[:, HALF:] * (y_hi * scale * gpost[:, HALF:])


def _combine(dest_t, ys, wts, shared, x1, gt2, g_post, seq):
    rows, d = x1.shape
    tiles_per_b = seq // COMB_TT
    return pl.pallas_call(
        _combine_body,
        grid=(rows // COMB_TT,),
        in_specs=[
            pl.BlockSpec((TOP_K, COMB_TT), lambda i: (0, i), memory_space=pltpu.SMEM),
            pl.BlockSpec(memory_space=pl.ANY),
            pl.BlockSpec((COMB_TT, TOP_K), lambda i: (i, 0)),
            pl.BlockSpec((COMB_TT, d), lambda i: (i, 0)),
            pl.BlockSpec((COMB_TT, d), lambda i: (i, 0)),
            pl.BlockSpec((1, 1, d), lambda i: (i // tiles_per_b, 0, 0)),
            _resident((1, d)),
        ],
        out_specs=pl.BlockSpec((COMB_TT, d), lambda i: (i, 0)),
        out_shape=jax.ShapeDtypeStruct((rows, d), F32),
        scratch_shapes=[
            pltpu.VMEM((TOP_K, COMB_TT, HALF), U32),
            pltpu.SemaphoreType.DMA,
        ],
        compiler_params=_cparams(("arbitrary",)),
        name="combine",
    )(dest_t, ys, wts, shared, x1, gt2, g_post)


def kernel(x, c, ctx, c_ctx, w_mod, b_mod, norm_mix_pre, norm_mix_post, norm_ffn_pre, norm_ffn_post, w_in, w_a2_fwd, b_a_fwd, w_a2_bwd, b_a_bwd, gla_norm, w_pool, pool_scale, w_out, w_router, router_bias, w_exp_gate, w_exp_up, w_exp_down, w_sh_gate, w_sh_up, w_sh_down):
    batch, seq, d = x.shape
    n_ctx = ctx.shape[1]
    assert w_mod.shape[0] == 1 and d == D_MODEL
    assert seq % SUPER == 0 and n_ctx % SUPER == 0 and seq % PROJ_TM == 0 and (batch * n_ctx) % PROJ_TM == 0
    rows = batch * seq

    mod_rows = 16
    c_all = jnp.concatenate([c, c_ctx[None, :], jnp.zeros((mod_rows - batch - 1, d), F32)], axis=0)
    mod_all = _modulation(c_all, w_mod[0], b_mod[0][None, :])
    sh1, sc1, gt1, sh2, sc2, gt2 = [m.reshape(batch, 1, d) for m in jnp.split(mod_all[:batch], 6, axis=-1)]
    csh1 = mod_all[batch, 0:d].reshape(1, 1, d)
    csc1 = mod_all[batch, d:2 * d].reshape(1, 1, d)

    kw, gw = GLA_KEY_WIDTH, GLA_WIDTH
    a0 = 2 * kw + 2 * gw
    w_in0 = w_in[0]
    w_main = jnp.concatenate([w_in0[:, :a0], w_in0[:, a0 + 2 * GLA_RANK:]], axis=1).astype(BF16)
    w_a = jnp.pad(w_in0[:, a0:a0 + 2 * GLA_RANK], ((0, 0), (0, 128 - 2 * GLA_RANK))).astype(BF16)
    w_ctx = w_main[:, kw:2 * kw + gw]
    w2f = jnp.pad(w_a2_fwd[0], ((0, 128 - GLA_RANK), (0, 0)))
    w2b = jnp.pad(w_a2_bwd[0], ((GLA_RANK, 128 - 2 * GLA_RANK), (0, 0)))
    g_mix_pre = norm_mix_pre[0][None, :]

    u_ctx, a_ctx = _in_projection(ctx.reshape(batch * n_ctx, d), g_mix_pre, csc1, csh1, w_ctx, w_a,
                                  batch * n_ctx)
    u_lat, a_lat = _in_projection(x.reshape(rows, d), g_mix_pre, sc1, sh1, w_main, w_a, seq)

    y_gla = _gla(u_lat, a_lat, u_ctx, a_ctx, w2f, b_a_fwd[0][None, :], w2b, b_a_bwd[0][None, :],
                 gla_norm[0][None, :], batch, seq, n_ctx)
    y_pool = _pool_mixer(u_lat, _col_window_matrices(), w_pool[0].astype(BF16), pool_scale[0][None, :],
                         batch, seq)

    x1, h_packed, shared, eidx_t, pos_t, wts_t, counts = _mix_and_route(
        y_gla, y_pool, x.reshape(rows, d), gt1, sc2, sh2, norm_mix_post[0][None, :],
        norm_ffn_pre[0][None, :], w_out[0].astype(BF16), w_router[0].T, router_bias[0][:, None],
        w_sh_gate[0].astype(BF16), w_sh_up[0].astype(BF16), w_sh_down[0].astype(BF16), seq)

    counts = counts[:, 0]
    padded = (counts + MOE_TILE - 1) // MOE_TILE * MOE_TILE
    pends = jnp.cumsum(padded)
    pstarts = pends - padded
    dest_t = pstarts[eidx_t] + pos_t
    n_blocks = rows * TOP_K // MOE_TILE + N_EXPERTS
    n_used = (pends[-1] // MOE_TILE).astype(I32)
    blk = jnp.minimum(jnp.arange(n_blocks, dtype=I32), n_used - 1)
    block_expert = jnp.minimum(jnp.searchsorted(pends, blk * MOE_TILE, side="right"), N_EXPERTS - 1).astype(I32)

    xs = _dispatch(h_packed, dest_t, pends.astype(I32), padded.astype(I32), n_blocks * MOE_TILE)
    ys = _experts(xs, block_expert, n_used.reshape(1), w_exp_gate[0], w_exp_up[0], w_exp_down[0])
    out = _combine(dest_t, ys, wts_t.T, shared, x1, gt2, norm_ffn_post[0][None, :], seq)
    return out.reshape(batch, seq, d)
```

```python
import functools

import numpy as np
import jax
import jax.numpy as jnp
from jax import lax
from jax.experimental import pallas as pl
from jax.experimental.pallas import tpu as pltpu

F32 = jnp.float32
BF16 = jnp.bfloat16
I32 = jnp.int32
U32 = jnp.uint32
HIGHEST = lax.Precision.HIGHEST

D_MODEL = 2048
GRID_W = 64
GLA_HEADS = 4
GLA_DK = 128
GLA_DV = 256
GLA_KEY_WIDTH = GLA_HEADS * GLA_DK
GLA_WIDTH = GLA_HEADS * GLA_DV
GLA_RANK = 16
GLA_TAU = 16.0
GLA_CHUNK = 64
POOL_WIDTH = 1024
POOL_WINDOWS = (2, 4, 8, 16)
POOL_GROUP = 256
N_EXPERTS = 64
TOP_K = 8
N_GROUPS = 8
GROUP_SIZE = N_EXPERTS // N_GROUPS
TOPK_GROUPS = 4
D_EXPERT = 512
D_SHARED = 512
ROUTED_SCALE = 2.5
EPS = 1e-6

HALF = D_MODEL // 2
SUPER = 4 * GLA_CHUNK
GLA_HPS = 4
POOL_PAD = 8 * GRID_W
VMEM_LIMIT = 56 * 1024 * 1024

MOD_TN = 1024
PROJ_TM = 512
PROJ_TN = 512
MIX_TM = 512
MOE_TILE = 512
DISP_TT = 256
COMB_TT = 128
OFFS_TN = 2048


def _cparams(sem):
    return pltpu.CompilerParams(dimension_semantics=sem, vmem_limit_bytes=VMEM_LIMIT)


def _resident(shape):
    nd = len(shape)
    return pl.BlockSpec(shape, lambda *_: (0,) * nd, pipeline_mode=pl.Buffered(1))


def _silu(v):
    return v * jax.nn.sigmoid(v)


def _pack_halves(lo, hi):
    lo_b = lax.bitcast_convert_type(lo.astype(BF16).astype(F32), U32)
    hi_b = lax.bitcast_convert_type(hi.astype(BF16).astype(F32), U32)
    return (hi_b & jnp.uint32(0xFFFF0000)) | (lo_b >> 16)


def _unpack_halves(p):
    lo = lax.bitcast_convert_type(p << 16, F32)
    hi = lax.bitcast_convert_type(p & jnp.uint32(0xFFFF0000), F32)
    return lo, hi


def _mod_body(c_ref, w_ref, b_ref, o_ref):
    s = _silu(c_ref[...])
    o_ref[...] = jnp.dot(s, w_ref[...], preferred_element_type=F32, precision=HIGHEST) + b_ref[...]


def _modulation(c_all, w_mod, b_mod):
    rows, d = c_all.shape
    n = w_mod.shape[1]
    return pl.pallas_call(
        _mod_body,
        grid=(n // MOD_TN,),
        in_specs=[
            pl.BlockSpec((rows, d), lambda j: (0, 0)),
            pl.BlockSpec((d, MOD_TN), lambda j: (0, j)),
            pl.BlockSpec((1, MOD_TN), lambda j: (0, j)),
        ],
        out_specs=pl.BlockSpec((rows, MOD_TN), lambda j: (0, j)),
        out_shape=jax.ShapeDtypeStruct((rows, n), F32),
        compiler_params=_cparams(("arbitrary",)),
        name="modulation",
    )(c_all, w_mod, b_mod)


def _rms_scale(x):
    return lax.rsqrt(jnp.mean(x * x, axis=-1, keepdims=True) + EPS)


def _inproj_body(x_ref, g_ref, sc_ref, sh_ref, w_ref, wa_ref, o_ref, a_ref, *, n_main):
    x = x_ref[...]
    h = x * _rms_scale(x) * g_ref[...]
    h = h * (1.0 + sc_ref[0]) + sh_ref[0]
    hb = h.astype(BF16)
    for n in range(n_main // PROJ_TN):
        cols = slice(n * PROJ_TN, (n + 1) * PROJ_TN)
        o_ref[:, cols] = jnp.dot(hb, w_ref[:, cols], preferred_element_type=F32).astype(BF16)
    a_ref[...] = jnp.dot(hb, wa_ref[...], preferred_element_type=F32)


def _in_projection(x2d, gain, sc, sh, w_main, w_a, rows_per_mod):
    rows, d = x2d.shape
    n_main = w_main.shape[1]
    tiles_per_mod = rows_per_mod // PROJ_TM
    mod_map = lambda i: (i // tiles_per_mod, 0, 0)
    return pl.pallas_call(
        functools.partial(_inproj_body, n_main=n_main),
        grid=(rows // PROJ_TM,),
        in_specs=[
            pl.BlockSpec((PROJ_TM, d), lambda i: (i, 0)),
            _resident((1, d)),
            pl.BlockSpec((1, 1, d), mod_map),
            pl.BlockSpec((1, 1, d), mod_map),
            _resident((d, n_main)),
            _resident((d, 128)),
        ],
        out_specs=[
            pl.BlockSpec((PROJ_TM, n_main), lambda i: (i, 0)),
            pl.BlockSpec((PROJ_TM, 128), lambda i: (i, 0)),
        ],
        out_shape=[
            jax.ShapeDtypeStruct((rows, n_main), BF16),
            jax.ShapeDtypeStruct((rows, 128), F32),
        ],
        compiler_params=_cparams(("arbitrary",)),
        name="in_projection",
    )(x2d, gain, sc, sh, w_main, w_a)


def _log_sigmoid(z):
    return jnp.minimum(z, 0.0) - jnp.log1p(jnp.exp(-jnp.abs(z)))


def _gla_super(q, k, v, a, w2, ba, tri, mask, st_ref, reverse):
    nc = SUPER // GLA_CHUNK
    z = jnp.dot(a.astype(BF16), w2, preferred_element_type=F32) + ba
    g = _log_sigmoid(z) * (1.0 / GLA_TAU)
    g_hi = g.astype(BF16)
    g_lo = (g - g_hi.astype(F32)).astype(BF16)
    G = jnp.dot(tri, g_hi, preferred_element_type=F32) + jnp.dot(tri, g_lo, preferred_element_type=F32)
    G = G.reshape(nc, GLA_CHUNK, GLA_DK)
    end_row = 0 if reverse else GLA_CHUNK - 1
    mid_row = GLA_CHUNK - 1 - GLA_CHUNK // 2 if reverse else GLA_CHUNK // 2
    g_end = G[:, end_row:end_row + 1, :]
    g_mid = G[:, mid_row:mid_row + 1, :]
    k4 = k.astype(F32).reshape(nc, GLA_CHUNK, GLA_DK)
    kd = (k4 * jnp.exp(g_end - G)).astype(BF16)
    dec = jnp.exp(g_end)
    o = None
    if q is not None:
        q4 = q.astype(F32).reshape(nc, GLA_CHUNK, GLA_DK) * (GLA_DK ** -0.5)
        qg = (q4 * jnp.exp(G - g_mid)).reshape(SUPER, GLA_DK).astype(BF16)
        kg = (k4 * jnp.exp(g_mid - G)).reshape(SUPER, GLA_DK).astype(BF16)
        qe = (q4 * jnp.exp(G)).astype(BF16)
        att = lax.dot_general(qg, kg, (((1,), (1,)), ((), ())), preferred_element_type=F32)
        att = jnp.where(mask, att, 0.0).astype(BF16)
        o = jnp.dot(att, v, preferred_element_type=F32)
    outs = [None] * nc
    order = range(nc - 1, -1, -1) if reverse else range(nc)
    for c in order:
        rows = slice(c * GLA_CHUNK, (c + 1) * GLA_CHUNK)
        st = st_ref[...]
        if q is not None:
            inter = lax.dot_general(qe[c], st.astype(BF16), (((1,), (1,)), ((), ())),
                                    preferred_element_type=F32)
            outs[c] = o[rows] + inter
        upd = lax.dot_general(v[rows], kd[c], (((0,), (0,)), ((), ())), preferred_element_type=F32)
        st_ref[...] = st * dec[c] + upd
    if q is None:
        return None
    return jnp.concatenate(outs, axis=0)


def _gla_body(q_ref, k_ref, v_ref, r_ref, a_ref, kc_ref, vc_ref, ac_ref,
              w2f_ref, baf_ref, w2b_ref, bab_ref, gn_ref, y_ref, o_acc, st, *, n_ctx):
    n_sup = q_ref.shape[0] // SUPER
    row = lax.broadcasted_iota(I32, (SUPER, SUPER), 0)
    col = lax.broadcasted_iota(I32, (SUPER, SUPER), 1)
    same_chunk = (row >> 6) == (col >> 6)
    mask_f = same_chunk & (col <= row)
    mask_b = same_chunk & (col >= row)
    tri_f = jnp.where(mask_f, 1.0, 0.0).astype(BF16)
    tri_b = jnp.where(mask_b, 1.0, 0.0).astype(BF16)
    heads = range(GLA_HPS)
    kcol = [slice(h * GLA_DK, (h + 1) * GLA_DK) for h in heads]
    vcol = [slice(h * GLA_DV, (h + 1) * GLA_DV) for h in heads]
    gate_f = [(w2f_ref[:, kcol[h]], baf_ref[:, kcol[h]]) for h in heads]
    gate_b = [(w2b_ref[:, kcol[h]], bab_ref[:, kcol[h]]) for h in heads]

    st[...] = jnp.zeros_like(st)
    n_csup = n_ctx // SUPER
    for s in range(n_csup):
        rf = slice(s * SUPER, (s + 1) * SUPER)
        rb = slice((n_csup - 1 - s) * SUPER, (n_csup - s) * SUPER)
        for h in heads:
            _gla_super(None, kc_ref[rf, kcol[h]], vc_ref[rf, vcol[h]], ac_ref[rf, :], *gate_f[h],
                       tri_f, mask_f, st.at[0, h], False)
            _gla_super(None, kc_ref[rb, kcol[h]], vc_ref[rb, vcol[h]], ac_ref[rb, :], *gate_b[h],
                       tri_b, mask_b, st.at[1, h], True)

    o_acc[...] = jnp.zeros_like(o_acc)

    def step(i, carry):
        rf = pl.ds(pl.multiple_of(i * SUPER, SUPER), SUPER)
        rb = pl.ds(pl.multiple_of((n_sup - 1 - i) * SUPER, SUPER), SUPER)
        a_f = a_ref[rf, :]
        a_b = a_ref[rb, :]
        for h in heads:
            of = _gla_super(q_ref[rf, kcol[h]], k_ref[rf, kcol[h]], v_ref[rf, vcol[h]], a_f, *gate_f[h],
                            tri_f, mask_f, st.at[0, h], False)
            o_acc[rf, vcol[h]] += of
            ob = _gla_super(q_ref[rb, kcol[h]], k_ref[rb, kcol[h]], v_ref[rb, vcol[h]], a_b, *gate_b[h],
                            tri_b, mask_b, st.at[1, h], True)
            o_acc[rb, vcol[h]] += ob
        return carry

    lax.fori_loop(0, n_sup, step, 0)

    for h in heads:
        o = o_acc[:, vcol[h]]
        o = o * _rms_scale(o) * gn_ref[:, vcol[h]]
        y_ref[:, vcol[h]] = (o * _silu(r_ref[:, vcol[h]].astype(F32))).astype(BF16)


def _gla(u_lat, a_lat, u_ctx, a_ctx, w2f, baf, w2b, bab, gla_norm, batch, seq, n_ctx):
    groups = GLA_HEADS // GLA_HPS
    kw, vw = GLA_HPS * GLA_DK, GLA_HPS * GLA_DV
    kb = GLA_KEY_WIDTH // kw
    vb = 2 * GLA_KEY_WIDTH // vw
    rb = vb + groups
    cvb = GLA_KEY_WIDTH // vw
    return pl.pallas_call(
        functools.partial(_gla_body, n_ctx=n_ctx),
        grid=(batch, groups),
        in_specs=[
            pl.BlockSpec((seq, kw), lambda b, h: (b, h)),
            pl.BlockSpec((seq, kw), lambda b, h: (b, kb + h)),
            pl.BlockSpec((seq, vw), lambda b, h: (b, vb + h)),
            pl.BlockSpec((seq, vw), lambda b, h: (b, rb + h)),
            pl.BlockSpec((seq, 128), lambda b, h: (b, 0)),
            pl.BlockSpec((n_ctx, kw), lambda b, h: (b, h)),
            pl.BlockSpec((n_ctx, vw), lambda b, h: (b, cvb + h)),
            pl.BlockSpec((n_ctx, 128), lambda b, h: (b, 0)),
            pl.BlockSpec((128, kw), lambda b, h: (0, h)),
            pl.BlockSpec((1, kw), lambda b, h: (0, h)),
            pl.BlockSpec((128, kw), lambda b, h: (0, h)),
            pl.BlockSpec((1, kw), lambda b, h: (0, h)),
            pl.BlockSpec((1, vw), lambda b, h: (0, h)),
        ],
        out_specs=pl.BlockSpec((seq, vw), lambda b, h: (b, h)),
        out_shape=jax.ShapeDtypeStruct((batch * seq, GLA_WIDTH), BF16),
        scratch_shapes=[
            pltpu.VMEM((seq, vw), F32),
            pltpu.VMEM((2, GLA_HPS, GLA_DV, GLA_DK), F32),
        ],
        compiler_params=_cparams(("arbitrary", "arbitrary")),
        name="gla",
    )(u_lat, u_lat, u_lat, u_lat, a_lat, u_ctx, u_ctx, a_ctx, w2f, baf, w2b, bab, gla_norm)


def _col_window_matrices():
    t = np.arange(SUPER)
    r, c = t // GRID_W, t % GRID_W
    mats = []
    for w in POOL_WINDOWS:
        lo = np.maximum(c - w // 2, 0)[:, None]
        hi = np.minimum(c + w // 2, GRID_W)[:, None]
        m = (r[:, None] == r[None, :]) & (c[None, :] >= lo) & (c[None, :] < hi)
        mats.append(m.astype(np.float32))
    return jnp.asarray(np.stack(mats), dtype=BF16)


def _pool_body(p_ref, cw_ref, wp_ref, ps_ref, y_ref, pad_ref):
    seq = p_ref.shape[0]
    n_rows = seq // GRID_W
    zeros = jnp.zeros((POOL_PAD, POOL_GROUP), F32)
    pad_ref[0:POOL_PAD, :] = zeros
    pad_ref[POOL_PAD + seq:POOL_PAD + seq + POOL_PAD, :] = zeros
    t = lax.broadcasted_iota(I32, (seq, POOL_GROUP), 0)
    r = t >> 6
    c = t & (GRID_W - 1)
    for gi, w in enumerate(POOL_WINDOWS):
        cols = slice(gi * POOL_GROUP, (gi + 1) * POOL_GROUP)
        cw = cw_ref[gi]
        for j in range(seq // SUPER):
            rows = slice(j * SUPER, (j + 1) * SUPER)
            pad_ref[POOL_PAD + j * SUPER:POOL_PAD + (j + 1) * SUPER, :] = jnp.dot(
                cw, p_ref[rows, cols], preferred_element_type=F32)
        total = None
        for d in range(-(w // 2), w // 2):
            start = POOL_PAD + d * GRID_W
            part = pad_ref[start:start + seq, :]
            total = part if total is None else total + part
        cnt_r = jnp.minimum(r + w // 2, n_rows) - jnp.maximum(r - w // 2, 0)
        cnt_c = jnp.minimum(c + w // 2, GRID_W) - jnp.maximum(c - w // 2, 0)
        mean = total / (cnt_r * cnt_c).astype(F32)
        diff = (mean - p_ref[:, cols].astype(F32)).astype(BF16)
        y = jnp.dot(diff, wp_ref[gi], preferred_element_type=F32) * ps_ref[:, cols]
        y_ref[:, cols] = y.astype(BF16)


def _pool_mixer(u_lat, col_mats, w_pool, pool_scale, batch, seq):
    pb = (u_lat.shape[1] - POOL_WIDTH) // POOL_WIDTH
    ng = len(POOL_WINDOWS)
    return pl.pallas_call(
        _pool_body,
        grid=(batch,),
        in_specs=[
            pl.BlockSpec((seq, POOL_WIDTH), lambda b: (b, pb)),
            _resident((ng, SUPER, SUPER)),
            _resident((ng, POOL_GROUP, POOL_GROUP)),
            _resident((1, POOL_WIDTH)),
        ],
        out_specs=pl.BlockSpec((seq, POOL_WIDTH), lambda b: (b, 0)),
        out_shape=jax.ShapeDtypeStruct((batch * seq, POOL_WIDTH), BF16),
        scratch_shapes=[pltpu.VMEM((seq + 2 * POOL_PAD, POOL_GROUP), F32)],
        compiler_params=_cparams(("arbitrary",)),
        name="pool_mixer",
    )(u_lat, col_mats, w_pool, pool_scale)


def _first_index(hit, iota, size, axis):
    return jnp.min(jnp.where(hit, iota, size), axis=axis, keepdims=True)


def _mix_body(yg_ref, yp_ref, x_ref, gt1_ref, sc2_ref, sh2_ref, gpost_ref, gpre_ref, wout_ref,
              wrh_ref, wrl_ref, rb_ref, wsg_ref, wsu_ref, wsd_ref, upper_ref,
              x1_ref, hp_ref, shr_ref, eidx_ref, pos_ref, wts_ref, cnt_ref, run_ref):
    tm = x_ref.shape[0]
    neg_inf = jnp.float32(-jnp.inf)

    @pl.when(pl.program_id(0) == 0)
    def _():
        run_ref[...] = jnp.zeros_like(run_ref)

    y = jnp.dot(yg_ref[...], wout_ref[0:GLA_WIDTH, :], preferred_element_type=F32)
    y = y + jnp.dot(yp_ref[...], wout_ref[GLA_WIDTH:, :], preferred_element_type=F32)
    x1 = x_ref[...] + gt1_ref[0] * (y * _rms_scale(y) * gpost_ref[...])
    x1_ref[...] = x1
    h = x1 * _rms_scale(x1) * gpre_ref[...]
    h = h * (1.0 + sc2_ref[0]) + sh2_ref[0]
    hp_ref[...] = _pack_halves(h[:, :HALF], h[:, HALF:])
    hb = h.astype(BF16)

    sg = jnp.dot(hb, wsg_ref[...], preferred_element_type=F32)
    su = jnp.dot(hb, wsu_ref[...], preferred_element_type=F32)
    act = (_silu(sg) * su).astype(BF16)
    shr_ref[...] = jnp.dot(act, wsd_ref[...], preferred_element_type=F32).astype(BF16)

    h_lo = (h - hb.astype(F32)).astype(BF16)
    lt = jnp.dot(hb, wrh_ref[...], preferred_element_type=F32)
    lt = lt + jnp.dot(hb, wrl_ref[...], preferred_element_type=F32)
    lt = lt + jnp.dot(h_lo, wrh_ref[...], preferred_element_type=F32)
    logits = lt.T[0:N_EXPERTS, :]
    scores = jax.nn.sigmoid(logits)
    sel = scores + rb_ref[...]
    shape3 = (N_GROUPS, GROUP_SIZE, tm)
    sel3 = sel.reshape(shape3)
    i_in = lax.broadcasted_iota(I32, shape3, 1).astype(F32)
    m1 = jnp.max(sel3, axis=1, keepdims=True)
    f1 = _first_index(sel3 == m1, i_in, float(GROUP_SIZE), 1)
    m2 = jnp.max(jnp.where(i_in == f1, neg_inf, sel3), axis=1, keepdims=True)
    grp = jnp.broadcast_to(m1 + m2, shape3).reshape(N_EXPERTS, tm)
    i_e = lax.broadcasted_iota(I32, (N_EXPERTS, tm), 0)
    i_grp = (i_e >> 3).astype(F32)
    i_e = i_e.astype(F32)
    allowed = jnp.zeros((N_EXPERTS, tm), F32)
    for _ in range(TOPK_GROUPS):
        m = jnp.max(grp, axis=0, keepdims=True)
        pick = i_grp == _first_index(grp == m, i_grp, float(N_GROUPS), 0)
        allowed = jnp.where(pick, 1.0, allowed)
        grp = jnp.where(pick, neg_inf, grp)
    cand = jnp.where(allowed > 0.0, sel, neg_inf)
    onehot = jnp.zeros((N_EXPERTS, tm), F32)
    picks, wts = [], []
    for k in range(TOP_K):
        m = jnp.max(cand, axis=0, keepdims=True)
        f = _first_index(cand == m, i_e, float(N_EXPERTS), 0)
        pick = i_e == f
        picks.append(pick)
        eidx_ref[k:k + 1, :] = f.astype(I32)
        wts.append(jnp.sum(jnp.where(pick, scores, 0.0), axis=0, keepdims=True))
        onehot = jnp.where(pick, 1.0, onehot)
        cand = jnp.where(pick, neg_inf, cand)
    w_sum = wts[0]
    for k in range(1, TOP_K):
        w_sum = w_sum + wts[k]
    for k in range(TOP_K):
        wts_ref[k:k + 1, :] = wts[k] / w_sum * ROUTED_SCALE

    before = jnp.dot(onehot.astype(BF16), upper_ref[...], preferred_element_type=F32)
    before = before + run_ref[:, 0:1]
    for k in range(TOP_K):
        pos_ref[k:k + 1, :] = jnp.sum(jnp.where(picks[k], before, 0.0), axis=0, keepdims=True).astype(I32)
    run_ref[...] = run_ref[...] + jnp.sum(onehot, axis=1, keepdims=True)
    cnt_ref[...] = run_ref[...].astype(I32)


def _mix_and_route(y_gla, y_pool, x2d, gt1, sc2, sh2, g_post, g_pre, w_out, wr_hi, wr_lo, router_bias,
                   w_sg, w_su, w_sd, seq):
    rows, d = x2d.shape
    tiles_per_b = seq // MIX_TM
    bmap = lambda i: (i // tiles_per_b, 0, 0)
    rmap = lambda i: (i, 0)
    tmap = lambda i: (0, i)
    upper = jnp.asarray(np.triu(np.ones((MIX_TM, MIX_TM), np.float32), 1), dtype=BF16)
    return pl.pallas_call(
        _mix_body,
        grid=(rows // MIX_TM,),
        in_specs=[
            pl.BlockSpec((MIX_TM, GLA_WIDTH), rmap),
            pl.BlockSpec((MIX_TM, POOL_WIDTH), rmap),
            pl.BlockSpec((MIX_TM, d), rmap),
            pl.BlockSpec((1, 1, d), bmap),
            pl.BlockSpec((1, 1, d), bmap),
            pl.BlockSpec((1, 1, d), bmap),
            _resident((1, d)),
            _resident((1, d)),
            _resident((d, d)),
            _resident((d, 128)),
            _resident((d, 128)),
            _resident((N_EXPERTS, 1)),
            _resident((d, D_SHARED)),
            _resident((d, D_SHARED)),
            _resident((D_SHARED, d)),
            _resident((MIX_TM, MIX_TM)),
        ],
        out_specs=[
            pl.BlockSpec((MIX_TM, d), rmap),
            pl.BlockSpec((MIX_TM, HALF), rmap),
            pl.BlockSpec((MIX_TM, d), rmap),
            pl.BlockSpec((TOP_K, MIX_TM), tmap),
            pl.BlockSpec((TOP_K, MIX_TM), tmap),
            pl.BlockSpec((TOP_K, MIX_TM), tmap),
            pl.BlockSpec((N_EXPERTS, 128), lambda i: (0, 0)),
        ],
        out_shape=[
            jax.ShapeDtypeStruct((rows, d), F32),
            jax.ShapeDtypeStruct((rows, HALF), U32),
            jax.ShapeDtypeStruct((rows, d), BF16),
            jax.ShapeDtypeStruct((TOP_K, rows), I32),
            jax.ShapeDtypeStruct((TOP_K, rows), I32),
            jax.ShapeDtypeStruct((TOP_K, rows), F32),
            jax.ShapeDtypeStruct((N_EXPERTS, 128), I32),
        ],
        scratch_shapes=[pltpu.VMEM((N_EXPERTS, 128), F32)],
        compiler_params=_cparams(("arbitrary",)),
        name="mix_and_route",
    )(y_gla, y_pool, x2d, gt1, sc2, sh2, g_post, g_pre, w_out, wr_hi, wr_lo, router_bias,
      w_sg, w_su, w_sd, upper)


def _offsets_body(ps_ref, e_ref, p_ref, d_ref):
    e = e_ref[...]
    d = p_ref[...]
    for x in range(N_EXPERTS):
        d = d + jnp.where(e == x, ps_ref[x], 0)
    d_ref[...] = d


def _route_offsets(eidx_t, pos_t, pstarts):
    k, rows = eidx_t.shape
    spec = pl.BlockSpec((k, OFFS_TN), lambda i, ps: (0, i))
    grid_spec = pltpu.PrefetchScalarGridSpec(
        num_scalar_prefetch=1, grid=(rows // OFFS_TN,), in_specs=[spec, spec], out_specs=spec)
    return pl.pallas_call(
        _offsets_body,
        grid_spec=grid_spec,
        out_shape=jax.ShapeDtypeStruct((k, rows), I32),
        compiler_params=_cparams(("arbitrary",)),
        name="route_offsets",
    )(pstarts, eidx_t, pos_t)


def _row_copy(src_ref, src_row, dst_ref, dst_row, sem):
    return pltpu.make_async_copy(src_ref.at[pl.ds(src_row, 1)], dst_ref.at[pl.ds(dst_row, 1)], sem)


def _dispatch_body(pend_ref, padded_ref, dest_ref, h_ref, xs_ref, zero_ref, zsem, sem):
    tt = h_ref.shape[0]

    def zero_copy(e):
        start = pl.multiple_of(pend_ref[e] - MOE_TILE, MOE_TILE)
        return pltpu.make_async_copy(zero_ref, xs_ref.at[pl.ds(start, MOE_TILE)], zsem)

    @pl.when(pl.program_id(0) == 0)
    def _():
        zero_ref[...] = jnp.zeros_like(zero_ref)

        def zstart(e, c):
            @pl.when(padded_ref[e] > 0)
            def _():
                zero_copy(e).start()
            return c

        def zwait(e, c):
            @pl.when(padded_ref[e] > 0)
            def _():
                zero_copy(e).wait()
            return c

        lax.fori_loop(0, N_EXPERTS, zstart, 0)
        lax.fori_loop(0, N_EXPERTS, zwait, 0)

    def start(t, c):
        for k in range(TOP_K):
            _row_copy(h_ref, t, xs_ref, dest_ref[k, t], sem).start()
        return c

    def wait(t, c):
        for k in range(TOP_K):
            _row_copy(h_ref, t, xs_ref, dest_ref[k, t], sem).wait()
        return c

    lax.fori_loop(0, tt, start, 0)
    lax.fori_loop(0, tt, wait, 0)


def _dispatch(h_packed, dest_t, pends, padded, n_rows):
    rows, half = h_packed.shape
    grid_spec = pltpu.PrefetchScalarGridSpec(
        num_scalar_prefetch=2,
        grid=(rows // DISP_TT,),
        in_specs=[
            pl.BlockSpec((TOP_K, DISP_TT), lambda i, *_: (0, i), memory_space=pltpu.SMEM),
            pl.BlockSpec((DISP_TT, half), lambda i, *_: (i, 0)),
        ],
        out_specs=pl.BlockSpec(memory_space=pl.ANY),
        scratch_shapes=[
            pltpu.VMEM((MOE_TILE, half), U32),
            pltpu.SemaphoreType.DMA,
            pltpu.SemaphoreType.DMA,
        ],
    )
    return pl.pallas_call(
        _dispatch_body,
        grid_spec=grid_spec,
        out_shape=jax.ShapeDtypeStruct((n_rows, half), U32),
        compiler_params=_cparams(("arbitrary",)),
        name="dispatch",
    )(pends, padded, dest_t, h_packed)


def _expert_body(be_ref, nu_ref, x_ref, wg_ref, wu_ref, wd_ref, y_ref, wg_b, wu_b, wd_b):
    i = pl.program_id(0)

    @pl.when((i == 0) | (be_ref[i] != be_ref[jnp.maximum(i - 1, 0)]))
    def _():
        wg_b[...] = wg_ref[0].astype(BF16)
        wu_b[...] = wu_ref[0].astype(BF16)
        wd_b[...] = wd_ref[0].astype(BF16)

    @pl.when(i < nu_ref[0])
    def _():
        lo, hi = _unpack_halves(x_ref[...])
        lo = lo.astype(BF16)
        hi = hi.astype(BF16)
        g = jnp.dot(lo, wg_b[:HALF, :], preferred_element_type=F32)
        g = g + jnp.dot(hi, wg_b[HALF:, :], preferred_element_type=F32)
        u = jnp.dot(lo, wu_b[:HALF, :], preferred_element_type=F32)
        u = u + jnp.dot(hi, wu_b[HALF:, :], preferred_element_type=F32)
        act = (_silu(g) * u).astype(BF16)
        y = jnp.dot(act, wd_b[...], preferred_element_type=F32)
        y_ref[...] = _pack_halves(y[:, :HALF], y[:, HALF:])


def _experts(xs, block_expert, n_used, w_eg, w_eu, w_ed):
    n_rows, half = xs.shape
    n_blocks = n_rows // MOE_TILE
    d, de = w_eg.shape[1], w_eg.shape[2]
    row_map = lambda i, be, nu: (jnp.minimum(i, nu[0] - 1), 0)
    grid_spec = pltpu.PrefetchScalarGridSpec(
        num_scalar_prefetch=2,
        grid=(n_blocks,),
        in_specs=[
            pl.BlockSpec((MOE_TILE, half), row_map),
            pl.BlockSpec((1, d, de), lambda i, be, nu: (be[i], 0, 0)),
            pl.BlockSpec((1, d, de), lambda i, be, nu: (be[i], 0, 0)),
            pl.BlockSpec((1, de, d), lambda i, be, nu: (be[i], 0, 0)),
        ],
        out_specs=pl.BlockSpec((MOE_TILE, half), row_map),
        scratch_shapes=[
            pltpu.VMEM((d, de), BF16),
            pltpu.VMEM((d, de), BF16),
            pltpu.VMEM((de, d), BF16),
        ],
    )
    return pl.pallas_call(
        _expert_body,
        grid_spec=grid_spec,
        out_shape=jax.ShapeDtypeStruct((n_rows, half), U32),
        compiler_params=_cparams(("arbitrary",)),
        name="experts",
    )(block_expert, n_used, xs, w_eg, w_eu, w_ed)


def _combine_body(dest_ref, ys_ref, w_ref, shr_ref, x1_ref, gt2_ref, gpost_ref, o_ref, buf, sem):
    tt = x1_ref.shape[0]

    def start(t, c):
        for k in range(TOP_K):
            _row_copy(ys_ref, dest_ref[k, t], buf.at[k], t, sem).start()
        return c

    def wait(t, c):
        for k in range(TOP_K):
            _row_copy(ys_ref, dest_ref[k, t], buf.at[k], t, sem).wait()
        return c

    lax.fori_loop(0, tt, start, 0)
    lax.fori_loop(0, tt, wait, 0)

    w = w_ref[...]
    y_lo = shr_ref[:, :HALF].astype(F32)
    y_hi = shr_ref[:, HALF:].astype(F32)
    for k in range(TOP_K):
        lo, hi = _unpack_halves(buf[k])
        y_lo = y_lo + w[:, k:k + 1] * lo
        y_hi = y_hi + w[:, k:k + 1] * hi
    ms = (jnp.sum(y_lo * y_lo, axis=-1, keepdims=True) + jnp.sum(y_hi * y_hi, axis=-1, keepdims=True)) / D_MODEL
    scale = lax.rsqrt(ms + EPS)
    gt2 = gt2_ref[0]
    gpost = gpost_ref[...]
    o_ref[:, :HALF] = x1_ref[:, :HALF] + gt2[:, :HALF] * (y_lo * scale * gpost[:, :HALF])
    o_ref[:, HALF:] = x1_ref[:, HALF:] + gt2[:, HALF:] * (y_hi * scale * gpost[:, HALF:])


def _combine(dest_t, ys, wts, shared, x1, gt2, g_post, seq):
    rows, d = x1.shape
    tiles_per_b = seq // COMB_TT
    return pl.pallas_call(
        _combine_body,
        grid=(rows // COMB_TT,),
        in_specs=[
            pl.BlockSpec((TOP_K, COMB_TT), lambda i: (0, i), memory_space=pltpu.SMEM),
            pl.BlockSpec(memory_space=pl.ANY),
            pl.BlockSpec((COMB_TT, TOP_K), lambda i: (i, 0)),
            pl.BlockSpec((COMB_TT, d), lambda i: (i, 0)),
            pl.BlockSpec((COMB_TT, d), lambda i: (i, 0)),
            pl.BlockSpec((1, 1, d), lambda i: (i // tiles_per_b, 0, 0)),
            _resident((1, d)),
        ],
        out_specs=pl.BlockSpec((COMB_TT, d), lambda i: (i, 0)),
        out_shape=jax.ShapeDtypeStruct((rows, d), F32),
        scratch_shapes=[
            pltpu.VMEM((TOP_K, COMB_TT, HALF), U32),
            pltpu.SemaphoreType.DMA,
        ],
        compiler_params=_cparams(("arbitrary",)),
        name="combine",
    )(dest_t, ys, wts, shared, x1, gt2, g_post)


def kernel(x, c, ctx, c_ctx, w_mod, b_mod, norm_mix_pre, norm_mix_post, norm_ffn_pre, norm_ffn_post, w_in, w_a2_fwd, b_a_fwd, w_a2_bwd, b_a_bwd, gla_norm, w_pool, pool_scale, w_out, w_router, router_bias, w_exp_gate, w_exp_up, w_exp_down, w_sh_gate, w_sh_up, w_sh_down):
    batch, seq, d = x.shape
    n_ctx = ctx.shape[1]
    assert w_mod.shape[0] == 1 and d == D_MODEL
    assert seq % SUPER == 0 and n_ctx % SUPER == 0 and seq % PROJ_TM == 0 and (batch * n_ctx) % PROJ_TM == 0
    rows = batch * seq

    mod_rows = 16
    c_all = jnp.concatenate([c, c_ctx[None, :], jnp.zeros((mod_rows - batch - 1, d), F32)], axis=0)
    mod_all = _modulation(c_all, w_mod[0], b_mod[0][None, :])
    sh1, sc1, gt1, sh2, sc2, gt2 = [m.reshape(batch, 1, d) for m in jnp.split(mod_all[:batch], 6, axis=-1)]
    csh1 = mod_all[batch, 0:d].reshape(1, 1, d)
    csc1 = mod_all[batch, d:2 * d].reshape(1, 1, d)

    kw, gw = GLA_KEY_WIDTH, GLA_WIDTH
    a0 = 2 * kw + 2 * gw
    w_in0 = w_in[0]
    w_main = jnp.concatenate([w_in0[:, :a0], w_in0[:, a0 + 2 * GLA_RANK:]], axis=1).astype(BF16)
    w_a = jnp.pad(w_in0[:, a0:a0 + 2 * GLA_RANK], ((0, 0), (0, 128 - 2 * GLA_RANK))).astype(BF16)
    w_ctx = w_main[:, kw:2 * kw + gw]
    w2f = jnp.pad(w_a2_fwd[0], ((0, 128 - GLA_RANK), (0, 0))).astype(BF16)
    w2b = jnp.pad(w_a2_bwd[0], ((GLA_RANK, 128 - 2 * GLA_RANK), (0, 0))).astype(BF16)
    g_mix_pre = norm_mix_pre[0][None, :]
    w_r = jnp.pad(w_router[0], ((0, 0), (0, 128 - N_EXPERTS)))
    wr_hi = w_r.astype(BF16)
    wr_lo = (w_r - wr_hi.astype(F32)).astype(BF16)

    u_ctx, a_ctx = _in_projection(ctx.reshape(batch * n_ctx, d), g_mix_pre, csc1, csh1, w_ctx, w_a,
                                  batch * n_ctx)
    u_lat, a_lat = _in_projection(x.reshape(rows, d), g_mix_pre, sc1, sh1, w_main, w_a, seq)

    y_gla = _gla(u_lat, a_lat, u_ctx, a_ctx, w2f, b_a_fwd[0][None, :], w2b, b_a_bwd[0][None, :],
                 gla_norm[0][None, :], batch, seq, n_ctx)
    y_pool = _pool_mixer(u_lat, _col_window_matrices(), w_pool[0].astype(BF16), pool_scale[0][None, :],
                         batch, seq)

    x1, h_packed, shared, eidx_t, pos_t, wts_t, counts = _mix_and_route(
        y_gla, y_pool, x.reshape(rows, d), gt1, sc2, sh2, norm_mix_post[0][None, :],
        norm_ffn_pre[0][None, :], w_out[0].astype(BF16), wr_hi, wr_lo, router_bias[0][:, None],
        w_sh_gate[0].astype(BF16), w_sh_up[0].astype(BF16), w_sh_down[0].astype(BF16), seq)

    counts = counts[:, 0]
    padded = (counts + MOE_TILE - 1) // MOE_TILE * MOE_TILE
    pends = jnp.cumsum(padded)
    pstarts = pends - padded
    dest_t = _route_offsets(eidx_t, pos_t, pstarts.astype(I32))
    n_blocks = rows * TOP_K // MOE_TILE + N_EXPERTS
    n_used = (pends[-1] // MOE_TILE).astype(I32)
    blk = jnp.minimum(jnp.arange(n_blocks, dtype=I32), n_used - 1)
    block_expert = jnp.sum((blk * MOE_TILE)[:, None] >= pends[None, :], axis=1).astype(I32)
    block_expert = jnp.minimum(block_expert, N_EXPERTS - 1)

    xs = _dispatch(h_packed, dest_t, pends.astype(I32), padded.astype(I32), n_blocks * MOE_TILE)
    ys = _experts(xs, block_expert, n_used.reshape(1), w_exp_gate[0], w_exp_up[0], w_exp_down[0])
    out = _combine(dest_t, ys, wts_t.T, shared, x1, gt2, norm_ffn_post[0][None, :], seq)
    return out.reshape(batch, seq, d)
```

```python
import functools

import numpy as np
import jax
import jax.numpy as jnp
from jax import lax
from jax.experimental import pallas as pl
from jax.experimental.pallas import tpu as pltpu
from jax.experimental.pallas import tpu_sc as plsc

F32 = jnp.float32
BF16 = jnp.bfloat16
I32 = jnp.int32
U32 = jnp.uint32
HIGHEST = lax.Precision.HIGHEST

D_MODEL = 2048
GRID_W = 64
GLA_HEADS = 4
GLA_DK = 128
GLA_DV = 256
GLA_KEY_WIDTH = GLA_HEADS * GLA_DK
GLA_WIDTH = GLA_HEADS * GLA_DV
GLA_RANK = 16
GLA_TAU = 16.0
GLA_CHUNK = 64
POOL_WIDTH = 1024
POOL_WINDOWS = (2, 4, 8, 16)
POOL_GROUP = 256
N_EXPERTS = 64
TOP_K = 8
N_GROUPS = 8
GROUP_SIZE = N_EXPERTS // N_GROUPS
TOPK_GROUPS = 4
D_EXPERT = 512
D_SHARED = 512
ROUTED_SCALE = 2.5
EPS = 1e-6

HALF = D_MODEL // 2
SUPER = 4 * GLA_CHUNK
GLA_HPS = 4
POOL_PAD = 8 * GRID_W
VMEM_LIMIT = 56 * 1024 * 1024

MOD_TN = 1024
PROJ_TM = 512
PROJ_TN = 512
MIX_TM = 512
MOE_TILE = 512
DISP_TT = 256
COMB_TT = 256
ROW_SUBLANES = 8
SC_CORES = 2
SC_WORKERS = 32
SC_CHUNK = 32
OFFS_TN = 2048


def _cparams(sem):
    return pltpu.CompilerParams(dimension_semantics=sem, vmem_limit_bytes=VMEM_LIMIT)


def _resident(shape):
    nd = len(shape)
    return pl.BlockSpec(shape, lambda *_: (0,) * nd, pipeline_mode=pl.Buffered(1))


def _silu(v):
    return v * jax.nn.sigmoid(v)


def _pack_halves(lo, hi):
    lo_b = lax.bitcast_convert_type(lo.astype(BF16).astype(F32), U32)
    hi_b = lax.bitcast_convert_type(hi.astype(BF16).astype(F32), U32)
    return (hi_b & jnp.uint32(0xFFFF0000)) | (lo_b >> 16)


def _unpack_halves(p):
    lo = lax.bitcast_convert_type(p << 16, F32)
    hi = lax.bitcast_convert_type(p & jnp.uint32(0xFFFF0000), F32)
    return lo, hi


def _mod_body(c_ref, w_ref, b_ref, o_ref):
    s = _silu(c_ref[...])
    o_ref[...] = jnp.dot(s, w_ref[...], preferred_element_type=F32, precision=HIGHEST) + b_ref[...]


def _modulation(c_all, w_mod, b_mod):
    rows, d = c_all.shape
    n = w_mod.shape[1]
    return pl.pallas_call(
        _mod_body,
        grid=(n // MOD_TN,),
        in_specs=[
            pl.BlockSpec((rows, d), lambda j: (0, 0)),
            pl.BlockSpec((d, MOD_TN), lambda j: (0, j)),
            pl.BlockSpec((1, MOD_TN), lambda j: (0, j)),
        ],
        out_specs=pl.BlockSpec((rows, MOD_TN), lambda j: (0, j)),
        out_shape=jax.ShapeDtypeStruct((rows, n), F32),
        compiler_params=_cparams(("arbitrary",)),
        name="modulation",
    )(c_all, w_mod, b_mod)


def _rms_scale(x):
    return lax.rsqrt(jnp.mean(x * x, axis=-1, keepdims=True) + EPS)


def _inproj_body(x_ref, g_ref, sc_ref, sh_ref, w_ref, wa_ref, o_ref, a_ref, *, n_main):
    x = x_ref[...]
    h = x * _rms_scale(x) * g_ref[...]
    h = h * (1.0 + sc_ref[0]) + sh_ref[0]
    hb = h.astype(BF16)
    for n in range(n_main // PROJ_TN):
        cols = slice(n * PROJ_TN, (n + 1) * PROJ_TN)
        o_ref[:, cols] = jnp.dot(hb, w_ref[:, cols], preferred_element_type=F32).astype(BF16)
    a_ref[...] = jnp.dot(hb, wa_ref[...], preferred_element_type=F32)


def _in_projection(x2d, gain, sc, sh, w_main, w_a, rows_per_mod):
    rows, d = x2d.shape
    n_main = w_main.shape[1]
    tiles_per_mod = rows_per_mod // PROJ_TM
    mod_map = lambda i: (i // tiles_per_mod, 0, 0)
    return pl.pallas_call(
        functools.partial(_inproj_body, n_main=n_main),
        grid=(rows // PROJ_TM,),
        in_specs=[
            pl.BlockSpec((PROJ_TM, d), lambda i: (i, 0)),
            _resident((1, d)),
            pl.BlockSpec((1, 1, d), mod_map),
            pl.BlockSpec((1, 1, d), mod_map),
            _resident((d, n_main)),
            _resident((d, 128)),
        ],
        out_specs=[
            pl.BlockSpec((PROJ_TM, n_main), lambda i: (i, 0)),
            pl.BlockSpec((PROJ_TM, 128), lambda i: (i, 0)),
        ],
        out_shape=[
            jax.ShapeDtypeStruct((rows, n_main), BF16),
            jax.ShapeDtypeStruct((rows, 128), F32),
        ],
        compiler_params=_cparams(("arbitrary",)),
        name="in_projection",
    )(x2d, gain, sc, sh, w_main, w_a)


def _log_sigmoid(z):
    return jnp.minimum(z, 0.0) - jnp.log1p(jnp.exp(-jnp.abs(z)))


def _gla_super(q, k, v, a, w2, ba, tri, mask, st_ref, reverse):
    nc = SUPER // GLA_CHUNK
    z = jnp.dot(a.astype(BF16), w2, preferred_element_type=F32) + ba
    g = _log_sigmoid(z) * (1.0 / GLA_TAU)
    g_hi = g.astype(BF16)
    g_lo = (g - g_hi.astype(F32)).astype(BF16)
    G = jnp.dot(tri, g_hi, preferred_element_type=F32) + jnp.dot(tri, g_lo, preferred_element_type=F32)
    G = G.reshape(nc, GLA_CHUNK, GLA_DK)
    end_row = 0 if reverse else GLA_CHUNK - 1
    mid_row = GLA_CHUNK - 1 - GLA_CHUNK // 2 if reverse else GLA_CHUNK // 2
    g_end = G[:, end_row:end_row + 1, :]
    g_mid = G[:, mid_row:mid_row + 1, :]
    k4 = k.astype(F32).reshape(nc, GLA_CHUNK, GLA_DK)
    kd = (k4 * jnp.exp(g_end - G)).astype(BF16)
    dec = jnp.exp(g_end)
    o = None
    if q is not None:
        q4 = q.astype(F32).reshape(nc, GLA_CHUNK, GLA_DK) * (GLA_DK ** -0.5)
        qg = (q4 * jnp.exp(G - g_mid)).reshape(SUPER, GLA_DK).astype(BF16)
        kg = (k4 * jnp.exp(g_mid - G)).reshape(SUPER, GLA_DK).astype(BF16)
        qe = (q4 * jnp.exp(G)).astype(BF16)
        att = lax.dot_general(qg, kg, (((1,), (1,)), ((), ())), preferred_element_type=F32)
        att = jnp.where(mask, att, 0.0).astype(BF16)
        o = jnp.dot(att, v, preferred_element_type=F32)
    outs = [None] * nc
    order = range(nc - 1, -1, -1) if reverse else range(nc)
    for c in order:
        rows = slice(c * GLA_CHUNK, (c + 1) * GLA_CHUNK)
        st = st_ref[...]
        if q is not None:
            inter = lax.dot_general(qe[c], st.astype(BF16), (((1,), (1,)), ((), ())),
                                    preferred_element_type=F32)
            outs[c] = o[rows] + inter
        upd = lax.dot_general(v[rows], kd[c], (((0,), (0,)), ((), ())), preferred_element_type=F32)
        st_ref[...] = st * dec[c] + upd
    if q is None:
        return None
    return jnp.concatenate(outs, axis=0)


def _gla_body(q_ref, k_ref, v_ref, r_ref, a_ref, kc_ref, vc_ref, ac_ref,
              w2f_ref, baf_ref, w2b_ref, bab_ref, gn_ref, y_ref, o_acc, st, *, n_ctx):
    n_sup = q_ref.shape[0] // SUPER
    row = lax.broadcasted_iota(I32, (SUPER, SUPER), 0)
    col = lax.broadcasted_iota(I32, (SUPER, SUPER), 1)
    same_chunk = (row >> 6) == (col >> 6)
    mask_f = same_chunk & (col <= row)
    mask_b = same_chunk & (col >= row)
    tri_f = jnp.where(mask_f, 1.0, 0.0).astype(BF16)
    tri_b = jnp.where(mask_b, 1.0, 0.0).astype(BF16)
    heads = range(GLA_HPS)
    kcol = [slice(h * GLA_DK, (h + 1) * GLA_DK) for h in heads]
    vcol = [slice(h * GLA_DV, (h + 1) * GLA_DV) for h in heads]
    gate_f = [(w2f_ref[:, kcol[h]], baf_ref[:, kcol[h]]) for h in heads]
    gate_b = [(w2b_ref[:, kcol[h]], bab_ref[:, kcol[h]]) for h in heads]

    st[...] = jnp.zeros_like(st)
    n_csup = n_ctx // SUPER
    for s in range(n_csup):
        rf = slice(s * SUPER, (s + 1) * SUPER)
        rb = slice((n_csup - 1 - s) * SUPER, (n_csup - s) * SUPER)
        for h in heads:
            _gla_super(None, kc_ref[rf, kcol[h]], vc_ref[rf, vcol[h]], ac_ref[rf, :], *gate_f[h],
                       tri_f, mask_f, st.at[0, h], False)
            _gla_super(None, kc_ref[rb, kcol[h]], vc_ref[rb, vcol[h]], ac_ref[rb, :], *gate_b[h],
                       tri_b, mask_b, st.at[1, h], True)

    o_acc[...] = jnp.zeros_like(o_acc)

    def step(i, carry):
        rf = pl.ds(pl.multiple_of(i * SUPER, SUPER), SUPER)
        rb = pl.ds(pl.multiple_of((n_sup - 1 - i) * SUPER, SUPER), SUPER)
        a_f = a_ref[rf, :]
        a_b = a_ref[rb, :]
        for h in heads:
            of = _gla_super(q_ref[rf, kcol[h]], k_ref[rf, kcol[h]], v_ref[rf, vcol[h]], a_f, *gate_f[h],
                            tri_f, mask_f, st.at[0, h], False)
            o_acc[rf, vcol[h]] += of
            ob = _gla_super(q_ref[rb, kcol[h]], k_ref[rb, kcol[h]], v_ref[rb, vcol[h]], a_b, *gate_b[h],
                            tri_b, mask_b, st.at[1, h], True)
            o_acc[rb, vcol[h]] += ob
        return carry

    lax.fori_loop(0, n_sup, step, 0)

    for h in heads:
        o = o_acc[:, vcol[h]]
        o = o * _rms_scale(o) * gn_ref[:, vcol[h]]
        y_ref[:, vcol[h]] = (o * _silu(r_ref[:, vcol[h]].astype(F32))).astype(BF16)


def _gla(u_lat, a_lat, u_ctx, a_ctx, w2f, baf, w2b, bab, gla_norm, batch, seq, n_ctx):
    groups = GLA_HEADS // GLA_HPS
    kw, vw = GLA_HPS * GLA_DK, GLA_HPS * GLA_DV
    kb = GLA_KEY_WIDTH // kw
    vb = 2 * GLA_KEY_WIDTH // vw
    rb = vb + groups
    cvb = GLA_KEY_WIDTH // vw
    return pl.pallas_call(
        functools.partial(_gla_body, n_ctx=n_ctx),
        grid=(batch, groups),
        in_specs=[
            pl.BlockSpec((seq, kw), lambda b, h: (b, h)),
            pl.BlockSpec((seq, kw), lambda b, h: (b, kb + h)),
            pl.BlockSpec((seq, vw), lambda b, h: (b, vb + h)),
            pl.BlockSpec((seq, vw), lambda b, h: (b, rb + h)),
            pl.BlockSpec((seq, 128), lambda b, h: (b, 0)),
            pl.BlockSpec((n_ctx, kw), lambda b, h: (b, h)),
            pl.BlockSpec((n_ctx, vw), lambda b, h: (b, cvb + h)),
            pl.BlockSpec((n_ctx, 128), lambda b, h: (b, 0)),
            pl.BlockSpec((128, kw), lambda b, h: (0, h)),
            pl.BlockSpec((1, kw), lambda b, h: (0, h)),
            pl.BlockSpec((128, kw), lambda b, h: (0, h)),
            pl.BlockSpec((1, kw), lambda b, h: (0, h)),
            pl.BlockSpec((1, vw), lambda b, h: (0, h)),
        ],
        out_specs=pl.BlockSpec((seq, vw), lambda b, h: (b, h)),
        out_shape=jax.ShapeDtypeStruct((batch * seq, GLA_WIDTH), BF16),
        scratch_shapes=[
            pltpu.VMEM((seq, vw), F32),
            pltpu.VMEM((2, GLA_HPS, GLA_DV, GLA_DK), F32),
        ],
        compiler_params=_cparams(("arbitrary", "arbitrary")),
        name="gla",
    )(u_lat, u_lat, u_lat, u_lat, a_lat, u_ctx, u_ctx, a_ctx, w2f, baf, w2b, bab, gla_norm)


def _col_window_matrices():
    t = np.arange(SUPER)
    r, c = t // GRID_W, t % GRID_W
    mats = []
    for w in POOL_WINDOWS:
        lo = np.maximum(c - w // 2, 0)[:, None]
        hi = np.minimum(c + w // 2, GRID_W)[:, None]
        m = (r[:, None] == r[None, :]) & (c[None, :] >= lo) & (c[None, :] < hi)
        mats.append(m.astype(np.float32))
    return jnp.asarray(np.stack(mats), dtype=BF16)


def _pool_body(p_ref, cw_ref, wp_ref, ps_ref, y_ref, pad_ref):
    seq = p_ref.shape[0]
    n_rows = seq // GRID_W
    zeros = jnp.zeros((POOL_PAD, POOL_GROUP), F32)
    pad_ref[0:POOL_PAD, :] = zeros
    pad_ref[POOL_PAD + seq:POOL_PAD + seq + POOL_PAD, :] = zeros
    t = lax.broadcasted_iota(I32, (seq, POOL_GROUP), 0)
    r = t >> 6
    c = t & (GRID_W - 1)
    for gi, w in enumerate(POOL_WINDOWS):
        cols = slice(gi * POOL_GROUP, (gi + 1) * POOL_GROUP)
        cw = cw_ref[gi]
        for j in range(seq // SUPER):
            rows = slice(j * SUPER, (j + 1) * SUPER)
            pad_ref[POOL_PAD + j * SUPER:POOL_PAD + (j + 1) * SUPER, :] = jnp.dot(
                cw, p_ref[rows, cols], preferred_element_type=F32)
        total = None
        for d in range(-(w // 2), w // 2):
            start = POOL_PAD + d * GRID_W
            part = pad_ref[start:start + seq, :]
            total = part if total is None else total + part
        cnt_r = jnp.minimum(r + w // 2, n_rows) - jnp.maximum(r - w // 2, 0)
        cnt_c = jnp.minimum(c + w // 2, GRID_W) - jnp.maximum(c - w // 2, 0)
        mean = total / (cnt_r * cnt_c).astype(F32)
        diff = (mean - p_ref[:, cols].astype(F32)).astype(BF16)
        y = jnp.dot(diff, wp_ref[gi], preferred_element_type=F32) * ps_ref[:, cols]
        y_ref[:, cols] = y.astype(BF16)


def _pool_mixer(u_lat, col_mats, w_pool, pool_scale, batch, seq):
    pb = (u_lat.shape[1] - POOL_WIDTH) // POOL_WIDTH
    ng = len(POOL_WINDOWS)
    return pl.pallas_call(
        _pool_body,
        grid=(batch,),
        in_specs=[
            pl.BlockSpec((seq, POOL_WIDTH), lambda b: (b, pb)),
            _resident((ng, SUPER, SUPER)),
            _resident((ng, POOL_GROUP, POOL_GROUP)),
            _resident((1, POOL_WIDTH)),
        ],
        out_specs=pl.BlockSpec((seq, POOL_WIDTH), lambda b: (b, 0)),
        out_shape=jax.ShapeDtypeStruct((batch * seq, POOL_WIDTH), BF16),
        scratch_shapes=[pltpu.VMEM((seq + 2 * POOL_PAD, POOL_GROUP), F32)],
        compiler_params=_cparams(("arbitrary",)),
        name="pool_mixer",
    )(u_lat, col_mats, w_pool, pool_scale)


def _first_index(hit, iota, size, axis):
    return jnp.min(jnp.where(hit, iota, size), axis=axis, keepdims=True)


def _mix_body(yg_ref, yp_ref, x_ref, gt1_ref, sc2_ref, sh2_ref, gpost_ref, gpre_ref, wout_ref,
              wrh_ref, wrl_ref, rb_ref, wsg_ref, wsu_ref, wsd_ref, upper_ref,
              x1_ref, hp_ref, shr_ref, eidx_ref, pos_ref, wts_ref, cnt_ref, run_ref):
    tm = x_ref.shape[0]
    neg_inf = jnp.float32(-jnp.inf)

    @pl.when(pl.program_id(0) == 0)
    def _():
        run_ref[...] = jnp.zeros_like(run_ref)

    y = jnp.dot(yg_ref[...], wout_ref[0:GLA_WIDTH, :], preferred_element_type=F32)
    y = y + jnp.dot(yp_ref[...], wout_ref[GLA_WIDTH:, :], preferred_element_type=F32)
    x1 = x_ref[...] + gt1_ref[0] * (y * _rms_scale(y) * gpost_ref[...])
    x1_ref[...] = x1
    h = x1 * _rms_scale(x1) * gpre_ref[...]
    h = h * (1.0 + sc2_ref[0]) + sh2_ref[0]
    hp_ref[...] = _pack_halves(h[:, :HALF], h[:, HALF:])
    hb = h.astype(BF16)

    sg = jnp.dot(hb, wsg_ref[...], preferred_element_type=F32)
    su = jnp.dot(hb, wsu_ref[...], preferred_element_type=F32)
    act = (_silu(sg) * su).astype(BF16)
    shr_ref[...] = jnp.dot(act, wsd_ref[...], preferred_element_type=F32).astype(BF16)

    h_lo = (h - hb.astype(F32)).astype(BF16)
    lt = jnp.dot(hb, wrh_ref[...], preferred_element_type=F32)
    lt = lt + jnp.dot(hb, wrl_ref[...], preferred_element_type=F32)
    lt = lt + jnp.dot(h_lo, wrh_ref[...], preferred_element_type=F32)
    logits = lt.T[0:N_EXPERTS, :]
    scores = jax.nn.sigmoid(logits)
    sel = scores + rb_ref[...]
    shape3 = (N_GROUPS, GROUP_SIZE, tm)
    sel3 = sel.reshape(shape3)
    i_in = lax.broadcasted_iota(I32, shape3, 1).astype(F32)
    m1 = jnp.max(sel3, axis=1, keepdims=True)
    f1 = _first_index(sel3 == m1, i_in, float(GROUP_SIZE), 1)
    m2 = jnp.max(jnp.where(i_in == f1, neg_inf, sel3), axis=1, keepdims=True)
    grp = jnp.broadcast_to(m1 + m2, shape3).reshape(N_EXPERTS, tm)
    i_e = lax.broadcasted_iota(I32, (N_EXPERTS, tm), 0)
    i_grp = (i_e >> 3).astype(F32)
    i_e = i_e.astype(F32)
    allowed = jnp.zeros((N_EXPERTS, tm), F32)
    for _ in range(TOPK_GROUPS):
        m = jnp.max(grp, axis=0, keepdims=True)
        pick = i_grp == _first_index(grp == m, i_grp, float(N_GROUPS), 0)
        allowed = jnp.where(pick, 1.0, allowed)
        grp = jnp.where(pick, neg_inf, grp)
    cand = jnp.where(allowed > 0.0, sel, neg_inf)
    onehot = jnp.zeros((N_EXPERTS, tm), F32)
    picks, wts = [], []
    for k in range(TOP_K):
        m = jnp.max(cand, axis=0, keepdims=True)
        f = _first_index(cand == m, i_e, float(N_EXPERTS), 0)
        pick = i_e == f
        picks.append(pick)
        eidx_ref[k:k + 1, :] = f.astype(I32)
        wts.append(jnp.sum(jnp.where(pick, scores, 0.0), axis=0, keepdims=True))
        onehot = jnp.where(pick, 1.0, onehot)
        cand = jnp.where(pick, neg_inf, cand)
    w_sum = wts[0]
    for k in range(1, TOP_K):
        w_sum = w_sum + wts[k]
    for k in range(TOP_K):
        wts_ref[k:k + 1, :] = wts[k] / w_sum * ROUTED_SCALE

    before = jnp.dot(onehot.astype(BF16), upper_ref[...], preferred_element_type=F32)
    before = before + run_ref[:, 0:1]
    for k in range(TOP_K):
        pos_ref[k:k + 1, :] = jnp.sum(jnp.where(picks[k], before, 0.0), axis=0, keepdims=True).astype(I32)
    run_ref[...] = run_ref[...] + jnp.sum(onehot, axis=1, keepdims=True)
    cnt_ref[...] = run_ref[...].astype(I32)


def _mix_and_route(y_gla, y_pool, x2d, gt1, sc2, sh2, g_post, g_pre, w_out, wr_hi, wr_lo, router_bias,
                   w_sg, w_su, w_sd, seq):
    rows, d = x2d.shape
    tiles_per_b = seq // MIX_TM
    bmap = lambda i: (i // tiles_per_b, 0, 0)
    rmap = lambda i: (i, 0)
    tmap = lambda i: (0, i)
    upper = jnp.asarray(np.triu(np.ones((MIX_TM, MIX_TM), np.float32), 1), dtype=BF16)
    return pl.pallas_call(
        _mix_body,
        grid=(rows // MIX_TM,),
        in_specs=[
            pl.BlockSpec((MIX_TM, GLA_WIDTH), rmap),
            pl.BlockSpec((MIX_TM, POOL_WIDTH), rmap),
            pl.BlockSpec((MIX_TM, d), rmap),
            pl.BlockSpec((1, 1, d), bmap),
            pl.BlockSpec((1, 1, d), bmap),
            pl.BlockSpec((1, 1, d), bmap),
            _resident((1, d)),
            _resident((1, d)),
            _resident((d, d)),
            _resident((d, 128)),
            _resident((d, 128)),
            _resident((N_EXPERTS, 1)),
            _resident((d, D_SHARED)),
            _resident((d, D_SHARED)),
            _resident((D_SHARED, d)),
            _resident((MIX_TM, MIX_TM)),
        ],
        out_specs=[
            pl.BlockSpec((MIX_TM, d), rmap),
            pl.BlockSpec((MIX_TM, HALF), rmap),
            pl.BlockSpec((MIX_TM, d), rmap),
            pl.BlockSpec((TOP_K, MIX_TM), tmap),
            pl.BlockSpec((TOP_K, MIX_TM), tmap),
            pl.BlockSpec((TOP_K, MIX_TM), tmap),
            pl.BlockSpec((N_EXPERTS, 128), lambda i: (0, 0)),
        ],
        out_shape=[
            jax.ShapeDtypeStruct((rows, d), F32),
            jax.ShapeDtypeStruct((rows, HALF), U32),
            jax.ShapeDtypeStruct((rows, d), BF16),
            jax.ShapeDtypeStruct((TOP_K, rows), I32),
            jax.ShapeDtypeStruct((TOP_K, rows), I32),
            jax.ShapeDtypeStruct((TOP_K, rows), F32),
            jax.ShapeDtypeStruct((N_EXPERTS, 128), I32),
        ],
        scratch_shapes=[pltpu.VMEM((N_EXPERTS, 128), F32)],
        compiler_params=_cparams(("arbitrary",)),
        name="mix_and_route",
    )(y_gla, y_pool, x2d, gt1, sc2, sh2, g_post, g_pre, w_out, wr_hi, wr_lo, router_bias,
      w_sg, w_su, w_sd, upper)


def _offsets_body(ps_ref, e_ref, p_ref, d_ref):
    e = e_ref[...]
    d = p_ref[...]
    for x in range(N_EXPERTS):
        d = d + jnp.where(e == x, ps_ref[x], 0)
    d_ref[...] = d


def _route_offsets(eidx_t, pos_t, pstarts):
    k, rows = eidx_t.shape
    spec = pl.BlockSpec((k, OFFS_TN), lambda i, ps: (0, i))
    grid_spec = pltpu.PrefetchScalarGridSpec(
        num_scalar_prefetch=1, grid=(rows // OFFS_TN,), in_specs=[spec, spec], out_specs=spec)
    return pl.pallas_call(
        _offsets_body,
        grid_spec=grid_spec,
        out_shape=jax.ShapeDtypeStruct((k, rows), I32),
        compiler_params=_cparams(("arbitrary",)),
        name="route_offsets",
    )(pstarts, eidx_t, pos_t)


def _row_copy(src_ref, src_row, dst_ref, dst_row, sem):
    return pltpu.make_async_copy(src_ref.at[pl.ds(src_row, 1)], dst_ref.at[pl.ds(dst_row, 1)], sem)


def _dispatch_body(pend_ref, padded_ref, dest_ref, h_ref, xs_ref, zero_ref, zsem, sem):
    tt = h_ref.shape[0]

    def zero_copy(e):
        start = pl.multiple_of(pend_ref[e] - MOE_TILE, MOE_TILE)
        return pltpu.make_async_copy(zero_ref, xs_ref.at[pl.ds(start, MOE_TILE)], zsem)

    @pl.when(pl.program_id(0) == 0)
    def _():
        zero_ref[...] = jnp.zeros_like(zero_ref)

        def zstart(e, c):
            @pl.when(padded_ref[e] > 0)
            def _():
                zero_copy(e).start()
            return c

        def zwait(e, c):
            @pl.when(padded_ref[e] > 0)
            def _():
                zero_copy(e).wait()
            return c

        lax.fori_loop(0, N_EXPERTS, zstart, 0)
        lax.fori_loop(0, N_EXPERTS, zwait, 0)

    def start(t, c):
        for k in range(TOP_K):
            _row_copy(h_ref, t, xs_ref, dest_ref[k, t], sem).start()
        return c

    def wait(t, c):
        for k in range(TOP_K):
            _row_copy(h_ref, t, xs_ref, dest_ref[k, t], sem).wait()
        return c

    lax.fori_loop(0, tt, start, 0)
    lax.fori_loop(0, tt, wait, 0)


def _dispatch(h_packed, dest_t, pends, padded, n_rows):
    rows, half = h_packed.shape
    grid_spec = pltpu.PrefetchScalarGridSpec(
        num_scalar_prefetch=2,
        grid=(rows // DISP_TT,),
        in_specs=[
            pl.BlockSpec((TOP_K, DISP_TT), lambda i, *_: (0, i), memory_space=pltpu.SMEM),
            pl.BlockSpec((DISP_TT, half), lambda i, *_: (i, 0)),
        ],
        out_specs=pl.BlockSpec(memory_space=pl.ANY),
        scratch_shapes=[
            pltpu.VMEM((MOE_TILE, half), U32),
            pltpu.SemaphoreType.DMA,
            pltpu.SemaphoreType.DMA,
        ],
    )
    return pl.pallas_call(
        _dispatch_body,
        grid_spec=grid_spec,
        out_shape=jax.ShapeDtypeStruct((n_rows, half), U32),
        compiler_params=_cparams(("arbitrary",)),
        name="dispatch",
    )(pends, padded, dest_t, h_packed)


def _expert_body(be_ref, nu_ref, x_ref, wg_ref, wu_ref, wd_ref, y_ref, wg_b, wu_b, wd_b):
    i = pl.program_id(0)

    @pl.when((i == 0) | (be_ref[i] != be_ref[jnp.maximum(i - 1, 0)]))
    def _():
        wg_b[...] = wg_ref[0].astype(BF16)
        wu_b[...] = wu_ref[0].astype(BF16)
        wd_b[...] = wd_ref[0].astype(BF16)

    @pl.when(i < nu_ref[0])
    def _():
        lo, hi = _unpack_halves(x_ref[...])
        lo = lo.astype(BF16)
        hi = hi.astype(BF16)
        g = jnp.dot(lo, wg_b[:HALF, :], preferred_element_type=F32)
        g = g + jnp.dot(hi, wg_b[HALF:, :], preferred_element_type=F32)
        u = jnp.dot(lo, wu_b[:HALF, :], preferred_element_type=F32)
        u = u + jnp.dot(hi, wu_b[HALF:, :], preferred_element_type=F32)
        act = (_silu(g) * u).astype(BF16)
        y = jnp.dot(act, wd_b[...], preferred_element_type=F32)
        packed = _pack_halves(y[:, :HALF], y[:, HALF:])
        for c in range(HALF // 128):
            y_ref[pl.ds(c, MOE_TILE, stride=ROW_SUBLANES), :] = packed[:, c * 128:(c + 1) * 128]


def _experts(xs, block_expert, n_used, w_eg, w_eu, w_ed):
    n_rows, half = xs.shape
    n_blocks = n_rows // MOE_TILE
    d, de = w_eg.shape[1], w_eg.shape[2]
    row_map = lambda i, be, nu: (jnp.minimum(i, nu[0] - 1), 0)
    grid_spec = pltpu.PrefetchScalarGridSpec(
        num_scalar_prefetch=2,
        grid=(n_blocks,),
        in_specs=[
            pl.BlockSpec((MOE_TILE, half), row_map),
            pl.BlockSpec((1, d, de), lambda i, be, nu: (be[i], 0, 0)),
            pl.BlockSpec((1, d, de), lambda i, be, nu: (be[i], 0, 0)),
            pl.BlockSpec((1, de, d), lambda i, be, nu: (be[i], 0, 0)),
        ],
        out_specs=pl.BlockSpec((MOE_TILE * ROW_SUBLANES, 128), row_map),
        scratch_shapes=[
            pltpu.VMEM((d, de), BF16),
            pltpu.VMEM((d, de), BF16),
            pltpu.VMEM((de, d), BF16),
        ],
    )
    return pl.pallas_call(
        _expert_body,
        grid_spec=grid_spec,
        out_shape=jax.ShapeDtypeStruct((n_rows * ROW_SUBLANES, 128), U32),
        compiler_params=_cparams(("arbitrary",)),
        name="experts",
    )(block_expert, n_used, xs, w_eg, w_eu, w_ed)


def _sc_gather_rows(table, idx):
    n_idx = idx.shape[0]
    per_worker = n_idx // SC_WORKERS
    assert per_worker * SC_WORKERS == n_idx and per_worker % SC_CHUNK == 0
    mesh = plsc.VectorSubcoreMesh(core_axis_name="c", subcore_axis_name="s")

    @functools.partial(
        pl.kernel, mesh=mesh,
        out_type=jax.ShapeDtypeStruct((n_idx,) + table.shape[1:], table.dtype),
        scratch_types=[
            pltpu.VMEM((per_worker,), I32),
            pltpu.VMEM((SC_CHUNK,) + table.shape[1:], table.dtype),
            pltpu.SemaphoreType.DMA,
        ],
    )
    def gather(table_hbm, idx_hbm, out_hbm, idx_v, rows_v, sem):
        worker = lax.axis_index("s") * SC_CORES + lax.axis_index("c")
        base = worker * per_worker
        pltpu.sync_copy(idx_hbm.at[pl.ds(base, per_worker)], idx_v)

        @pl.loop(0, per_worker // SC_CHUNK)
        def _(j):
            off = j * SC_CHUNK
            pltpu.async_copy(table_hbm.at[idx_v.at[pl.ds(off, SC_CHUNK)]], rows_v, sem).wait()
            pltpu.sync_copy(rows_v, out_hbm.at[pl.ds(base + off, SC_CHUNK)])

    return gather(table, idx)


def _combine_body(*refs):
    yk_refs = refs[:TOP_K]
    w_ref, shr_ref, x1_ref, gt2_ref, gpost_ref, o_ref = refs[TOP_K:]
    tt = x1_ref.shape[0]
    w = w_ref[...]
    ssq = jnp.zeros((tt, 1), F32)
    for c in range(HALF // 128):
        c_lo = slice(c * 128, (c + 1) * 128)
        c_hi = slice(HALF + c * 128, HALF + (c + 1) * 128)
        y_lo = shr_ref[:, c_lo].astype(F32)
        y_hi = shr_ref[:, c_hi].astype(F32)
        for k in range(TOP_K):
            lo, hi = _unpack_halves(yk_refs[k][pl.ds(c, tt, stride=ROW_SUBLANES), :])
            y_lo = y_lo + w[:, k:k + 1] * lo
            y_hi = y_hi + w[:, k:k + 1] * hi
        ssq = ssq + jnp.sum(y_lo * y_lo, axis=-1, keepdims=True) + jnp.sum(y_hi * y_hi, axis=-1, keepdims=True)
        o_ref[:, c_lo] = y_lo
        o_ref[:, c_hi] = y_hi
    scale = lax.rsqrt(ssq / D_MODEL + EPS)
    o_ref[...] = x1_ref[...] + gt2_ref[0] * (o_ref[...] * scale * gpost_ref[...])


def _combine(yu, wts, shared, x1, gt2, g_post, seq):
    rows, d = x1.shape
    tiles = rows // COMB_TT
    tiles_per_b = seq // COMB_TT
    yk_specs = [pl.BlockSpec((COMB_TT * ROW_SUBLANES, 128), functools.partial(lambda i, k: (k * tiles + i, 0), k=k))
                for k in range(TOP_K)]
    return pl.pallas_call(
        _combine_body,
        grid=(tiles,),
        in_specs=yk_specs + [
            pl.BlockSpec((COMB_TT, TOP_K), lambda i: (i, 0)),
            pl.BlockSpec((COMB_TT, d), lambda i: (i, 0)),
            pl.BlockSpec((COMB_TT, d), lambda i: (i, 0)),
            pl.BlockSpec((1, 1, d), lambda i: (i // tiles_per_b, 0, 0)),
            _resident((1, d)),
        ],
        out_specs=pl.BlockSpec((COMB_TT, d), lambda i: (i, 0)),
        out_shape=jax.ShapeDtypeStruct((rows, d), F32),
        compiler_params=_cparams(("arbitrary",)),
        name="combine",
    )(*([yu] * TOP_K), wts, shared, x1, gt2, g_post)


def kernel(x, c, ctx, c_ctx, w_mod, b_mod, norm_mix_pre, norm_mix_post, norm_ffn_pre, norm_ffn_post, w_in, w_a2_fwd, b_a_fwd, w_a2_bwd, b_a_bwd, gla_norm, w_pool, pool_scale, w_out, w_router, router_bias, w_exp_gate, w_exp_up, w_exp_down, w_sh_gate, w_sh_up, w_sh_down):
    batch, seq, d = x.shape
    n_ctx = ctx.shape[1]
    assert w_mod.shape[0] == 1 and d == D_MODEL
    assert seq % SUPER == 0 and n_ctx % SUPER == 0 and seq % PROJ_TM == 0 and (batch * n_ctx) % PROJ_TM == 0
    rows = batch * seq

    mod_rows = 16
    c_all = jnp.concatenate([c, c_ctx[None, :], jnp.zeros((mod_rows - batch - 1, d), F32)], axis=0)
    mod_all = _modulation(c_all, w_mod[0], b_mod[0][None, :])
    sh1, sc1, gt1, sh2, sc2, gt2 = [m.reshape(batch, 1, d) for m in jnp.split(mod_all[:batch], 6, axis=-1)]
    csh1 = mod_all[batch, 0:d].reshape(1, 1, d)
    csc1 = mod_all[batch, d:2 * d].reshape(1, 1, d)

    kw, gw = GLA_KEY_WIDTH, GLA_WIDTH
    a0 = 2 * kw + 2 * gw
    w_in0 = w_in[0]
    w_main = jnp.concatenate([w_in0[:, :a0], w_in0[:, a0 + 2 * GLA_RANK:]], axis=1).astype(BF16)
    w_a = jnp.pad(w_in0[:, a0:a0 + 2 * GLA_RANK], ((0, 0), (0, 128 - 2 * GLA_RANK))).astype(BF16)
    w_ctx = w_main[:, kw:2 * kw + gw]
    w2f = jnp.pad(w_a2_fwd[0], ((0, 128 - GLA_RANK), (0, 0))).astype(BF16)
    w2b = jnp.pad(w_a2_bwd[0], ((GLA_RANK, 128 - 2 * GLA_RANK), (0, 0))).astype(BF16)
    g_mix_pre = norm_mix_pre[0][None, :]
    w_r = jnp.pad(w_router[0], ((0, 0), (0, 128 - N_EXPERTS)))
    wr_hi = w_r.astype(BF16)
    wr_lo = (w_r - wr_hi.astype(F32)).astype(BF16)

    u_ctx, a_ctx = _in_projection(ctx.reshape(batch * n_ctx, d), g_mix_pre, csc1, csh1, w_ctx, w_a,
                                  batch * n_ctx)
    u_lat, a_lat = _in_projection(x.reshape(rows, d), g_mix_pre, sc1, sh1, w_main, w_a, seq)

    y_gla = _gla(u_lat, a_lat, u_ctx, a_ctx, w2f, b_a_fwd[0][None, :], w2b, b_a_bwd[0][None, :],
                 gla_norm[0][None, :], batch, seq, n_ctx)
    y_pool = _pool_mixer(u_lat, _col_window_matrices(), w_pool[0].astype(BF16), pool_scale[0][None, :],
                         batch, seq)

    x1, h_packed, shared, eidx_t, pos_t, wts_t, counts = _mix_and_route(
        y_gla, y_pool, x.reshape(rows, d), gt1, sc2, sh2, norm_mix_post[0][None, :],
        norm_ffn_pre[0][None, :], w_out[0].astype(BF16), wr_hi, wr_lo, router_bias[0][:, None],
        w_sh_gate[0].astype(BF16), w_sh_up[0].astype(BF16), w_sh_down[0].astype(BF16), seq)

    counts = counts[:, 0]
    padded = (counts + MOE_TILE - 1) // MOE_TILE * MOE_TILE
    pends = jnp.cumsum(padded)
    pstarts = pends - padded
    dest_t = _route_offsets(eidx_t, pos_t, pstarts.astype(I32))
    n_blocks = rows * TOP_K // MOE_TILE + N_EXPERTS
    n_used = (pends[-1] // MOE_TILE).astype(I32)
    blk = jnp.minimum(jnp.arange(n_blocks, dtype=I32), n_used - 1)
    block_expert = jnp.sum((blk * MOE_TILE)[:, None] >= pends[None, :], axis=1).astype(I32)
    block_expert = jnp.minimum(block_expert, N_EXPERTS - 1)

    xs = _dispatch(h_packed, dest_t, pends.astype(I32), padded.astype(I32), n_blocks * MOE_TILE)
    ys = _experts(xs, block_expert, n_used.reshape(1), w_exp_gate[0], w_exp_up[0], w_exp_down[0])
    yu = _sc_gather_rows(ys.reshape(-1, ROW_SUBLANES, 128), dest_t.reshape(-1))
    out = _combine(yu.reshape(-1, 128), wts_t.T, shared, x1, gt2, norm_ffn_post[0][None, :], seq)
    return out.reshape(batch, seq, d)
```

```python
import functools

import numpy as np
import jax
import jax.numpy as jnp
from jax import lax
from jax.experimental import pallas as pl
from jax.experimental.pallas import tpu as pltpu
from jax.experimental.pallas import tpu_sc as plsc

F32 = jnp.float32
BF16 = jnp.bfloat16
I32 = jnp.int32
U32 = jnp.uint32
HIGHEST = lax.Precision.HIGHEST

D_MODEL = 2048
GRID_W = 64
GLA_HEADS = 4
GLA_DK = 128
GLA_DV = 256
GLA_KEY_WIDTH = GLA_HEADS * GLA_DK
GLA_WIDTH = GLA_HEADS * GLA_DV
GLA_RANK = 16
GLA_TAU = 16.0
GLA_CHUNK = 64
POOL_WIDTH = 1024
POOL_WINDOWS = (2, 4, 8, 16)
POOL_GROUP = 256
N_EXPERTS = 64
TOP_K = 8
N_GROUPS = 8
GROUP_SIZE = N_EXPERTS // N_GROUPS
TOPK_GROUPS = 4
D_EXPERT = 512
D_SHARED = 512
ROUTED_SCALE = 2.5
EPS = 1e-6

HALF = D_MODEL // 2
SUPER = 4 * GLA_CHUNK
GLA_HPS = 4
POOL_PAD = 8 * GRID_W
VMEM_LIMIT = 56 * 1024 * 1024

MOD_TN = 1024
PROJ_TM = 512
PROJ_TN = 512
MIX_TM = 512
MOE_TILE = 512
COMB_TT = 256
ROW_SUBLANES = 8
SC_CORES = 2
SC_WORKERS = 32
SC_CHUNK = 32
OFFS_TN = 2048


def _cparams(sem):
    return pltpu.CompilerParams(dimension_semantics=sem, vmem_limit_bytes=VMEM_LIMIT)


def _resident(shape):
    nd = len(shape)
    return pl.BlockSpec(shape, lambda *_: (0,) * nd, pipeline_mode=pl.Buffered(1))


def _silu(v):
    return v * jax.nn.sigmoid(v)


def _pack_halves(lo, hi):
    lo_b = lax.bitcast_convert_type(lo.astype(BF16).astype(F32), U32)
    hi_b = lax.bitcast_convert_type(hi.astype(BF16).astype(F32), U32)
    return (hi_b & jnp.uint32(0xFFFF0000)) | (lo_b >> 16)


def _unpack_halves(p):
    lo = lax.bitcast_convert_type(p << 16, F32)
    hi = lax.bitcast_convert_type(p & jnp.uint32(0xFFFF0000), F32)
    return lo, hi


def _split_bf16(w):
    hi = w.astype(BF16)
    return jnp.stack([hi, (w - hi.astype(F32)).astype(BF16)])


def _store_row_tiles(ref, packed):
    n = packed.shape[0]
    for c in range(HALF // 128):
        ref[pl.ds(c, n, stride=ROW_SUBLANES), :] = packed[:, c * 128:(c + 1) * 128]


def _load_row_tiles(ref):
    n = ref.shape[0] // ROW_SUBLANES
    return jnp.concatenate([ref[pl.ds(c, n, stride=ROW_SUBLANES), :] for c in range(HALF // 128)], axis=1)


def _mod_body(c_ref, w_ref, b_ref, o_ref):
    s = _silu(c_ref[...])
    o_ref[...] = jnp.dot(s, w_ref[...], preferred_element_type=F32, precision=HIGHEST) + b_ref[...]


def _modulation(c_all, w_mod, b_mod):
    rows, d = c_all.shape
    n = w_mod.shape[1]
    return pl.pallas_call(
        _mod_body,
        grid=(n // MOD_TN,),
        in_specs=[
            pl.BlockSpec((rows, d), lambda j: (0, 0)),
            pl.BlockSpec((d, MOD_TN), lambda j: (0, j)),
            pl.BlockSpec((1, MOD_TN), lambda j: (0, j)),
        ],
        out_specs=pl.BlockSpec((rows, MOD_TN), lambda j: (0, j)),
        out_shape=jax.ShapeDtypeStruct((rows, n), F32),
        compiler_params=_cparams(("arbitrary",)),
        name="modulation",
    )(c_all, w_mod, b_mod)


def _rms_scale(x):
    return lax.rsqrt(jnp.mean(x * x, axis=-1, keepdims=True) + EPS)


def _inproj_body(x_ref, g_ref, sc_ref, sh_ref, w_ref, wa_ref, o_ref, a_ref, *, n_main):
    x = x_ref[...]
    h = x * _rms_scale(x) * g_ref[...]
    h = h * (1.0 + sc_ref[0]) + sh_ref[0]
    hb = h.astype(BF16)
    for n in range(n_main // PROJ_TN):
        cols = slice(n * PROJ_TN, (n + 1) * PROJ_TN)
        o_ref[:, cols] = jnp.dot(hb, w_ref[:, cols], preferred_element_type=F32).astype(BF16)
    a_ref[...] = jnp.dot(hb, wa_ref[...], preferred_element_type=F32)


def _in_projection(x2d, gain, sc, sh, w_main, w_a, rows_per_mod):
    rows, d = x2d.shape
    n_main = w_main.shape[1]
    tiles_per_mod = rows_per_mod // PROJ_TM
    mod_map = lambda i: (i // tiles_per_mod, 0, 0)
    return pl.pallas_call(
        functools.partial(_inproj_body, n_main=n_main),
        grid=(rows // PROJ_TM,),
        in_specs=[
            pl.BlockSpec((PROJ_TM, d), lambda i: (i, 0)),
            _resident((1, d)),
            pl.BlockSpec((1, 1, d), mod_map),
            pl.BlockSpec((1, 1, d), mod_map),
            _resident((d, n_main)),
            _resident((d, 128)),
        ],
        out_specs=[
            pl.BlockSpec((PROJ_TM, n_main), lambda i: (i, 0)),
            pl.BlockSpec((PROJ_TM, 128), lambda i: (i, 0)),
        ],
        out_shape=[
            jax.ShapeDtypeStruct((rows, n_main), BF16),
            jax.ShapeDtypeStruct((rows, 128), F32),
        ],
        compiler_params=_cparams(("arbitrary",)),
        name="in_projection",
    )(x2d, gain, sc, sh, w_main, w_a)


def _log_sigmoid(z):
    return jnp.minimum(z, 0.0) - jnp.log1p(jnp.exp(-jnp.abs(z)))


def _gla_super(q, k, v, a, w2, ba, tri, mask, st_ref, reverse):
    nc = SUPER // GLA_CHUNK
    a_hi = a.astype(BF16)
    a_lo = (a - a_hi.astype(F32)).astype(BF16)
    z = jnp.dot(a_hi, w2[0], preferred_element_type=F32) + jnp.dot(a_lo, w2[0], preferred_element_type=F32)
    z = z + jnp.dot(a_hi, w2[1], preferred_element_type=F32) + ba
    g = _log_sigmoid(z) * (1.0 / GLA_TAU)
    g_hi = g.astype(BF16)
    g_lo = (g - g_hi.astype(F32)).astype(BF16)
    G = jnp.dot(tri, g_hi, preferred_element_type=F32) + jnp.dot(tri, g_lo, preferred_element_type=F32)
    G = G.reshape(nc, GLA_CHUNK, GLA_DK)
    end_row = 0 if reverse else GLA_CHUNK - 1
    mid_row = GLA_CHUNK - 1 - GLA_CHUNK // 2 if reverse else GLA_CHUNK // 2
    g_end = G[:, end_row:end_row + 1, :]
    g_mid = G[:, mid_row:mid_row + 1, :]
    k4 = k.astype(F32).reshape(nc, GLA_CHUNK, GLA_DK)
    kd = (k4 * jnp.exp(g_end - G)).astype(BF16)
    dec = jnp.exp(g_end)
    o = None
    if q is not None:
        q4 = q.astype(F32).reshape(nc, GLA_CHUNK, GLA_DK) * (GLA_DK ** -0.5)
        qg = (q4 * jnp.exp(G - g_mid)).reshape(SUPER, GLA_DK).astype(BF16)
        kg = (k4 * jnp.exp(g_mid - G)).reshape(SUPER, GLA_DK).astype(BF16)
        qe = (q4 * jnp.exp(G)).astype(BF16)
        att = lax.dot_general(qg, kg, (((1,), (1,)), ((), ())), preferred_element_type=F32)
        att = jnp.where(mask, att, 0.0).astype(BF16)
        o = jnp.dot(att, v, preferred_element_type=F32)
    outs = [None] * nc
    order = range(nc - 1, -1, -1) if reverse else range(nc)
    for c in order:
        rows = slice(c * GLA_CHUNK, (c + 1) * GLA_CHUNK)
        st = st_ref[...]
        if q is not None:
            inter = lax.dot_general(qe[c], st.astype(BF16), (((1,), (1,)), ((), ())),
                                    preferred_element_type=F32)
            outs[c] = o[rows] + inter
        upd = lax.dot_general(v[rows], kd[c], (((0,), (0,)), ((), ())), preferred_element_type=F32)
        st_ref[...] = st * dec[c] + upd
    if q is None:
        return None
    return jnp.concatenate(outs, axis=0)


def _gla_body(q_ref, k_ref, v_ref, r_ref, a_ref, kc_ref, vc_ref, ac_ref,
              w2f_ref, baf_ref, w2b_ref, bab_ref, gn_ref, y_ref, o_acc, st, *, n_ctx):
    n_sup = q_ref.shape[0] // SUPER
    row = lax.broadcasted_iota(I32, (SUPER, SUPER), 0)
    col = lax.broadcasted_iota(I32, (SUPER, SUPER), 1)
    same_chunk = (row >> 6) == (col >> 6)
    mask_f = same_chunk & (col <= row)
    mask_b = same_chunk & (col >= row)
    tri_f = jnp.where(mask_f, 1.0, 0.0).astype(BF16)
    tri_b = jnp.where(mask_b, 1.0, 0.0).astype(BF16)
    heads = range(GLA_HPS)
    kcol = [slice(h * GLA_DK, (h + 1) * GLA_DK) for h in heads]
    vcol = [slice(h * GLA_DV, (h + 1) * GLA_DV) for h in heads]
    gate_f = [(w2f_ref[:, :, kcol[h]], baf_ref[:, kcol[h]]) for h in heads]
    gate_b = [(w2b_ref[:, :, kcol[h]], bab_ref[:, kcol[h]]) for h in heads]

    st[...] = jnp.zeros_like(st)
    n_csup = n_ctx // SUPER
    for s in range(n_csup):
        rf = slice(s * SUPER, (s + 1) * SUPER)
        rb = slice((n_csup - 1 - s) * SUPER, (n_csup - s) * SUPER)
        for h in heads:
            _gla_super(None, kc_ref[rf, kcol[h]], vc_ref[rf, vcol[h]], ac_ref[rf, :], *gate_f[h],
                       tri_f, mask_f, st.at[0, h], False)
            _gla_super(None, kc_ref[rb, kcol[h]], vc_ref[rb, vcol[h]], ac_ref[rb, :], *gate_b[h],
                       tri_b, mask_b, st.at[1, h], True)

    o_acc[...] = jnp.zeros_like(o_acc)

    def step(i, carry):
        rf = pl.ds(pl.multiple_of(i * SUPER, SUPER), SUPER)
        rb = pl.ds(pl.multiple_of((n_sup - 1 - i) * SUPER, SUPER), SUPER)
        a_f = a_ref[rf, :]
        a_b = a_ref[rb, :]
        for h in heads:
            of = _gla_super(q_ref[rf, kcol[h]], k_ref[rf, kcol[h]], v_ref[rf, vcol[h]], a_f, *gate_f[h],
                            tri_f, mask_f, st.at[0, h], False)
            o_acc[rf, vcol[h]] += of
            ob = _gla_super(q_ref[rb, kcol[h]], k_ref[rb, kcol[h]], v_ref[rb, vcol[h]], a_b, *gate_b[h],
                            tri_b, mask_b, st.at[1, h], True)
            o_acc[rb, vcol[h]] += ob
        return carry

    lax.fori_loop(0, n_sup, step, 0)

    for h in heads:
        o = o_acc[:, vcol[h]]
        o = o * _rms_scale(o) * gn_ref[:, vcol[h]]
        y_ref[:, vcol[h]] = (o * _silu(r_ref[:, vcol[h]].astype(F32))).astype(BF16)


def _gla(u_lat, a_lat, u_ctx, a_ctx, w2f, baf, w2b, bab, gla_norm, batch, seq, n_ctx):
    groups = GLA_HEADS // GLA_HPS
    kw, vw = GLA_HPS * GLA_DK, GLA_HPS * GLA_DV
    kb = GLA_KEY_WIDTH // kw
    vb = 2 * GLA_KEY_WIDTH // vw
    rb = vb + groups
    cvb = GLA_KEY_WIDTH // vw
    return pl.pallas_call(
        functools.partial(_gla_body, n_ctx=n_ctx),
        grid=(batch, groups),
        in_specs=[
            pl.BlockSpec((seq, kw), lambda b, h: (b, h)),
            pl.BlockSpec((seq, kw), lambda b, h: (b, kb + h)),
            pl.BlockSpec((seq, vw), lambda b, h: (b, vb + h)),
            pl.BlockSpec((seq, vw), lambda b, h: (b, rb + h)),
            pl.BlockSpec((seq, 128), lambda b, h: (b, 0)),
            pl.BlockSpec((n_ctx, kw), lambda b, h: (b, h)),
            pl.BlockSpec((n_ctx, vw), lambda b, h: (b, cvb + h)),
            pl.BlockSpec((n_ctx, 128), lambda b, h: (b, 0)),
            pl.BlockSpec((2, 128, kw), lambda b, h: (0, 0, h)),
            pl.BlockSpec((1, kw), lambda b, h: (0, h)),
            pl.BlockSpec((2, 128, kw), lambda b, h: (0, 0, h)),
            pl.BlockSpec((1, kw), lambda b, h: (0, h)),
            pl.BlockSpec((1, vw), lambda b, h: (0, h)),
        ],
        out_specs=pl.BlockSpec((seq, vw), lambda b, h: (b, h)),
        out_shape=jax.ShapeDtypeStruct((batch * seq, GLA_WIDTH), BF16),
        scratch_shapes=[
            pltpu.VMEM((seq, vw), F32),
            pltpu.VMEM((2, GLA_HPS, GLA_DV, GLA_DK), F32),
        ],
        compiler_params=_cparams(("arbitrary", "arbitrary")),
        name="gla",
    )(u_lat, u_lat, u_lat, u_lat, a_lat, u_ctx, u_ctx, a_ctx, w2f, baf, w2b, bab, gla_norm)


def _col_window_matrices():
    t = np.arange(SUPER)
    r, c = t // GRID_W, t % GRID_W
    mats = []
    for w in POOL_WINDOWS:
        lo = np.maximum(c - w // 2, 0)[:, None]
        hi = np.minimum(c + w // 2, GRID_W)[:, None]
        m = (r[:, None] == r[None, :]) & (c[None, :] >= lo) & (c[None, :] < hi)
        mats.append(m.astype(np.float32))
    return jnp.asarray(np.stack(mats), dtype=BF16)


def _pool_body(p_ref, cw_ref, wp_ref, ps_ref, y_ref, pad_ref):
    seq = p_ref.shape[0]
    n_rows = seq // GRID_W
    zeros = jnp.zeros((POOL_PAD, POOL_GROUP), F32)
    pad_ref[0:POOL_PAD, :] = zeros
    pad_ref[POOL_PAD + seq:POOL_PAD + seq + POOL_PAD, :] = zeros
    t = lax.broadcasted_iota(I32, (seq, POOL_GROUP), 0)
    r = t >> 6
    c = t & (GRID_W - 1)
    for gi, w in enumerate(POOL_WINDOWS):
        cols = slice(gi * POOL_GROUP, (gi + 1) * POOL_GROUP)
        cw = cw_ref[gi]
        for j in range(seq // SUPER):
            rows = slice(j * SUPER, (j + 1) * SUPER)
            pad_ref[POOL_PAD + j * SUPER:POOL_PAD + (j + 1) * SUPER, :] = jnp.dot(
                cw, p_ref[rows, cols], preferred_element_type=F32)
        total = None
        for d in range(-(w // 2), w // 2):
            start = POOL_PAD + d * GRID_W
            part = pad_ref[start:start + seq, :]
            total = part if total is None else total + part
        cnt_r = jnp.minimum(r + w // 2, n_rows) - jnp.maximum(r - w // 2, 0)
        cnt_c = jnp.minimum(c + w // 2, GRID_W) - jnp.maximum(c - w // 2, 0)
        mean = total / (cnt_r * cnt_c).astype(F32)
        diff = (mean - p_ref[:, cols].astype(F32)).astype(BF16)
        y = jnp.dot(diff, wp_ref[gi], preferred_element_type=F32) * ps_ref[:, cols]
        y_ref[:, cols] = y.astype(BF16)


def _pool_mixer(u_lat, col_mats, w_pool, pool_scale, batch, seq):
    pb = (u_lat.shape[1] - POOL_WIDTH) // POOL_WIDTH
    ng = len(POOL_WINDOWS)
    return pl.pallas_call(
        _pool_body,
        grid=(batch,),
        in_specs=[
            pl.BlockSpec((seq, POOL_WIDTH), lambda b: (b, pb)),
            _resident((ng, SUPER, SUPER)),
            _resident((ng, POOL_GROUP, POOL_GROUP)),
            _resident((1, POOL_WIDTH)),
        ],
        out_specs=pl.BlockSpec((seq, POOL_WIDTH), lambda b: (b, 0)),
        out_shape=jax.ShapeDtypeStruct((batch * seq, POOL_WIDTH), BF16),
        scratch_shapes=[pltpu.VMEM((seq + 2 * POOL_PAD, POOL_GROUP), F32)],
        compiler_params=_cparams(("arbitrary",)),
        name="pool_mixer",
    )(u_lat, col_mats, w_pool, pool_scale)


def _first_index(hit, iota, size, axis):
    return jnp.min(jnp.where(hit, iota, size), axis=axis, keepdims=True)


def _mix_body(yg_ref, yp_ref, x_ref, gt1_ref, sc2_ref, sh2_ref, gpost_ref, gpre_ref, wout_ref,
              wrh_ref, wrl_ref, rb_ref, wsg_ref, wsu_ref, wsd_ref, upper_ref,
              x1_ref, hp_ref, shr_ref, eidx_ref, pos_ref, wts_ref, cnt_ref, run_ref):
    tm = x_ref.shape[0]
    neg_inf = jnp.float32(-jnp.inf)

    @pl.when(pl.program_id(0) == 0)
    def _():
        run_ref[...] = jnp.zeros_like(run_ref)

    y = jnp.dot(yg_ref[...], wout_ref[0:GLA_WIDTH, :], preferred_element_type=F32)
    y = y + jnp.dot(yp_ref[...], wout_ref[GLA_WIDTH:, :], preferred_element_type=F32)
    x1 = x_ref[...] + gt1_ref[0] * (y * _rms_scale(y) * gpost_ref[...])
    x1_ref[...] = x1
    h = x1 * _rms_scale(x1) * gpre_ref[...]
    h = h * (1.0 + sc2_ref[0]) + sh2_ref[0]
    _store_row_tiles(hp_ref, _pack_halves(h[:, :HALF], h[:, HALF:]))
    hb = h.astype(BF16)

    sg = jnp.dot(hb, wsg_ref[...], preferred_element_type=F32)
    su = jnp.dot(hb, wsu_ref[...], preferred_element_type=F32)
    act = (_silu(sg) * su).astype(BF16)
    shr_ref[...] = jnp.dot(act, wsd_ref[...], preferred_element_type=F32).astype(BF16)

    h_lo = (h - hb.astype(F32)).astype(BF16)
    lt = jnp.dot(hb, wrh_ref[...], preferred_element_type=F32)
    lt = lt + jnp.dot(hb, wrl_ref[...], preferred_element_type=F32)
    lt = lt + jnp.dot(h_lo, wrh_ref[...], preferred_element_type=F32)
    logits = lt.T[0:N_EXPERTS, :]
    scores = jax.nn.sigmoid(logits)
    sel = scores + rb_ref[...]
    shape3 = (N_GROUPS, GROUP_SIZE, tm)
    sel3 = sel.reshape(shape3)
    i_in = lax.broadcasted_iota(I32, shape3, 1).astype(F32)
    m1 = jnp.max(sel3, axis=1, keepdims=True)
    f1 = _first_index(sel3 == m1, i_in, float(GROUP_SIZE), 1)
    m2 = jnp.max(jnp.where(i_in == f1, neg_inf, sel3), axis=1, keepdims=True)
    grp = jnp.broadcast_to(m1 + m2, shape3).reshape(N_EXPERTS, tm)
    i_e = lax.broadcasted_iota(I32, (N_EXPERTS, tm), 0)
    i_grp = (i_e >> 3).astype(F32)
    i_e = i_e.astype(F32)
    allowed = jnp.zeros((N_EXPERTS, tm), F32)
    for _ in range(TOPK_GROUPS):
        m = jnp.max(grp, axis=0, keepdims=True)
        pick = i_grp == _first_index(grp == m, i_grp, float(N_GROUPS), 0)
        allowed = jnp.where(pick, 1.0, allowed)
        grp = jnp.where(pick, neg_inf, grp)
    cand = jnp.where(allowed > 0.0, sel, neg_inf)
    onehot = jnp.zeros((N_EXPERTS, tm), F32)
    picks, wts = [], []
    for k in range(TOP_K):
        m = jnp.max(cand, axis=0, keepdims=True)
        f = _first_index(cand == m, i_e, float(N_EXPERTS), 0)
        pick = i_e == f
        picks.append(pick)
        eidx_ref[k:k + 1, :] = f.astype(I32)
        wts.append(jnp.sum(jnp.where(pick, scores, 0.0), axis=0, keepdims=True))
        onehot = jnp.where(pick, 1.0, onehot)
        cand = jnp.where(pick, neg_inf, cand)
    w_sum = wts[0]
    for k in range(1, TOP_K):
        w_sum = w_sum + wts[k]
    for k in range(TOP_K):
        wts_ref[k:k + 1, :] = wts[k] / w_sum * ROUTED_SCALE

    before = jnp.dot(onehot.astype(BF16), upper_ref[...], preferred_element_type=F32)
    before = before + run_ref[:, 0:1]
    for k in range(TOP_K):
        pos_ref[k:k + 1, :] = jnp.sum(jnp.where(picks[k], before, 0.0), axis=0, keepdims=True).astype(I32)
    run_ref[...] = run_ref[...] + jnp.sum(onehot, axis=1, keepdims=True)
    cnt_ref[...] = run_ref[...].astype(I32)


def _mix_and_route(y_gla, y_pool, x2d, gt1, sc2, sh2, g_post, g_pre, w_out, wr_hi, wr_lo, router_bias,
                   w_sg, w_su, w_sd, seq):
    rows, d = x2d.shape
    tiles_per_b = seq // MIX_TM
    bmap = lambda i: (i // tiles_per_b, 0, 0)
    rmap = lambda i: (i, 0)
    tmap = lambda i: (0, i)
    upper = jnp.asarray(np.triu(np.ones((MIX_TM, MIX_TM), np.float32), 1), dtype=BF16)
    return pl.pallas_call(
        _mix_body,
        grid=(rows // MIX_TM,),
        in_specs=[
            pl.BlockSpec((MIX_TM, GLA_WIDTH), rmap),
            pl.BlockSpec((MIX_TM, POOL_WIDTH), rmap),
            pl.BlockSpec((MIX_TM, d), rmap),
            pl.BlockSpec((1, 1, d), bmap),
            pl.BlockSpec((1, 1, d), bmap),
            pl.BlockSpec((1, 1, d), bmap),
            _resident((1, d)),
            _resident((1, d)),
            _resident((d, d)),
            _resident((d, 128)),
            _resident((d, 128)),
            _resident((N_EXPERTS, 1)),
            _resident((d, D_SHARED)),
            _resident((d, D_SHARED)),
            _resident((D_SHARED, d)),
            _resident((MIX_TM, MIX_TM)),
        ],
        out_specs=[
            pl.BlockSpec((MIX_TM, d), rmap),
            pl.BlockSpec((MIX_TM * ROW_SUBLANES, 128), rmap),
            pl.BlockSpec((MIX_TM, d), rmap),
            pl.BlockSpec((TOP_K, MIX_TM), tmap),
            pl.BlockSpec((TOP_K, MIX_TM), tmap),
            pl.BlockSpec((TOP_K, MIX_TM), tmap),
            pl.BlockSpec((N_EXPERTS, 128), lambda i: (0, 0)),
        ],
        out_shape=[
            jax.ShapeDtypeStruct((rows, d), F32),
            jax.ShapeDtypeStruct((rows * ROW_SUBLANES, 128), U32),
            jax.ShapeDtypeStruct((rows, d), BF16),
            jax.ShapeDtypeStruct((TOP_K, rows), I32),
            jax.ShapeDtypeStruct((TOP_K, rows), I32),
            jax.ShapeDtypeStruct((TOP_K, rows), F32),
            jax.ShapeDtypeStruct((N_EXPERTS, 128), I32),
        ],
        scratch_shapes=[pltpu.VMEM((N_EXPERTS, 128), F32)],
        compiler_params=_cparams(("arbitrary",)),
        name="mix_and_route",
    )(y_gla, y_pool, x2d, gt1, sc2, sh2, g_post, g_pre, w_out, wr_hi, wr_lo, router_bias,
      w_sg, w_su, w_sd, upper)


def _offsets_body(ps_ref, e_ref, p_ref, d_ref):
    e = e_ref[...]
    d = p_ref[...]
    for x in range(N_EXPERTS):
        d = d + jnp.where(e == x, ps_ref[x], 0)
    d_ref[...] = d


def _route_offsets(eidx_t, pos_t, pstarts):
    k, rows = eidx_t.shape
    spec = pl.BlockSpec((k, OFFS_TN), lambda i, ps: (0, i))
    grid_spec = pltpu.PrefetchScalarGridSpec(
        num_scalar_prefetch=1, grid=(rows // OFFS_TN,), in_specs=[spec, spec], out_specs=spec)
    return pl.pallas_call(
        _offsets_body,
        grid_spec=grid_spec,
        out_shape=jax.ShapeDtypeStruct((k, rows), I32),
        compiler_params=_cparams(("arbitrary",)),
        name="route_offsets",
    )(pstarts, eidx_t, pos_t)


def _expert_body(be_ref, nu_ref, x_ref, wg_ref, wu_ref, wd_ref, y_ref, wg_b, wu_b, wd_b):
    i = pl.program_id(0)

    @pl.when((i == 0) | (be_ref[i] != be_ref[jnp.maximum(i - 1, 0)]))
    def _():
        wg_b[...] = wg_ref[0].astype(BF16)
        wu_b[...] = wu_ref[0].astype(BF16)
        wd_b[...] = wd_ref[0].astype(BF16)

    @pl.when(i < nu_ref[0])
    def _():
        lo, hi = _unpack_halves(_load_row_tiles(x_ref))
        lo = lo.astype(BF16)
        hi = hi.astype(BF16)
        g = jnp.dot(lo, wg_b[:HALF, :], preferred_element_type=F32)
        g = g + jnp.dot(hi, wg_b[HALF:, :], preferred_element_type=F32)
        u = jnp.dot(lo, wu_b[:HALF, :], preferred_element_type=F32)
        u = u + jnp.dot(hi, wu_b[HALF:, :], preferred_element_type=F32)
        act = (_silu(g) * u).astype(BF16)
        y = jnp.dot(act, wd_b[...], preferred_element_type=F32)
        _store_row_tiles(y_ref, _pack_halves(y[:, :HALF], y[:, HALF:]))


def _experts(xs, block_expert, n_used, w_eg, w_eu, w_ed):
    n_rows = xs.shape[0] // ROW_SUBLANES
    n_blocks = n_rows // MOE_TILE
    d, de = w_eg.shape[1], w_eg.shape[2]
    row_map = lambda i, be, nu: (jnp.minimum(i, nu[0] - 1), 0)
    grid_spec = pltpu.PrefetchScalarGridSpec(
        num_scalar_prefetch=2,
        grid=(n_blocks,),
        in_specs=[
            pl.BlockSpec((MOE_TILE * ROW_SUBLANES, 128), row_map),
            pl.BlockSpec((1, d, de), lambda i, be, nu: (be[i], 0, 0)),
            pl.BlockSpec((1, d, de), lambda i, be, nu: (be[i], 0, 0)),
            pl.BlockSpec((1, de, d), lambda i, be, nu: (be[i], 0, 0)),
        ],
        out_specs=pl.BlockSpec((MOE_TILE * ROW_SUBLANES, 128), row_map),
        scratch_shapes=[
            pltpu.VMEM((d, de), BF16),
            pltpu.VMEM((d, de), BF16),
            pltpu.VMEM((de, d), BF16),
        ],
    )
    return pl.pallas_call(
        _expert_body,
        grid_spec=grid_spec,
        out_shape=jax.ShapeDtypeStruct((n_rows * ROW_SUBLANES, 128), U32),
        compiler_params=_cparams(("arbitrary",)),
        name="experts",
    )(block_expert, n_used, xs, w_eg, w_eu, w_ed)


def _sc_mesh():
    return plsc.VectorSubcoreMesh(core_axis_name="c", subcore_axis_name="s")


def _sc_worker():
    return lax.axis_index("s") * SC_CORES + lax.axis_index("c")


def _sc_gather_rows(table, idx):
    n_idx = idx.shape[0]
    per_worker = n_idx // SC_WORKERS
    n_chunks = per_worker // SC_CHUNK
    assert per_worker * SC_WORKERS == n_idx and n_chunks * SC_CHUNK == per_worker and n_chunks % 2 == 0
    row_shape = table.shape[1:]

    @functools.partial(
        pl.kernel, mesh=_sc_mesh(),
        out_type=jax.ShapeDtypeStruct((n_idx,) + row_shape, table.dtype),
        scratch_types=[
            pltpu.VMEM((per_worker,), I32),
            pltpu.VMEM((SC_CHUNK,) + row_shape, table.dtype),
            pltpu.VMEM((SC_CHUNK,) + row_shape, table.dtype),
        ] + [pltpu.SemaphoreType.DMA] * 4,
    )
    def gather(table_hbm, idx_hbm, out_hbm, idx_v, buf0, buf1, g0, g1, w0, w1):
        bufs, gsem, wsem = (buf0, buf1), (g0, g1), (w0, w1)
        base = _sc_worker() * per_worker
        pltpu.sync_copy(idx_hbm.at[pl.ds(base, per_worker)], idx_v)

        def fetch(j, b):
            return pltpu.make_async_copy(table_hbm.at[idx_v.at[pl.ds(j * SC_CHUNK, SC_CHUNK)]], bufs[b], gsem[b])

        def flush(j, b):
            return pltpu.make_async_copy(bufs[b], out_hbm.at[pl.ds(base + j * SC_CHUNK, SC_CHUNK)], wsem[b])

        fetch(0, 0).start()
        fetch(0, 0).wait()
        fetch(1, 1).start()
        flush(0, 0).start()

        @pl.loop(1, n_chunks - 1, step=2)
        def _(j):
            for off in range(2):
                jj, b = j + off, (1 + off) % 2
                fetch(jj, b).wait()
                flush(jj - 1, 1 - b).wait()
                fetch(jj + 1, 1 - b).start()
                flush(jj, b).start()

        fetch(n_chunks - 1, 1).wait()
        flush(n_chunks - 1, 1).start()
        flush(n_chunks - 2, 0).wait()
        flush(n_chunks - 1, 1).wait()

    return gather(table, idx)


def _sc_scatter_rows(rows, dest_t, n_out):
    n_rows = rows.shape[0]
    per_worker = n_rows // SC_WORKERS
    n_chunks = per_worker // SC_CHUNK
    assert per_worker * SC_WORKERS == n_rows and n_chunks * SC_CHUNK == per_worker
    row_shape = rows.shape[1:]
    idx_w = dest_t.reshape(TOP_K, SC_WORKERS, n_chunks, SC_CHUNK).transpose(1, 2, 0, 3)
    idx_w = idx_w.reshape(SC_WORKERS, n_chunks * TOP_K, SC_CHUNK)

    @functools.partial(
        pl.kernel, mesh=_sc_mesh(),
        out_type=jax.ShapeDtypeStruct((n_out,) + row_shape, rows.dtype),
        scratch_types=[
            pltpu.VMEM((n_chunks * TOP_K, SC_CHUNK), I32),
            pltpu.VMEM((SC_CHUNK,) + row_shape, rows.dtype),
            pltpu.VMEM((SC_CHUNK,) + row_shape, rows.dtype),
        ] + [pltpu.SemaphoreType.DMA] * 4,
    )
    def scatter(rows_hbm, idx_hbm, out_hbm, idx_v, buf0, buf1, r0, r1, s0, s1):
        bufs, rsem, ssem = (buf0, buf1), (r0, r1), (s0, s1)
        worker = _sc_worker()
        base = worker * per_worker
        pltpu.sync_copy(idx_hbm.at[worker], idx_v)

        def fetch(j, b):
            return pltpu.make_async_copy(rows_hbm.at[pl.ds(base + j * SC_CHUNK, SC_CHUNK)], bufs[b], rsem[b])

        def send(j, k, b):
            return pltpu.make_async_copy(bufs[b], out_hbm.at[idx_v.at[j * TOP_K + k]], ssem[b])

        fetch(0, 0).start()
        for j in range(n_chunks):
            b = j % 2
            fetch(j, b).wait()
            if j + 1 < n_chunks:
                if j >= 1:
                    for k in range(TOP_K):
                        send(j - 1, k, 1 - b).wait()
                fetch(j + 1, 1 - b).start()
            for k in range(TOP_K):
                send(j, k, b).start()
        for j in range(max(n_chunks - 2, 0), n_chunks):
            for k in range(TOP_K):
                send(j, k, j % 2).wait()

    return scatter(rows, idx_w)


def _combine_body(*refs):
    yk_refs = refs[:TOP_K]
    w_ref, shr_ref, x1_ref, gt2_ref, gpost_ref, o_ref = refs[TOP_K:]
    tt = x1_ref.shape[0]
    w = w_ref[...]
    ssq = jnp.zeros((tt, 1), F32)
    for c in range(HALF // 128):
        c_lo = slice(c * 128, (c + 1) * 128)
        c_hi = slice(HALF + c * 128, HALF + (c + 1) * 128)
        y_lo = shr_ref[:, c_lo].astype(F32)
        y_hi = shr_ref[:, c_hi].astype(F32)
        for k in range(TOP_K):
            lo, hi = _unpack_halves(yk_refs[k][pl.ds(c, tt, stride=ROW_SUBLANES), :])
            y_lo = y_lo + w[:, k:k + 1] * lo
            y_hi = y_hi + w[:, k:k + 1] * hi
        ssq = ssq + jnp.sum(y_lo * y_lo, axis=-1, keepdims=True) + jnp.sum(y_hi * y_hi, axis=-1, keepdims=True)
        o_ref[:, c_lo] = y_lo
        o_ref[:, c_hi] = y_hi
    scale = lax.rsqrt(ssq / D_MODEL + EPS)
    o_ref[...] = x1_ref[...] + gt2_ref[0] * (o_ref[...] * scale * gpost_ref[...])


def _combine(yu, wts, shared, x1, gt2, g_post, seq):
    rows, d = x1.shape
    tiles = rows // COMB_TT
    tiles_per_b = seq // COMB_TT
    yk_specs = [pl.BlockSpec((COMB_TT * ROW_SUBLANES, 128), functools.partial(lambda i, k: (k * tiles + i, 0), k=k))
                for k in range(TOP_K)]
    return pl.pallas_call(
        _combine_body,
        grid=(tiles,),
        in_specs=yk_specs + [
            pl.BlockSpec((COMB_TT, TOP_K), lambda i: (i, 0)),
            pl.BlockSpec((COMB_TT, d), lambda i: (i, 0)),
            pl.BlockSpec((COMB_TT, d), lambda i: (i, 0)),
            pl.BlockSpec((1, 1, d), lambda i: (i // tiles_per_b, 0, 0)),
            _resident((1, d)),
        ],
        out_specs=pl.BlockSpec((COMB_TT, d), lambda i: (i, 0)),
        out_shape=jax.ShapeDtypeStruct((rows, d), F32),
        compiler_params=_cparams(("arbitrary",)),
        name="combine",
    )(*([yu] * TOP_K), wts, shared, x1, gt2, g_post)


def kernel(x, c, ctx, c_ctx, w_mod, b_mod, norm_mix_pre, norm_mix_post, norm_ffn_pre, norm_ffn_post, w_in, w_a2_fwd, b_a_fwd, w_a2_bwd, b_a_bwd, gla_norm, w_pool, pool_scale, w_out, w_router, router_bias, w_exp_gate, w_exp_up, w_exp_down, w_sh_gate, w_sh_up, w_sh_down):
    batch, seq, d = x.shape
    n_ctx = ctx.shape[1]
    assert w_mod.shape[0] == 1 and d == D_MODEL
    assert seq % SUPER == 0 and n_ctx % SUPER == 0 and seq % PROJ_TM == 0 and (batch * n_ctx) % PROJ_TM == 0
    rows = batch * seq

    mod_rows = 16
    c_all = jnp.concatenate([c, c_ctx[None, :], jnp.zeros((mod_rows - batch - 1, d), F32)], axis=0)
    mod_all = _modulation(c_all, w_mod[0], b_mod[0][None, :])
    sh1, sc1, gt1, sh2, sc2, gt2 = [m.reshape(batch, 1, d) for m in jnp.split(mod_all[:batch], 6, axis=-1)]
    csh1 = mod_all[batch, 0:d].reshape(1, 1, d)
    csc1 = mod_all[batch, d:2 * d].reshape(1, 1, d)

    kw, gw = GLA_KEY_WIDTH, GLA_WIDTH
    a0 = 2 * kw + 2 * gw
    w_in0 = w_in[0]
    w_main = jnp.concatenate([w_in0[:, :a0], w_in0[:, a0 + 2 * GLA_RANK:]], axis=1).astype(BF16)
    w_a = jnp.pad(w_in0[:, a0:a0 + 2 * GLA_RANK], ((0, 0), (0, 128 - 2 * GLA_RANK))).astype(BF16)
    w_ctx = w_main[:, kw:2 * kw + gw]
    w2f = _split_bf16(jnp.pad(w_a2_fwd[0], ((0, 128 - GLA_RANK), (0, 0))))
    w2b = _split_bf16(jnp.pad(w_a2_bwd[0], ((GLA_RANK, 128 - 2 * GLA_RANK), (0, 0))))
    g_mix_pre = norm_mix_pre[0][None, :]
    w_r = jnp.pad(w_router[0], ((0, 0), (0, 128 - N_EXPERTS)))
    wr_hi = w_r.astype(BF16)
    wr_lo = (w_r - wr_hi.astype(F32)).astype(BF16)

    u_ctx, a_ctx = _in_projection(ctx.reshape(batch * n_ctx, d), g_mix_pre, csc1, csh1, w_ctx, w_a,
                                  batch * n_ctx)
    u_lat, a_lat = _in_projection(x.reshape(rows, d), g_mix_pre, sc1, sh1, w_main, w_a, seq)

    y_gla = _gla(u_lat, a_lat, u_ctx, a_ctx, w2f, b_a_fwd[0][None, :], w2b, b_a_bwd[0][None, :],
                 gla_norm[0][None, :], batch, seq, n_ctx)
    y_pool = _pool_mixer(u_lat, _col_window_matrices(), w_pool[0].astype(BF16), pool_scale[0][None, :],
                         batch, seq)

    x1, h_packed, shared, eidx_t, pos_t, wts_t, counts = _mix_and_route(
        y_gla, y_pool, x.reshape(rows, d), gt1, sc2, sh2, norm_mix_post[0][None, :],
        norm_ffn_pre[0][None, :], w_out[0].astype(BF16), wr_hi, wr_lo, router_bias[0][:, None],
        w_sh_gate[0].astype(BF16), w_sh_up[0].astype(BF16), w_sh_down[0].astype(BF16), seq)

    counts = counts[:, 0]
    padded = (counts + MOE_TILE - 1) // MOE_TILE * MOE_TILE
    pends = jnp.cumsum(padded)
    pstarts = pends - padded
    dest_t = _route_offsets(eidx_t, pos_t, pstarts.astype(I32))
    n_blocks = rows * TOP_K // MOE_TILE + N_EXPERTS
    n_used = (pends[-1] // MOE_TILE).astype(I32)
    blk = jnp.minimum(jnp.arange(n_blocks, dtype=I32), n_used - 1)
    block_expert = jnp.sum((blk * MOE_TILE)[:, None] >= pends[None, :], axis=1).astype(I32)
    block_expert = jnp.minimum(block_expert, N_EXPERTS - 1)

    xs = _sc_scatter_rows(h_packed.reshape(-1, ROW_SUBLANES, 128), dest_t, n_blocks * MOE_TILE)
    xs = xs.reshape(-1, 128)
    ys = _experts(xs, block_expert, n_used.reshape(1), w_exp_gate[0], w_exp_up[0], w_exp_down[0])
    yu = _sc_gather_rows(ys.reshape(-1, ROW_SUBLANES, 128), dest_t.reshape(-1))
    out = _combine(yu.reshape(-1, 128), wts_t.T, shared, x1, gt2, norm_ffn_post[0][None, :], seq)
    return out.reshape(batch, seq, d)
```

```python
import functools

import numpy as np
import jax
import jax.numpy as jnp
from jax import lax
from jax.experimental import pallas as pl
from jax.experimental.pallas import tpu as pltpu
from jax.experimental.pallas import tpu_sc as plsc

F32 = jnp.float32
BF16 = jnp.bfloat16
I32 = jnp.int32
U32 = jnp.uint32
HIGHEST = lax.Precision.HIGHEST

D_MODEL = 2048
GRID_W = 64
GLA_HEADS = 4
GLA_DK = 128
GLA_DV = 256
GLA_KEY_WIDTH = GLA_HEADS * GLA_DK
GLA_WIDTH = GLA_HEADS * GLA_DV
GLA_RANK = 16
GLA_TAU = 16.0
GLA_CHUNK = 64
POOL_WIDTH = 1024
POOL_WINDOWS = (2, 4, 8, 16)
POOL_GROUP = 256
N_EXPERTS = 64
TOP_K = 8
N_GROUPS = 8
GROUP_SIZE = N_EXPERTS // N_GROUPS
TOPK_GROUPS = 4
D_EXPERT = 512
D_SHARED = 512
ROUTED_SCALE = 2.5
EPS = 1e-6

HALF = D_MODEL // 2
SUPER = 4 * GLA_CHUNK
GLA_HPS = 4
POOL_PAD = 8 * GRID_W
VMEM_LIMIT = 56 * 1024 * 1024

MOD_TN = 1024
PROJ_TM = 512
PROJ_TN = 512
MIX_TM = 512
MOE_TILE = 512
SHARED_TM = 1024
COMB_TT = 256
ROW_SUBLANES = 8
SC_CORES = 2
SC_WORKERS = 32
SC_CHUNK = 32
OFFS_TN = 2048


def _cparams(sem):
    return pltpu.CompilerParams(dimension_semantics=sem, vmem_limit_bytes=VMEM_LIMIT)


def _resident(shape):
    nd = len(shape)
    return pl.BlockSpec(shape, lambda *_: (0,) * nd, pipeline_mode=pl.Buffered(1))


def _silu(v):
    return v * jax.nn.sigmoid(v)


def _pack_halves(lo, hi):
    lo_b = lax.bitcast_convert_type(lo.astype(BF16).astype(F32), U32)
    hi_b = lax.bitcast_convert_type(hi.astype(BF16).astype(F32), U32)
    return (hi_b & jnp.uint32(0xFFFF0000)) | (lo_b >> 16)


def _unpack_halves(p):
    lo = lax.bitcast_convert_type(p << 16, F32)
    hi = lax.bitcast_convert_type(p & jnp.uint32(0xFFFF0000), F32)
    return lo, hi


def _bf16_terms(x):
    hi = lax.bitcast_convert_type(lax.bitcast_convert_type(x, U32) & jnp.uint32(0xFFFF0000), F32)
    return hi.astype(BF16), (x - hi).astype(BF16)


def _store_row_tiles(ref, packed):
    n = packed.shape[0]
    for c in range(HALF // 128):
        ref[pl.ds(c, n, stride=ROW_SUBLANES), :] = packed[:, c * 128:(c + 1) * 128]


def _load_row_tiles(ref):
    n = ref.shape[0] // ROW_SUBLANES
    return jnp.concatenate([ref[pl.ds(c, n, stride=ROW_SUBLANES), :] for c in range(HALF // 128)], axis=1)


def _mod_body(c_ref, w_ref, b_ref, o_ref):
    s = _silu(c_ref[...])
    o_ref[...] = jnp.dot(s, w_ref[...], preferred_element_type=F32, precision=HIGHEST) + b_ref[...]


def _modulation(c_all, w_mod, b_mod):
    rows, d = c_all.shape
    n = w_mod.shape[1]
    return pl.pallas_call(
        _mod_body,
        grid=(n // MOD_TN,),
        in_specs=[
            pl.BlockSpec((rows, d), lambda j: (0, 0)),
            pl.BlockSpec((d, MOD_TN), lambda j: (0, j)),
            pl.BlockSpec((1, MOD_TN), lambda j: (0, j)),
        ],
        out_specs=pl.BlockSpec((rows, MOD_TN), lambda j: (0, j)),
        out_shape=jax.ShapeDtypeStruct((rows, n), F32),
        compiler_params=_cparams(("arbitrary",)),
        name="modulation",
    )(c_all, w_mod, b_mod)


def _rms_scale(x):
    return lax.rsqrt(jnp.mean(x * x, axis=-1, keepdims=True) + EPS)


def _inproj_body(x_ref, g_ref, sc_ref, sh_ref, w_ref, wa_ref, o_ref, a_ref, *, n_main):
    x = x_ref[...]
    h = x * _rms_scale(x) * g_ref[...]
    h = h * (1.0 + sc_ref[0]) + sh_ref[0]
    hb = h.astype(BF16)
    for n in range(n_main // PROJ_TN):
        cols = slice(n * PROJ_TN, (n + 1) * PROJ_TN)
        o_ref[:, cols] = jnp.dot(hb, w_ref[:, cols], preferred_element_type=F32).astype(BF16)
    a_ref[...] = jnp.dot(hb, wa_ref[...], preferred_element_type=F32)


def _in_projection(x2d, gain, sc, sh, w_main, w_a, rows_per_mod):
    rows, d = x2d.shape
    n_main = w_main.shape[1]
    tiles_per_mod = rows_per_mod // PROJ_TM
    mod_map = lambda i: (i // tiles_per_mod, 0, 0)
    return pl.pallas_call(
        functools.partial(_inproj_body, n_main=n_main),
        grid=(rows // PROJ_TM,),
        in_specs=[
            pl.BlockSpec((PROJ_TM, d), lambda i: (i, 0)),
            _resident((1, d)),
            pl.BlockSpec((1, 1, d), mod_map),
            pl.BlockSpec((1, 1, d), mod_map),
            _resident((d, n_main)),
            _resident((d, 128)),
        ],
        out_specs=[
            pl.BlockSpec((PROJ_TM, n_main), lambda i: (i, 0)),
            pl.BlockSpec((PROJ_TM, 128), lambda i: (i, 0)),
        ],
        out_shape=[
            jax.ShapeDtypeStruct((rows, n_main), BF16),
            jax.ShapeDtypeStruct((rows, 128), F32),
        ],
        compiler_params=_cparams(("arbitrary",)),
        name="in_projection",
    )(x2d, gain, sc, sh, w_main, w_a)


def _log_sigmoid(z):
    return jnp.minimum(z, 0.0) - jnp.log1p(jnp.exp(-jnp.abs(z)))


def _gla_super(q, k, v, a, w2, ba, tri, mask, st_ref, reverse):
    nc = SUPER // GLA_CHUNK
    z = jnp.dot(a.astype(BF16), w2, preferred_element_type=F32) + ba
    g = _log_sigmoid(z) * (1.0 / GLA_TAU)
    g_hi, g_lo = _bf16_terms(g)
    G = jnp.dot(tri, g_hi, preferred_element_type=F32) + jnp.dot(tri, g_lo, preferred_element_type=F32)
    G = G.reshape(nc, GLA_CHUNK, GLA_DK)
    end_row = 0 if reverse else GLA_CHUNK - 1
    mid_row = GLA_CHUNK - 1 - GLA_CHUNK // 2 if reverse else GLA_CHUNK // 2
    g_end = G[:, end_row:end_row + 1, :]
    g_mid = G[:, mid_row:mid_row + 1, :]
    k4 = k.astype(F32).reshape(nc, GLA_CHUNK, GLA_DK)
    kd = (k4 * jnp.exp(g_end - G)).astype(BF16)
    dec = jnp.exp(g_end)
    o = None
    if q is not None:
        q4 = q.astype(F32).reshape(nc, GLA_CHUNK, GLA_DK) * (GLA_DK ** -0.5)
        qg = (q4 * jnp.exp(G - g_mid)).reshape(SUPER, GLA_DK).astype(BF16)
        kg = (k4 * jnp.exp(g_mid - G)).reshape(SUPER, GLA_DK).astype(BF16)
        qe = (q4 * jnp.exp(G)).astype(BF16)
        att = lax.dot_general(qg, kg, (((1,), (1,)), ((), ())), preferred_element_type=F32)
        att = jnp.where(mask, att, 0.0).astype(BF16)
        o = jnp.dot(att, v, preferred_element_type=F32)
    outs = [None] * nc
    order = range(nc - 1, -1, -1) if reverse else range(nc)
    for c in order:
        rows = slice(c * GLA_CHUNK, (c + 1) * GLA_CHUNK)
        st = st_ref[...]
        if q is not None:
            inter = lax.dot_general(qe[c], st.astype(BF16), (((1,), (1,)), ((), ())),
                                    preferred_element_type=F32)
            outs[c] = o[rows] + inter
        upd = lax.dot_general(v[rows], kd[c], (((0,), (0,)), ((), ())), preferred_element_type=F32)
        st_ref[...] = st * dec[c] + upd
    if q is None:
        return None
    return jnp.concatenate(outs, axis=0)


def _gla_body(q_ref, k_ref, v_ref, r_ref, a_ref, kc_ref, vc_ref, ac_ref,
              w2f_ref, baf_ref, w2b_ref, bab_ref, gn_ref, y_ref, o_acc, st, *, n_ctx):
    n_sup = q_ref.shape[0] // SUPER
    row = lax.broadcasted_iota(I32, (SUPER, SUPER), 0)
    col = lax.broadcasted_iota(I32, (SUPER, SUPER), 1)
    same_chunk = (row >> 6) == (col >> 6)
    mask_f = same_chunk & (col <= row)
    mask_b = same_chunk & (col >= row)
    tri_f = jnp.where(mask_f, 1.0, 0.0).astype(BF16)
    tri_b = jnp.where(mask_b, 1.0, 0.0).astype(BF16)
    heads = range(GLA_HPS)
    kcol = [slice(h * GLA_DK, (h + 1) * GLA_DK) for h in heads]
    vcol = [slice(h * GLA_DV, (h + 1) * GLA_DV) for h in heads]
    gate_f = [(w2f_ref[:, kcol[h]], baf_ref[:, kcol[h]]) for h in heads]
    gate_b = [(w2b_ref[:, kcol[h]], bab_ref[:, kcol[h]]) for h in heads]

    st[...] = jnp.zeros_like(st)
    n_csup = n_ctx // SUPER
    for s in range(n_csup):
        rf = slice(s * SUPER, (s + 1) * SUPER)
        rb = slice((n_csup - 1 - s) * SUPER, (n_csup - s) * SUPER)
        for h in heads:
            _gla_super(None, kc_ref[rf, kcol[h]], vc_ref[rf, vcol[h]], ac_ref[rf, :], *gate_f[h],
                       tri_f, mask_f, st.at[0, h], False)
            _gla_super(None, kc_ref[rb, kcol[h]], vc_ref[rb, vcol[h]], ac_ref[rb, :], *gate_b[h],
                       tri_b, mask_b, st.at[1, h], True)

    o_acc[...] = jnp.zeros_like(o_acc)

    def step(i, carry):
        rf = pl.ds(pl.multiple_of(i * SUPER, SUPER), SUPER)
        rb = pl.ds(pl.multiple_of((n_sup - 1 - i) * SUPER, SUPER), SUPER)
        a_f = a_ref[rf, :]
        a_b = a_ref[rb, :]
        for h in heads:
            of = _gla_super(q_ref[rf, kcol[h]], k_ref[rf, kcol[h]], v_ref[rf, vcol[h]], a_f, *gate_f[h],
                            tri_f, mask_f, st.at[0, h], False)
            o_acc[rf, vcol[h]] += of
            ob = _gla_super(q_ref[rb, kcol[h]], k_ref[rb, kcol[h]], v_ref[rb, vcol[h]], a_b, *gate_b[h],
                            tri_b, mask_b, st.at[1, h], True)
            o_acc[rb, vcol[h]] += ob
        return carry

    lax.fori_loop(0, n_sup, step, 0)

    for h in heads:
        o = o_acc[:, vcol[h]]
        o = o * _rms_scale(o) * gn_ref[:, vcol[h]]
        y_ref[:, vcol[h]] = (o * _silu(r_ref[:, vcol[h]].astype(F32))).astype(BF16)


def _gla(u_lat, a_lat, u_ctx, a_ctx, w2f, baf, w2b, bab, gla_norm, batch, seq, n_ctx):
    groups = GLA_HEADS // GLA_HPS
    kw, vw = GLA_HPS * GLA_DK, GLA_HPS * GLA_DV
    kb = GLA_KEY_WIDTH // kw
    vb = 2 * GLA_KEY_WIDTH // vw
    rb = vb + groups
    cvb = GLA_KEY_WIDTH // vw
    return pl.pallas_call(
        functools.partial(_gla_body, n_ctx=n_ctx),
        grid=(batch, groups),
        in_specs=[
            pl.BlockSpec((seq, kw), lambda b, h: (b, h)),
            pl.BlockSpec((seq, kw), lambda b, h: (b, kb + h)),
            pl.BlockSpec((seq, vw), lambda b, h: (b, vb + h)),
            pl.BlockSpec((seq, vw), lambda b, h: (b, rb + h)),
            pl.BlockSpec((seq, 128), lambda b, h: (b, 0)),
            pl.BlockSpec((n_ctx, kw), lambda b, h: (b, h)),
            pl.BlockSpec((n_ctx, vw), lambda b, h: (b, cvb + h)),
            pl.BlockSpec((n_ctx, 128), lambda b, h: (b, 0)),
            pl.BlockSpec((128, kw), lambda b, h: (0, h)),
            pl.BlockSpec((1, kw), lambda b, h: (0, h)),
            pl.BlockSpec((128, kw), lambda b, h: (0, h)),
            pl.BlockSpec((1, kw), lambda b, h: (0, h)),
            pl.BlockSpec((1, vw), lambda b, h: (0, h)),
        ],
        out_specs=pl.BlockSpec((seq, vw), lambda b, h: (b, h)),
        out_shape=jax.ShapeDtypeStruct((batch * seq, GLA_WIDTH), BF16),
        scratch_shapes=[
            pltpu.VMEM((seq, vw), F32),
            pltpu.VMEM((2, GLA_HPS, GLA_DV, GLA_DK), F32),
        ],
        compiler_params=_cparams(("arbitrary", "arbitrary")),
        name="gla",
    )(u_lat, u_lat, u_lat, u_lat, a_lat, u_ctx, u_ctx, a_ctx, w2f, baf, w2b, bab, gla_norm)


def _col_window_matrices():
    t = np.arange(SUPER)
    r, c = t // GRID_W, t % GRID_W
    mats = []
    for w in POOL_WINDOWS:
        lo = np.maximum(c - w // 2, 0)[:, None]
        hi = np.minimum(c + w // 2, GRID_W)[:, None]
        m = (r[:, None] == r[None, :]) & (c[None, :] >= lo) & (c[None, :] < hi)
        mats.append(m.astype(np.float32))
    return jnp.asarray(np.stack(mats), dtype=BF16)


def _pool_body(p_ref, cw_ref, wp_ref, ps_ref, y_ref, pad_ref):
    seq = p_ref.shape[0]
    n_rows = seq // GRID_W
    zeros = jnp.zeros((POOL_PAD, POOL_GROUP), F32)
    pad_ref[0:POOL_PAD, :] = zeros
    pad_ref[POOL_PAD + seq:POOL_PAD + seq + POOL_PAD, :] = zeros
    t = lax.broadcasted_iota(I32, (seq, POOL_GROUP), 0)
    r = t >> 6
    c = t & (GRID_W - 1)
    for gi, w in enumerate(POOL_WINDOWS):
        cols = slice(gi * POOL_GROUP, (gi + 1) * POOL_GROUP)
        cw = cw_ref[gi]
        for j in range(seq // SUPER):
            rows = slice(j * SUPER, (j + 1) * SUPER)
            pad_ref[POOL_PAD + j * SUPER:POOL_PAD + (j + 1) * SUPER, :] = jnp.dot(
                cw, p_ref[rows, cols], preferred_element_type=F32)
        total = None
        for d in range(-(w // 2), w // 2):
            start = POOL_PAD + d * GRID_W
            part = pad_ref[start:start + seq, :]
            total = part if total is None else total + part
        cnt_r = jnp.minimum(r + w // 2, n_rows) - jnp.maximum(r - w // 2, 0)
        cnt_c = jnp.minimum(c + w // 2, GRID_W) - jnp.maximum(c - w // 2, 0)
        mean = total / (cnt_r * cnt_c).astype(F32)
        diff = (mean - p_ref[:, cols].astype(F32)).astype(BF16)
        y = jnp.dot(diff, wp_ref[gi], preferred_element_type=F32) * ps_ref[:, cols]
        y_ref[:, cols] = y.astype(BF16)


def _pool_mixer(u_lat, col_mats, w_pool, pool_scale, batch, seq):
    pb = (u_lat.shape[1] - POOL_WIDTH) // POOL_WIDTH
    ng = len(POOL_WINDOWS)
    return pl.pallas_call(
        _pool_body,
        grid=(batch,),
        in_specs=[
            pl.BlockSpec((seq, POOL_WIDTH), lambda b: (b, pb)),
            _resident((ng, SUPER, SUPER)),
            _resident((ng, POOL_GROUP, POOL_GROUP)),
            _resident((1, POOL_WIDTH)),
        ],
        out_specs=pl.BlockSpec((seq, POOL_WIDTH), lambda b: (b, 0)),
        out_shape=jax.ShapeDtypeStruct((batch * seq, POOL_WIDTH), BF16),
        scratch_shapes=[pltpu.VMEM((seq + 2 * POOL_PAD, POOL_GROUP), F32)],
        compiler_params=_cparams(("arbitrary",)),
        name="pool_mixer",
    )(u_lat, col_mats, w_pool, pool_scale)


def _first_index(hit, iota, size, axis):
    return jnp.min(jnp.where(hit, iota, size), axis=axis, keepdims=True)


def _mix_body(yg_ref, yp_ref, x_ref, gt1_ref, sc2_ref, sh2_ref, gpost_ref, gpre_ref, wout_ref,
              wrh_ref, wrl_ref, rb_ref, upper_ref,
              x1_ref, hp_ref, eidx_ref, pos_ref, wts_ref, cnt_ref, run_ref):
    tm = x_ref.shape[0]
    neg_inf = jnp.float32(-jnp.inf)

    @pl.when(pl.program_id(0) == 0)
    def _():
        run_ref[...] = jnp.zeros_like(run_ref)

    y = jnp.dot(yg_ref[...], wout_ref[0:GLA_WIDTH, :], preferred_element_type=F32)
    y = y + jnp.dot(yp_ref[...], wout_ref[GLA_WIDTH:, :], preferred_element_type=F32)
    x1 = x_ref[...] + gt1_ref[0] * (y * _rms_scale(y) * gpost_ref[...])
    x1_ref[...] = x1
    h = x1 * _rms_scale(x1) * gpre_ref[...]
    h = h * (1.0 + sc2_ref[0]) + sh2_ref[0]
    _store_row_tiles(hp_ref, _pack_halves(h[:, :HALF], h[:, HALF:]))

    h_hi, h_lo = _bf16_terms(h)
    lt = jnp.dot(h_hi, wrh_ref[...], preferred_element_type=F32)
    lt = lt + jnp.dot(h_hi, wrl_ref[...], preferred_element_type=F32)
    lt = lt + jnp.dot(h_lo, wrh_ref[...], preferred_element_type=F32)
    logits = lt.T[0:N_EXPERTS, :]
    scores = jax.nn.sigmoid(logits)
    sel = scores + rb_ref[...]
    shape3 = (N_GROUPS, GROUP_SIZE, tm)
    sel3 = sel.reshape(shape3)
    i_in = lax.broadcasted_iota(I32, shape3, 1).astype(F32)
    m1 = jnp.max(sel3, axis=1, keepdims=True)
    f1 = _first_index(sel3 == m1, i_in, float(GROUP_SIZE), 1)
    m2 = jnp.max(jnp.where(i_in == f1, neg_inf, sel3), axis=1, keepdims=True)
    grp = jnp.broadcast_to(m1 + m2, shape3).reshape(N_EXPERTS, tm)
    i_e = lax.broadcasted_iota(I32, (N_EXPERTS, tm), 0)
    i_grp = (i_e >> 3).astype(F32)
    i_e = i_e.astype(F32)
    allowed = jnp.zeros((N_EXPERTS, tm), F32)
    for _ in range(TOPK_GROUPS):
        m = jnp.max(grp, axis=0, keepdims=True)
        pick = i_grp == _first_index(grp == m, i_grp, float(N_GROUPS), 0)
        allowed = jnp.where(pick, 1.0, allowed)
        grp = jnp.where(pick, neg_inf, grp)
    cand = jnp.where(allowed > 0.0, sel, neg_inf)
    onehot = jnp.zeros((N_EXPERTS, tm), F32)
    picks, wts = [], []
    for k in range(TOP_K):
        m = jnp.max(cand, axis=0, keepdims=True)
        f = _first_index(cand == m, i_e, float(N_EXPERTS), 0)
        pick = i_e == f
        picks.append(pick)
        eidx_ref[k:k + 1, :] = f.astype(I32)
        wts.append(jnp.sum(jnp.where(pick, scores, 0.0), axis=0, keepdims=True))
        onehot = jnp.where(pick, 1.0, onehot)
        cand = jnp.where(pick, neg_inf, cand)
    w_sum = wts[0]
    for k in range(1, TOP_K):
        w_sum = w_sum + wts[k]
    for k in range(TOP_K):
        wts_ref[k:k + 1, :] = wts[k] / w_sum * ROUTED_SCALE

    before = jnp.dot(onehot.astype(BF16), upper_ref[...], preferred_element_type=F32)
    before = before + run_ref[:, 0:1]
    for k in range(TOP_K):
        pos_ref[k:k + 1, :] = jnp.sum(jnp.where(picks[k], before, 0.0), axis=0, keepdims=True).astype(I32)
    run_ref[...] = run_ref[...] + jnp.sum(onehot, axis=1, keepdims=True)
    cnt_ref[...] = run_ref[...].astype(I32)


def _mix_and_route(y_gla, y_pool, x2d, gt1, sc2, sh2, g_post, g_pre, w_out, wr_hi, wr_lo, router_bias, seq):
    rows, d = x2d.shape
    tiles_per_b = seq // MIX_TM
    bmap = lambda i: (i // tiles_per_b, 0, 0)
    rmap = lambda i: (i, 0)
    tmap = lambda i: (0, i)
    upper = jnp.asarray(np.triu(np.ones((MIX_TM, MIX_TM), np.float32), 1), dtype=BF16)
    return pl.pallas_call(
        _mix_body,
        grid=(rows // MIX_TM,),
        in_specs=[
            pl.BlockSpec((MIX_TM, GLA_WIDTH), rmap),
            pl.BlockSpec((MIX_TM, POOL_WIDTH), rmap),
            pl.BlockSpec((MIX_TM, d), rmap),
            pl.BlockSpec((1, 1, d), bmap),
            pl.BlockSpec((1, 1, d), bmap),
            pl.BlockSpec((1, 1, d), bmap),
            _resident((1, d)),
            _resident((1, d)),
            _resident((d, d)),
            _resident((d, 128)),
            _resident((d, 128)),
            _resident((N_EXPERTS, 1)),
            _resident((MIX_TM, MIX_TM)),
        ],
        out_specs=[
            pl.BlockSpec((MIX_TM, d), rmap),
            pl.BlockSpec((MIX_TM * ROW_SUBLANES, 128), rmap),
            pl.BlockSpec((TOP_K, MIX_TM), tmap),
            pl.BlockSpec((TOP_K, MIX_TM), tmap),
            pl.BlockSpec((TOP_K, MIX_TM), tmap),
            pl.BlockSpec((N_EXPERTS, 128), lambda i: (0, 0)),
        ],
        out_shape=[
            jax.ShapeDtypeStruct((rows, d), F32),
            jax.ShapeDtypeStruct((rows * ROW_SUBLANES, 128), U32),
            jax.ShapeDtypeStruct((TOP_K, rows), I32),
            jax.ShapeDtypeStruct((TOP_K, rows), I32),
            jax.ShapeDtypeStruct((TOP_K, rows), F32),
            jax.ShapeDtypeStruct((N_EXPERTS, 128), I32),
        ],
        scratch_shapes=[pltpu.VMEM((N_EXPERTS, 128), F32)],
        compiler_params=_cparams(("arbitrary",)),
        name="mix_and_route",
    )(y_gla, y_pool, x2d, gt1, sc2, sh2, g_post, g_pre, w_out, wr_hi, wr_lo, router_bias, upper)


def _swiglu(lo, hi, wg_ref, wu_ref, wd_ref):
    g = jnp.dot(lo, wg_ref[:HALF, :], preferred_element_type=F32)
    g = g + jnp.dot(hi, wg_ref[HALF:, :], preferred_element_type=F32)
    u = jnp.dot(lo, wu_ref[:HALF, :], preferred_element_type=F32)
    u = u + jnp.dot(hi, wu_ref[HALF:, :], preferred_element_type=F32)
    act = (_silu(g) * u).astype(BF16)
    return jnp.dot(act, wd_ref[...], preferred_element_type=F32)


def _shared_body(hp_ref, wg_ref, wu_ref, wd_ref, o_ref):
    lo, hi = _unpack_halves(_load_row_tiles(hp_ref))
    o_ref[...] = _swiglu(lo.astype(BF16), hi.astype(BF16), wg_ref, wu_ref, wd_ref).astype(BF16)


def _shared_expert(h_packed, w_sg, w_su, w_sd):
    d, ds = w_sg.shape
    rows = h_packed.shape[0] // ROW_SUBLANES
    return pl.pallas_call(
        _shared_body,
        grid=(rows // SHARED_TM,),
        in_specs=[
            pl.BlockSpec((SHARED_TM * ROW_SUBLANES, 128), lambda i: (i, 0)),
            _resident((d, ds)),
            _resident((d, ds)),
            _resident((ds, d)),
        ],
        out_specs=pl.BlockSpec((SHARED_TM, d), lambda i: (i, 0)),
        out_shape=jax.ShapeDtypeStruct((rows, d), BF16),
        compiler_params=_cparams(("arbitrary",)),
        name="shared_expert",
    )(h_packed, w_sg, w_su, w_sd)


def _offsets_body(ps_ref, e_ref, p_ref, d_ref):
    e = e_ref[...]
    d = p_ref[...]
    for x in range(N_EXPERTS):
        d = d + jnp.where(e == x, ps_ref[x], 0)
    d_ref[...] = d


def _route_offsets(eidx_t, pos_t, pstarts):
    k, rows = eidx_t.shape
    spec = pl.BlockSpec((k, OFFS_TN), lambda i, ps: (0, i))
    grid_spec = pltpu.PrefetchScalarGridSpec(
        num_scalar_prefetch=1, grid=(rows // OFFS_TN,), in_specs=[spec, spec], out_specs=spec)
    return pl.pallas_call(
        _offsets_body,
        grid_spec=grid_spec,
        out_shape=jax.ShapeDtypeStruct((k, rows), I32),
        compiler_params=_cparams(("arbitrary",)),
        name="route_offsets",
    )(pstarts, eidx_t, pos_t)


def _expert_body(be_ref, nu_ref, valid_ref, x_ref, wg_ref, wu_ref, wd_ref, y_ref, wg_b, wu_b, wd_b):
    i = pl.program_id(0)
    valid = valid_ref[i]

    @pl.when((i == 0) | (be_ref[i] != be_ref[jnp.maximum(i - 1, 0)]))
    def _():
        wg_b[...] = wg_ref[0].astype(BF16)
        wu_b[...] = wu_ref[0].astype(BF16)
        wd_b[...] = wd_ref[0].astype(BF16)

    def compute(n_rows):
        tiles = pl.ds(0, n_rows * ROW_SUBLANES)
        lo, hi = _unpack_halves(_load_row_tiles(x_ref.at[tiles]))
        y = _swiglu(lo.astype(BF16), hi.astype(BF16), wg_b, wu_b, wd_b)
        _store_row_tiles(y_ref.at[tiles], _pack_halves(y[:, :HALF], y[:, HALF:]))

    @pl.when(valid > MOE_TILE // 2)
    def _():
        compute(MOE_TILE)

    @pl.when((valid > 0) & (valid <= MOE_TILE // 2))
    def _():
        compute(MOE_TILE // 2)


def _experts(xs, block_expert, n_used, valid, w_eg, w_eu, w_ed):
    n_rows = xs.shape[0] // ROW_SUBLANES
    n_blocks = n_rows // MOE_TILE
    d, de = w_eg.shape[1], w_eg.shape[2]
    row_map = lambda i, be, nu, va: (jnp.minimum(i, nu[0] - 1), 0)
    grid_spec = pltpu.PrefetchScalarGridSpec(
        num_scalar_prefetch=3,
        grid=(n_blocks,),
        in_specs=[
            pl.BlockSpec((MOE_TILE * ROW_SUBLANES, 128), row_map),
            pl.BlockSpec((1, d, de), lambda i, be, nu, va: (be[i], 0, 0)),
            pl.BlockSpec((1, d, de), lambda i, be, nu, va: (be[i], 0, 0)),
            pl.BlockSpec((1, de, d), lambda i, be, nu, va: (be[i], 0, 0)),
        ],
        out_specs=pl.BlockSpec((MOE_TILE * ROW_SUBLANES, 128), row_map),
        scratch_shapes=[
            pltpu.VMEM((d, de), BF16),
            pltpu.VMEM((d, de), BF16),
            pltpu.VMEM((de, d), BF16),
        ],
    )
    return pl.pallas_call(
        _expert_body,
        grid_spec=grid_spec,
        out_shape=jax.ShapeDtypeStruct((n_rows * ROW_SUBLANES, 128), U32),
        compiler_params=_cparams(("arbitrary",)),
        name="experts",
    )(block_expert, n_used, valid, xs, w_eg, w_eu, w_ed)


def _sc_mesh():
    return plsc.VectorSubcoreMesh(core_axis_name="c", subcore_axis_name="s")


def _sc_worker():
    return lax.axis_index("s") * SC_CORES + lax.axis_index("c")


def _sc_gather_rows(table, idx):
    n_idx = idx.shape[0]
    per_worker = n_idx // SC_WORKERS
    n_chunks = per_worker // SC_CHUNK
    assert per_worker * SC_WORKERS == n_idx and n_chunks * SC_CHUNK == per_worker and n_chunks % 2 == 0
    row_shape = table.shape[1:]

    @functools.partial(
        pl.kernel, mesh=_sc_mesh(),
        out_type=jax.ShapeDtypeStruct((n_idx,) + row_shape, table.dtype),
        scratch_types=[
            pltpu.VMEM((per_worker,), I32),
            pltpu.VMEM((SC_CHUNK,) + row_shape, table.dtype),
            pltpu.VMEM((SC_CHUNK,) + row_shape, table.dtype),
        ] + [pltpu.SemaphoreType.DMA] * 4,
    )
    def gather(table_hbm, idx_hbm, out_hbm, idx_v, buf0, buf1, g0, g1, w0, w1):
        bufs, gsem, wsem = (buf0, buf1), (g0, g1), (w0, w1)
        base = _sc_worker() * per_worker
        pltpu.sync_copy(idx_hbm.at[pl.ds(base, per_worker)], idx_v)

        def fetch(j, b):
            return pltpu.make_async_copy(table_hbm.at[idx_v.at[pl.ds(j * SC_CHUNK, SC_CHUNK)]], bufs[b], gsem[b])

        def flush(j, b):
            return pltpu.make_async_copy(bufs[b], out_hbm.at[pl.ds(base + j * SC_CHUNK, SC_CHUNK)], wsem[b])

        fetch(0, 0).start()
        fetch(0, 0).wait()
        fetch(1, 1).start()
        flush(0, 0).start()

        @pl.loop(1, n_chunks - 1, step=2)
        def _(j):
            for off in range(2):
                jj, b = j + off, (1 + off) % 2
                fetch(jj, b).wait()
                flush(jj - 1, 1 - b).wait()
                fetch(jj + 1, 1 - b).start()
                flush(jj, b).start()

        fetch(n_chunks - 1, 1).wait()
        flush(n_chunks - 1, 1).start()
        flush(n_chunks - 2, 0).wait()
        flush(n_chunks - 1, 1).wait()

    return gather(table, idx)


def _sc_scatter_rows(rows, dest_t, n_out):
    n_rows = rows.shape[0]
    per_worker = n_rows // SC_WORKERS
    n_chunks = per_worker // SC_CHUNK
    assert per_worker * SC_WORKERS == n_rows and n_chunks * SC_CHUNK == per_worker
    row_shape = rows.shape[1:]
    idx_w = dest_t.reshape(TOP_K, SC_WORKERS, n_chunks, SC_CHUNK).transpose(1, 2, 0, 3)
    idx_w = idx_w.reshape(SC_WORKERS, n_chunks * TOP_K, SC_CHUNK)

    @functools.partial(
        pl.kernel, mesh=_sc_mesh(),
        out_type=jax.ShapeDtypeStruct((n_out,) + row_shape, rows.dtype),
        scratch_types=[
            pltpu.VMEM((n_chunks * TOP_K, SC_CHUNK), I32),
            pltpu.VMEM((SC_CHUNK,) + row_shape, rows.dtype),
            pltpu.VMEM((SC_CHUNK,) + row_shape, rows.dtype),
        ] + [pltpu.SemaphoreType.DMA] * 4,
    )
    def scatter(rows_hbm, idx_hbm, out_hbm, idx_v, buf0, buf1, r0, r1, s0, s1):
        bufs, rsem, ssem = (buf0, buf1), (r0, r1), (s0, s1)
        worker = _sc_worker()
        base = worker * per_worker
        pltpu.sync_copy(idx_hbm.at[worker], idx_v)

        def fetch(j, b):
            return pltpu.make_async_copy(rows_hbm.at[pl.ds(base + j * SC_CHUNK, SC_CHUNK)], bufs[b], rsem[b])

        def send(j, k, b):
            return pltpu.make_async_copy(bufs[b], out_hbm.at[idx_v.at[j * TOP_K + k]], ssem[b])

        fetch(0, 0).start()
        for j in range(n_chunks):
            b = j % 2
            fetch(j, b).wait()
            if j + 1 < n_chunks:
                if j >= 1:
                    for k in range(TOP_K):
                        send(j - 1, k, 1 - b).wait()
                fetch(j + 1, 1 - b).start()
            for k in range(TOP_K):
                send(j, k, b).start()
        for j in range(max(n_chunks - 2, 0), n_chunks):
            for k in range(TOP_K):
                send(j, k, j % 2).wait()

    return scatter(rows, idx_w)


def _combine_body(*refs):
    yk_refs = refs[:TOP_K]
    w_ref, shr_ref, x1_ref, gt2_ref, gpost_ref, o_ref = refs[TOP_K:]
    tt = x1_ref.shape[0]
    w = w_ref[...]
    ssq = jnp.zeros((tt, 1), F32)
    for c in range(HALF // 128):
        c_lo = slice(c * 128, (c + 1) * 128)
        c_hi = slice(HALF + c * 128, HALF + (c + 1) * 128)
        y_lo = shr_ref[:, c_lo].astype(F32)
        y_hi = shr_ref[:, c_hi].astype(F32)
        for k in range(TOP_K):
            lo, hi = _unpack_halves(yk_refs[k][pl.ds(c, tt, stride=ROW_SUBLANES), :])
            y_lo = y_lo + w[:, k:k + 1] * lo
            y_hi = y_hi + w[:, k:k + 1] * hi
        ssq = ssq + jnp.sum(y_lo * y_lo, axis=-1, keepdims=True) + jnp.sum(y_hi * y_hi, axis=-1, keepdims=True)
        o_ref[:, c_lo] = y_lo
        o_ref[:, c_hi] = y_hi
    scale = lax.rsqrt(ssq / D_MODEL + EPS)
    o_ref[...] = x1_ref[...] + gt2_ref[0] * (o_ref[...] * scale * gpost_ref[...])


def _combine(yu, wts, shared, x1, gt2, g_post, seq):
    rows, d = x1.shape
    tiles = rows // COMB_TT
    tiles_per_b = seq // COMB_TT
    yk_specs = [pl.BlockSpec((COMB_TT * ROW_SUBLANES, 128), functools.partial(lambda i, k: (k * tiles + i, 0), k=k))
                for k in range(TOP_K)]
    return pl.pallas_call(
        _combine_body,
        grid=(tiles,),
        in_specs=yk_specs + [
            pl.BlockSpec((COMB_TT, TOP_K), lambda i: (i, 0)),
            pl.BlockSpec((COMB_TT, d), lambda i: (i, 0)),
            pl.BlockSpec((COMB_TT, d), lambda i: (i, 0)),
            pl.BlockSpec((1, 1, d), lambda i: (i // tiles_per_b, 0, 0)),
            _resident((1, d)),
        ],
        out_specs=pl.BlockSpec((COMB_TT, d), lambda i: (i, 0)),
        out_shape=jax.ShapeDtypeStruct((rows, d), F32),
        compiler_params=_cparams(("arbitrary",)),
        name="combine",
    )(*([yu] * TOP_K), wts, shared, x1, gt2, g_post)


def kernel(x, c, ctx, c_ctx, w_mod, b_mod, norm_mix_pre, norm_mix_post, norm_ffn_pre, norm_ffn_post, w_in, w_a2_fwd, b_a_fwd, w_a2_bwd, b_a_bwd, gla_norm, w_pool, pool_scale, w_out, w_router, router_bias, w_exp_gate, w_exp_up, w_exp_down, w_sh_gate, w_sh_up, w_sh_down):
    batch, seq, d = x.shape
    n_ctx = ctx.shape[1]
    assert w_mod.shape[0] == 1 and d == D_MODEL
    assert seq % SUPER == 0 and n_ctx % SUPER == 0 and seq % PROJ_TM == 0 and (batch * n_ctx) % PROJ_TM == 0
    rows = batch * seq

    mod_rows = 16
    c_all = jnp.concatenate([c, c_ctx[None, :], jnp.zeros((mod_rows - batch - 1, d), F32)], axis=0)
    mod_all = _modulation(c_all, w_mod[0], b_mod[0][None, :])
    sh1, sc1, gt1, sh2, sc2, gt2 = [m.reshape(batch, 1, d) for m in jnp.split(mod_all[:batch], 6, axis=-1)]
    csh1 = mod_all[batch, 0:d].reshape(1, 1, d)
    csc1 = mod_all[batch, d:2 * d].reshape(1, 1, d)

    kw, gw = GLA_KEY_WIDTH, GLA_WIDTH
    a0 = 2 * kw + 2 * gw
    w_in0 = w_in[0]
    w_main = jnp.concatenate([w_in0[:, :a0], w_in0[:, a0 + 2 * GLA_RANK:]], axis=1).astype(BF16)
    w_a = jnp.pad(w_in0[:, a0:a0 + 2 * GLA_RANK], ((0, 0), (0, 128 - 2 * GLA_RANK))).astype(BF16)
    w_ctx = w_main[:, kw:2 * kw + gw]
    w2f = jnp.pad(w_a2_fwd[0], ((0, 128 - GLA_RANK), (0, 0))).astype(BF16)
    w2b = jnp.pad(w_a2_bwd[0], ((GLA_RANK, 128 - 2 * GLA_RANK), (0, 0))).astype(BF16)
    g_mix_pre = norm_mix_pre[0][None, :]
    wr_hi, wr_lo = _bf16_terms(jnp.pad(w_router[0], ((0, 0), (0, 128 - N_EXPERTS))))

    u_ctx, a_ctx = _in_projection(ctx.reshape(batch * n_ctx, d), g_mix_pre, csc1, csh1, w_ctx, w_a,
                                  batch * n_ctx)
    u_lat, a_lat = _in_projection(x.reshape(rows, d), g_mix_pre, sc1, sh1, w_main, w_a, seq)

    y_gla = _gla(u_lat, a_lat, u_ctx, a_ctx, w2f, b_a_fwd[0][None, :], w2b, b_a_bwd[0][None, :],
                 gla_norm[0][None, :], batch, seq, n_ctx)
    y_pool = _pool_mixer(u_lat, _col_window_matrices(), w_pool[0].astype(BF16), pool_scale[0][None, :],
                         batch, seq)

    x1, h_packed, eidx_t, pos_t, wts_t, counts = _mix_and_route(
        y_gla, y_pool, x.reshape(rows, d), gt1, sc2, sh2, norm_mix_post[0][None, :],
        norm_ffn_pre[0][None, :], w_out[0].astype(BF16), wr_hi, wr_lo, router_bias[0][:, None], seq)
    shared = _shared_expert(h_packed, w_sh_gate[0].astype(BF16), w_sh_up[0].astype(BF16),
                            w_sh_down[0].astype(BF16))

    counts = counts[:, 0]
    padded = (counts + MOE_TILE - 1) // MOE_TILE * MOE_TILE
    pends = jnp.cumsum(padded)
    pstarts = pends - padded
    dest_t = _route_offsets(eidx_t, pos_t, pstarts.astype(I32))
    n_blocks = rows * TOP_K // MOE_TILE + N_EXPERTS
    n_used = (pends[-1] // MOE_TILE).astype(I32)
    blk = jnp.minimum(jnp.arange(n_blocks, dtype=I32), n_used - 1)
    block_expert = jnp.sum((blk * MOE_TILE)[:, None] >= pends[None, :], axis=1).astype(I32)
    block_expert = jnp.minimum(block_expert, N_EXPERTS - 1)
    valid = jnp.clip((pstarts + counts)[block_expert] - blk * MOE_TILE, 0, MOE_TILE).astype(I32)
    valid = jnp.where(jnp.arange(n_blocks, dtype=I32) < n_used, valid, 0)

    xs = _sc_scatter_rows(h_packed.reshape(-1, ROW_SUBLANES, 128), dest_t, n_blocks * MOE_TILE)
    xs = xs.reshape(-1, 128)
    ys = _experts(xs, block_expert, n_used.reshape(1), valid, w_exp_gate[0], w_exp_up[0], w_exp_down[0])
    yu = _sc_gather_rows(ys.reshape(-1, ROW_SUBLANES, 128), dest_t.reshape(-1))
    out = _combine(yu.reshape(-1, 128), wts_t.T, shared, x1, gt2, norm_ffn_post[0][None, :], seq)
    return out.reshape(batch, seq, d)
```

```python
import functools

import numpy as np
import jax
import jax.numpy as jnp
from jax import lax
from jax.experimental import pallas as pl
from jax.experimental.pallas import tpu as pltpu
from jax.experimental.pallas import tpu_sc as plsc

F32 = jnp.float32
BF16 = jnp.bfloat16
I32 = jnp.int32
U32 = jnp.uint32
HIGHEST = lax.Precision.HIGHEST

D_MODEL = 2048
GRID_W = 64
GLA_HEADS = 4
GLA_DK = 128
GLA_DV = 256
GLA_KEY_WIDTH = GLA_HEADS * GLA_DK
GLA_WIDTH = GLA_HEADS * GLA_DV
GLA_RANK = 16
GLA_TAU = 16.0
GLA_CHUNK = 64
POOL_WIDTH = 1024
POOL_WINDOWS = (2, 4, 8, 16)
POOL_GROUP = 256
N_EXPERTS = 64
TOP_K = 8
N_GROUPS = 8
GROUP_SIZE = N_EXPERTS // N_GROUPS
TOPK_GROUPS = 4
D_EXPERT = 512
D_SHARED = 512
ROUTED_SCALE = 2.5
EPS = 1e-6

HALF = D_MODEL // 2
SUPER = 4 * GLA_CHUNK
GLA_HPS = 4
POOL_PAD = 8 * GRID_W
VMEM_LIMIT = 56 * 1024 * 1024

MOD_TN = 1024
PROJ_TM = 512
PROJ_TN = 512
MIX_TM = 512
MOE_TILE = 512
SHARED_TM = 1024
COMB_TT = 256
ROW_SUBLANES = 8
SC_CORES = 2
SC_WORKERS = 32
SC_CHUNK = 32
OFFS_TN = 2048


def _cparams(sem):
    return pltpu.CompilerParams(dimension_semantics=sem, vmem_limit_bytes=VMEM_LIMIT)


def _resident(shape):
    nd = len(shape)
    return pl.BlockSpec(shape, lambda *_: (0,) * nd, pipeline_mode=pl.Buffered(1))


def _silu(v):
    return v * jax.nn.sigmoid(v)


def _pack_halves(lo, hi):
    lo_b = lax.bitcast_convert_type(lo.astype(BF16).astype(F32), U32)
    hi_b = lax.bitcast_convert_type(hi.astype(BF16).astype(F32), U32)
    return (hi_b & jnp.uint32(0xFFFF0000)) | (lo_b >> 16)


def _unpack_halves(p):
    lo = lax.bitcast_convert_type(p << 16, F32)
    hi = lax.bitcast_convert_type(p & jnp.uint32(0xFFFF0000), F32)
    return lo, hi


def _bf16_terms(x):
    hi = lax.bitcast_convert_type(lax.bitcast_convert_type(x, U32) & jnp.uint32(0xFFFF0000), F32)
    return hi.astype(BF16), (x - hi).astype(BF16)


def _store_row_tiles(ref, packed):
    n = packed.shape[0]
    for c in range(HALF // 128):
        ref[pl.ds(c, n, stride=ROW_SUBLANES), :] = packed[:, c * 128:(c + 1) * 128]


def _load_row_tiles(ref):
    n = ref.shape[0] // ROW_SUBLANES
    return jnp.concatenate([ref[pl.ds(c, n, stride=ROW_SUBLANES), :] for c in range(HALF // 128)], axis=1)


def _mod_body(c_ref, w_ref, b_ref, o_ref):
    s = _silu(c_ref[...])
    o_ref[...] = jnp.dot(s, w_ref[...], preferred_element_type=F32, precision=HIGHEST) + b_ref[...]


def _modulation(c_all, w_mod, b_mod):
    rows, d = c_all.shape
    n = w_mod.shape[1]
    return pl.pallas_call(
        _mod_body,
        grid=(n // MOD_TN,),
        in_specs=[
            pl.BlockSpec((rows, d), lambda j: (0, 0)),
            pl.BlockSpec((d, MOD_TN), lambda j: (0, j)),
            pl.BlockSpec((1, MOD_TN), lambda j: (0, j)),
        ],
        out_specs=pl.BlockSpec((rows, MOD_TN), lambda j: (0, j)),
        out_shape=jax.ShapeDtypeStruct((rows, n), F32),
        compiler_params=_cparams(("arbitrary",)),
        name="modulation",
    )(c_all, w_mod, b_mod)


def _rms_scale(x):
    return lax.rsqrt(jnp.mean(x * x, axis=-1, keepdims=True) + EPS)


def _inproj_body(x_ref, g_ref, sc_ref, sh_ref, w_ref, wa_ref, o_ref, a_ref, *, n_main):
    x = x_ref[...]
    h = x * _rms_scale(x) * g_ref[...]
    h = h * (1.0 + sc_ref[0]) + sh_ref[0]
    hb = h.astype(BF16)
    for n in range(n_main // PROJ_TN):
        cols = slice(n * PROJ_TN, (n + 1) * PROJ_TN)
        o_ref[:, cols] = jnp.dot(hb, w_ref[:, cols], preferred_element_type=F32).astype(BF16)
    a_ref[...] = jnp.dot(hb, wa_ref[...], preferred_element_type=F32)


def _in_projection(x2d, gain, sc, sh, w_main, w_a, rows_per_mod):
    rows, d = x2d.shape
    n_main = w_main.shape[1]
    tiles_per_mod = rows_per_mod // PROJ_TM
    mod_map = lambda i: (i // tiles_per_mod, 0, 0)
    return pl.pallas_call(
        functools.partial(_inproj_body, n_main=n_main),
        grid=(rows // PROJ_TM,),
        in_specs=[
            pl.BlockSpec((PROJ_TM, d), lambda i: (i, 0)),
            _resident((1, d)),
            pl.BlockSpec((1, 1, d), mod_map),
            pl.BlockSpec((1, 1, d), mod_map),
            _resident((d, n_main)),
            _resident((d, 128)),
        ],
        out_specs=[
            pl.BlockSpec((PROJ_TM, n_main), lambda i: (i, 0)),
            pl.BlockSpec((PROJ_TM, 128), lambda i: (i, 0)),
        ],
        out_shape=[
            jax.ShapeDtypeStruct((rows, n_main), BF16),
            jax.ShapeDtypeStruct((rows, 128), F32),
        ],
        compiler_params=_cparams(("arbitrary",)),
        name="in_projection",
    )(x2d, gain, sc, sh, w_main, w_a)


def _log_sigmoid(z):
    return jnp.minimum(z, 0.0) - jnp.log1p(jnp.exp(-jnp.abs(z)))


def _gla_super(q, k, v, a, w2, ba, tri, mask, st_ref, reverse):
    nc = SUPER // GLA_CHUNK
    z = jnp.dot(a.astype(BF16), w2, preferred_element_type=F32) + ba
    g = _log_sigmoid(z) * (1.0 / GLA_TAU)
    g_hi, g_lo = _bf16_terms(g)
    G = jnp.dot(tri, g_hi, preferred_element_type=F32) + jnp.dot(tri, g_lo, preferred_element_type=F32)
    G = G.reshape(nc, GLA_CHUNK, GLA_DK)
    end_row = 0 if reverse else GLA_CHUNK - 1
    mid_row = GLA_CHUNK - 1 - GLA_CHUNK // 2 if reverse else GLA_CHUNK // 2
    g_end = G[:, end_row:end_row + 1, :]
    g_mid = G[:, mid_row:mid_row + 1, :]
    k4 = k.astype(F32).reshape(nc, GLA_CHUNK, GLA_DK)
    kd = (k4 * jnp.exp(g_end - G)).astype(BF16)
    dec = jnp.exp(g_end)
    o = None
    if q is not None:
        q4 = q.astype(F32).reshape(nc, GLA_CHUNK, GLA_DK) * (GLA_DK ** -0.5)
        qg = (q4 * jnp.exp(G - g_mid)).reshape(SUPER, GLA_DK).astype(BF16)
        kg = (k4 * jnp.exp(g_mid - G)).reshape(SUPER, GLA_DK).astype(BF16)
        qe = (q4 * jnp.exp(G)).astype(BF16)
        att = lax.dot_general(qg, kg, (((1,), (1,)), ((), ())), preferred_element_type=F32)
        att = jnp.where(mask, att, 0.0).astype(BF16)
        o = jnp.dot(att, v, preferred_element_type=F32)
    outs = [None] * nc
    order = range(nc - 1, -1, -1) if reverse else range(nc)
    for c in order:
        rows = slice(c * GLA_CHUNK, (c + 1) * GLA_CHUNK)
        st = st_ref[...]
        if q is not None:
            inter = lax.dot_general(qe[c], st.astype(BF16), (((1,), (1,)), ((), ())),
                                    preferred_element_type=F32)
            outs[c] = o[rows] + inter
        upd = lax.dot_general(v[rows], kd[c], (((0,), (0,)), ((), ())), preferred_element_type=F32)
        st_ref[...] = st * dec[c] + upd
    if q is None:
        return None
    return jnp.concatenate(outs, axis=0)


def _gla_body(q_ref, k_ref, v_ref, r_ref, a_ref, kc_ref, vc_ref, ac_ref,
              w2f_ref, baf_ref, w2b_ref, bab_ref, gn_ref, y_ref, o_acc, st, *, n_ctx):
    n_sup = q_ref.shape[0] // SUPER
    row = lax.broadcasted_iota(I32, (SUPER, SUPER), 0)
    col = lax.broadcasted_iota(I32, (SUPER, SUPER), 1)
    same_chunk = (row >> 6) == (col >> 6)
    mask_f = same_chunk & (col <= row)
    mask_b = same_chunk & (col >= row)
    tri_f = jnp.where(mask_f, 1.0, 0.0).astype(BF16)
    tri_b = jnp.where(mask_b, 1.0, 0.0).astype(BF16)
    heads = range(GLA_HPS)
    kcol = [slice(h * GLA_DK, (h + 1) * GLA_DK) for h in heads]
    vcol = [slice(h * GLA_DV, (h + 1) * GLA_DV) for h in heads]
    gate_f = [(w2f_ref[:, kcol[h]], baf_ref[:, kcol[h]]) for h in heads]
    gate_b = [(w2b_ref[:, kcol[h]], bab_ref[:, kcol[h]]) for h in heads]

    st[...] = jnp.zeros_like(st)
    n_csup = n_ctx // SUPER
    for s in range(n_csup):
        rf = slice(s * SUPER, (s + 1) * SUPER)
        rb = slice((n_csup - 1 - s) * SUPER, (n_csup - s) * SUPER)
        for h in heads:
            _gla_super(None, kc_ref[rf, kcol[h]], vc_ref[rf, vcol[h]], ac_ref[rf, :], *gate_f[h],
                       tri_f, mask_f, st.at[0, h], False)
            _gla_super(None, kc_ref[rb, kcol[h]], vc_ref[rb, vcol[h]], ac_ref[rb, :], *gate_b[h],
                       tri_b, mask_b, st.at[1, h], True)

    o_acc[...] = jnp.zeros_like(o_acc)

    def step(i, carry):
        rf = pl.ds(pl.multiple_of(i * SUPER, SUPER), SUPER)
        rb = pl.ds(pl.multiple_of((n_sup - 1 - i) * SUPER, SUPER), SUPER)
        a_f = a_ref[rf, :]
        a_b = a_ref[rb, :]
        for h in heads:
            of = _gla_super(q_ref[rf, kcol[h]], k_ref[rf, kcol[h]], v_ref[rf, vcol[h]], a_f, *gate_f[h],
                            tri_f, mask_f, st.at[0, h], False)
            o_acc[rf, vcol[h]] += of
            ob = _gla_super(q_ref[rb, kcol[h]], k_ref[rb, kcol[h]], v_ref[rb, vcol[h]], a_b, *gate_b[h],
                            tri_b, mask_b, st.at[1, h], True)
            o_acc[rb, vcol[h]] += ob
        return carry

    lax.fori_loop(0, n_sup, step, 0)

    for h in heads:
        o = o_acc[:, vcol[h]]
        o = o * _rms_scale(o) * gn_ref[:, vcol[h]]
        y_ref[:, vcol[h]] = (o * _silu(r_ref[:, vcol[h]].astype(F32))).astype(BF16)


def _gla(u_lat, a_lat, u_ctx, a_ctx, w2f, baf, w2b, bab, gla_norm, batch, seq, n_ctx):
    groups = GLA_HEADS // GLA_HPS
    kw, vw = GLA_HPS * GLA_DK, GLA_HPS * GLA_DV
    kb = GLA_KEY_WIDTH // kw
    vb = 2 * GLA_KEY_WIDTH // vw
    rb = vb + groups
    assert GLA_KEY_WIDTH % kw == 0 and (2 * GLA_KEY_WIDTH) % vw == 0 and GLA_WIDTH % kw == 0
    ckb = GLA_WIDTH // kw
    return pl.pallas_call(
        functools.partial(_gla_body, n_ctx=n_ctx),
        grid=(batch, groups),
        in_specs=[
            pl.BlockSpec((seq, kw), lambda b, h: (b, h)),
            pl.BlockSpec((seq, kw), lambda b, h: (b, kb + h)),
            pl.BlockSpec((seq, vw), lambda b, h: (b, vb + h)),
            pl.BlockSpec((seq, vw), lambda b, h: (b, rb + h)),
            pl.BlockSpec((seq, 128), lambda b, h: (b, 0)),
            pl.BlockSpec((n_ctx, kw), lambda b, h: (b, ckb + h)),
            pl.BlockSpec((n_ctx, vw), lambda b, h: (b, h)),
            pl.BlockSpec((n_ctx, 128), lambda b, h: (b, 0)),
            pl.BlockSpec((128, kw), lambda b, h: (0, h)),
            pl.BlockSpec((1, kw), lambda b, h: (0, h)),
            pl.BlockSpec((128, kw), lambda b, h: (0, h)),
            pl.BlockSpec((1, kw), lambda b, h: (0, h)),
            pl.BlockSpec((1, vw), lambda b, h: (0, h)),
        ],
        out_specs=pl.BlockSpec((seq, vw), lambda b, h: (b, h)),
        out_shape=jax.ShapeDtypeStruct((batch * seq, GLA_WIDTH), BF16),
        scratch_shapes=[
            pltpu.VMEM((seq, vw), F32),
            pltpu.VMEM((2, GLA_HPS, GLA_DV, GLA_DK), F32),
        ],
        compiler_params=_cparams(("arbitrary", "arbitrary")),
        name="gla",
    )(u_lat, u_lat, u_lat, u_lat, a_lat, u_ctx, u_ctx, a_ctx, w2f, baf, w2b, bab, gla_norm)


def _col_window_matrices():
    t = np.arange(SUPER)
    r, c = t // GRID_W, t % GRID_W
    mats = []
    for w in POOL_WINDOWS:
        lo = np.maximum(c - w // 2, 0)[:, None]
        hi = np.minimum(c + w // 2, GRID_W)[:, None]
        m = (r[:, None] == r[None, :]) & (c[None, :] >= lo) & (c[None, :] < hi)
        mats.append(m.astype(np.float32))
    return jnp.asarray(np.stack(mats), dtype=BF16)


def _pool_body(p_ref, cw_ref, wp_ref, ps_ref, y_ref, pad_ref):
    seq = p_ref.shape[0]
    n_rows = seq // GRID_W
    zeros = jnp.zeros((POOL_PAD, POOL_GROUP), F32)
    pad_ref[0:POOL_PAD, :] = zeros
    pad_ref[POOL_PAD + seq:POOL_PAD + seq + POOL_PAD, :] = zeros
    t = lax.broadcasted_iota(I32, (seq, POOL_GROUP), 0)
    r = t >> 6
    c = t & (GRID_W - 1)
    for gi, w in enumerate(POOL_WINDOWS):
        cols = slice(gi * POOL_GROUP, (gi + 1) * POOL_GROUP)
        cw = cw_ref[gi]
        for j in range(seq // SUPER):
            rows = slice(j * SUPER, (j + 1) * SUPER)
            pad_ref[POOL_PAD + j * SUPER:POOL_PAD + (j + 1) * SUPER, :] = jnp.dot(
                cw, p_ref[rows, cols], preferred_element_type=F32)
        total = None
        for d in range(-(w // 2), w // 2):
            start = POOL_PAD + d * GRID_W
            part = pad_ref[start:start + seq, :]
            total = part if total is None else total + part
        cnt_r = jnp.minimum(r + w // 2, n_rows) - jnp.maximum(r - w // 2, 0)
        cnt_c = jnp.minimum(c + w // 2, GRID_W) - jnp.maximum(c - w // 2, 0)
        mean = total / (cnt_r * cnt_c).astype(F32)
        diff = (mean - p_ref[:, cols].astype(F32)).astype(BF16)
        y = jnp.dot(diff, wp_ref[gi], preferred_element_type=F32) * ps_ref[:, cols]
        y_ref[:, cols] = y.astype(BF16)


def _pool_mixer(u_lat, col_mats, w_pool, pool_scale, batch, seq):
    pb = (u_lat.shape[1] - POOL_WIDTH) // POOL_WIDTH
    ng = len(POOL_WINDOWS)
    return pl.pallas_call(
        _pool_body,
        grid=(batch,),
        in_specs=[
            pl.BlockSpec((seq, POOL_WIDTH), lambda b: (b, pb)),
            _resident((ng, SUPER, SUPER)),
            _resident((ng, POOL_GROUP, POOL_GROUP)),
            _resident((1, POOL_WIDTH)),
        ],
        out_specs=pl.BlockSpec((seq, POOL_WIDTH), lambda b: (b, 0)),
        out_shape=jax.ShapeDtypeStruct((batch * seq, POOL_WIDTH), BF16),
        scratch_shapes=[pltpu.VMEM((seq + 2 * POOL_PAD, POOL_GROUP), F32)],
        compiler_params=_cparams(("arbitrary",)),
        name="pool_mixer",
    )(u_lat, col_mats, w_pool, pool_scale)


def _first_index(hit, iota, size, axis):
    return jnp.min(jnp.where(hit, iota, size), axis=axis, keepdims=True)


def _mix_body(yg_ref, yp_ref, x_ref, gt1_ref, sc2_ref, sh2_ref, gpost_ref, gpre_ref, wout_ref,
              wrh_ref, wrl_ref, rb_ref, upper_ref,
              x1_ref, hp_ref, eidx_ref, pos_ref, wts_ref, cnt_ref, run_ref):
    tm = x_ref.shape[0]
    neg_inf = jnp.float32(-jnp.inf)

    @pl.when(pl.program_id(0) == 0)
    def _():
        run_ref[...] = jnp.zeros_like(run_ref)

    y = jnp.dot(yg_ref[...], wout_ref[0:GLA_WIDTH, :], preferred_element_type=F32)
    y = y + jnp.dot(yp_ref[...], wout_ref[GLA_WIDTH:, :], preferred_element_type=F32)
    x1 = x_ref[...] + gt1_ref[0] * (y * _rms_scale(y) * gpost_ref[...])
    x1_ref[...] = x1
    h = x1 * _rms_scale(x1) * gpre_ref[...]
    h = h * (1.0 + sc2_ref[0]) + sh2_ref[0]
    _store_row_tiles(hp_ref, _pack_halves(h[:, :HALF], h[:, HALF:]))

    h_hi, h_lo = _bf16_terms(h)
    lt = jnp.dot(h_hi, wrh_ref[...], preferred_element_type=F32)
    lt = lt + jnp.dot(h_hi, wrl_ref[...], preferred_element_type=F32)
    lt = lt + jnp.dot(h_lo, wrh_ref[...], preferred_element_type=F32)
    logits = lt.T[0:N_EXPERTS, :]
    scores = jax.nn.sigmoid(logits)
    sel = scores + rb_ref[...]
    shape3 = (N_GROUPS, GROUP_SIZE, tm)
    sel3 = sel.reshape(shape3)
    i_in = lax.broadcasted_iota(I32, shape3, 1).astype(F32)
    m1 = jnp.max(sel3, axis=1, keepdims=True)
    f1 = _first_index(sel3 == m1, i_in, float(GROUP_SIZE), 1)
    m2 = jnp.max(jnp.where(i_in == f1, neg_inf, sel3), axis=1, keepdims=True)
    grp = jnp.broadcast_to(m1 + m2, shape3).reshape(N_EXPERTS, tm)
    i_e = lax.broadcasted_iota(I32, (N_EXPERTS, tm), 0)
    i_grp = (i_e >> 3).astype(F32)
    i_e = i_e.astype(F32)
    allowed = jnp.zeros((N_EXPERTS, tm), F32)
    for _ in range(TOPK_GROUPS):
        m = jnp.max(grp, axis=0, keepdims=True)
        pick = i_grp == _first_index(grp == m, i_grp, float(N_GROUPS), 0)
        allowed = jnp.where(pick, 1.0, allowed)
        grp = jnp.where(pick, neg_inf, grp)
    cand = jnp.where(allowed > 0.0, sel, neg_inf)
    onehot = jnp.zeros((N_EXPERTS, tm), F32)
    picks, wts = [], []
    for k in range(TOP_K):
        m = jnp.max(cand, axis=0, keepdims=True)
        f = _first_index(cand == m, i_e, float(N_EXPERTS), 0)
        pick = i_e == f
        picks.append(pick)
        eidx_ref[k:k + 1, :] = f.astype(I32)
        wts.append(jnp.sum(jnp.where(pick, scores, 0.0), axis=0, keepdims=True))
        onehot = jnp.where(pick, 1.0, onehot)
        cand = jnp.where(pick, neg_inf, cand)
    w_sum = wts[0]
    for k in range(1, TOP_K):
        w_sum = w_sum + wts[k]
    for k in range(TOP_K):
        wts_ref[k:k + 1, :] = wts[k] / w_sum * ROUTED_SCALE

    before = jnp.dot(onehot.astype(BF16), upper_ref[...], preferred_element_type=F32)
    before = before + run_ref[:, 0:1]
    for k in range(TOP_K):
        pos_ref[k:k + 1, :] = jnp.sum(jnp.where(picks[k], before, 0.0), axis=0, keepdims=True).astype(I32)
    run_ref[...] = run_ref[...] + jnp.sum(onehot, axis=1, keepdims=True)
    cnt_ref[...] = run_ref[...].astype(I32)


def _mix_and_route(y_gla, y_pool, x2d, gt1, sc2, sh2, g_post, g_pre, w_out, wr_hi, wr_lo, router_bias, seq):
    rows, d = x2d.shape
    tiles_per_b = seq // MIX_TM
    bmap = lambda i: (i // tiles_per_b, 0, 0)
    rmap = lambda i: (i, 0)
    tmap = lambda i: (0, i)
    upper = jnp.asarray(np.triu(np.ones((MIX_TM, MIX_TM), np.float32), 1), dtype=BF16)
    return pl.pallas_call(
        _mix_body,
        grid=(rows // MIX_TM,),
        in_specs=[
            pl.BlockSpec((MIX_TM, GLA_WIDTH), rmap),
            pl.BlockSpec((MIX_TM, POOL_WIDTH), rmap),
            pl.BlockSpec((MIX_TM, d), rmap),
            pl.BlockSpec((1, 1, d), bmap),
            pl.BlockSpec((1, 1, d), bmap),
            pl.BlockSpec((1, 1, d), bmap),
            _resident((1, d)),
            _resident((1, d)),
            _resident((d, d)),
            _resident((d, 128)),
            _resident((d, 128)),
            _resident((N_EXPERTS, 1)),
            _resident((MIX_TM, MIX_TM)),
        ],
        out_specs=[
            pl.BlockSpec((MIX_TM, d), rmap),
            pl.BlockSpec((MIX_TM * ROW_SUBLANES, 128), rmap),
            pl.BlockSpec((TOP_K, MIX_TM), tmap),
            pl.BlockSpec((TOP_K, MIX_TM), tmap),
            pl.BlockSpec((TOP_K, MIX_TM), tmap),
            pl.BlockSpec((N_EXPERTS, 128), lambda i: (0, 0)),
        ],
        out_shape=[
            jax.ShapeDtypeStruct((rows, d), F32),
            jax.ShapeDtypeStruct((rows * ROW_SUBLANES, 128), U32),
            jax.ShapeDtypeStruct((TOP_K, rows), I32),
            jax.ShapeDtypeStruct((TOP_K, rows), I32),
            jax.ShapeDtypeStruct((TOP_K, rows), F32),
            jax.ShapeDtypeStruct((N_EXPERTS, 128), I32),
        ],
        scratch_shapes=[pltpu.VMEM((N_EXPERTS, 128), F32)],
        compiler_params=_cparams(("arbitrary",)),
        name="mix_and_route",
    )(y_gla, y_pool, x2d, gt1, sc2, sh2, g_post, g_pre, w_out, wr_hi, wr_lo, router_bias, upper)


def _swiglu(lo, hi, wg_ref, wu_ref, wd_ref):
    g = jnp.dot(lo, wg_ref[:HALF, :], preferred_element_type=F32)
    g = g + jnp.dot(hi, wg_ref[HALF:, :], preferred_element_type=F32)
    u = jnp.dot(lo, wu_ref[:HALF, :], preferred_element_type=F32)
    u = u + jnp.dot(hi, wu_ref[HALF:, :], preferred_element_type=F32)
    act = (_silu(g) * u).astype(BF16)
    return jnp.dot(act, wd_ref[...], preferred_element_type=F32)


def _shared_body(hp_ref, wg_ref, wu_ref, wd_ref, o_ref):
    lo, hi = _unpack_halves(_load_row_tiles(hp_ref))
    o_ref[...] = _swiglu(lo.astype(BF16), hi.astype(BF16), wg_ref, wu_ref, wd_ref).astype(BF16)


def _shared_expert(h_packed, w_sg, w_su, w_sd):
    d, ds = w_sg.shape
    rows = h_packed.shape[0] // ROW_SUBLANES
    return pl.pallas_call(
        _shared_body,
        grid=(rows // SHARED_TM,),
        in_specs=[
            pl.BlockSpec((SHARED_TM * ROW_SUBLANES, 128), lambda i: (i, 0)),
            _resident((d, ds)),
            _resident((d, ds)),
            _resident((ds, d)),
        ],
        out_specs=pl.BlockSpec((SHARED_TM, d), lambda i: (i, 0)),
        out_shape=jax.ShapeDtypeStruct((rows, d), BF16),
        compiler_params=_cparams(("arbitrary",)),
        name="shared_expert",
    )(h_packed, w_sg, w_su, w_sd)


def _offsets_body(ps_ref, e_ref, p_ref, d_ref):
    e = e_ref[...]
    d = p_ref[...]
    for x in range(N_EXPERTS):
        d = d + jnp.where(e == x, ps_ref[x], 0)
    d_ref[...] = d


def _route_offsets(eidx_t, pos_t, pstarts):
    k, rows = eidx_t.shape
    spec = pl.BlockSpec((k, OFFS_TN), lambda i, ps: (0, i))
    grid_spec = pltpu.PrefetchScalarGridSpec(
        num_scalar_prefetch=1, grid=(rows // OFFS_TN,), in_specs=[spec, spec], out_specs=spec)
    return pl.pallas_call(
        _offsets_body,
        grid_spec=grid_spec,
        out_shape=jax.ShapeDtypeStruct((k, rows), I32),
        compiler_params=_cparams(("arbitrary",)),
        name="route_offsets",
    )(pstarts, eidx_t, pos_t)


def _expert_body(be_ref, nu_ref, valid_ref, nxt_ref, slot_ref, x_ref, wg_hbm, wu_hbm, wd_hbm, y_ref,
                 wg_f, wu_f, wd_f, wg_b, wu_b, wd_b, sems):
    i = pl.program_id(0)
    valid = valid_ref[i]
    expert = be_ref[i]
    slot = slot_ref[i]

    def weight_copies(e, s):
        return [pltpu.make_async_copy(src.at[e], dst.at[s], sems.at[s, j])
                for j, (src, dst) in enumerate(((wg_hbm, wg_f), (wu_hbm, wu_f), (wd_hbm, wd_f)))]

    @pl.when(i == 0)
    def _():
        for cp in weight_copies(expert, slot):
            cp.start()

    @pl.when((i == 0) | (expert != be_ref[jnp.maximum(i - 1, 0)]))
    def _():
        for cp in weight_copies(expert, slot):
            cp.wait()

        @pl.when(nxt_ref[i] != expert)
        def _():
            for cp in weight_copies(nxt_ref[i], 1 - slot):
                cp.start()

        wg_b[...] = wg_f[slot].astype(BF16)
        wu_b[...] = wu_f[slot].astype(BF16)
        wd_b[...] = wd_f[slot].astype(BF16)

    def compute(n_rows):
        tiles = pl.ds(0, n_rows * ROW_SUBLANES)
        lo, hi = _unpack_halves(_load_row_tiles(x_ref.at[tiles]))
        y = _swiglu(lo.astype(BF16), hi.astype(BF16), wg_b, wu_b, wd_b)
        _store_row_tiles(y_ref.at[tiles], _pack_halves(y[:, :HALF], y[:, HALF:]))

    @pl.when(valid > MOE_TILE // 2)
    def _():
        compute(MOE_TILE)

    @pl.when((valid > 0) & (valid <= MOE_TILE // 2))
    def _():
        compute(MOE_TILE // 2)


def _experts(xs, block_expert, n_used, valid, next_expert, slot, w_eg, w_eu, w_ed):
    n_rows = xs.shape[0] // ROW_SUBLANES
    n_blocks = n_rows // MOE_TILE
    d, de = w_eg.shape[1], w_eg.shape[2]
    row_map = lambda i, be, nu, *_: (jnp.minimum(i, nu[0] - 1), 0)
    grid_spec = pltpu.PrefetchScalarGridSpec(
        num_scalar_prefetch=5,
        grid=(n_blocks,),
        in_specs=[
            pl.BlockSpec((MOE_TILE * ROW_SUBLANES, 128), row_map),
            pl.BlockSpec(memory_space=pl.ANY),
            pl.BlockSpec(memory_space=pl.ANY),
            pl.BlockSpec(memory_space=pl.ANY),
        ],
        out_specs=pl.BlockSpec((MOE_TILE * ROW_SUBLANES, 128), row_map),
        scratch_shapes=[
            pltpu.VMEM((2, d, de), F32),
            pltpu.VMEM((2, d, de), F32),
            pltpu.VMEM((2, de, d), F32),
            pltpu.VMEM((d, de), BF16),
            pltpu.VMEM((d, de), BF16),
            pltpu.VMEM((de, d), BF16),
            pltpu.SemaphoreType.DMA((2, 3)),
        ],
    )
    return pl.pallas_call(
        _expert_body,
        grid_spec=grid_spec,
        out_shape=jax.ShapeDtypeStruct((n_rows * ROW_SUBLANES, 128), U32),
        compiler_params=_cparams(("arbitrary",)),
        name="experts",
    )(block_expert, n_used, valid, next_expert, slot, xs, w_eg, w_eu, w_ed)


def _sc_mesh():
    return plsc.VectorSubcoreMesh(core_axis_name="c", subcore_axis_name="s")


def _sc_worker():
    return lax.axis_index("s") * SC_CORES + lax.axis_index("c")


def _sc_gather_rows(table, idx):
    n_idx = idx.shape[0]
    per_worker = n_idx // SC_WORKERS
    n_chunks = per_worker // SC_CHUNK
    assert per_worker * SC_WORKERS == n_idx and n_chunks * SC_CHUNK == per_worker and n_chunks % 2 == 0
    row_shape = table.shape[1:]

    @functools.partial(
        pl.kernel, mesh=_sc_mesh(),
        out_type=jax.ShapeDtypeStruct((n_idx,) + row_shape, table.dtype),
        scratch_types=[
            pltpu.VMEM((per_worker,), I32),
            pltpu.VMEM((SC_CHUNK,) + row_shape, table.dtype),
            pltpu.VMEM((SC_CHUNK,) + row_shape, table.dtype),
        ] + [pltpu.SemaphoreType.DMA] * 4,
    )
    def gather(table_hbm, idx_hbm, out_hbm, idx_v, buf0, buf1, g0, g1, w0, w1):
        bufs, gsem, wsem = (buf0, buf1), (g0, g1), (w0, w1)
        base = _sc_worker() * per_worker
        pltpu.sync_copy(idx_hbm.at[pl.ds(base, per_worker)], idx_v)

        def fetch(j, b):
            return pltpu.make_async_copy(table_hbm.at[idx_v.at[pl.ds(j * SC_CHUNK, SC_CHUNK)]], bufs[b], gsem[b])

        def flush(j, b):
            return pltpu.make_async_copy(bufs[b], out_hbm.at[pl.ds(base + j * SC_CHUNK, SC_CHUNK)], wsem[b])

        fetch(0, 0).start()
        fetch(0, 0).wait()
        fetch(1, 1).start()
        flush(0, 0).start()

        @pl.loop(1, n_chunks - 1, step=2)
        def _(j):
            for off in range(2):
                jj, b = j + off, (1 + off) % 2
                fetch(jj, b).wait()
                flush(jj - 1, 1 - b).wait()
                fetch(jj + 1, 1 - b).start()
                flush(jj, b).start()

        fetch(n_chunks - 1, 1).wait()
        flush(n_chunks - 1, 1).start()
        flush(n_chunks - 2, 0).wait()
        flush(n_chunks - 1, 1).wait()

    return gather(table, idx)


def _sc_scatter_rows(rows, dest_t, n_out):
    n_rows = rows.shape[0]
    per_worker = n_rows // SC_WORKERS
    n_chunks = per_worker // SC_CHUNK
    assert per_worker * SC_WORKERS == n_rows and n_chunks * SC_CHUNK == per_worker
    row_shape = rows.shape[1:]
    idx_w = dest_t.reshape(TOP_K, SC_WORKERS, n_chunks, SC_CHUNK).transpose(1, 2, 0, 3)
    idx_w = idx_w.reshape(SC_WORKERS, n_chunks * TOP_K, SC_CHUNK)

    @functools.partial(
        pl.kernel, mesh=_sc_mesh(),
        out_type=jax.ShapeDtypeStruct((n_out,) + row_shape, rows.dtype),
        scratch_types=[
            pltpu.VMEM((n_chunks * TOP_K, SC_CHUNK), I32),
            pltpu.VMEM((SC_CHUNK,) + row_shape, rows.dtype),
            pltpu.VMEM((SC_CHUNK,) + row_shape, rows.dtype),
        ] + [pltpu.SemaphoreType.DMA] * 4,
    )
    def scatter(rows_hbm, idx_hbm, out_hbm, idx_v, buf0, buf1, r0, r1, s0, s1):
        bufs, rsem, ssem = (buf0, buf1), (r0, r1), (s0, s1)
        worker = _sc_worker()
        base = worker * per_worker
        pltpu.sync_copy(idx_hbm.at[worker], idx_v)

        def fetch(j, b):
            return pltpu.make_async_copy(rows_hbm.at[pl.ds(base + j * SC_CHUNK, SC_CHUNK)], bufs[b], rsem[b])

        def send(j, k, b):
            return pltpu.make_async_copy(bufs[b], out_hbm.at[idx_v.at[j * TOP_K + k]], ssem[b])

        fetch(0, 0).start()
        for j in range(n_chunks):
            b = j % 2
            fetch(j, b).wait()
            if j + 1 < n_chunks:
                if j >= 1:
                    for k in range(TOP_K):
                        send(j - 1, k, 1 - b).wait()
                fetch(j + 1, 1 - b).start()
            for k in range(TOP_K):
                send(j, k, b).start()
        for j in range(max(n_chunks - 2, 0), n_chunks):
            for k in range(TOP_K):
                send(j, k, j % 2).wait()

    return scatter(rows, idx_w)


def _combine_body(*refs):
    yk_refs = refs[:TOP_K]
    w_ref, shr_ref, x1_ref, gt2_ref, gpost_ref, o_ref = refs[TOP_K:]
    tt = x1_ref.shape[0]
    w = w_ref[...]
    ssq = jnp.zeros((tt, 1), F32)
    for c in range(HALF // 128):
        c_lo = slice(c * 128, (c + 1) * 128)
        c_hi = slice(HALF + c * 128, HALF + (c + 1) * 128)
        y_lo = shr_ref[:, c_lo].astype(F32)
        y_hi = shr_ref[:, c_hi].astype(F32)
        for k in range(TOP_K):
            lo, hi = _unpack_halves(yk_refs[k][pl.ds(c, tt, stride=ROW_SUBLANES), :])
            y_lo = y_lo + w[:, k:k + 1] * lo
            y_hi = y_hi + w[:, k:k + 1] * hi
        ssq = ssq + jnp.sum(y_lo * y_lo, axis=-1, keepdims=True) + jnp.sum(y_hi * y_hi, axis=-1, keepdims=True)
        o_ref[:, c_lo] = y_lo
        o_ref[:, c_hi] = y_hi
    scale = lax.rsqrt(ssq / D_MODEL + EPS)
    o_ref[...] = x1_ref[...] + gt2_ref[0] * (o_ref[...] * scale * gpost_ref[...])


def _combine(yu, wts, shared, x1, gt2, g_post, seq):
    rows, d = x1.shape
    tiles = rows // COMB_TT
    tiles_per_b = seq // COMB_TT
    yk_specs = [pl.BlockSpec((COMB_TT * ROW_SUBLANES, 128), functools.partial(lambda i, k: (k * tiles + i, 0), k=k))
                for k in range(TOP_K)]
    return pl.pallas_call(
        _combine_body,
        grid=(tiles,),
        in_specs=yk_specs + [
            pl.BlockSpec((COMB_TT, TOP_K), lambda i: (i, 0)),
            pl.BlockSpec((COMB_TT, d), lambda i: (i, 0)),
            pl.BlockSpec((COMB_TT, d), lambda i: (i, 0)),
            pl.BlockSpec((1, 1, d), lambda i: (i // tiles_per_b, 0, 0)),
            _resident((1, d)),
        ],
        out_specs=pl.BlockSpec((COMB_TT, d), lambda i: (i, 0)),
        out_shape=jax.ShapeDtypeStruct((rows, d), F32),
        compiler_params=_cparams(("arbitrary",)),
        name="combine",
    )(*([yu] * TOP_K), wts, shared, x1, gt2, g_post)


def kernel(x, c, ctx, c_ctx, w_mod, b_mod, norm_mix_pre, norm_mix_post, norm_ffn_pre, norm_ffn_post, w_in, w_a2_fwd, b_a_fwd, w_a2_bwd, b_a_bwd, gla_norm, w_pool, pool_scale, w_out, w_router, router_bias, w_exp_gate, w_exp_up, w_exp_down, w_sh_gate, w_sh_up, w_sh_down):
    batch, seq, d = x.shape
    n_ctx = ctx.shape[1]
    assert w_mod.shape[0] == 1 and d == D_MODEL
    assert seq % SUPER == 0 and n_ctx % SUPER == 0 and seq % PROJ_TM == 0 and (batch * n_ctx) % PROJ_TM == 0
    rows = batch * seq

    mod_rows = 16
    c_all = jnp.concatenate([c, c_ctx[None, :], jnp.zeros((mod_rows - batch - 1, d), F32)], axis=0)
    mod_all = _modulation(c_all, w_mod[0], b_mod[0][None, :])
    sh1, sc1, gt1, sh2, sc2, gt2 = [m.reshape(batch, 1, d) for m in jnp.split(mod_all[:batch], 6, axis=-1)]
    csh1 = mod_all[batch, 0:d].reshape(1, 1, d)
    csc1 = mod_all[batch, d:2 * d].reshape(1, 1, d)

    kw, gw = GLA_KEY_WIDTH, GLA_WIDTH
    a0 = 2 * kw + 2 * gw
    w_in0 = w_in[0]
    w_main = jnp.concatenate([w_in0[:, :a0], w_in0[:, a0 + 2 * GLA_RANK:]], axis=1).astype(BF16)
    w_a = jnp.pad(w_in0[:, a0:a0 + 2 * GLA_RANK], ((0, 0), (0, 128 - 2 * GLA_RANK))).astype(BF16)
    w_ctx = jnp.concatenate([w_main[:, 2 * kw:2 * kw + gw], w_main[:, kw:2 * kw]], axis=1)
    w2f = jnp.pad(w_a2_fwd[0], ((0, 128 - GLA_RANK), (0, 0))).astype(BF16)
    w2b = jnp.pad(w_a2_bwd[0], ((GLA_RANK, 128 - 2 * GLA_RANK), (0, 0))).astype(BF16)
    g_mix_pre = norm_mix_pre[0][None, :]
    wr_hi, wr_lo = _bf16_terms(jnp.pad(w_router[0], ((0, 0), (0, 128 - N_EXPERTS))))

    u_ctx, a_ctx = _in_projection(ctx.reshape(batch * n_ctx, d), g_mix_pre, csc1, csh1, w_ctx, w_a,
                                  batch * n_ctx)
    u_lat, a_lat = _in_projection(x.reshape(rows, d), g_mix_pre, sc1, sh1, w_main, w_a, seq)

    y_gla = _gla(u_lat, a_lat, u_ctx, a_ctx, w2f, b_a_fwd[0][None, :], w2b, b_a_bwd[0][None, :],
                 gla_norm[0][None, :], batch, seq, n_ctx)
    y_pool = _pool_mixer(u_lat, _col_window_matrices(), w_pool[0].astype(BF16), pool_scale[0][None, :],
                         batch, seq)

    x1, h_packed, eidx_t, pos_t, wts_t, counts = _mix_and_route(
        y_gla, y_pool, x.reshape(rows, d), gt1, sc2, sh2, norm_mix_post[0][None, :],
        norm_ffn_pre[0][None, :], w_out[0].astype(BF16), wr_hi, wr_lo, router_bias[0][:, None], seq)
    shared = _shared_expert(h_packed, w_sh_gate[0].astype(BF16), w_sh_up[0].astype(BF16),
                            w_sh_down[0].astype(BF16))

    counts = counts[:, 0]
    padded = (counts + MOE_TILE - 1) // MOE_TILE * MOE_TILE
    pends = jnp.cumsum(padded)
    pstarts = pends - padded
    dest_t = _route_offsets(eidx_t, pos_t, pstarts.astype(I32))
    n_blocks = rows * TOP_K // MOE_TILE + N_EXPERTS
    n_used = (pends[-1] // MOE_TILE).astype(I32)
    blk = jnp.minimum(jnp.arange(n_blocks, dtype=I32), n_used - 1)
    block_expert = jnp.sum((blk * MOE_TILE)[:, None] >= pends[None, :], axis=1).astype(I32)
    block_expert = jnp.minimum(block_expert, N_EXPERTS - 1)
    e_ids = jnp.arange(N_EXPERTS, dtype=I32)
    is_block_expert = block_expert[:, None] == e_ids[None, :]
    per_block = lambda v: jnp.sum(jnp.where(is_block_expert, v[None, :], 0), axis=1).astype(I32)
    valid = jnp.clip(per_block(pstarts + counts) - blk * MOE_TILE, 0, MOE_TILE)
    valid = jnp.where(jnp.arange(n_blocks, dtype=I32) < n_used, valid, 0).astype(I32)
    has_rows = padded > 0
    later = jnp.where((e_ids[None, :] > e_ids[:, None]) & has_rows[None, :], e_ids[None, :], N_EXPERTS)
    next_e = jnp.min(later, axis=1)
    next_e = jnp.where(next_e == N_EXPERTS, e_ids, next_e)
    slot_e = (jnp.cumsum(has_rows.astype(I32)) - has_rows.astype(I32)) % 2

    xs = _sc_scatter_rows(h_packed.reshape(-1, ROW_SUBLANES, 128), dest_t, n_blocks * MOE_TILE)
    xs = xs.reshape(-1, 128)
    ys = _experts(xs, block_expert, n_used.reshape(1), valid, per_block(next_e), per_block(slot_e),
                  w_exp_gate[0], w_exp_up[0], w_exp_down[0])
    yu = _sc_gather_rows(ys.reshape(-1, ROW_SUBLANES, 128), dest_t.reshape(-1))
    out = _combine(yu.reshape(-1, 128), wts_t.T, shared, x1, gt2, norm_ffn_post[0][None, :], seq)
    return out.reshape(batch, seq, d)
```

```python
import functools

import numpy as np
import jax
import jax.numpy as jnp
from jax import lax
from jax.experimental import pallas as pl
from jax.experimental.pallas import tpu as pltpu
from jax.experimental.pallas import tpu_sc as plsc

F32 = jnp.float32
BF16 = jnp.bfloat16
I32 = jnp.int32
U32 = jnp.uint32
HIGHEST = lax.Precision.HIGHEST

D_MODEL = 2048
GRID_W = 64
GLA_HEADS = 4
GLA_DK = 128
GLA_DV = 256
GLA_KEY_WIDTH = GLA_HEADS * GLA_DK
GLA_WIDTH = GLA_HEADS * GLA_DV
GLA_RANK = 16
GLA_TAU = 16.0
GLA_CHUNK = 64
POOL_WIDTH = 1024
POOL_WINDOWS = (2, 4, 8, 16)
POOL_GROUP = 256
N_EXPERTS = 64
TOP_K = 8
N_GROUPS = 8
GROUP_SIZE = N_EXPERTS // N_GROUPS
TOPK_GROUPS = 4
D_EXPERT = 512
D_SHARED = 512
ROUTED_SCALE = 2.5
EPS = 1e-6

HALF = D_MODEL // 2
SUPER = 4 * GLA_CHUNK
GLA_HPS = 4
POOL_PAD = 8 * GRID_W
VMEM_LIMIT = 56 * 1024 * 1024

MOD_TN = 1024
PROJ_TM = 512
PROJ_TN = 512
MIX_TM = 512
MOE_TILE = 512
SHARED_TM = 1024
COMB_TT = 256
ROW_SUBLANES = 8
SC_CORES = 2
SC_WORKERS = 32
SC_CHUNK = 32
OFFS_TN = 2048


def _cparams(sem):
    return pltpu.CompilerParams(dimension_semantics=sem, vmem_limit_bytes=VMEM_LIMIT)


def _resident(shape):
    nd = len(shape)
    return pl.BlockSpec(shape, lambda *_: (0,) * nd, pipeline_mode=pl.Buffered(1))


def _silu(v):
    return v * jax.nn.sigmoid(v)


def _pack_halves(lo, hi):
    lo_b = lax.bitcast_convert_type(lo.astype(BF16).astype(F32), U32)
    hi_b = lax.bitcast_convert_type(hi.astype(BF16).astype(F32), U32)
    return (hi_b & jnp.uint32(0xFFFF0000)) | (lo_b >> 16)


def _unpack_halves(p):
    lo = lax.bitcast_convert_type(p << 16, F32)
    hi = lax.bitcast_convert_type(p & jnp.uint32(0xFFFF0000), F32)
    return lo, hi


def _bf16_terms(x):
    hi = lax.bitcast_convert_type(lax.bitcast_convert_type(x, U32) & jnp.uint32(0xFFFF0000), F32)
    return hi.astype(BF16), (x - hi).astype(BF16)


def _store_row_tiles(ref, packed):
    n = packed.shape[0]
    for c in range(HALF // 128):
        ref[pl.ds(c, n, stride=ROW_SUBLANES), :] = packed[:, c * 128:(c + 1) * 128]


def _load_row_tiles(ref):
    n = ref.shape[0] // ROW_SUBLANES
    return jnp.concatenate([ref[pl.ds(c, n, stride=ROW_SUBLANES), :] for c in range(HALF // 128)], axis=1)


def _mod_body(c_ref, w_ref, b_ref, o_ref):
    s = _silu(c_ref[...])
    o_ref[...] = jnp.dot(s, w_ref[...], preferred_element_type=F32, precision=HIGHEST) + b_ref[...]


def _modulation(c_all, w_mod, b_mod):
    rows, d = c_all.shape
    n = w_mod.shape[1]
    return pl.pallas_call(
        _mod_body,
        grid=(n // MOD_TN,),
        in_specs=[
            pl.BlockSpec((rows, d), lambda j: (0, 0)),
            pl.BlockSpec((d, MOD_TN), lambda j: (0, j)),
            pl.BlockSpec((1, MOD_TN), lambda j: (0, j)),
        ],
        out_specs=pl.BlockSpec((rows, MOD_TN), lambda j: (0, j)),
        out_shape=jax.ShapeDtypeStruct((rows, n), F32),
        compiler_params=_cparams(("arbitrary",)),
        name="modulation",
    )(c_all, w_mod, b_mod)


def _rms_scale(x):
    return lax.rsqrt(jnp.mean(x * x, axis=-1, keepdims=True) + EPS)


def _inproj_body(x_ref, g_ref, sc_ref, sh_ref, w_ref, wa_ref, o_ref, a_ref, *, n_main):
    x = x_ref[...]
    h = x * _rms_scale(x) * g_ref[...]
    h = h * (1.0 + sc_ref[0]) + sh_ref[0]
    hb = h.astype(BF16)
    for n in range(n_main // PROJ_TN):
        cols = slice(n * PROJ_TN, (n + 1) * PROJ_TN)
        o_ref[:, cols] = jnp.dot(hb, w_ref[:, cols], preferred_element_type=F32).astype(BF16)
    a_ref[...] = jnp.dot(hb, wa_ref[...], preferred_element_type=F32)


def _in_projection(x2d, gain, sc, sh, w_main, w_a, rows_per_mod):
    rows, d = x2d.shape
    n_main = w_main.shape[1]
    tiles_per_mod = rows_per_mod // PROJ_TM
    mod_map = lambda i: (i // tiles_per_mod, 0, 0)
    return pl.pallas_call(
        functools.partial(_inproj_body, n_main=n_main),
        grid=(rows // PROJ_TM,),
        in_specs=[
            pl.BlockSpec((PROJ_TM, d), lambda i: (i, 0)),
            _resident((1, d)),
            pl.BlockSpec((1, 1, d), mod_map),
            pl.BlockSpec((1, 1, d), mod_map),
            _resident((d, n_main)),
            _resident((d, 128)),
        ],
        out_specs=[
            pl.BlockSpec((PROJ_TM, n_main), lambda i: (i, 0)),
            pl.BlockSpec((PROJ_TM, 128), lambda i: (i, 0)),
        ],
        out_shape=[
            jax.ShapeDtypeStruct((rows, n_main), BF16),
            jax.ShapeDtypeStruct((rows, 128), F32),
        ],
        compiler_params=_cparams(("arbitrary",)),
        name="in_projection",
    )(x2d, gain, sc, sh, w_main, w_a)


def _log_sigmoid(z):
    return jnp.minimum(z, 0.0) - jnp.log1p(jnp.exp(-jnp.abs(z)))


def _gla_prep(q, k, a, w2, ba, tri, reverse):
    nc = SUPER // GLA_CHUNK
    z = jnp.dot(a.astype(BF16), w2, preferred_element_type=F32) + ba
    g = _log_sigmoid(z) * (1.0 / GLA_TAU)
    g_hi, g_lo = _bf16_terms(g)
    G = jnp.dot(tri, g_hi, preferred_element_type=F32) + jnp.dot(tri, g_lo, preferred_element_type=F32)
    G = G.reshape(nc, GLA_CHUNK, GLA_DK)
    end_row = 0 if reverse else GLA_CHUNK - 1
    mid_row = GLA_CHUNK - 1 - GLA_CHUNK // 2 if reverse else GLA_CHUNK // 2
    g_end = G[:, end_row:end_row + 1, :]
    g_mid = G[:, mid_row:mid_row + 1, :]
    k4 = k.astype(F32).reshape(nc, GLA_CHUNK, GLA_DK)
    dec = jnp.broadcast_to(jnp.exp(g_end), (nc, 8, GLA_DK)).reshape(nc * 8, GLA_DK)
    flat = lambda t: t.reshape(SUPER, GLA_DK).astype(BF16)
    if q is None:
        return None, None, None, flat(k4 * jnp.exp(g_end - G)), dec
    q4 = q.astype(F32).reshape(nc, GLA_CHUNK, GLA_DK) * (GLA_DK ** -0.5)
    qg = q4 * jnp.exp(G - g_mid)
    kg = k4 * jnp.exp(g_mid - G)
    qe = qg * jnp.exp(g_mid)
    kd = kg * jnp.exp(g_end - g_mid)
    return flat(qg), flat(kg), flat(qe), flat(kd), dec


def _gla_apply(qg, kg, qe, kd, dec, v, mask, st_ref, reverse):
    nc = SUPER // GLA_CHUNK
    o = None
    if qg is not None:
        att = lax.dot_general(qg, kg, (((1,), (1,)), ((), ())), preferred_element_type=F32)
        att = jnp.where(mask, att, 0.0).astype(BF16)
        o = jnp.dot(att, v, preferred_element_type=F32)
    outs = [None] * nc
    order = range(nc - 1, -1, -1) if reverse else range(nc)
    for c in order:
        rows = slice(c * GLA_CHUNK, (c + 1) * GLA_CHUNK)
        st = st_ref[...]
        if qg is not None:
            inter = lax.dot_general(qe[rows], st.astype(BF16), (((1,), (1,)), ((), ())),
                                    preferred_element_type=F32)
            outs[c] = o[rows] + inter
        upd = lax.dot_general(v[rows], kd[rows], (((0,), (0,)), ((), ())), preferred_element_type=F32)
        st_ref[...] = st * dec[8 * c:8 * c + 1, :] + upd
    if qg is None:
        return None
    return jnp.concatenate(outs, axis=0)


def _gla_body(q_ref, k_ref, v_ref, r_ref, a_ref, kc_ref, vc_ref, ac_ref,
              w2f_ref, baf_ref, w2b_ref, bab_ref, gn_ref, y_ref, o_acc, st, ops_a, ops_b, dec_a, dec_b,
              *, n_ctx):
    n_sup = q_ref.shape[0] // SUPER
    row = lax.broadcasted_iota(I32, (SUPER, SUPER), 0)
    col = lax.broadcasted_iota(I32, (SUPER, SUPER), 1)
    same_chunk = (row >> 6) == (col >> 6)
    mask_f = same_chunk & (col <= row)
    mask_b = same_chunk & (col >= row)
    tri_f = jnp.where(mask_f, 1.0, 0.0).astype(BF16)
    tri_b = jnp.where(mask_b, 1.0, 0.0).astype(BF16)
    heads = range(GLA_HPS)
    kcol = [slice(h * GLA_DK, (h + 1) * GLA_DK) for h in heads]
    vcol = [slice(h * GLA_DV, (h + 1) * GLA_DV) for h in heads]
    chains = []
    for h in heads:
        chains.append((h, False, (w2f_ref[:, kcol[h]], baf_ref[:, kcol[h]]), tri_f, mask_f, st.at[0, h]))
        chains.append((h, True, (w2b_ref[:, kcol[h]], bab_ref[:, kcol[h]]), tri_b, mask_b, st.at[1, h]))

    st[...] = jnp.zeros_like(st)
    n_csup = n_ctx // SUPER
    for s in range(n_csup):
        for h, reverse, gate, tri, mask, st_ref in chains:
            sc = n_csup - 1 - s if reverse else s
            rows = slice(sc * SUPER, (sc + 1) * SUPER)
            _, _, _, kd, dec = _gla_prep(None, kc_ref[rows, kcol[h]], ac_ref[rows, :], *gate, tri, reverse)
            _gla_apply(None, None, None, kd, dec, vc_ref[rows, vcol[h]], mask, st_ref, reverse)

    o_acc[...] = jnp.zeros_like(o_acc)

    def rows_of(i, reverse):
        sc = n_sup - 1 - i if reverse else i
        return pl.ds(pl.multiple_of(sc * SUPER, SUPER), SUPER)

    def prepare(i, ops, decs, ci):
        h, reverse, gate, tri, mask, st_ref = chains[ci]
        rows = rows_of(i, reverse)
        vals = _gla_prep(q_ref[rows, kcol[h]], k_ref[rows, kcol[h]], a_ref[rows, :], *gate, tri, reverse)
        for j in range(4):
            ops[ci, j] = vals[j]
        decs[ci] = vals[4]

    def apply(i, ops, decs, ci):
        h, reverse, gate, tri, mask, st_ref = chains[ci]
        rows = rows_of(i, reverse)
        out = _gla_apply(ops[ci, 0], ops[ci, 1], ops[ci, 2], ops[ci, 3], decs[ci],
                         v_ref[rows, vcol[h]], mask, st_ref, reverse)
        o_acc[rows, vcol[h]] += out

    n_chains = len(chains)
    for ci in range(n_chains):
        prepare(0, ops_a, dec_a, ci)

    def step(j, carry):
        i = 2 * j
        for ci in range(n_chains):
            prepare(i + 1, ops_b, dec_b, ci)
            apply(i, ops_a, dec_a, ci)
        nxt = jnp.minimum(i + 2, n_sup - 1)
        for ci in range(n_chains):
            prepare(nxt, ops_a, dec_a, ci)
            apply(i + 1, ops_b, dec_b, ci)
        return carry

    lax.fori_loop(0, n_sup // 2, step, 0)

    for h in heads:
        o = o_acc[:, vcol[h]]
        o = o * _rms_scale(o) * gn_ref[:, vcol[h]]
        y_ref[:, vcol[h]] = (o * _silu(r_ref[:, vcol[h]].astype(F32))).astype(BF16)


def _gla(u_lat, a_lat, u_ctx, a_ctx, w2f, baf, w2b, bab, gla_norm, batch, seq, n_ctx):
    groups = GLA_HEADS // GLA_HPS
    kw, vw = GLA_HPS * GLA_DK, GLA_HPS * GLA_DV
    kb = GLA_KEY_WIDTH // kw
    vb = 2 * GLA_KEY_WIDTH // vw
    rb = vb + groups
    assert GLA_KEY_WIDTH % kw == 0 and (2 * GLA_KEY_WIDTH) % vw == 0 and GLA_WIDTH % kw == 0
    ckb = GLA_WIDTH // kw
    return pl.pallas_call(
        functools.partial(_gla_body, n_ctx=n_ctx),
        grid=(batch, groups),
        in_specs=[
            pl.BlockSpec((seq, kw), lambda b, h: (b, h)),
            pl.BlockSpec((seq, kw), lambda b, h: (b, kb + h)),
            pl.BlockSpec((seq, vw), lambda b, h: (b, vb + h)),
            pl.BlockSpec((seq, vw), lambda b, h: (b, rb + h)),
            pl.BlockSpec((seq, 128), lambda b, h: (b, 0)),
            pl.BlockSpec((n_ctx, kw), lambda b, h: (b, ckb + h)),
            pl.BlockSpec((n_ctx, vw), lambda b, h: (b, h)),
            pl.BlockSpec((n_ctx, 128), lambda b, h: (b, 0)),
            pl.BlockSpec((128, kw), lambda b, h: (0, h)),
            pl.BlockSpec((1, kw), lambda b, h: (0, h)),
            pl.BlockSpec((128, kw), lambda b, h: (0, h)),
            pl.BlockSpec((1, kw), lambda b, h: (0, h)),
            pl.BlockSpec((1, vw), lambda b, h: (0, h)),
        ],
        out_specs=pl.BlockSpec((seq, vw), lambda b, h: (b, h)),
        out_shape=jax.ShapeDtypeStruct((batch * seq, GLA_WIDTH), BF16),
        scratch_shapes=[
            pltpu.VMEM((seq, vw), F32),
            pltpu.VMEM((2, GLA_HPS, GLA_DV, GLA_DK), F32),
            pltpu.VMEM((2 * GLA_HPS, 4, SUPER, GLA_DK), BF16),
            pltpu.VMEM((2 * GLA_HPS, 4, SUPER, GLA_DK), BF16),
            pltpu.VMEM((2 * GLA_HPS, 8 * (SUPER // GLA_CHUNK), GLA_DK), F32),
            pltpu.VMEM((2 * GLA_HPS, 8 * (SUPER // GLA_CHUNK), GLA_DK), F32),
        ],
        compiler_params=_cparams(("arbitrary", "arbitrary")),
        name="gla",
    )(u_lat, u_lat, u_lat, u_lat, a_lat, u_ctx, u_ctx, a_ctx, w2f, baf, w2b, bab, gla_norm)


def _col_window_matrices():
    t = np.arange(SUPER)
    r, c = t // GRID_W, t % GRID_W
    mats = []
    for w in POOL_WINDOWS:
        lo = np.maximum(c - w // 2, 0)[:, None]
        hi = np.minimum(c + w // 2, GRID_W)[:, None]
        m = (r[:, None] == r[None, :]) & (c[None, :] >= lo) & (c[None, :] < hi)
        mats.append(m.astype(np.float32))
    return jnp.asarray(np.stack(mats), dtype=BF16)


def _pool_body(p_ref, cw_ref, wp_ref, ps_ref, y_ref, pad_ref):
    seq = p_ref.shape[0]
    n_rows = seq // GRID_W
    zeros = jnp.zeros((POOL_PAD, POOL_GROUP), F32)
    pad_ref[0:POOL_PAD, :] = zeros
    pad_ref[POOL_PAD + seq:POOL_PAD + seq + POOL_PAD, :] = zeros
    t = lax.broadcasted_iota(I32, (seq, POOL_GROUP), 0)
    r = t >> 6
    c = t & (GRID_W - 1)
    for gi, w in enumerate(POOL_WINDOWS):
        cols = slice(gi * POOL_GROUP, (gi + 1) * POOL_GROUP)
        cw = cw_ref[gi]
        for j in range(seq // SUPER):
            rows = slice(j * SUPER, (j + 1) * SUPER)
            pad_ref[POOL_PAD + j * SUPER:POOL_PAD + (j + 1) * SUPER, :] = jnp.dot(
                cw, p_ref[rows, cols], preferred_element_type=F32)
        total = None
        for d in range(-(w // 2), w // 2):
            start = POOL_PAD + d * GRID_W
            part = pad_ref[start:start + seq, :]
            total = part if total is None else total + part
        cnt_r = jnp.minimum(r + w // 2, n_rows) - jnp.maximum(r - w // 2, 0)
        cnt_c = jnp.minimum(c + w // 2, GRID_W) - jnp.maximum(c - w // 2, 0)
        mean = total / (cnt_r * cnt_c).astype(F32)
        diff = (mean - p_ref[:, cols].astype(F32)).astype(BF16)
        y = jnp.dot(diff, wp_ref[gi], preferred_element_type=F32) * ps_ref[:, cols]
        y_ref[:, cols] = y.astype(BF16)


def _pool_mixer(u_lat, col_mats, w_pool, pool_scale, batch, seq):
    pb = (u_lat.shape[1] - POOL_WIDTH) // POOL_WIDTH
    ng = len(POOL_WINDOWS)
    return pl.pallas_call(
        _pool_body,
        grid=(batch,),
        in_specs=[
            pl.BlockSpec((seq, POOL_WIDTH), lambda b: (b, pb)),
            _resident((ng, SUPER, SUPER)),
            _resident((ng, POOL_GROUP, POOL_GROUP)),
            _resident((1, POOL_WIDTH)),
        ],
        out_specs=pl.BlockSpec((seq, POOL_WIDTH), lambda b: (b, 0)),
        out_shape=jax.ShapeDtypeStruct((batch * seq, POOL_WIDTH), BF16),
        scratch_shapes=[pltpu.VMEM((seq + 2 * POOL_PAD, POOL_GROUP), F32)],
        compiler_params=_cparams(("arbitrary",)),
        name="pool_mixer",
    )(u_lat, col_mats, w_pool, pool_scale)


def _first_index(hit, iota, size, axis):
    return jnp.min(jnp.where(hit, iota, size), axis=axis, keepdims=True)


def _mix_body(yg_ref, yp_ref, x_ref, gt1_ref, sc2_ref, sh2_ref, gpost_ref, gpre_ref, wout_ref,
              wrh_ref, wrl_ref, rb_ref, upper_ref,
              x1_ref, hp_ref, eidx_ref, pos_ref, wts_ref, cnt_ref, run_ref, wrow_ref, y_scr):
    tm = x_ref.shape[0]
    neg_inf = jnp.float32(-jnp.inf)
    step = pl.program_id(0)

    @pl.when(step == 0)
    def _():
        run_ref[...] = jnp.zeros_like(run_ref)
        wrow_ref[...] = jnp.zeros_like(wrow_ref)
        y_scr[...] = jnp.zeros_like(y_scr)

    y = y_scr[...]
    y_new = jnp.dot(yg_ref[...], wout_ref[0:GLA_WIDTH, :], preferred_element_type=F32)
    y_scr[...] = y_new + jnp.dot(yp_ref[...], wout_ref[GLA_WIDTH:, :], preferred_element_type=F32)
    x1 = x_ref[...] + gt1_ref[0] * (y * _rms_scale(y) * gpost_ref[...])
    x1_ref[...] = x1
    h = x1 * _rms_scale(x1) * gpre_ref[...]
    h = h * (1.0 + sc2_ref[0]) + sh2_ref[0]
    _store_row_tiles(hp_ref, _pack_halves(h[:, :HALF], h[:, HALF:]))

    h_hi, h_lo = _bf16_terms(h)
    lt = jnp.dot(h_hi, wrh_ref[...], preferred_element_type=F32)
    lt = lt + jnp.dot(h_hi, wrl_ref[...], preferred_element_type=F32)
    lt = lt + jnp.dot(h_lo, wrh_ref[...], preferred_element_type=F32)
    logits = lt.T[0:N_EXPERTS, :]
    scores = jax.nn.sigmoid(logits)
    sel = scores + rb_ref[...]
    shape3 = (N_GROUPS, GROUP_SIZE, tm)
    sel3 = sel.reshape(shape3)
    i_in = lax.broadcasted_iota(I32, shape3, 1).astype(F32)
    m1 = jnp.max(sel3, axis=1, keepdims=True)
    f1 = _first_index(sel3 == m1, i_in, float(GROUP_SIZE), 1)
    m2 = jnp.max(jnp.where(i_in == f1, neg_inf, sel3), axis=1, keepdims=True)
    grp = jnp.broadcast_to(m1 + m2, shape3).reshape(N_EXPERTS, tm)
    i_e = lax.broadcasted_iota(I32, (N_EXPERTS, tm), 0)
    i_grp = (i_e >> 3).astype(F32)
    i_e = i_e.astype(F32)
    allowed = jnp.zeros((N_EXPERTS, tm), F32)
    for _ in range(TOPK_GROUPS):
        m = jnp.max(grp, axis=0, keepdims=True)
        pick = i_grp == _first_index(grp == m, i_grp, float(N_GROUPS), 0)
        allowed = jnp.where(pick, 1.0, allowed)
        grp = jnp.where(pick, neg_inf, grp)
    cand = jnp.where(allowed > 0.0, sel, neg_inf)
    onehot = jnp.zeros((N_EXPERTS, tm), F32)
    picks, wts = [], []
    for k in range(TOP_K):
        m = jnp.max(cand, axis=0, keepdims=True)
        f = _first_index(cand == m, i_e, float(N_EXPERTS), 0)
        pick = i_e == f
        picks.append(pick)
        eidx_ref[k:k + 1, :] = f.astype(I32)
        wts.append(jnp.sum(jnp.where(pick, scores, 0.0), axis=0, keepdims=True))
        onehot = jnp.where(pick, 1.0, onehot)
        cand = jnp.where(pick, neg_inf, cand)
    w_sum = wts[0]
    for k in range(1, TOP_K):
        w_sum = w_sum + wts[k]
    for k in range(TOP_K):
        wrow_ref[k:k + 1, :] = wts[k] / w_sum * ROUTED_SCALE
    wts_ref[...] = wrow_ref[...].T

    before = jnp.dot(onehot.astype(BF16), upper_ref[...], preferred_element_type=F32)
    before = before + run_ref[:, 0:1]
    for k in range(TOP_K):
        pos_ref[k:k + 1, :] = jnp.sum(jnp.where(picks[k], before, 0.0), axis=0, keepdims=True).astype(I32)
    counted = jnp.where(step > 0, 1.0, 0.0)
    run_ref[...] = run_ref[...] + counted * jnp.sum(onehot, axis=1, keepdims=True)
    cnt_ref[...] = run_ref[...].astype(I32)


def _mix_and_route(y_gla, y_pool, x2d, gt1, sc2, sh2, g_post, g_pre, w_out, wr_hi, wr_lo, router_bias, seq):
    rows, d = x2d.shape
    tiles = rows // MIX_TM
    tiles_per_b = seq // MIX_TM
    proj = lambda i: (jnp.minimum(i, tiles - 1), 0)
    bmap = lambda i: (jnp.maximum(i - 1, 0) // tiles_per_b, 0, 0)
    rmap = lambda i: (jnp.maximum(i - 1, 0), 0)
    tmap = lambda i: (0, jnp.maximum(i - 1, 0))
    upper = jnp.asarray(np.triu(np.ones((MIX_TM, MIX_TM), np.float32), 1), dtype=BF16)
    return pl.pallas_call(
        _mix_body,
        grid=(tiles + 1,),
        in_specs=[
            pl.BlockSpec((MIX_TM, GLA_WIDTH), proj),
            pl.BlockSpec((MIX_TM, POOL_WIDTH), proj),
            pl.BlockSpec((MIX_TM, d), rmap),
            pl.BlockSpec((1, 1, d), bmap),
            pl.BlockSpec((1, 1, d), bmap),
            pl.BlockSpec((1, 1, d), bmap),
            _resident((1, d)),
            _resident((1, d)),
            _resident((d, d)),
            _resident((d, 128)),
            _resident((d, 128)),
            _resident((N_EXPERTS, 1)),
            _resident((MIX_TM, MIX_TM)),
        ],
        out_specs=[
            pl.BlockSpec((MIX_TM, d), rmap),
            pl.BlockSpec((MIX_TM * ROW_SUBLANES, 128), rmap),
            pl.BlockSpec((TOP_K, MIX_TM), tmap),
            pl.BlockSpec((TOP_K, MIX_TM), tmap),
            pl.BlockSpec((MIX_TM, 128), rmap),
            pl.BlockSpec((N_EXPERTS, 128), lambda i: (0, 0)),
        ],
        out_shape=[
            jax.ShapeDtypeStruct((rows, d), F32),
            jax.ShapeDtypeStruct((rows * ROW_SUBLANES, 128), U32),
            jax.ShapeDtypeStruct((TOP_K, rows), I32),
            jax.ShapeDtypeStruct((TOP_K, rows), I32),
            jax.ShapeDtypeStruct((rows, 128), F32),
            jax.ShapeDtypeStruct((N_EXPERTS, 128), I32),
        ],
        scratch_shapes=[pltpu.VMEM((N_EXPERTS, 128), F32), pltpu.VMEM((128, MIX_TM), F32),
                        pltpu.VMEM((MIX_TM, d), F32)],
        compiler_params=_cparams(("arbitrary",)),
        name="mix_and_route",
    )(y_gla, y_pool, x2d, gt1, sc2, sh2, g_post, g_pre, w_out, wr_hi, wr_lo, router_bias, upper)


def _swiglu(lo, hi, wg_ref, wu_ref, wd_ref):
    g = jnp.dot(lo, wg_ref[:HALF, :], preferred_element_type=F32)
    g = g + jnp.dot(hi, wg_ref[HALF:, :], preferred_element_type=F32)
    u = jnp.dot(lo, wu_ref[:HALF, :], preferred_element_type=F32)
    u = u + jnp.dot(hi, wu_ref[HALF:, :], preferred_element_type=F32)
    act = (_silu(g) * u).astype(BF16)
    return jnp.dot(act, wd_ref[...], preferred_element_type=F32)


def _shared_body(hp_ref, wg_ref, wu_ref, wd_ref, o_ref):
    lo, hi = _unpack_halves(_load_row_tiles(hp_ref))
    o_ref[...] = _swiglu(lo.astype(BF16), hi.astype(BF16), wg_ref, wu_ref, wd_ref).astype(BF16)


def _shared_expert(h_packed, w_sg, w_su, w_sd):
    d, ds = w_sg.shape
    rows = h_packed.shape[0] // ROW_SUBLANES
    return pl.pallas_call(
        _shared_body,
        grid=(rows // SHARED_TM,),
        in_specs=[
            pl.BlockSpec((SHARED_TM * ROW_SUBLANES, 128), lambda i: (i, 0)),
            _resident((d, ds)),
            _resident((d, ds)),
            _resident((ds, d)),
        ],
        out_specs=pl.BlockSpec((SHARED_TM, d), lambda i: (i, 0)),
        out_shape=jax.ShapeDtypeStruct((rows, d), BF16),
        compiler_params=_cparams(("arbitrary",)),
        name="shared_expert",
    )(h_packed, w_sg, w_su, w_sd)


def _offsets_body(ps_ref, e_ref, p_ref, d_ref):
    e = e_ref[...]
    d = p_ref[...]
    for x in range(N_EXPERTS):
        d = d + jnp.where(e == x, ps_ref[x], 0)
    d_ref[...] = d


def _route_offsets(eidx_t, pos_t, pstarts):
    k, rows = eidx_t.shape
    spec = pl.BlockSpec((k, OFFS_TN), lambda i, ps: (0, i))
    grid_spec = pltpu.PrefetchScalarGridSpec(
        num_scalar_prefetch=1, grid=(rows // OFFS_TN,), in_specs=[spec, spec], out_specs=spec)
    return pl.pallas_call(
        _offsets_body,
        grid_spec=grid_spec,
        out_shape=jax.ShapeDtypeStruct((k, rows), I32),
        compiler_params=_cparams(("arbitrary",)),
        name="route_offsets",
    )(pstarts, eidx_t, pos_t)


def _expert_body(be_ref, nu_ref, valid_ref, nxt_ref, slot_ref, x_ref, wg_hbm, wu_hbm, wd_hbm, y_ref,
                 wg_f, wu_f, wd_f, wg_b, wu_b, wd_b, sems):
    i = pl.program_id(0)
    valid = valid_ref[i]
    expert = be_ref[i]
    slot = slot_ref[i]

    def weight_copies(e, s):
        return [pltpu.make_async_copy(src.at[e], dst.at[s], sems.at[s, j])
                for j, (src, dst) in enumerate(((wg_hbm, wg_f), (wu_hbm, wu_f), (wd_hbm, wd_f)))]

    @pl.when(i == 0)
    def _():
        for cp in weight_copies(expert, slot):
            cp.start()

    @pl.when((i == 0) | (expert != be_ref[jnp.maximum(i - 1, 0)]))
    def _():
        for cp in weight_copies(expert, slot):
            cp.wait()

        @pl.when(nxt_ref[i] != expert)
        def _():
            for cp in weight_copies(nxt_ref[i], 1 - slot):
                cp.start()

        wg_b[...] = wg_f[slot].astype(BF16)
        wu_b[...] = wu_f[slot].astype(BF16)
        wd_b[...] = wd_f[slot].astype(BF16)

    def compute(n_rows):
        tiles = pl.ds(0, n_rows * ROW_SUBLANES)
        lo, hi = _unpack_halves(_load_row_tiles(x_ref.at[tiles]))
        y = _swiglu(lo.astype(BF16), hi.astype(BF16), wg_b, wu_b, wd_b)
        _store_row_tiles(y_ref.at[tiles], _pack_halves(y[:, :HALF], y[:, HALF:]))

    quarter = MOE_TILE // 4
    for n in range(1, 5):
        @pl.when((valid > (n - 1) * quarter) & (valid <= n * quarter))
        def _(n=n):
            compute(n * quarter)


def _experts(xs, block_expert, n_used, valid, next_expert, slot, w_eg, w_eu, w_ed):
    n_rows = xs.shape[0] // ROW_SUBLANES
    n_blocks = n_rows // MOE_TILE
    d, de = w_eg.shape[1], w_eg.shape[2]
    row_map = lambda i, be, nu, *_: (jnp.minimum(i, nu[0] - 1), 0)
    grid_spec = pltpu.PrefetchScalarGridSpec(
        num_scalar_prefetch=5,
        grid=(n_blocks,),
        in_specs=[
            pl.BlockSpec((MOE_TILE * ROW_SUBLANES, 128), row_map),
            pl.BlockSpec(memory_space=pl.ANY),
            pl.BlockSpec(memory_space=pl.ANY),
            pl.BlockSpec(memory_space=pl.ANY),
        ],
        out_specs=pl.BlockSpec((MOE_TILE * ROW_SUBLANES, 128), row_map),
        scratch_shapes=[
            pltpu.VMEM((2, d, de), F32),
            pltpu.VMEM((2, d, de), F32),
            pltpu.VMEM((2, de, d), F32),
            pltpu.VMEM((d, de), BF16),
            pltpu.VMEM((d, de), BF16),
            pltpu.VMEM((de, d), BF16),
            pltpu.SemaphoreType.DMA((2, 3)),
        ],
    )
    return pl.pallas_call(
        _expert_body,
        grid_spec=grid_spec,
        out_shape=jax.ShapeDtypeStruct((n_rows * ROW_SUBLANES, 128), U32),
        compiler_params=_cparams(("arbitrary",)),
        name="experts",
    )(block_expert, n_used, valid, next_expert, slot, xs, w_eg, w_eu, w_ed)


def _sc_mesh():
    return plsc.VectorSubcoreMesh(core_axis_name="c", subcore_axis_name="s")


def _sc_worker():
    return lax.axis_index("s") * SC_CORES + lax.axis_index("c")


def _sc_gather_rows(table, idx):
    n_idx = idx.shape[0]
    per_worker = n_idx // SC_WORKERS
    n_chunks = per_worker // SC_CHUNK
    assert per_worker * SC_WORKERS == n_idx and n_chunks * SC_CHUNK == per_worker and n_chunks % 2 == 0
    row_shape = table.shape[1:]

    @functools.partial(
        pl.kernel, mesh=_sc_mesh(),
        out_type=jax.ShapeDtypeStruct((n_idx,) + row_shape, table.dtype),
        scratch_types=[
            pltpu.VMEM((per_worker,), I32),
            pltpu.VMEM((SC_CHUNK,) + row_shape, table.dtype),
            pltpu.VMEM((SC_CHUNK,) + row_shape, table.dtype),
        ] + [pltpu.SemaphoreType.DMA] * 4,
    )
    def gather(table_hbm, idx_hbm, out_hbm, idx_v, buf0, buf1, g0, g1, w0, w1):
        bufs, gsem, wsem = (buf0, buf1), (g0, g1), (w0, w1)
        base = _sc_worker() * per_worker
        pltpu.sync_copy(idx_hbm.at[pl.ds(base, per_worker)], idx_v)

        def fetch(j, b):
            return pltpu.make_async_copy(table_hbm.at[idx_v.at[pl.ds(j * SC_CHUNK, SC_CHUNK)]], bufs[b], gsem[b])

        def flush(j, b):
            return pltpu.make_async_copy(bufs[b], out_hbm.at[pl.ds(base + j * SC_CHUNK, SC_CHUNK)], wsem[b])

        fetch(0, 0).start()
        fetch(0, 0).wait()
        fetch(1, 1).start()
        flush(0, 0).start()

        @pl.loop(1, n_chunks - 1, step=2)
        def _(j):
            for off in range(2):
                jj, b = j + off, (1 + off) % 2
                fetch(jj, b).wait()
                flush(jj - 1, 1 - b).wait()
                fetch(jj + 1, 1 - b).start()
                flush(jj, b).start()

        fetch(n_chunks - 1, 1).wait()
        flush(n_chunks - 1, 1).start()
        flush(n_chunks - 2, 0).wait()
        flush(n_chunks - 1, 1).wait()

    return gather(table, idx)


def _sc_scatter_rows(rows, dest_t, n_out):
    n_rows = rows.shape[0]
    per_worker = n_rows // SC_WORKERS
    n_chunks = per_worker // SC_CHUNK
    assert per_worker * SC_WORKERS == n_rows and n_chunks * SC_CHUNK == per_worker
    row_shape = rows.shape[1:]
    idx_w = dest_t.reshape(TOP_K, SC_WORKERS, n_chunks, SC_CHUNK).transpose(1, 2, 0, 3)
    idx_w = idx_w.reshape(SC_WORKERS, n_chunks * TOP_K, SC_CHUNK)

    @functools.partial(
        pl.kernel, mesh=_sc_mesh(),
        out_type=jax.ShapeDtypeStruct((n_out,) + row_shape, rows.dtype),
        scratch_types=[
            pltpu.VMEM((n_chunks * TOP_K, SC_CHUNK), I32),
            pltpu.VMEM((SC_CHUNK,) + row_shape, rows.dtype),
            pltpu.VMEM((SC_CHUNK,) + row_shape, rows.dtype),
        ] + [pltpu.SemaphoreType.DMA] * 4,
    )
    def scatter(rows_hbm, idx_hbm, out_hbm, idx_v, buf0, buf1, r0, r1, s0, s1):
        bufs, rsem, ssem = (buf0, buf1), (r0, r1), (s0, s1)
        worker = _sc_worker()
        base = worker * per_worker
        pltpu.sync_copy(idx_hbm.at[worker], idx_v)

        def fetch(j, b):
            return pltpu.make_async_copy(rows_hbm.at[pl.ds(base + j * SC_CHUNK, SC_CHUNK)], bufs[b], rsem[b])

        def send(j, k, b):
            return pltpu.make_async_copy(bufs[b], out_hbm.at[idx_v.at[j * TOP_K + k]], ssem[b])

        fetch(0, 0).start()
        for j in range(n_chunks):
            b = j % 2
            fetch(j, b).wait()
            if j + 1 < n_chunks:
                if j >= 1:
                    for k in range(TOP_K):
                        send(j - 1, k, 1 - b).wait()
                fetch(j + 1, 1 - b).start()
            for k in range(TOP_K):
                send(j, k, b).start()
        for j in range(max(n_chunks - 2, 0), n_chunks):
            for k in range(TOP_K):
                send(j, k, j % 2).wait()

    return scatter(rows, idx_w)


def _combine_body(*refs):
    yk_refs = refs[:TOP_K]
    w_ref, shr_ref, x1_ref, gt2_ref, gpost_ref, o_ref = refs[TOP_K:]
    tt = x1_ref.shape[0]
    w = w_ref[...]
    ssq = jnp.zeros((tt, 1), F32)
    for c in range(HALF // 128):
        c_lo = slice(c * 128, (c + 1) * 128)
        c_hi = slice(HALF + c * 128, HALF + (c + 1) * 128)
        y_lo = shr_ref[:, c_lo].astype(F32)
        y_hi = shr_ref[:, c_hi].astype(F32)
        for k in range(TOP_K):
            lo, hi = _unpack_halves(yk_refs[k][pl.ds(c, tt, stride=ROW_SUBLANES), :])
            y_lo = y_lo + w[:, k:k + 1] * lo
            y_hi = y_hi + w[:, k:k + 1] * hi
        ssq = ssq + jnp.sum(y_lo * y_lo, axis=-1, keepdims=True) + jnp.sum(y_hi * y_hi, axis=-1, keepdims=True)
        o_ref[:, c_lo] = y_lo
        o_ref[:, c_hi] = y_hi
    scale = lax.rsqrt(ssq / D_MODEL + EPS)
    o_ref[...] = x1_ref[...] + gt2_ref[0] * (o_ref[...] * scale * gpost_ref[...])


def _combine(yu, wts, shared, x1, gt2, g_post, seq):
    rows, d = x1.shape
    tiles = rows // COMB_TT
    tiles_per_b = seq // COMB_TT
    yk_specs = [pl.BlockSpec((COMB_TT * ROW_SUBLANES, 128), functools.partial(lambda i, k: (k * tiles + i, 0), k=k))
                for k in range(TOP_K)]
    return pl.pallas_call(
        _combine_body,
        grid=(tiles,),
        in_specs=yk_specs + [
            pl.BlockSpec((COMB_TT, 128), lambda i: (i, 0)),
            pl.BlockSpec((COMB_TT, d), lambda i: (i, 0)),
            pl.BlockSpec((COMB_TT, d), lambda i: (i, 0)),
            pl.BlockSpec((1, 1, d), lambda i: (i // tiles_per_b, 0, 0)),
            _resident((1, d)),
        ],
        out_specs=pl.BlockSpec((COMB_TT, d), lambda i: (i, 0)),
        out_shape=jax.ShapeDtypeStruct((rows, d), F32),
        compiler_params=_cparams(("arbitrary",)),
        name="combine",
    )(*([yu] * TOP_K), wts, shared, x1, gt2, g_post)


def kernel(x, c, ctx, c_ctx, w_mod, b_mod, norm_mix_pre, norm_mix_post, norm_ffn_pre, norm_ffn_post, w_in, w_a2_fwd, b_a_fwd, w_a2_bwd, b_a_bwd, gla_norm, w_pool, pool_scale, w_out, w_router, router_bias, w_exp_gate, w_exp_up, w_exp_down, w_sh_gate, w_sh_up, w_sh_down):
    batch, seq, d = x.shape
    n_ctx = ctx.shape[1]
    assert w_mod.shape[0] == 1 and d == D_MODEL
    assert seq % (2 * SUPER) == 0 and n_ctx % SUPER == 0 and seq % PROJ_TM == 0 and (batch * n_ctx) % PROJ_TM == 0
    rows = batch * seq

    mod_rows = 16
    c_all = jnp.concatenate([c, c_ctx[None, :], jnp.zeros((mod_rows - batch - 1, d), F32)], axis=0)
    mod_all = _modulation(c_all, w_mod[0], b_mod[0][None, :])
    sh1, sc1, gt1, sh2, sc2, gt2 = [m.reshape(batch, 1, d) for m in jnp.split(mod_all[:batch], 6, axis=-1)]
    csh1 = mod_all[batch, 0:d].reshape(1, 1, d)
    csc1 = mod_all[batch, d:2 * d].reshape(1, 1, d)

    kw, gw = GLA_KEY_WIDTH, GLA_WIDTH
    a0 = 2 * kw + 2 * gw
    w_in0 = w_in[0]
    w_main = jnp.concatenate([w_in0[:, :a0], w_in0[:, a0 + 2 * GLA_RANK:]], axis=1).astype(BF16)
    w_a = jnp.pad(w_in0[:, a0:a0 + 2 * GLA_RANK], ((0, 0), (0, 128 - 2 * GLA_RANK))).astype(BF16)
    w_ctx = jnp.concatenate([w_main[:, 2 * kw:2 * kw + gw], w_main[:, kw:2 * kw]], axis=1)
    w2f = jnp.pad(w_a2_fwd[0], ((0, 128 - GLA_RANK), (0, 0))).astype(BF16)
    w2b = jnp.pad(w_a2_bwd[0], ((GLA_RANK, 128 - 2 * GLA_RANK), (0, 0))).astype(BF16)
    g_mix_pre = norm_mix_pre[0][None, :]
    wr_hi, wr_lo = _bf16_terms(jnp.pad(w_router[0], ((0, 0), (0, 128 - N_EXPERTS))))

    u_ctx, a_ctx = _in_projection(ctx.reshape(batch * n_ctx, d), g_mix_pre, csc1, csh1, w_ctx, w_a,
                                  batch * n_ctx)
    u_lat, a_lat = _in_projection(x.reshape(rows, d), g_mix_pre, sc1, sh1, w_main, w_a, seq)

    y_gla = _gla(u_lat, a_lat, u_ctx, a_ctx, w2f, b_a_fwd[0][None, :], w2b, b_a_bwd[0][None, :],
                 gla_norm[0][None, :], batch, seq, n_ctx)
    y_pool = _pool_mixer(u_lat, _col_window_matrices(), w_pool[0].astype(BF16), pool_scale[0][None, :],
                         batch, seq)

    x1, h_packed, eidx_t, pos_t, wts, counts = _mix_and_route(
        y_gla, y_pool, x.reshape(rows, d), gt1, sc2, sh2, norm_mix_post[0][None, :],
        norm_ffn_pre[0][None, :], w_out[0].astype(BF16), wr_hi, wr_lo, router_bias[0][:, None], seq)
    shared = _shared_expert(h_packed, w_sh_gate[0].astype(BF16), w_sh_up[0].astype(BF16),
                            w_sh_down[0].astype(BF16))

    counts = counts[:, 0]
    padded = (counts + MOE_TILE - 1) // MOE_TILE * MOE_TILE
    pends = jnp.cumsum(padded)
    pstarts = pends - padded
    dest_t = _route_offsets(eidx_t, pos_t, pstarts.astype(I32))
    n_blocks = rows * TOP_K // MOE_TILE + N_EXPERTS
    n_used = (pends[-1] // MOE_TILE).astype(I32)
    blk = jnp.minimum(jnp.arange(n_blocks, dtype=I32), n_used - 1)
    block_expert = jnp.sum((blk * MOE_TILE)[:, None] >= pends[None, :], axis=1).astype(I32)
    block_expert = jnp.minimum(block_expert, N_EXPERTS - 1)
    e_ids = jnp.arange(N_EXPERTS, dtype=I32)
    is_block_expert = block_expert[:, None] == e_ids[None, :]
    per_block = lambda v: jnp.sum(jnp.where(is_block_expert, v[None, :], 0), axis=1).astype(I32)
    valid = jnp.clip(per_block(pstarts + counts) - blk * MOE_TILE, 0, MOE_TILE)
    valid = jnp.where(jnp.arange(n_blocks, dtype=I32) < n_used, valid, 0).astype(I32)
    has_rows = padded > 0
    later = jnp.where((e_ids[None, :] > e_ids[:, None]) & has_rows[None, :], e_ids[None, :], N_EXPERTS)
    next_e = jnp.min(later, axis=1)
    next_e = jnp.where(next_e == N_EXPERTS, e_ids, next_e)
    slot_e = (jnp.cumsum(has_rows.astype(I32)) - has_rows.astype(I32)) % 2

    xs = _sc_scatter_rows(h_packed.reshape(-1, ROW_SUBLANES, 128), dest_t, n_blocks * MOE_TILE)
    xs = xs.reshape(-1, 128)
    ys = _experts(xs, block_expert, n_used.reshape(1), valid, per_block(next_e), per_block(slot_e),
                  w_exp_gate[0], w_exp_up[0], w_exp_down[0])
    yu = _sc_gather_rows(ys.reshape(-1, ROW_SUBLANES, 128), dest_t.reshape(-1))
    out = _combine(yu.reshape(-1, 128), wts, shared, x1, gt2, norm_ffn_post[0][None, :], seq)
    return out.reshape(batch, seq, d)
```

```python
import functools

import numpy as np
import jax
import jax.numpy as jnp
from jax import lax
from jax.experimental import pallas as pl
from jax.experimental.pallas import tpu as pltpu
from jax.experimental.pallas import tpu_sc as plsc

F32 = jnp.float32
BF16 = jnp.bfloat16
I32 = jnp.int32
U32 = jnp.uint32
HIGHEST = lax.Precision.HIGHEST

D_MODEL = 2048
GRID_W = 64
GLA_HEADS = 4
GLA_DK = 128
GLA_DV = 256
GLA_KEY_WIDTH = GLA_HEADS * GLA_DK
GLA_WIDTH = GLA_HEADS * GLA_DV
GLA_RANK = 16
GLA_TAU = 16.0
GLA_CHUNK = 64
POOL_WIDTH = 1024
POOL_WINDOWS = (2, 4, 8, 16)
POOL_GROUP = 256
N_EXPERTS = 64
TOP_K = 8
N_GROUPS = 8
GROUP_SIZE = N_EXPERTS // N_GROUPS
TOPK_GROUPS = 4
D_EXPERT = 512
D_SHARED = 512
ROUTED_SCALE = 2.5
EPS = 1e-6

HALF = D_MODEL // 2
SUPER = 4 * GLA_CHUNK
GLA_HPS = 4
POOL_PAD = 8 * GRID_W
VMEM_LIMIT = 56 * 1024 * 1024

MOD_TN = 1024
PROJ_TM = 512
PROJ_TN = 512
MIX_TM = 512
MOE_TILE = 512
SHARED_TM = 1024
COMB_TT = 256
ROW_SUBLANES = 8
SC_CORES = 2
SC_WORKERS = 32
SC_CHUNK = 32
OFFS_TN = 2048


def _cparams(sem):
    return pltpu.CompilerParams(dimension_semantics=sem, vmem_limit_bytes=VMEM_LIMIT)


def _resident(shape):
    nd = len(shape)
    return pl.BlockSpec(shape, lambda *_: (0,) * nd, pipeline_mode=pl.Buffered(1))


def _silu(v):
    return v * jax.nn.sigmoid(v)


def _pack_halves(lo, hi):
    lo_b = lax.bitcast_convert_type(lo.astype(BF16).astype(F32), U32)
    hi_b = lax.bitcast_convert_type(hi.astype(BF16).astype(F32), U32)
    return (hi_b & jnp.uint32(0xFFFF0000)) | (lo_b >> 16)


def _unpack_halves(p):
    lo = lax.bitcast_convert_type(p << 16, F32)
    hi = lax.bitcast_convert_type(p & jnp.uint32(0xFFFF0000), F32)
    return lo, hi


def _bf16_terms(x):
    hi = lax.bitcast_convert_type(lax.bitcast_convert_type(x, U32) & jnp.uint32(0xFFFF0000), F32)
    return hi.astype(BF16), (x - hi).astype(BF16)


def _store_row_tiles(ref, packed):
    n = packed.shape[0]
    for c in range(HALF // 128):
        ref[pl.ds(c, n, stride=ROW_SUBLANES), :] = packed[:, c * 128:(c + 1) * 128]


def _load_row_tiles(ref):
    n = ref.shape[0] // ROW_SUBLANES
    return jnp.concatenate([ref[pl.ds(c, n, stride=ROW_SUBLANES), :] for c in range(HALF // 128)], axis=1)


def _mod_body(c_ref, w_ref, b_ref, o_ref):
    s = _silu(c_ref[...])
    o_ref[...] = jnp.dot(s, w_ref[...], preferred_element_type=F32, precision=HIGHEST) + b_ref[...]


def _modulation(c_all, w_mod, b_mod):
    rows, d = c_all.shape
    n = w_mod.shape[1]
    return pl.pallas_call(
        _mod_body,
        grid=(n // MOD_TN,),
        in_specs=[
            pl.BlockSpec((rows, d), lambda j: (0, 0)),
            pl.BlockSpec((d, MOD_TN), lambda j: (0, j)),
            pl.BlockSpec((1, MOD_TN), lambda j: (0, j)),
        ],
        out_specs=pl.BlockSpec((rows, MOD_TN), lambda j: (0, j)),
        out_shape=jax.ShapeDtypeStruct((rows, n), F32),
        compiler_params=_cparams(("arbitrary",)),
        name="modulation",
    )(c_all, w_mod, b_mod)


def _rms_scale(x):
    return lax.rsqrt(jnp.mean(x * x, axis=-1, keepdims=True) + EPS)


def _inproj_body(x_ref, g_ref, sc_ref, sh_ref, w_ref, wa_ref, o_ref, a_ref, *, n_main):
    x = x_ref[...]
    h = x * _rms_scale(x) * g_ref[...]
    h = h * (1.0 + sc_ref[0]) + sh_ref[0]
    hb = h.astype(BF16)
    for n in range(n_main // PROJ_TN):
        cols = slice(n * PROJ_TN, (n + 1) * PROJ_TN)
        o_ref[:, cols] = jnp.dot(hb, w_ref[:, cols], preferred_element_type=F32).astype(BF16)
    a_ref[...] = jnp.dot(hb, wa_ref[...], preferred_element_type=F32)


def _in_projection(x2d, gain, sc, sh, w_main, w_a, rows_per_mod):
    rows, d = x2d.shape
    n_main = w_main.shape[1]
    tiles_per_mod = rows_per_mod // PROJ_TM
    mod_map = lambda i: (i // tiles_per_mod, 0, 0)
    return pl.pallas_call(
        functools.partial(_inproj_body, n_main=n_main),
        grid=(rows // PROJ_TM,),
        in_specs=[
            pl.BlockSpec((PROJ_TM, d), lambda i: (i, 0)),
            _resident((1, d)),
            pl.BlockSpec((1, 1, d), mod_map),
            pl.BlockSpec((1, 1, d), mod_map),
            _resident((d, n_main)),
            _resident((d, 128)),
        ],
        out_specs=[
            pl.BlockSpec((PROJ_TM, n_main), lambda i: (i, 0)),
            pl.BlockSpec((PROJ_TM, 128), lambda i: (i, 0)),
        ],
        out_shape=[
            jax.ShapeDtypeStruct((rows, n_main), BF16),
            jax.ShapeDtypeStruct((rows, 128), F32),
        ],
        compiler_params=_cparams(("arbitrary",)),
        name="in_projection",
    )(x2d, gain, sc, sh, w_main, w_a)


def _log_sigmoid(z):
    return jnp.minimum(z, 0.0) - jnp.log1p(jnp.exp(-jnp.abs(z)))


def _gla_cumulative_decay(a, w2, ba, tri):
    z = jnp.dot(a.astype(BF16), w2, preferred_element_type=F32) + ba
    g = _log_sigmoid(z) * (1.0 / GLA_TAU)
    g_hi, g_lo = _bf16_terms(g)
    return jnp.dot(tri, g_hi, preferred_element_type=F32) + jnp.dot(tri, g_lo, preferred_element_type=F32)


def _gla_prep(q, k, G, reverse):
    nc = SUPER // GLA_CHUNK
    G = G.reshape(nc, GLA_CHUNK, GLA_DK)
    end_row = 0 if reverse else GLA_CHUNK - 1
    mid_row = GLA_CHUNK - 1 - GLA_CHUNK // 2 if reverse else GLA_CHUNK // 2
    g_end = G[:, end_row:end_row + 1, :]
    g_mid = G[:, mid_row:mid_row + 1, :]
    k4 = k.astype(F32).reshape(nc, GLA_CHUNK, GLA_DK)
    dec = jnp.broadcast_to(jnp.exp(g_end), (nc, 8, GLA_DK)).reshape(nc * 8, GLA_DK)
    flat = lambda t: t.reshape(SUPER, GLA_DK).astype(BF16)
    if q is None:
        return None, None, None, flat(k4 * jnp.exp(g_end - G)), dec
    q4 = q.astype(F32).reshape(nc, GLA_CHUNK, GLA_DK) * (GLA_DK ** -0.5)
    qg = q4 * jnp.exp(G - g_mid)
    kg = k4 * jnp.exp(g_mid - G)
    qe = qg * jnp.exp(g_mid)
    kd = kg * jnp.exp(g_end - g_mid)
    return flat(qg), flat(kg), flat(qe), flat(kd), dec


def _gla_apply(qg, kg, qe, kd, dec, v, mask, st_ref, reverse):
    nc = SUPER // GLA_CHUNK
    o = None
    if qg is not None:
        att = lax.dot_general(qg, kg, (((1,), (1,)), ((), ())), preferred_element_type=F32)
        att = jnp.where(mask, att, 0.0).astype(BF16)
        o = jnp.dot(att, v, preferred_element_type=F32)
    outs = [None] * nc
    order = range(nc - 1, -1, -1) if reverse else range(nc)
    for c in order:
        rows = slice(c * GLA_CHUNK, (c + 1) * GLA_CHUNK)
        st = st_ref[...]
        if qg is not None:
            inter = lax.dot_general(qe[rows], st.astype(BF16), (((1,), (1,)), ((), ())),
                                    preferred_element_type=F32)
            outs[c] = o[rows] + inter
        upd = lax.dot_general(v[rows], kd[rows], (((0,), (0,)), ((), ())), preferred_element_type=F32)
        st_ref[...] = st * dec[8 * c:8 * c + 1, :] + upd
    if qg is None:
        return None
    return jnp.concatenate(outs, axis=0)


def _gla_body(q_ref, k_ref, v_ref, r_ref, a_ref, kc_ref, vc_ref, ac_ref,
              w2f_ref, baf_ref, w2b_ref, bab_ref, gn_ref, y_ref, o_acc, st, ops_a, ops_b, dec_a, dec_b,
              *, n_ctx):
    n_sup = q_ref.shape[0] // SUPER
    row = lax.broadcasted_iota(I32, (SUPER, SUPER), 0)
    col = lax.broadcasted_iota(I32, (SUPER, SUPER), 1)
    same_chunk = (row >> 6) == (col >> 6)
    mask_f = same_chunk & (col <= row)
    mask_b = same_chunk & (col >= row)
    tri_f = jnp.where(mask_f, 1.0, 0.0).astype(BF16)
    tri_b = jnp.where(mask_b, 1.0, 0.0).astype(BF16)
    heads = range(GLA_HPS)
    kcol = [slice(h * GLA_DK, (h + 1) * GLA_DK) for h in heads]
    vcol = [slice(h * GLA_DV, (h + 1) * GLA_DV) for h in heads]
    dirs = ((False, w2f_ref, baf_ref, tri_f, mask_f), (True, w2b_ref, bab_ref, tri_b, mask_b))

    st[...] = jnp.zeros_like(st)
    n_csup = n_ctx // SUPER
    for s in range(n_csup):
        for d, (reverse, w2_ref, ba_ref, tri, mask) in enumerate(dirs):
            sc = n_csup - 1 - s if reverse else s
            rows = slice(sc * SUPER, (sc + 1) * SUPER)
            G = _gla_cumulative_decay(ac_ref[rows, :], w2_ref[...], ba_ref[...], tri)
            for h in heads:
                _, _, _, kd, dec = _gla_prep(None, kc_ref[rows, kcol[h]], G[:, kcol[h]], reverse)
                _gla_apply(None, None, None, kd, dec, vc_ref[rows, vcol[h]], mask, st.at[d, h], reverse)

    o_acc[...] = jnp.zeros_like(o_acc)

    def rows_of(i, reverse):
        sc = n_sup - 1 - i if reverse else i
        return pl.ds(pl.multiple_of(sc * SUPER, SUPER), SUPER)

    def prepare(i, ops, decs, d):
        reverse, w2_ref, ba_ref, tri, mask = dirs[d]
        rows = rows_of(i, reverse)
        G = _gla_cumulative_decay(a_ref[rows, :], w2_ref[...], ba_ref[...], tri)
        for h in heads:
            vals = _gla_prep(q_ref[rows, kcol[h]], k_ref[rows, kcol[h]], G[:, kcol[h]], reverse)
            for j in range(4):
                ops[GLA_HPS * d + h, j] = vals[j]
            decs[GLA_HPS * d + h] = vals[4]

    def apply(i, ops, decs, d):
        reverse, w2_ref, ba_ref, tri, mask = dirs[d]
        rows = rows_of(i, reverse)
        for h in heads:
            ci = GLA_HPS * d + h
            out = _gla_apply(ops[ci, 0], ops[ci, 1], ops[ci, 2], ops[ci, 3], decs[ci],
                             v_ref[rows, vcol[h]], mask, st.at[d, h], reverse)
            o_acc[rows, vcol[h]] += out

    for d in range(2):
        prepare(0, ops_a, dec_a, d)

    def step(j, carry):
        i = 2 * j
        for d in range(2):
            prepare(i + 1, ops_b, dec_b, d)
            apply(i, ops_a, dec_a, d)
        nxt = jnp.minimum(i + 2, n_sup - 1)
        for d in range(2):
            prepare(nxt, ops_a, dec_a, d)
            apply(i + 1, ops_b, dec_b, d)
        return carry

    lax.fori_loop(0, n_sup // 2, step, 0)

    for h in heads:
        o = o_acc[:, vcol[h]]
        o = o * _rms_scale(o) * gn_ref[:, vcol[h]]
        y_ref[:, vcol[h]] = (o * _silu(r_ref[:, vcol[h]].astype(F32))).astype(BF16)


def _gla(u_lat, a_lat, u_ctx, a_ctx, w2f, baf, w2b, bab, gla_norm, batch, seq, n_ctx):
    groups = GLA_HEADS // GLA_HPS
    kw, vw = GLA_HPS * GLA_DK, GLA_HPS * GLA_DV
    kb = GLA_KEY_WIDTH // kw
    vb = 2 * GLA_KEY_WIDTH // vw
    rb = vb + groups
    assert GLA_KEY_WIDTH % kw == 0 and (2 * GLA_KEY_WIDTH) % vw == 0 and GLA_WIDTH % kw == 0
    ckb = GLA_WIDTH // kw
    return pl.pallas_call(
        functools.partial(_gla_body, n_ctx=n_ctx),
        grid=(batch, groups),
        in_specs=[
            pl.BlockSpec((seq, kw), lambda b, h: (b, h)),
            pl.BlockSpec((seq, kw), lambda b, h: (b, kb + h)),
            pl.BlockSpec((seq, vw), lambda b, h: (b, vb + h)),
            pl.BlockSpec((seq, vw), lambda b, h: (b, rb + h)),
            pl.BlockSpec((seq, 128), lambda b, h: (b, 0)),
            pl.BlockSpec((n_ctx, kw), lambda b, h: (b, ckb + h)),
            pl.BlockSpec((n_ctx, vw), lambda b, h: (b, h)),
            pl.BlockSpec((n_ctx, 128), lambda b, h: (b, 0)),
            pl.BlockSpec((128, kw), lambda b, h: (0, h)),
            pl.BlockSpec((1, kw), lambda b, h: (0, h)),
            pl.BlockSpec((128, kw), lambda b, h: (0, h)),
            pl.BlockSpec((1, kw), lambda b, h: (0, h)),
            pl.BlockSpec((1, vw), lambda b, h: (0, h)),
        ],
        out_specs=pl.BlockSpec((seq, vw), lambda b, h: (b, h)),
        out_shape=jax.ShapeDtypeStruct((batch * seq, GLA_WIDTH), BF16),
        scratch_shapes=[
            pltpu.VMEM((seq, vw), F32),
            pltpu.VMEM((2, GLA_HPS, GLA_DV, GLA_DK), F32),
            pltpu.VMEM((2 * GLA_HPS, 4, SUPER, GLA_DK), BF16),
            pltpu.VMEM((2 * GLA_HPS, 4, SUPER, GLA_DK), BF16),
            pltpu.VMEM((2 * GLA_HPS, 8 * (SUPER // GLA_CHUNK), GLA_DK), F32),
            pltpu.VMEM((2 * GLA_HPS, 8 * (SUPER // GLA_CHUNK), GLA_DK), F32),
        ],
        compiler_params=_cparams(("arbitrary", "arbitrary")),
        name="gla",
    )(u_lat, u_lat, u_lat, u_lat, a_lat, u_ctx, u_ctx, a_ctx, w2f, baf, w2b, bab, gla_norm)


def _col_window_matrices():
    t = np.arange(SUPER)
    r, c = t // GRID_W, t % GRID_W
    mats = []
    for w in POOL_WINDOWS:
        lo = np.maximum(c - w // 2, 0)[:, None]
        hi = np.minimum(c + w // 2, GRID_W)[:, None]
        m = (r[:, None] == r[None, :]) & (c[None, :] >= lo) & (c[None, :] < hi)
        mats.append(m.astype(np.float32))
    return jnp.asarray(np.stack(mats), dtype=BF16)


def _pool_body(p_ref, cw_ref, wp_ref, ps_ref, y_ref, pad_ref):
    seq = p_ref.shape[0]
    n_rows = seq // GRID_W
    zeros = jnp.zeros((POOL_PAD, POOL_GROUP), F32)
    pad_ref[0:POOL_PAD, :] = zeros
    pad_ref[POOL_PAD + seq:POOL_PAD + seq + POOL_PAD, :] = zeros
    t = lax.broadcasted_iota(I32, (seq, POOL_GROUP), 0)
    r = t >> 6
    c = t & (GRID_W - 1)
    for gi, w in enumerate(POOL_WINDOWS):
        cols = slice(gi * POOL_GROUP, (gi + 1) * POOL_GROUP)
        cw = cw_ref[gi]
        for j in range(seq // SUPER):
            rows = slice(j * SUPER, (j + 1) * SUPER)
            pad_ref[POOL_PAD + j * SUPER:POOL_PAD + (j + 1) * SUPER, :] = jnp.dot(
                cw, p_ref[rows, cols], preferred_element_type=F32)
        total = None
        for d in range(-(w // 2), w // 2):
            start = POOL_PAD + d * GRID_W
            part = pad_ref[start:start + seq, :]
            total = part if total is None else total + part
        cnt_r = jnp.minimum(r + w // 2, n_rows) - jnp.maximum(r - w // 2, 0)
        cnt_c = jnp.minimum(c + w // 2, GRID_W) - jnp.maximum(c - w // 2, 0)
        mean = total / (cnt_r * cnt_c).astype(F32)
        diff = (mean - p_ref[:, cols].astype(F32)).astype(BF16)
        y = jnp.dot(diff, wp_ref[gi], preferred_element_type=F32) * ps_ref[:, cols]
        y_ref[:, cols] = y.astype(BF16)


def _pool_mixer(u_lat, col_mats, w_pool, pool_scale, batch, seq):
    pb = (u_lat.shape[1] - POOL_WIDTH) // POOL_WIDTH
    ng = len(POOL_WINDOWS)
    return pl.pallas_call(
        _pool_body,
        grid=(batch,),
        in_specs=[
            pl.BlockSpec((seq, POOL_WIDTH), lambda b: (b, pb)),
            _resident((ng, SUPER, SUPER)),
            _resident((ng, POOL_GROUP, POOL_GROUP)),
            _resident((1, POOL_WIDTH)),
        ],
        out_specs=pl.BlockSpec((seq, POOL_WIDTH), lambda b: (b, 0)),
        out_shape=jax.ShapeDtypeStruct((batch * seq, POOL_WIDTH), BF16),
        scratch_shapes=[pltpu.VMEM((seq + 2 * POOL_PAD, POOL_GROUP), F32)],
        compiler_params=_cparams(("arbitrary",)),
        name="pool_mixer",
    )(u_lat, col_mats, w_pool, pool_scale)


def _first_index(hit, iota, size, axis):
    return jnp.min(jnp.where(hit, iota, size), axis=axis, keepdims=True)


def _mix_body(yg_ref, yp_ref, x_ref, gt1_ref, sc2_ref, sh2_ref, gpost_ref, gpre_ref, wout_ref,
              wrh_ref, wrl_ref, rb_ref, upper_ref,
              x1_ref, hp_ref, eidx_ref, pos_ref, wts_ref, cnt_ref, run_ref, wrow_ref, y_scr):
    tm = x_ref.shape[0]
    neg_inf = jnp.float32(-jnp.inf)
    step = pl.program_id(0)

    @pl.when(step == 0)
    def _():
        run_ref[...] = jnp.zeros_like(run_ref)
        wrow_ref[...] = jnp.zeros_like(wrow_ref)
        y_scr[...] = jnp.zeros_like(y_scr)

    y = y_scr[...]
    y_new = jnp.dot(yg_ref[...], wout_ref[0:GLA_WIDTH, :], preferred_element_type=F32)
    y_scr[...] = y_new + jnp.dot(yp_ref[...], wout_ref[GLA_WIDTH:, :], preferred_element_type=F32)
    x1 = x_ref[...] + gt1_ref[0] * (y * _rms_scale(y) * gpost_ref[...])
    x1_ref[...] = x1
    h = x1 * _rms_scale(x1) * gpre_ref[...]
    h = h * (1.0 + sc2_ref[0]) + sh2_ref[0]
    _store_row_tiles(hp_ref, _pack_halves(h[:, :HALF], h[:, HALF:]))

    h_hi, h_lo = _bf16_terms(h)
    lt = jnp.dot(h_hi, wrh_ref[...], preferred_element_type=F32)
    lt = lt + jnp.dot(h_hi, wrl_ref[...], preferred_element_type=F32)
    lt = lt + jnp.dot(h_lo, wrh_ref[...], preferred_element_type=F32)
    logits = lt.T[0:N_EXPERTS, :]
    scores = jax.nn.sigmoid(logits)
    sel = scores + rb_ref[...]
    shape3 = (N_GROUPS, GROUP_SIZE, tm)
    sel3 = sel.reshape(shape3)
    i_in = lax.broadcasted_iota(I32, shape3, 1).astype(F32)
    m1 = jnp.max(sel3, axis=1, keepdims=True)
    f1 = _first_index(sel3 == m1, i_in, float(GROUP_SIZE), 1)
    m2 = jnp.max(jnp.where(i_in == f1, neg_inf, sel3), axis=1, keepdims=True)
    grp = jnp.broadcast_to(m1 + m2, shape3).reshape(N_EXPERTS, tm)
    i_e = lax.broadcasted_iota(I32, (N_EXPERTS, tm), 0)
    i_grp = (i_e >> 3).astype(F32)
    i_e = i_e.astype(F32)
    allowed = jnp.zeros((N_EXPERTS, tm), F32)
    for _ in range(TOPK_GROUPS):
        m = jnp.max(grp, axis=0, keepdims=True)
        pick = i_grp == _first_index(grp == m, i_grp, float(N_GROUPS), 0)
        allowed = jnp.where(pick, 1.0, allowed)
        grp = jnp.where(pick, neg_inf, grp)
    cand = jnp.where(allowed > 0.0, sel, neg_inf)
    onehot = jnp.zeros((N_EXPERTS, tm), F32)
    picks, wts = [], []
    for k in range(TOP_K):
        m = jnp.max(cand, axis=0, keepdims=True)
        f = _first_index(cand == m, i_e, float(N_EXPERTS), 0)
        pick = i_e == f
        picks.append(pick)
        eidx_ref[k:k + 1, :] = f.astype(I32)
        wts.append(jnp.sum(jnp.where(pick, scores, 0.0), axis=0, keepdims=True))
        onehot = jnp.where(pick, 1.0, onehot)
        cand = jnp.where(pick, neg_inf, cand)
    w_sum = wts[0]
    for k in range(1, TOP_K):
        w_sum = w_sum + wts[k]
    for k in range(TOP_K):
        wrow_ref[k:k + 1, :] = wts[k] / w_sum * ROUTED_SCALE
    wts_ref[...] = wrow_ref[...].T

    before = jnp.dot(onehot.astype(BF16), upper_ref[...], preferred_element_type=F32)
    before = before + run_ref[:, 0:1]
    for k in range(TOP_K):
        pos_ref[k:k + 1, :] = jnp.sum(jnp.where(picks[k], before, 0.0), axis=0, keepdims=True).astype(I32)
    counted = jnp.where(step > 0, 1.0, 0.0)
    run_ref[...] = run_ref[...] + counted * jnp.sum(onehot, axis=1, keepdims=True)
    cnt_ref[...] = run_ref[...].astype(I32)


def _mix_and_route(y_gla, y_pool, x2d, gt1, sc2, sh2, g_post, g_pre, w_out, wr_hi, wr_lo, router_bias, seq):
    rows, d = x2d.shape
    tiles = rows // MIX_TM
    tiles_per_b = seq // MIX_TM
    proj = lambda i: (jnp.minimum(i, tiles - 1), 0)
    bmap = lambda i: (jnp.maximum(i - 1, 0) // tiles_per_b, 0, 0)
    rmap = lambda i: (jnp.maximum(i - 1, 0), 0)
    tmap = lambda i: (0, jnp.maximum(i - 1, 0))
    upper = jnp.asarray(np.triu(np.ones((MIX_TM, MIX_TM), np.float32), 1), dtype=BF16)
    return pl.pallas_call(
        _mix_body,
        grid=(tiles + 1,),
        in_specs=[
            pl.BlockSpec((MIX_TM, GLA_WIDTH), proj),
            pl.BlockSpec((MIX_TM, POOL_WIDTH), proj),
            pl.BlockSpec((MIX_TM, d), rmap),
            pl.BlockSpec((1, 1, d), bmap),
            pl.BlockSpec((1, 1, d), bmap),
            pl.BlockSpec((1, 1, d), bmap),
            _resident((1, d)),
            _resident((1, d)),
            _resident((d, d)),
            _resident((d, 128)),
            _resident((d, 128)),
            _resident((N_EXPERTS, 1)),
            _resident((MIX_TM, MIX_TM)),
        ],
        out_specs=[
            pl.BlockSpec((MIX_TM, d), rmap),
            pl.BlockSpec((MIX_TM * ROW_SUBLANES, 128), rmap),
            pl.BlockSpec((TOP_K, MIX_TM), tmap),
            pl.BlockSpec((TOP_K, MIX_TM), tmap),
            pl.BlockSpec((MIX_TM, 128), rmap),
            pl.BlockSpec((N_EXPERTS, 128), lambda i: (0, 0)),
        ],
        out_shape=[
            jax.ShapeDtypeStruct((rows, d), F32),
            jax.ShapeDtypeStruct((rows * ROW_SUBLANES, 128), U32),
            jax.ShapeDtypeStruct((TOP_K, rows), I32),
            jax.ShapeDtypeStruct((TOP_K, rows), I32),
            jax.ShapeDtypeStruct((rows, 128), F32),
            jax.ShapeDtypeStruct((N_EXPERTS, 128), I32),
        ],
        scratch_shapes=[pltpu.VMEM((N_EXPERTS, 128), F32), pltpu.VMEM((128, MIX_TM), F32),
                        pltpu.VMEM((MIX_TM, d), F32)],
        compiler_params=_cparams(("arbitrary",)),
        name="mix_and_route",
    )(y_gla, y_pool, x2d, gt1, sc2, sh2, g_post, g_pre, w_out, wr_hi, wr_lo, router_bias, upper)


def _swiglu(lo, hi, wg_ref, wu_ref, wd_ref):
    g = jnp.dot(lo, wg_ref[:HALF, :], preferred_element_type=F32)
    g = g + jnp.dot(hi, wg_ref[HALF:, :], preferred_element_type=F32)
    u = jnp.dot(lo, wu_ref[:HALF, :], preferred_element_type=F32)
    u = u + jnp.dot(hi, wu_ref[HALF:, :], preferred_element_type=F32)
    act = (_silu(g) * u).astype(BF16)
    return jnp.dot(act, wd_ref[...], preferred_element_type=F32)


def _shared_body(hp_ref, wg_ref, wu_ref, wd_ref, o_ref):
    lo, hi = _unpack_halves(_load_row_tiles(hp_ref))
    o_ref[...] = _swiglu(lo.astype(BF16), hi.astype(BF16), wg_ref, wu_ref, wd_ref).astype(BF16)


def _shared_expert(h_packed, w_sg, w_su, w_sd):
    d, ds = w_sg.shape
    rows = h_packed.shape[0] // ROW_SUBLANES
    return pl.pallas_call(
        _shared_body,
        grid=(rows // SHARED_TM,),
        in_specs=[
            pl.BlockSpec((SHARED_TM * ROW_SUBLANES, 128), lambda i: (i, 0)),
            _resident((d, ds)),
            _resident((d, ds)),
            _resident((ds, d)),
        ],
        out_specs=pl.BlockSpec((SHARED_TM, d), lambda i: (i, 0)),
        out_shape=jax.ShapeDtypeStruct((rows, d), BF16),
        compiler_params=_cparams(("arbitrary",)),
        name="shared_expert",
    )(h_packed, w_sg, w_su, w_sd)


def _offsets_body(ps_ref, e_ref, p_ref, d_ref):
    e = e_ref[...]
    d = p_ref[...]
    for x in range(N_EXPERTS):
        d = d + jnp.where(e == x, ps_ref[x], 0)
    d_ref[...] = d


def _route_offsets(eidx_t, pos_t, pstarts):
    k, rows = eidx_t.shape
    spec = pl.BlockSpec((k, OFFS_TN), lambda i, ps: (0, i))
    grid_spec = pltpu.PrefetchScalarGridSpec(
        num_scalar_prefetch=1, grid=(rows // OFFS_TN,), in_specs=[spec, spec], out_specs=spec)
    return pl.pallas_call(
        _offsets_body,
        grid_spec=grid_spec,
        out_shape=jax.ShapeDtypeStruct((k, rows), I32),
        compiler_params=_cparams(("arbitrary",)),
        name="route_offsets",
    )(pstarts, eidx_t, pos_t)


def _expert_body(be_ref, nu_ref, valid_ref, nxt_ref, slot_ref, x_ref, wg_hbm, wu_hbm, wd_hbm, y_ref,
                 wg_f, wu_f, wd_f, wg_b, wu_b, wd_b, sems):
    i = pl.program_id(0)
    valid = valid_ref[i]
    expert = be_ref[i]
    slot = slot_ref[i]

    def weight_copies(e, s):
        return [pltpu.make_async_copy(src.at[e], dst.at[s], sems.at[s, j])
                for j, (src, dst) in enumerate(((wg_hbm, wg_f), (wu_hbm, wu_f), (wd_hbm, wd_f)))]

    @pl.when(i == 0)
    def _():
        for cp in weight_copies(expert, slot):
            cp.start()

    @pl.when((i == 0) | (expert != be_ref[jnp.maximum(i - 1, 0)]))
    def _():
        for cp in weight_copies(expert, slot):
            cp.wait()

        @pl.when(nxt_ref[i] != expert)
        def _():
            for cp in weight_copies(nxt_ref[i], 1 - slot):
                cp.start()

        wg_b[...] = wg_f[slot].astype(BF16)
        wu_b[...] = wu_f[slot].astype(BF16)
        wd_b[...] = wd_f[slot].astype(BF16)

    def compute(n_rows):
        tiles = pl.ds(0, n_rows * ROW_SUBLANES)
        lo, hi = _unpack_halves(_load_row_tiles(x_ref.at[tiles]))
        y = _swiglu(lo.astype(BF16), hi.astype(BF16), wg_b, wu_b, wd_b)
        _store_row_tiles(y_ref.at[tiles], _pack_halves(y[:, :HALF], y[:, HALF:]))

    quarter = MOE_TILE // 4
    for n in range(1, 5):
        @pl.when((valid > (n - 1) * quarter) & (valid <= n * quarter))
        def _(n=n):
            compute(n * quarter)


def _experts(xs, block_expert, n_used, valid, next_expert, slot, w_eg, w_eu, w_ed):
    n_rows = xs.shape[0] // ROW_SUBLANES
    n_blocks = n_rows // MOE_TILE
    d, de = w_eg.shape[1], w_eg.shape[2]
    row_map = lambda i, be, nu, *_: (jnp.minimum(i, nu[0] - 1), 0)
    grid_spec = pltpu.PrefetchScalarGridSpec(
        num_scalar_prefetch=5,
        grid=(n_blocks,),
        in_specs=[
            pl.BlockSpec((MOE_TILE * ROW_SUBLANES, 128), row_map),
            pl.BlockSpec(memory_space=pl.ANY),
            pl.BlockSpec(memory_space=pl.ANY),
            pl.BlockSpec(memory_space=pl.ANY),
        ],
        out_specs=pl.BlockSpec((MOE_TILE * ROW_SUBLANES, 128), row_map),
        scratch_shapes=[
            pltpu.VMEM((2, d, de), F32),
            pltpu.VMEM((2, d, de), F32),
            pltpu.VMEM((2, de, d), F32),
            pltpu.VMEM((d, de), BF16),
            pltpu.VMEM((d, de), BF16),
            pltpu.VMEM((de, d), BF16),
            pltpu.SemaphoreType.DMA((2, 3)),
        ],
    )
    return pl.pallas_call(
        _expert_body,
        grid_spec=grid_spec,
        out_shape=jax.ShapeDtypeStruct((n_rows * ROW_SUBLANES, 128), U32),
        compiler_params=_cparams(("arbitrary",)),
        name="experts",
    )(block_expert, n_used, valid, next_expert, slot, xs, w_eg, w_eu, w_ed)


def _sc_mesh():
    return plsc.VectorSubcoreMesh(core_axis_name="c", subcore_axis_name="s")


def _sc_worker():
    return lax.axis_index("s") * SC_CORES + lax.axis_index("c")


def _sc_gather_rows(table, idx):
    n_idx = idx.shape[0]
    per_worker = n_idx // SC_WORKERS
    n_chunks = per_worker // SC_CHUNK
    assert per_worker * SC_WORKERS == n_idx and n_chunks * SC_CHUNK == per_worker and n_chunks % 2 == 0
    row_shape = table.shape[1:]

    @functools.partial(
        pl.kernel, mesh=_sc_mesh(),
        out_type=jax.ShapeDtypeStruct((n_idx,) + row_shape, table.dtype),
        scratch_types=[
            pltpu.VMEM((per_worker,), I32),
            pltpu.VMEM((SC_CHUNK,) + row_shape, table.dtype),
            pltpu.VMEM((SC_CHUNK,) + row_shape, table.dtype),
        ] + [pltpu.SemaphoreType.DMA] * 4,
    )
    def gather(table_hbm, idx_hbm, out_hbm, idx_v, buf0, buf1, g0, g1, w0, w1):
        bufs, gsem, wsem = (buf0, buf1), (g0, g1), (w0, w1)
        base = _sc_worker() * per_worker
        pltpu.sync_copy(idx_hbm.at[pl.ds(base, per_worker)], idx_v)

        def fetch(j, b):
            return pltpu.make_async_copy(table_hbm.at[idx_v.at[pl.ds(j * SC_CHUNK, SC_CHUNK)]], bufs[b], gsem[b])

        def flush(j, b):
            return pltpu.make_async_copy(bufs[b], out_hbm.at[pl.ds(base + j * SC_CHUNK, SC_CHUNK)], wsem[b])

        fetch(0, 0).start()
        fetch(0, 0).wait()
        fetch(1, 1).start()
        flush(0, 0).start()

        @pl.loop(1, n_chunks - 1, step=2)
        def _(j):
            for off in range(2):
                jj, b = j + off, (1 + off) % 2
                fetch(jj, b).wait()
                flush(jj - 1, 1 - b).wait()
                fetch(jj + 1, 1 - b).start()
                flush(jj, b).start()

        fetch(n_chunks - 1, 1).wait()
        flush(n_chunks - 1, 1).start()
        flush(n_chunks - 2, 0).wait()
        flush(n_chunks - 1, 1).wait()

    return gather(table, idx)


def _sc_scatter_rows(rows, dest_t, n_out):
    n_rows = rows.shape[0]
    per_worker = n_rows // SC_WORKERS
    n_chunks = per_worker // SC_CHUNK
    assert per_worker * SC_WORKERS == n_rows and n_chunks * SC_CHUNK == per_worker
    row_shape = rows.shape[1:]
    idx_w = dest_t.reshape(TOP_K, SC_WORKERS, n_chunks, SC_CHUNK).transpose(1, 2, 0, 3)
    idx_w = idx_w.reshape(SC_WORKERS, n_chunks * TOP_K, SC_CHUNK)

    @functools.partial(
        pl.kernel, mesh=_sc_mesh(),
        out_type=jax.ShapeDtypeStruct((n_out,) + row_shape, rows.dtype),
        scratch_types=[
            pltpu.VMEM((n_chunks * TOP_K, SC_CHUNK), I32),
            pltpu.VMEM((SC_CHUNK,) + row_shape, rows.dtype),
            pltpu.VMEM((SC_CHUNK,) + row_shape, rows.dtype),
        ] + [pltpu.SemaphoreType.DMA] * 4,
    )
    def scatter(rows_hbm, idx_hbm, out_hbm, idx_v, buf0, buf1, r0, r1, s0, s1):
        bufs, rsem, ssem = (buf0, buf1), (r0, r1), (s0, s1)
        worker = _sc_worker()
        base = worker * per_worker
        pltpu.sync_copy(idx_hbm.at[worker], idx_v)

        def fetch(j, b):
            return pltpu.make_async_copy(rows_hbm.at[pl.ds(base + j * SC_CHUNK, SC_CHUNK)], bufs[b], rsem[b])

        def send(j, k, b):
            return pltpu.make_async_copy(bufs[b], out_hbm.at[idx_v.at[j * TOP_K + k]], ssem[b])

        fetch(0, 0).start()
        for j in range(n_chunks):
            b = j % 2
            fetch(j, b).wait()
            if j + 1 < n_chunks:
                if j >= 1:
                    for k in range(TOP_K):
                        send(j - 1, k, 1 - b).wait()
                fetch(j + 1, 1 - b).start()
            for k in range(TOP_K):
                send(j, k, b).start()
        for j in range(max(n_chunks - 2, 0), n_chunks):
            for k in range(TOP_K):
                send(j, k, j % 2).wait()

    return scatter(rows, idx_w)


def _combine_body(*refs):
    yk_refs = refs[:TOP_K]
    w_ref, shr_ref, x1_ref, gt2_ref, gpost_ref, o_ref = refs[TOP_K:]
    tt = x1_ref.shape[0]
    w = w_ref[...]
    ssq = jnp.zeros((tt, 1), F32)
    for c in range(HALF // 128):
        c_lo = slice(c * 128, (c + 1) * 128)
        c_hi = slice(HALF + c * 128, HALF + (c + 1) * 128)
        y_lo = shr_ref[:, c_lo].astype(F32)
        y_hi = shr_ref[:, c_hi].astype(F32)
        for k in range(TOP_K):
            lo, hi = _unpack_halves(yk_refs[k][pl.ds(c, tt, stride=ROW_SUBLANES), :])
            y_lo = y_lo + w[:, k:k + 1] * lo
            y_hi = y_hi + w[:, k:k + 1] * hi
        ssq = ssq + jnp.sum(y_lo * y_lo, axis=-1, keepdims=True) + jnp.sum(y_hi * y_hi, axis=-1, keepdims=True)
        o_ref[:, c_lo] = y_lo
        o_ref[:, c_hi] = y_hi
    scale = lax.rsqrt(ssq / D_MODEL + EPS)
    o_ref[...] = x1_ref[...] + gt2_ref[0] * (o_ref[...] * scale * gpost_ref[...])


def _combine(yu, wts, shared, x1, gt2, g_post, seq):
    rows, d = x1.shape
    tiles = rows // COMB_TT
    tiles_per_b = seq // COMB_TT
    yk_specs = [pl.BlockSpec((COMB_TT * ROW_SUBLANES, 128), functools.partial(lambda i, k: (k * tiles + i, 0), k=k))
                for k in range(TOP_K)]
    return pl.pallas_call(
        _combine_body,
        grid=(tiles,),
        in_specs=yk_specs + [
            pl.BlockSpec((COMB_TT, 128), lambda i: (i, 0)),
            pl.BlockSpec((COMB_TT, d), lambda i: (i, 0)),
            pl.BlockSpec((COMB_TT, d), lambda i: (i, 0)),
            pl.BlockSpec((1, 1, d), lambda i: (i // tiles_per_b, 0, 0)),
            _resident((1, d)),
        ],
        out_specs=pl.BlockSpec((COMB_TT, d), lambda i: (i, 0)),
        out_shape=jax.ShapeDtypeStruct((rows, d), F32),
        compiler_params=_cparams(("arbitrary",)),
        name="combine",
    )(*([yu] * TOP_K), wts, shared, x1, gt2, g_post)


def kernel(x, c, ctx, c_ctx, w_mod, b_mod, norm_mix_pre, norm_mix_post, norm_ffn_pre, norm_ffn_post, w_in, w_a2_fwd, b_a_fwd, w_a2_bwd, b_a_bwd, gla_norm, w_pool, pool_scale, w_out, w_router, router_bias, w_exp_gate, w_exp_up, w_exp_down, w_sh_gate, w_sh_up, w_sh_down):
    batch, seq, d = x.shape
    n_ctx = ctx.shape[1]
    assert w_mod.shape[0] == 1 and d == D_MODEL
    assert seq % (2 * SUPER) == 0 and n_ctx % SUPER == 0 and seq % PROJ_TM == 0 and (batch * n_ctx) % PROJ_TM == 0
    rows = batch * seq

    mod_rows = 16
    c_all = jnp.concatenate([c, c_ctx[None, :], jnp.zeros((mod_rows - batch - 1, d), F32)], axis=0)
    mod_all = _modulation(c_all, w_mod[0], b_mod[0][None, :])
    sh1, sc1, gt1, sh2, sc2, gt2 = [m.reshape(batch, 1, d) for m in jnp.split(mod_all[:batch], 6, axis=-1)]
    csh1 = mod_all[batch, 0:d].reshape(1, 1, d)
    csc1 = mod_all[batch, d:2 * d].reshape(1, 1, d)

    kw, gw = GLA_KEY_WIDTH, GLA_WIDTH
    a0 = 2 * kw + 2 * gw
    w_in0 = w_in[0]
    w_main = jnp.concatenate([w_in0[:, :a0], w_in0[:, a0 + 2 * GLA_RANK:]], axis=1).astype(BF16)
    w_a = jnp.pad(w_in0[:, a0:a0 + 2 * GLA_RANK], ((0, 0), (0, 128 - 2 * GLA_RANK))).astype(BF16)
    w_ctx = jnp.concatenate([w_main[:, 2 * kw:2 * kw + gw], w_main[:, kw:2 * kw]], axis=1)
    w2f = jnp.pad(w_a2_fwd[0], ((0, 128 - GLA_RANK), (0, 0))).astype(BF16)
    w2b = jnp.pad(w_a2_bwd[0], ((GLA_RANK, 128 - 2 * GLA_RANK), (0, 0))).astype(BF16)
    g_mix_pre = norm_mix_pre[0][None, :]
    wr_hi, wr_lo = _bf16_terms(jnp.pad(w_router[0], ((0, 0), (0, 128 - N_EXPERTS))))

    u_ctx, a_ctx = _in_projection(ctx.reshape(batch * n_ctx, d), g_mix_pre, csc1, csh1, w_ctx, w_a,
                                  batch * n_ctx)
    u_lat, a_lat = _in_projection(x.reshape(rows, d), g_mix_pre, sc1, sh1, w_main, w_a, seq)

    y_gla = _gla(u_lat, a_lat, u_ctx, a_ctx, w2f, b_a_fwd[0][None, :], w2b, b_a_bwd[0][None, :],
                 gla_norm[0][None, :], batch, seq, n_ctx)
    y_pool = _pool_mixer(u_lat, _col_window_matrices(), w_pool[0].astype(BF16), pool_scale[0][None, :],
                         batch, seq)

    x1, h_packed, eidx_t, pos_t, wts, counts = _mix_and_route(
        y_gla, y_pool, x.reshape(rows, d), gt1, sc2, sh2, norm_mix_post[0][None, :],
        norm_ffn_pre[0][None, :], w_out[0].astype(BF16), wr_hi, wr_lo, router_bias[0][:, None], seq)
    shared = _shared_expert(h_packed, w_sh_gate[0].astype(BF16), w_sh_up[0].astype(BF16),
                            w_sh_down[0].astype(BF16))

    counts = counts[:, 0]
    padded = (counts + MOE_TILE - 1) // MOE_TILE * MOE_TILE
    pends = jnp.cumsum(padded)
    pstarts = pends - padded
    dest_t = _route_offsets(eidx_t, pos_t, pstarts.astype(I32))
    n_blocks = rows * TOP_K // MOE_TILE + N_EXPERTS
    n_used = (pends[-1] // MOE_TILE).astype(I32)
    blk = jnp.minimum(jnp.arange(n_blocks, dtype=I32), n_used - 1)
    block_expert = jnp.sum((blk * MOE_TILE)[:, None] >= pends[None, :], axis=1).astype(I32)
    block_expert = jnp.minimum(block_expert, N_EXPERTS - 1)
    e_ids = jnp.arange(N_EXPERTS, dtype=I32)
    is_block_expert = block_expert[:, None] == e_ids[None, :]
    per_block = lambda v: jnp.sum(jnp.where(is_block_expert, v[None, :], 0), axis=1).astype(I32)
    valid = jnp.clip(per_block(pstarts + counts) - blk * MOE_TILE, 0, MOE_TILE)
    valid = jnp.where(jnp.arange(n_blocks, dtype=I32) < n_used, valid, 0).astype(I32)
    has_rows = padded > 0
    later = jnp.where((e_ids[None, :] > e_ids[:, None]) & has_rows[None, :], e_ids[None, :], N_EXPERTS)
    next_e = jnp.min(later, axis=1)
    next_e = jnp.where(next_e == N_EXPERTS, e_ids, next_e)
    slot_e = (jnp.cumsum(has_rows.astype(I32)) - has_rows.astype(I32)) % 2

    xs = _sc_scatter_rows(h_packed.reshape(-1, ROW_SUBLANES, 128), dest_t, n_blocks * MOE_TILE)
    xs = xs.reshape(-1, 128)
    ys = _experts(xs, block_expert, n_used.reshape(1), valid, per_block(next_e), per_block(slot_e),
                  w_exp_gate[0], w_exp_up[0], w_exp_down[0])
    yu = _sc_gather_rows(ys.reshape(-1, ROW_SUBLANES, 128), dest_t.reshape(-1))
    out = _combine(yu.reshape(-1, 128), wts, shared, x1, gt2, norm_ffn_post[0][None, :], seq)
    return out.reshape(batch, seq, d)
```

```python
import functools

import numpy as np
import jax
import jax.numpy as jnp
from jax import lax
from jax.experimental import pallas as pl
from jax.experimental.pallas import tpu as pltpu
from jax.experimental.pallas import tpu_sc as plsc

F32 = jnp.float32
BF16 = jnp.bfloat16
I32 = jnp.int32
U32 = jnp.uint32

D_MODEL = 2048
GRID_W = 64
GLA_HEADS = 4
GLA_DK = 128
GLA_DV = 256
GLA_KEY_WIDTH = GLA_HEADS * GLA_DK
GLA_WIDTH = GLA_HEADS * GLA_DV
GLA_RANK = 16
GLA_TAU = 16.0
GLA_CHUNK = 64
POOL_WIDTH = 1024
POOL_WINDOWS = (2, 4, 8, 16)
POOL_GROUP = 256
N_EXPERTS = 64
TOP_K = 8
N_GROUPS = 8
GROUP_SIZE = N_EXPERTS // N_GROUPS
TOPK_GROUPS = 4
D_EXPERT = 512
D_SHARED = 512
ROUTED_SCALE = 2.5
EPS = 1e-6

HALF = D_MODEL // 2
SUPER = 4 * GLA_CHUNK
GLA_HPS = 4
POOL_PAD = 8 * GRID_W
VMEM_LIMIT = 56 * 1024 * 1024

MOD_TN = 1024
PROJ_TM = 512
PROJ_TN = 512
MIX_TM = 512
MOE_TILE = 512
SHARED_TM = 1024
COMB_TT = 256
ROW_SUBLANES = 8
SC_CORES = 2
SC_WORKERS = 32
SC_CHUNK = 32
OFFS_TN = 2048


def _cparams(sem):
    return pltpu.CompilerParams(dimension_semantics=sem, vmem_limit_bytes=VMEM_LIMIT)


def _resident(shape):
    nd = len(shape)
    return pl.BlockSpec(shape, lambda *_: (0,) * nd, pipeline_mode=pl.Buffered(1))


def _silu(v):
    return v * jax.nn.sigmoid(v)


def _pack_halves(lo, hi):
    lo_b = lax.bitcast_convert_type(lo.astype(BF16).astype(F32), U32)
    hi_b = lax.bitcast_convert_type(hi.astype(BF16).astype(F32), U32)
    return (hi_b & jnp.uint32(0xFFFF0000)) | (lo_b >> 16)


def _unpack_halves(p):
    lo = lax.bitcast_convert_type(p << 16, F32)
    hi = lax.bitcast_convert_type(p & jnp.uint32(0xFFFF0000), F32)
    return lo, hi


def _bf16_terms(x):
    hi = lax.bitcast_convert_type(lax.bitcast_convert_type(x, U32) & jnp.uint32(0xFFFF0000), F32)
    return hi.astype(BF16), (x - hi).astype(BF16)


def _store_row_tiles(ref, packed):
    n = packed.shape[0]
    for c in range(HALF // 128):
        ref[pl.ds(c, n, stride=ROW_SUBLANES), :] = packed[:, c * 128:(c + 1) * 128]


def _load_row_tiles(ref):
    n = ref.shape[0] // ROW_SUBLANES
    return jnp.concatenate([ref[pl.ds(c, n, stride=ROW_SUBLANES), :] for c in range(HALF // 128)], axis=1)


def _mod_body(c_ref, w_ref, b_ref, o_ref):
    s_hi, s_lo = _bf16_terms(_silu(c_ref[...]))
    w_hi, w_lo = _bf16_terms(w_ref[...])
    acc = jnp.dot(s_hi, w_hi, preferred_element_type=F32)
    acc = acc + jnp.dot(s_lo, w_hi, preferred_element_type=F32)
    acc = acc + jnp.dot(s_hi, w_lo, preferred_element_type=F32)
    o_ref[...] = acc + b_ref[...]


def _modulation(c_all, w_mod, b_mod):
    rows, d = c_all.shape
    n = w_mod.shape[1]
    return pl.pallas_call(
        _mod_body,
        grid=(n // MOD_TN,),
        in_specs=[
            pl.BlockSpec((rows, d), lambda j: (0, 0)),
            pl.BlockSpec((d, MOD_TN), lambda j: (0, j)),
            pl.BlockSpec((1, MOD_TN), lambda j: (0, j)),
        ],
        out_specs=pl.BlockSpec((rows, MOD_TN), lambda j: (0, j)),
        out_shape=jax.ShapeDtypeStruct((rows, n), F32),
        compiler_params=_cparams(("arbitrary",)),
        name="modulation",
    )(c_all, w_mod, b_mod)


def _rms_scale(x):
    return lax.rsqrt(jnp.mean(x * x, axis=-1, keepdims=True) + EPS)


def _inproj_body(x_ref, g_ref, sc_ref, sh_ref, *refs):
    w_refs, (wa_ref, o_ref, a_ref) = refs[:-3], refs[-3:]
    x = x_ref[...]
    h = x * _rms_scale(x) * g_ref[...]
    h = h * (1.0 + sc_ref[0]) + sh_ref[0]
    hb = h.astype(BF16)
    col = 0
    for w_ref in w_refs:
        for n in range(w_ref.shape[1] // PROJ_TN):
            cols = slice(n * PROJ_TN, (n + 1) * PROJ_TN)
            o_ref[:, col:col + PROJ_TN] = jnp.dot(hb, w_ref[:, cols], preferred_element_type=F32).astype(BF16)
            col += PROJ_TN
    a_ref[...] = jnp.dot(hb, wa_ref[...], preferred_element_type=F32)


def _in_projection(x2d, gain, sc, sh, pieces, w_a, rows_per_mod):
    rows, d = x2d.shape
    n_main = sum(width for _, width, _ in pieces)
    tiles_per_mod = rows_per_mod // PROJ_TM
    mod_map = lambda i: (i // tiles_per_mod, 0, 0)
    piece_specs = [pl.BlockSpec((d, width), functools.partial(lambda i, b: (0, b), b=block),
                                pipeline_mode=pl.Buffered(1)) for _, width, block in pieces]
    return pl.pallas_call(
        _inproj_body,
        grid=(rows // PROJ_TM,),
        in_specs=[
            pl.BlockSpec((PROJ_TM, d), lambda i: (i, 0)),
            _resident((1, d)),
            pl.BlockSpec((1, 1, d), mod_map),
            pl.BlockSpec((1, 1, d), mod_map),
            *piece_specs,
            _resident((d, 128)),
        ],
        out_specs=[
            pl.BlockSpec((PROJ_TM, n_main), lambda i: (i, 0)),
            pl.BlockSpec((PROJ_TM, 128), lambda i: (i, 0)),
        ],
        out_shape=[
            jax.ShapeDtypeStruct((rows, n_main), BF16),
            jax.ShapeDtypeStruct((rows, 128), F32),
        ],
        compiler_params=_cparams(("arbitrary",)),
        name="in_projection",
    )(x2d, gain, sc, sh, *[w for w, _, _ in pieces], w_a)


def _log_sigmoid(z):
    return jnp.minimum(z, 0.0) - jnp.log1p(jnp.exp(-jnp.abs(z)))


def _gla_cumulative_decay(a, w2, ba, tri):
    z = jnp.dot(a.astype(BF16), w2, preferred_element_type=F32) + ba
    g = _log_sigmoid(z) * (1.0 / GLA_TAU)
    g_hi, g_lo = _bf16_terms(g)
    return jnp.dot(tri, g_hi, preferred_element_type=F32) + jnp.dot(tri, g_lo, preferred_element_type=F32)


def _gla_prep(q, k, G, reverse):
    nc = SUPER // GLA_CHUNK
    G = G.reshape(nc, GLA_CHUNK, GLA_DK)
    end_row = 0 if reverse else GLA_CHUNK - 1
    mid_row = GLA_CHUNK - 1 - GLA_CHUNK // 2 if reverse else GLA_CHUNK // 2
    g_end = G[:, end_row:end_row + 1, :]
    g_mid = G[:, mid_row:mid_row + 1, :]
    k4 = k.astype(F32).reshape(nc, GLA_CHUNK, GLA_DK)
    dec = jnp.broadcast_to(jnp.exp(g_end), (nc, 8, GLA_DK)).reshape(nc * 8, GLA_DK)
    flat = lambda t: t.reshape(SUPER, GLA_DK).astype(BF16)
    if q is None:
        return None, None, None, flat(k4 * jnp.exp(g_end - G)), dec
    q4 = q.astype(F32).reshape(nc, GLA_CHUNK, GLA_DK) * (GLA_DK ** -0.5)
    qg = q4 * jnp.exp(G - g_mid)
    kg = k4 * jnp.exp(g_mid - G)
    qe = qg * jnp.exp(g_mid)
    kd = kg * jnp.exp(g_end - g_mid)
    return flat(qg), flat(kg), flat(qe), flat(kd), dec


def _gla_apply(qg, kg, qe, kd, dec, v, mask, st_ref, reverse):
    nc = SUPER // GLA_CHUNK
    o = None
    if qg is not None:
        att = lax.dot_general(qg, kg, (((1,), (1,)), ((), ())), preferred_element_type=F32)
        att = jnp.where(mask, att, 0.0).astype(BF16)
        o = jnp.dot(att, v, preferred_element_type=F32)
    outs = [None] * nc
    order = range(nc - 1, -1, -1) if reverse else range(nc)
    for c in order:
        rows = slice(c * GLA_CHUNK, (c + 1) * GLA_CHUNK)
        st = st_ref[...]
        if qg is not None:
            inter = lax.dot_general(qe[rows], st.astype(BF16), (((1,), (1,)), ((), ())),
                                    preferred_element_type=F32)
            outs[c] = o[rows] + inter
        upd = lax.dot_general(v[rows], kd[rows], (((0,), (0,)), ((), ())), preferred_element_type=F32)
        st_ref[...] = st * dec[8 * c:8 * c + 1, :] + upd
    if qg is None:
        return None
    return jnp.concatenate(outs, axis=0)


def _gla_body(q_ref, k_ref, v_ref, r_ref, a_ref, kc_ref, vc_ref, ac_ref,
              w2f_ref, baf_ref, w2b_ref, bab_ref, gn_ref, y_ref, o_acc, st, ops_a, ops_b, dec_a, dec_b,
              *, n_ctx):
    n_sup = q_ref.shape[0] // SUPER
    row = lax.broadcasted_iota(I32, (SUPER, SUPER), 0)
    col = lax.broadcasted_iota(I32, (SUPER, SUPER), 1)
    same_chunk = (row >> 6) == (col >> 6)
    mask_f = same_chunk & (col <= row)
    mask_b = same_chunk & (col >= row)
    tri_f = jnp.where(mask_f, 1.0, 0.0).astype(BF16)
    tri_b = jnp.where(mask_b, 1.0, 0.0).astype(BF16)
    heads = range(GLA_HPS)
    kcol = [slice(h * GLA_DK, (h + 1) * GLA_DK) for h in heads]
    vcol = [slice(h * GLA_DV, (h + 1) * GLA_DV) for h in heads]
    dirs = ((False, w2f_ref, baf_ref, tri_f, mask_f), (True, w2b_ref, bab_ref, tri_b, mask_b))

    st[...] = jnp.zeros_like(st)
    n_csup = n_ctx // SUPER
    for s in range(n_csup):
        for d, (reverse, w2_ref, ba_ref, tri, mask) in enumerate(dirs):
            sc = n_csup - 1 - s if reverse else s
            rows = slice(sc * SUPER, (sc + 1) * SUPER)
            G = _gla_cumulative_decay(ac_ref[rows, :], w2_ref[...], ba_ref[...], tri)
            for h in heads:
                _, _, _, kd, dec = _gla_prep(None, kc_ref[rows, kcol[h]], G[:, kcol[h]], reverse)
                _gla_apply(None, None, None, kd, dec, vc_ref[rows, vcol[h]], mask, st.at[d, h], reverse)

    o_acc[...] = jnp.zeros_like(o_acc)

    def rows_of(i, reverse):
        sc = n_sup - 1 - i if reverse else i
        return pl.ds(pl.multiple_of(sc * SUPER, SUPER), SUPER)

    def prepare(i, ops, decs, d):
        reverse, w2_ref, ba_ref, tri, mask = dirs[d]
        rows = rows_of(i, reverse)
        G = _gla_cumulative_decay(a_ref[rows, :], w2_ref[...], ba_ref[...], tri)
        for h in heads:
            vals = _gla_prep(q_ref[rows, kcol[h]], k_ref[rows, kcol[h]], G[:, kcol[h]], reverse)
            for j in range(4):
                ops[GLA_HPS * d + h, j] = vals[j]
            decs[GLA_HPS * d + h] = vals[4]

    def apply(i, ops, decs, d):
        reverse, w2_ref, ba_ref, tri, mask = dirs[d]
        rows = rows_of(i, reverse)
        for h in heads:
            ci = GLA_HPS * d + h
            out = _gla_apply(ops[ci, 0], ops[ci, 1], ops[ci, 2], ops[ci, 3], decs[ci],
                             v_ref[rows, vcol[h]], mask, st.at[d, h], reverse)
            o_acc[rows, vcol[h]] += out

    for d in range(2):
        prepare(0, ops_a, dec_a, d)

    def step(j, carry):
        i = 2 * j
        for d in range(2):
            prepare(i + 1, ops_b, dec_b, d)
            apply(i, ops_a, dec_a, d)
        nxt = jnp.minimum(i + 2, n_sup - 1)
        for d in range(2):
            prepare(nxt, ops_a, dec_a, d)
            apply(i + 1, ops_b, dec_b, d)
        return carry

    lax.fori_loop(0, n_sup // 2, step, 0)

    for h in heads:
        o = o_acc[:, vcol[h]]
        o = o * _rms_scale(o) * gn_ref[:, vcol[h]]
        y_ref[:, vcol[h]] = (o * _silu(r_ref[:, vcol[h]].astype(F32))).astype(BF16)


def _gla(u_lat, a_lat, u_ctx, a_ctx, w2f, baf, w2b, bab, gla_norm, batch, seq, n_ctx):
    groups = GLA_HEADS // GLA_HPS
    kw, vw = GLA_HPS * GLA_DK, GLA_HPS * GLA_DV
    kb = GLA_KEY_WIDTH // kw
    vb = 2 * GLA_KEY_WIDTH // vw
    rb = vb + groups
    assert GLA_KEY_WIDTH % kw == 0 and (2 * GLA_KEY_WIDTH) % vw == 0 and GLA_WIDTH % kw == 0
    ckb = GLA_WIDTH // kw
    return pl.pallas_call(
        functools.partial(_gla_body, n_ctx=n_ctx),
        grid=(batch, groups),
        in_specs=[
            pl.BlockSpec((seq, kw), lambda b, h: (b, h)),
            pl.BlockSpec((seq, kw), lambda b, h: (b, kb + h)),
            pl.BlockSpec((seq, vw), lambda b, h: (b, vb + h)),
            pl.BlockSpec((seq, vw), lambda b, h: (b, rb + h)),
            pl.BlockSpec((seq, 128), lambda b, h: (b, 0)),
            pl.BlockSpec((n_ctx, kw), lambda b, h: (b, ckb + h)),
            pl.BlockSpec((n_ctx, vw), lambda b, h: (b, h)),
            pl.BlockSpec((n_ctx, 128), lambda b, h: (b, 0)),
            pl.BlockSpec((128, kw), lambda b, h: (0, h)),
            pl.BlockSpec((1, kw), lambda b, h: (0, h)),
            pl.BlockSpec((128, kw), lambda b, h: (0, h)),
            pl.BlockSpec((1, kw), lambda b, h: (0, h)),
            pl.BlockSpec((1, vw), lambda b, h: (0, h)),
        ],
        out_specs=pl.BlockSpec((seq, vw), lambda b, h: (b, h)),
        out_shape=jax.ShapeDtypeStruct((batch * seq, GLA_WIDTH), BF16),
        scratch_shapes=[
            pltpu.VMEM((seq, vw), F32),
            pltpu.VMEM((2, GLA_HPS, GLA_DV, GLA_DK), F32),
            pltpu.VMEM((2 * GLA_HPS, 4, SUPER, GLA_DK), BF16),
            pltpu.VMEM((2 * GLA_HPS, 4, SUPER, GLA_DK), BF16),
            pltpu.VMEM((2 * GLA_HPS, 8 * (SUPER // GLA_CHUNK), GLA_DK), F32),
            pltpu.VMEM((2 * GLA_HPS, 8 * (SUPER // GLA_CHUNK), GLA_DK), F32),
        ],
        compiler_params=_cparams(("arbitrary", "arbitrary")),
        name="gla",
    )(u_lat, u_lat, u_lat, u_lat, a_lat, u_ctx, u_ctx, a_ctx, w2f, baf, w2b, bab, gla_norm)


def _col_window_matrices():
    t = np.arange(SUPER)
    r, c = t // GRID_W, t % GRID_W
    mats = []
    for w in POOL_WINDOWS:
        lo = np.maximum(c - w // 2, 0)[:, None]
        hi = np.minimum(c + w // 2, GRID_W)[:, None]
        m = (r[:, None] == r[None, :]) & (c[None, :] >= lo) & (c[None, :] < hi)
        mats.append(m.astype(np.float32))
    return jnp.asarray(np.stack(mats), dtype=BF16)


def _pool_body(p_ref, cw_ref, wp_ref, ps_ref, y_ref, pad_ref):
    seq = p_ref.shape[0]
    n_rows = seq // GRID_W
    zeros = jnp.zeros((POOL_PAD, POOL_GROUP), F32)
    pad_ref[0:POOL_PAD, :] = zeros
    pad_ref[POOL_PAD + seq:POOL_PAD + seq + POOL_PAD, :] = zeros
    t = lax.broadcasted_iota(I32, (seq, POOL_GROUP), 0)
    r = t >> 6
    c = t & (GRID_W - 1)
    for gi, w in enumerate(POOL_WINDOWS):
        cols = slice(gi * POOL_GROUP, (gi + 1) * POOL_GROUP)
        cw = cw_ref[gi]
        for j in range(seq // SUPER):
            rows = slice(j * SUPER, (j + 1) * SUPER)
            pad_ref[POOL_PAD + j * SUPER:POOL_PAD + (j + 1) * SUPER, :] = jnp.dot(
                cw, p_ref[rows, cols], preferred_element_type=F32)
        total = None
        for d in range(-(w // 2), w // 2):
            start = POOL_PAD + d * GRID_W
            part = pad_ref[start:start + seq, :]
            total = part if total is None else total + part
        cnt_r = jnp.minimum(r + w // 2, n_rows) - jnp.maximum(r - w // 2, 0)
        cnt_c = jnp.minimum(c + w // 2, GRID_W) - jnp.maximum(c - w // 2, 0)
        mean = total / (cnt_r * cnt_c).astype(F32)
        diff = (mean - p_ref[:, cols].astype(F32)).astype(BF16)
        y = jnp.dot(diff, wp_ref[gi], preferred_element_type=F32) * ps_ref[:, cols]
        y_ref[:, cols] = y.astype(BF16)


def _pool_mixer(u_lat, col_mats, w_pool, pool_scale, batch, seq):
    pb = (u_lat.shape[1] - POOL_WIDTH) // POOL_WIDTH
    ng = len(POOL_WINDOWS)
    return pl.pallas_call(
        _pool_body,
        grid=(batch,),
        in_specs=[
            pl.BlockSpec((seq, POOL_WIDTH), lambda b: (b, pb)),
            _resident((ng, SUPER, SUPER)),
            _resident((ng, POOL_GROUP, POOL_GROUP)),
            _resident((1, POOL_WIDTH)),
        ],
        out_specs=pl.BlockSpec((seq, POOL_WIDTH), lambda b: (b, 0)),
        out_shape=jax.ShapeDtypeStruct((batch * seq, POOL_WIDTH), BF16),
        scratch_shapes=[pltpu.VMEM((seq + 2 * POOL_PAD, POOL_GROUP), F32)],
        compiler_params=_cparams(("arbitrary",)),
        name="pool_mixer",
    )(u_lat, col_mats, w_pool, pool_scale)


def _first_index(hit, iota, size, axis):
    return jnp.min(jnp.where(hit, iota, size), axis=axis, keepdims=True)


def _mix_body(yg_ref, yp_ref, x_ref, gt1_ref, sc2_ref, sh2_ref, gpost_ref, gpre_ref, wout_ref,
              wr_ref, rb_ref, upper_ref,
              x1_ref, hp_ref, eidx_ref, pos_ref, wts_ref, cnt_ref, run_ref, wrow_ref, y_scr):
    tm = x_ref.shape[0]
    neg_inf = jnp.float32(-jnp.inf)
    step = pl.program_id(0)

    @pl.when(step == 0)
    def _():
        run_ref[...] = jnp.zeros_like(run_ref)
        wrow_ref[...] = jnp.zeros_like(wrow_ref)
        y_scr[...] = jnp.zeros_like(y_scr)

    y = y_scr[...]
    y_new = jnp.dot(yg_ref[...], wout_ref[0:GLA_WIDTH, :], preferred_element_type=F32)
    y_scr[...] = y_new + jnp.dot(yp_ref[...], wout_ref[GLA_WIDTH:, :], preferred_element_type=F32)
    x1 = x_ref[...] + gt1_ref[0] * (y * _rms_scale(y) * gpost_ref[...])
    x1_ref[...] = x1
    h = x1 * _rms_scale(x1) * gpre_ref[...]
    h = h * (1.0 + sc2_ref[0]) + sh2_ref[0]
    _store_row_tiles(hp_ref, _pack_halves(h[:, :HALF], h[:, HALF:]))

    h_hi, h_lo = _bf16_terms(h)
    both = jnp.dot(h_hi, wr_ref[...], preferred_element_type=F32)
    lt = both[:, :128] + both[:, 128:] + jnp.dot(h_lo, wr_ref[:, :128], preferred_element_type=F32)
    logits = lt.T[0:N_EXPERTS, :]
    scores = jax.nn.sigmoid(logits)
    sel = scores + rb_ref[...]
    shape3 = (N_GROUPS, GROUP_SIZE, tm)
    sel3 = sel.reshape(shape3)
    i_in = lax.broadcasted_iota(I32, shape3, 1).astype(F32)
    m1 = jnp.max(sel3, axis=1, keepdims=True)
    f1 = _first_index(sel3 == m1, i_in, float(GROUP_SIZE), 1)
    m2 = jnp.max(jnp.where(i_in == f1, neg_inf, sel3), axis=1, keepdims=True)
    grp = jnp.broadcast_to(m1 + m2, shape3).reshape(N_EXPERTS, tm)
    i_e = lax.broadcasted_iota(I32, (N_EXPERTS, tm), 0)
    i_grp = (i_e >> 3).astype(F32)
    i_e = i_e.astype(F32)
    allowed = jnp.zeros((N_EXPERTS, tm), F32)
    for _ in range(TOPK_GROUPS):
        m = jnp.max(grp, axis=0, keepdims=True)
        pick = i_grp == _first_index(grp == m, i_grp, float(N_GROUPS), 0)
        allowed = jnp.where(pick, 1.0, allowed)
        grp = jnp.where(pick, neg_inf, grp)
    cand = jnp.where(allowed > 0.0, sel, neg_inf)
    onehot = jnp.zeros((N_EXPERTS, tm), F32)
    picks, wts = [], []
    for k in range(TOP_K):
        m = jnp.max(cand, axis=0, keepdims=True)
        f = _first_index(cand == m, i_e, float(N_EXPERTS), 0)
        pick = i_e == f
        picks.append(pick)
        eidx_ref[k:k + 1, :] = f.astype(I32)
        wts.append(jnp.sum(jnp.where(pick, scores, 0.0), axis=0, keepdims=True))
        onehot = jnp.where(pick, 1.0, onehot)
        cand = jnp.where(pick, neg_inf, cand)
    w_sum = wts[0]
    for k in range(1, TOP_K):
        w_sum = w_sum + wts[k]
    for k in range(TOP_K):
        wrow_ref[k:k + 1, :] = wts[k] / w_sum * ROUTED_SCALE
    wts_ref[...] = wrow_ref[...].T

    before = jnp.dot(onehot.astype(BF16), upper_ref[...], preferred_element_type=F32)
    before = before + run_ref[:, 0:1]
    for k in range(TOP_K):
        pos_ref[k:k + 1, :] = jnp.sum(jnp.where(picks[k], before, 0.0), axis=0, keepdims=True).astype(I32)
    counted = jnp.where(step > 0, 1.0, 0.0)
    run_ref[...] = run_ref[...] + counted * jnp.sum(onehot, axis=1, keepdims=True)
    cnt_ref[...] = run_ref[...].astype(I32)


def _mix_and_route(y_gla, y_pool, x2d, gt1, sc2, sh2, g_post, g_pre, w_out, w_router2, router_bias, seq):
    rows, d = x2d.shape
    tiles = rows // MIX_TM
    tiles_per_b = seq // MIX_TM
    proj = lambda i: (jnp.minimum(i, tiles - 1), 0)
    bmap = lambda i: (jnp.maximum(i - 1, 0) // tiles_per_b, 0, 0)
    rmap = lambda i: (jnp.maximum(i - 1, 0), 0)
    tmap = lambda i: (0, jnp.maximum(i - 1, 0))
    upper = jnp.asarray(np.triu(np.ones((MIX_TM, MIX_TM), np.float32), 1), dtype=BF16)
    return pl.pallas_call(
        _mix_body,
        grid=(tiles + 1,),
        in_specs=[
            pl.BlockSpec((MIX_TM, GLA_WIDTH), proj),
            pl.BlockSpec((MIX_TM, POOL_WIDTH), proj),
            pl.BlockSpec((MIX_TM, d), rmap),
            pl.BlockSpec((1, 1, d), bmap),
            pl.BlockSpec((1, 1, d), bmap),
            pl.BlockSpec((1, 1, d), bmap),
            _resident((1, d)),
            _resident((1, d)),
            _resident((d, d)),
            _resident((d, 256)),
            _resident((N_EXPERTS, 1)),
            _resident((MIX_TM, MIX_TM)),
        ],
        out_specs=[
            pl.BlockSpec((MIX_TM, d), rmap),
            pl.BlockSpec((MIX_TM * ROW_SUBLANES, 128), rmap),
            pl.BlockSpec((TOP_K, MIX_TM), tmap),
            pl.BlockSpec((TOP_K, MIX_TM), tmap),
            pl.BlockSpec((MIX_TM, 128), rmap),
            pl.BlockSpec((N_EXPERTS, 128), lambda i: (0, 0)),
        ],
        out_shape=[
            jax.ShapeDtypeStruct((rows, d), F32),
            jax.ShapeDtypeStruct((rows * ROW_SUBLANES, 128), U32),
            jax.ShapeDtypeStruct((TOP_K, rows), I32),
            jax.ShapeDtypeStruct((TOP_K, rows), I32),
            jax.ShapeDtypeStruct((rows, 128), F32),
            jax.ShapeDtypeStruct((N_EXPERTS, 128), I32),
        ],
        scratch_shapes=[pltpu.VMEM((N_EXPERTS, 128), F32), pltpu.VMEM((128, MIX_TM), F32),
                        pltpu.VMEM((MIX_TM, d), F32)],
        compiler_params=_cparams(("arbitrary",)),
        name="mix_and_route",
    )(y_gla, y_pool, x2d, gt1, sc2, sh2, g_post, g_pre, w_out, w_router2, router_bias, upper)


def _swiglu(lo, hi, wg_ref, wu_ref, wd_ref):
    g = jnp.dot(lo, wg_ref[:HALF, :], preferred_element_type=F32)
    g = g + jnp.dot(hi, wg_ref[HALF:, :], preferred_element_type=F32)
    u = jnp.dot(lo, wu_ref[:HALF, :], preferred_element_type=F32)
    u = u + jnp.dot(hi, wu_ref[HALF:, :], preferred_element_type=F32)
    act = (_silu(g) * u).astype(BF16)
    return jnp.dot(act, wd_ref[...], preferred_element_type=F32)


def _shared_body(hp_ref, wg_ref, wu_ref, wd_ref, o_ref):
    lo, hi = _unpack_halves(_load_row_tiles(hp_ref))
    o_ref[...] = _swiglu(lo.astype(BF16), hi.astype(BF16), wg_ref, wu_ref, wd_ref).astype(BF16)


def _shared_expert(h_packed, w_sg, w_su, w_sd):
    d, ds = w_sg.shape
    rows = h_packed.shape[0] // ROW_SUBLANES
    return pl.pallas_call(
        _shared_body,
        grid=(rows // SHARED_TM,),
        in_specs=[
            pl.BlockSpec((SHARED_TM * ROW_SUBLANES, 128), lambda i: (i, 0)),
            _resident((d, ds)),
            _resident((d, ds)),
            _resident((ds, d)),
        ],
        out_specs=pl.BlockSpec((SHARED_TM, d), lambda i: (i, 0)),
        out_shape=jax.ShapeDtypeStruct((rows, d), BF16),
        compiler_params=_cparams(("arbitrary",)),
        name="shared_expert",
    )(h_packed, w_sg, w_su, w_sd)


def _offsets_body(ps_ref, e_ref, p_ref, d_ref):
    e = e_ref[...]
    d = p_ref[...]
    for x in range(N_EXPERTS):
        d = d + jnp.where(e == x, ps_ref[x], 0)
    d_ref[...] = d


def _route_offsets(eidx_t, pos_t, pstarts):
    k, rows = eidx_t.shape
    spec = pl.BlockSpec((k, OFFS_TN), lambda i, ps: (0, i))
    grid_spec = pltpu.PrefetchScalarGridSpec(
        num_scalar_prefetch=1, grid=(rows // OFFS_TN,), in_specs=[spec, spec], out_specs=spec)
    return pl.pallas_call(
        _offsets_body,
        grid_spec=grid_spec,
        out_shape=jax.ShapeDtypeStruct((k, rows), I32),
        compiler_params=_cparams(("arbitrary",)),
        name="route_offsets",
    )(pstarts, eidx_t, pos_t)


def _expert_body(be_ref, nu_ref, valid_ref, nxt_ref, slot_ref, x_ref, wg_hbm, wu_hbm, wd_hbm, y_ref,
                 wg_f, wu_f, wd_f, wg_b, wu_b, wd_b, sems):
    i = pl.program_id(0)
    valid = valid_ref[i]
    expert = be_ref[i]
    slot = slot_ref[i]

    def weight_copies(e, s):
        return [pltpu.make_async_copy(src.at[e], dst.at[s], sems.at[s, j])
                for j, (src, dst) in enumerate(((wg_hbm, wg_f), (wu_hbm, wu_f), (wd_hbm, wd_f)))]

    @pl.when(i == 0)
    def _():
        for cp in weight_copies(expert, slot):
            cp.start()

    @pl.when((i == 0) | (expert != be_ref[jnp.maximum(i - 1, 0)]))
    def _():
        for cp in weight_copies(expert, slot):
            cp.wait()

        @pl.when(nxt_ref[i] != expert)
        def _():
            for cp in weight_copies(nxt_ref[i], 1 - slot):
                cp.start()

        wg_b[...] = wg_f[slot].astype(BF16)
        wu_b[...] = wu_f[slot].astype(BF16)
        wd_b[...] = wd_f[slot].astype(BF16)

    def compute(n_rows):
        tiles = pl.ds(0, n_rows * ROW_SUBLANES)
        lo, hi = _unpack_halves(_load_row_tiles(x_ref.at[tiles]))
        y = _swiglu(lo.astype(BF16), hi.astype(BF16), wg_b, wu_b, wd_b)
        _store_row_tiles(y_ref.at[tiles], _pack_halves(y[:, :HALF], y[:, HALF:]))

    quarter = MOE_TILE // 4
    for n in range(1, 5):
        @pl.when((valid > (n - 1) * quarter) & (valid <= n * quarter))
        def _(n=n):
            compute(n * quarter)


def _experts(xs, block_expert, n_used, valid, next_expert, slot, w_eg, w_eu, w_ed):
    n_rows = xs.shape[0] // ROW_SUBLANES
    n_blocks = n_rows // MOE_TILE
    d, de = w_eg.shape[1], w_eg.shape[2]
    row_map = lambda i, be, nu, *_: (jnp.minimum(i, nu[0] - 1), 0)
    grid_spec = pltpu.PrefetchScalarGridSpec(
        num_scalar_prefetch=5,
        grid=(n_blocks,),
        in_specs=[
            pl.BlockSpec((MOE_TILE * ROW_SUBLANES, 128), row_map),
            pl.BlockSpec(memory_space=pl.ANY),
            pl.BlockSpec(memory_space=pl.ANY),
            pl.BlockSpec(memory_space=pl.ANY),
        ],
        out_specs=pl.BlockSpec((MOE_TILE * ROW_SUBLANES, 128), row_map),
        scratch_shapes=[
            pltpu.VMEM((2, d, de), F32),
            pltpu.VMEM((2, d, de), F32),
            pltpu.VMEM((2, de, d), F32),
            pltpu.VMEM((d, de), BF16),
            pltpu.VMEM((d, de), BF16),
            pltpu.VMEM((de, d), BF16),
            pltpu.SemaphoreType.DMA((2, 3)),
        ],
    )
    return pl.pallas_call(
        _expert_body,
        grid_spec=grid_spec,
        out_shape=jax.ShapeDtypeStruct((n_rows * ROW_SUBLANES, 128), U32),
        compiler_params=_cparams(("arbitrary",)),
        name="experts",
    )(block_expert, n_used, valid, next_expert, slot, xs, w_eg, w_eu, w_ed)


def _sc_mesh():
    return plsc.VectorSubcoreMesh(core_axis_name="c", subcore_axis_name="s")


def _sc_worker():
    return lax.axis_index("s") * SC_CORES + lax.axis_index("c")


def _sc_gather_rows(table, idx):
    n_idx = idx.shape[0]
    per_worker = n_idx // SC_WORKERS
    n_chunks = per_worker // SC_CHUNK
    assert per_worker * SC_WORKERS == n_idx and n_chunks * SC_CHUNK == per_worker and n_chunks % 2 == 0
    row_shape = table.shape[1:]

    @functools.partial(
        pl.kernel, mesh=_sc_mesh(),
        out_type=jax.ShapeDtypeStruct((n_idx,) + row_shape, table.dtype),
        scratch_types=[
            pltpu.VMEM((per_worker,), I32),
            pltpu.VMEM((SC_CHUNK,) + row_shape, table.dtype),
            pltpu.VMEM((SC_CHUNK,) + row_shape, table.dtype),
        ] + [pltpu.SemaphoreType.DMA] * 4,
    )
    def gather(table_hbm, idx_hbm, out_hbm, idx_v, buf0, buf1, g0, g1, w0, w1):
        bufs, gsem, wsem = (buf0, buf1), (g0, g1), (w0, w1)
        base = _sc_worker() * per_worker
        pltpu.sync_copy(idx_hbm.at[pl.ds(base, per_worker)], idx_v)

        def fetch(j, b):
            return pltpu.make_async_copy(table_hbm.at[idx_v.at[pl.ds(j * SC_CHUNK, SC_CHUNK)]], bufs[b], gsem[b])

        def flush(j, b):
            return pltpu.make_async_copy(bufs[b], out_hbm.at[pl.ds(base + j * SC_CHUNK, SC_CHUNK)], wsem[b])

        fetch(0, 0).start()
        fetch(0, 0).wait()
        fetch(1, 1).start()
        flush(0, 0).start()

        @pl.loop(1, n_chunks - 1, step=2)
        def _(j):
            for off in range(2):
                jj, b = j + off, (1 + off) % 2
                fetch(jj, b).wait()
                flush(jj - 1, 1 - b).wait()
                fetch(jj + 1, 1 - b).start()
                flush(jj, b).start()

        fetch(n_chunks - 1, 1).wait()
        flush(n_chunks - 1, 1).start()
        flush(n_chunks - 2, 0).wait()
        flush(n_chunks - 1, 1).wait()

    return gather(table, idx)


def _sc_scatter_rows(rows, dest_t, n_out):
    n_rows = rows.shape[0]
    per_worker = n_rows // SC_WORKERS
    n_chunks = per_worker // SC_CHUNK
    assert per_worker * SC_WORKERS == n_rows and n_chunks * SC_CHUNK == per_worker
    row_shape = rows.shape[1:]
    idx_w = dest_t.reshape(TOP_K, SC_WORKERS, n_chunks, SC_CHUNK).transpose(1, 2, 0, 3)
    idx_w = idx_w.reshape(SC_WORKERS, n_chunks * TOP_K, SC_CHUNK)

    @functools.partial(
        pl.kernel, mesh=_sc_mesh(),
        out_type=jax.ShapeDtypeStruct((n_out,) + row_shape, rows.dtype),
        scratch_types=[
            pltpu.VMEM((n_chunks * TOP_K, SC_CHUNK), I32),
            pltpu.VMEM((SC_CHUNK,) + row_shape, rows.dtype),
            pltpu.VMEM((SC_CHUNK,) + row_shape, rows.dtype),
        ] + [pltpu.SemaphoreType.DMA] * 4,
    )
    def scatter(rows_hbm, idx_hbm, out_hbm, idx_v, buf0, buf1, r0, r1, s0, s1):
        bufs, rsem, ssem = (buf0, buf1), (r0, r1), (s0, s1)
        worker = _sc_worker()
        base = worker * per_worker
        pltpu.sync_copy(idx_hbm.at[worker], idx_v)

        def fetch(j, b):
            return pltpu.make_async_copy(rows_hbm.at[pl.ds(base + j * SC_CHUNK, SC_CHUNK)], bufs[b], rsem[b])

        def send(j, k, b):
            return pltpu.make_async_copy(bufs[b], out_hbm.at[idx_v.at[j * TOP_K + k]], ssem[b])

        fetch(0, 0).start()
        for j in range(n_chunks):
            b = j % 2
            fetch(j, b).wait()
            if j + 1 < n_chunks:
                if j >= 1:
                    for k in range(TOP_K):
                        send(j - 1, k, 1 - b).wait()
                fetch(j + 1, 1 - b).start()
            for k in range(TOP_K):
                send(j, k, b).start()
        for j in range(max(n_chunks - 2, 0), n_chunks):
            for k in range(TOP_K):
                send(j, k, j % 2).wait()

    return scatter(rows, idx_w)


def _combine_body(*refs):
    yk_refs = refs[:TOP_K]
    w_ref, shr_ref, x1_ref, gt2_ref, gpost_ref, o_ref = refs[TOP_K:]
    tt = x1_ref.shape[0]
    w = w_ref[...]
    ssq = jnp.zeros((tt, 1), F32)
    for c in range(HALF // 128):
        c_lo = slice(c * 128, (c + 1) * 128)
        c_hi = slice(HALF + c * 128, HALF + (c + 1) * 128)
        y_lo = shr_ref[:, c_lo].astype(F32)
        y_hi = shr_ref[:, c_hi].astype(F32)
        for k in range(TOP_K):
            lo, hi = _unpack_halves(yk_refs[k][pl.ds(c, tt, stride=ROW_SUBLANES), :])
            y_lo = y_lo + w[:, k:k + 1] * lo
            y_hi = y_hi + w[:, k:k + 1] * hi
        ssq = ssq + jnp.sum(y_lo * y_lo, axis=-1, keepdims=True) + jnp.sum(y_hi * y_hi, axis=-1, keepdims=True)
        o_ref[:, c_lo] = y_lo
        o_ref[:, c_hi] = y_hi
    scale = lax.rsqrt(ssq / D_MODEL + EPS)
    o_ref[...] = x1_ref[...] + gt2_ref[0] * (o_ref[...] * scale * gpost_ref[...])


def _combine(yu, wts, shared, x1, gt2, g_post, seq):
    rows, d = x1.shape
    tiles = rows // COMB_TT
    tiles_per_b = seq // COMB_TT
    yk_specs = [pl.BlockSpec((COMB_TT * ROW_SUBLANES, 128), functools.partial(lambda i, k: (k * tiles + i, 0), k=k))
                for k in range(TOP_K)]
    return pl.pallas_call(
        _combine_body,
        grid=(tiles,),
        in_specs=yk_specs + [
            pl.BlockSpec((COMB_TT, 128), lambda i: (i, 0)),
            pl.BlockSpec((COMB_TT, d), lambda i: (i, 0)),
            pl.BlockSpec((COMB_TT, d), lambda i: (i, 0)),
            pl.BlockSpec((1, 1, d), lambda i: (i // tiles_per_b, 0, 0)),
            _resident((1, d)),
        ],
        out_specs=pl.BlockSpec((COMB_TT, d), lambda i: (i, 0)),
        out_shape=jax.ShapeDtypeStruct((rows, d), F32),
        compiler_params=_cparams(("arbitrary",)),
        name="combine",
    )(*([yu] * TOP_K), wts, shared, x1, gt2, g_post)


def kernel(x, c, ctx, c_ctx, w_mod, b_mod, norm_mix_pre, norm_mix_post, norm_ffn_pre, norm_ffn_post, w_in, w_a2_fwd, b_a_fwd, w_a2_bwd, b_a_bwd, gla_norm, w_pool, pool_scale, w_out, w_router, router_bias, w_exp_gate, w_exp_up, w_exp_down, w_sh_gate, w_sh_up, w_sh_down):
    batch, seq, d = x.shape
    n_ctx = ctx.shape[1]
    assert w_mod.shape[0] == 1 and d == D_MODEL
    assert seq % (2 * SUPER) == 0 and n_ctx % SUPER == 0 and seq % PROJ_TM == 0 and (batch * n_ctx) % PROJ_TM == 0
    rows = batch * seq

    mod_rows = 16
    c_all = jnp.concatenate([c, c_ctx[None, :], jnp.zeros((mod_rows - batch - 1, d), F32)], axis=0)
    mod_all = _modulation(c_all, w_mod[0], b_mod[0][None, :])
    sh1, sc1, gt1, sh2, sc2, gt2 = [m.reshape(batch, 1, d) for m in jnp.split(mod_all[:batch], 6, axis=-1)]
    csh1 = mod_all[batch, 0:d].reshape(1, 1, d)
    csc1 = mod_all[batch, d:2 * d].reshape(1, 1, d)

    kw, gw = GLA_KEY_WIDTH, GLA_WIDTH
    a0 = 2 * kw + 2 * gw
    w_in0 = w_in[0]
    w_bf = w_in0.astype(BF16)
    w_a = jnp.pad(w_bf[:, a0:a0 + 2 * GLA_RANK], ((0, 0), (0, 128 - 2 * GLA_RANK)))
    lat_pieces = [(w_bf, a0, 0), (w_bf[:, a0 + 2 * GLA_RANK:], POOL_WIDTH, 0)]
    ctx_pieces = [(w_bf, gw, 2 * kw // gw), (w_bf, kw, 1)]
    w2f = jnp.pad(w_a2_fwd[0], ((0, 128 - GLA_RANK), (0, 0))).astype(BF16)
    w2b = jnp.pad(w_a2_bwd[0], ((GLA_RANK, 128 - 2 * GLA_RANK), (0, 0))).astype(BF16)
    g_mix_pre = norm_mix_pre[0][None, :]
    w_router2 = jnp.concatenate(_bf16_terms(jnp.pad(w_router[0], ((0, 0), (0, 128 - N_EXPERTS)))), axis=1)

    u_ctx, a_ctx = _in_projection(ctx.reshape(batch * n_ctx, d), g_mix_pre, csc1, csh1, ctx_pieces, w_a,
                                  batch * n_ctx)
    u_lat, a_lat = _in_projection(x.reshape(rows, d), g_mix_pre, sc1, sh1, lat_pieces, w_a, seq)

    y_gla = _gla(u_lat, a_lat, u_ctx, a_ctx, w2f, b_a_fwd[0][None, :], w2b, b_a_bwd[0][None, :],
                 gla_norm[0][None, :], batch, seq, n_ctx)
    y_pool = _pool_mixer(u_lat, _col_window_matrices(), w_pool[0].astype(BF16), pool_scale[0][None, :],
                         batch, seq)

    x1, h_packed, eidx_t, pos_t, wts, counts = _mix_and_route(
        y_gla, y_pool, x.reshape(rows, d), gt1, sc2, sh2, norm_mix_post[0][None, :],
        norm_ffn_pre[0][None, :], w_out[0].astype(BF16), w_router2, router_bias[0][:, None], seq)
    shared = _shared_expert(h_packed, w_sh_gate[0].astype(BF16), w_sh_up[0].astype(BF16),
                            w_sh_down[0].astype(BF16))

    counts = counts[:, 0]
    padded = (counts + MOE_TILE - 1) // MOE_TILE * MOE_TILE
    pends = jnp.cumsum(padded)
    pstarts = pends - padded
    dest_t = _route_offsets(eidx_t, pos_t, pstarts.astype(I32))
    n_blocks = rows * TOP_K // MOE_TILE + N_EXPERTS
    n_used = (pends[-1] // MOE_TILE).astype(I32)
    blk = jnp.minimum(jnp.arange(n_blocks, dtype=I32), n_used - 1)
    block_expert = jnp.sum((blk * MOE_TILE)[:, None] >= pends[None, :], axis=1).astype(I32)
    block_expert = jnp.minimum(block_expert, N_EXPERTS - 1)
    e_ids = jnp.arange(N_EXPERTS, dtype=I32)
    is_block_expert = block_expert[:, None] == e_ids[None, :]
    per_block = lambda v: jnp.sum(jnp.where(is_block_expert, v[None, :], 0), axis=1).astype(I32)
    valid = jnp.clip(per_block(pstarts + counts) - blk * MOE_TILE, 0, MOE_TILE)
    valid = jnp.where(jnp.arange(n_blocks, dtype=I32) < n_used, valid, 0).astype(I32)
    has_rows = padded > 0
    later = jnp.where((e_ids[None, :] > e_ids[:, None]) & has_rows[None, :], e_ids[None, :], N_EXPERTS)
    next_e = jnp.min(later, axis=1)
    next_e = jnp.where(next_e == N_EXPERTS, e_ids, next_e)
    slot_e = (jnp.cumsum(has_rows.astype(I32)) - has_rows.astype(I32)) % 2

    xs = _sc_scatter_rows(h_packed.reshape(-1, ROW_SUBLANES, 128), dest_t, n_blocks * MOE_TILE)
    xs = xs.reshape(-1, 128)
    ys = _experts(xs, block_expert, n_used.reshape(1), valid, per_block(next_e), per_block(slot_e),
                  w_exp_gate[0], w_exp_up[0], w_exp_down[0])
    yu = _sc_gather_rows(ys.reshape(-1, ROW_SUBLANES, 128), dest_t.reshape(-1))
    out = _combine(yu.reshape(-1, 128), wts, shared, x1, gt2, norm_ffn_post[0][None, :], seq)
    return out.reshape(batch, seq, d)
```

```python
import functools

import numpy as np
import jax
import jax.numpy as jnp
from jax import lax
from jax.experimental import pallas as pl
from jax.experimental.pallas import tpu as pltpu
from jax.experimental.pallas import tpu_sc as plsc

F32 = jnp.float32
BF16 = jnp.bfloat16
I32 = jnp.int32
U32 = jnp.uint32

D_MODEL = 2048
GRID_W = 64
GLA_HEADS = 4
GLA_DK = 128
GLA_DV = 256
GLA_KEY_WIDTH = GLA_HEADS * GLA_DK
GLA_WIDTH = GLA_HEADS * GLA_DV
GLA_RANK = 16
GLA_TAU = 16.0
GLA_CHUNK = 64
POOL_WIDTH = 1024
POOL_WINDOWS = (2, 4, 8, 16)
POOL_GROUP = 256
N_EXPERTS = 64
TOP_K = 8
N_GROUPS = 8
GROUP_SIZE = N_EXPERTS // N_GROUPS
TOPK_GROUPS = 4
D_EXPERT = 512
D_SHARED = 512
ROUTED_SCALE = 2.5
EPS = 1e-6

HALF = D_MODEL // 2
SUPER = 4 * GLA_CHUNK
GLA_HPS = 4
POOL_PAD = 8 * GRID_W
VMEM_LIMIT = 56 * 1024 * 1024

MOD_TN = 1024
PROJ_TM = 512
PROJ_TN = 512
MIX_TM = 512
MOE_TILE = 512
SHARED_TM = 1024
COMB_TT = 256
MOE_PARTS = 2
ROW_SUBLANES = 8
SC_CORES = 2
SC_WORKERS = 32
SC_CHUNK = 32
OFFS_TN = 2048


def _cparams(sem):
    return pltpu.CompilerParams(dimension_semantics=sem, vmem_limit_bytes=VMEM_LIMIT)


def _resident(shape):
    nd = len(shape)
    return pl.BlockSpec(shape, lambda *_: (0,) * nd, pipeline_mode=pl.Buffered(1))


def _silu(v):
    return v * jax.nn.sigmoid(v)


def _pack_halves(lo, hi):
    lo_b = lax.bitcast_convert_type(lo.astype(BF16).astype(F32), U32)
    hi_b = lax.bitcast_convert_type(hi.astype(BF16).astype(F32), U32)
    return (hi_b & jnp.uint32(0xFFFF0000)) | (lo_b >> 16)


def _unpack_halves(p):
    lo = lax.bitcast_convert_type(p << 16, F32)
    hi = lax.bitcast_convert_type(p & jnp.uint32(0xFFFF0000), F32)
    return lo, hi


def _bf16_terms(x):
    hi = lax.bitcast_convert_type(lax.bitcast_convert_type(x, U32) & jnp.uint32(0xFFFF0000), F32)
    return hi.astype(BF16), (x - hi).astype(BF16)


def _store_row_tiles(ref, packed):
    n = packed.shape[0]
    for c in range(HALF // 128):
        ref[pl.ds(c, n, stride=ROW_SUBLANES), :] = packed[:, c * 128:(c + 1) * 128]


def _load_row_tiles(ref):
    n = ref.shape[0] // ROW_SUBLANES
    return jnp.concatenate([ref[pl.ds(c, n, stride=ROW_SUBLANES), :] for c in range(HALF // 128)], axis=1)


def _mod_body(c_ref, w_ref, b_ref, o_ref):
    s_hi, s_lo = _bf16_terms(_silu(c_ref[...]))
    w_hi, w_lo = _bf16_terms(w_ref[...])
    acc = jnp.dot(s_hi, w_hi, preferred_element_type=F32)
    acc = acc + jnp.dot(s_lo, w_hi, preferred_element_type=F32)
    acc = acc + jnp.dot(s_hi, w_lo, preferred_element_type=F32)
    o_ref[...] = acc + b_ref[...]


def _modulation(c_all, w_mod, b_mod):
    rows, d = c_all.shape
    n = w_mod.shape[1]
    return pl.pallas_call(
        _mod_body,
        grid=(n // MOD_TN,),
        in_specs=[
            pl.BlockSpec((rows, d), lambda j: (0, 0)),
            pl.BlockSpec((d, MOD_TN), lambda j: (0, j)),
            pl.BlockSpec((1, MOD_TN), lambda j: (0, j)),
        ],
        out_specs=pl.BlockSpec((rows, MOD_TN), lambda j: (0, j)),
        out_shape=jax.ShapeDtypeStruct((rows, n), F32),
        compiler_params=_cparams(("arbitrary",)),
        name="modulation",
    )(c_all, w_mod, b_mod)


def _rms_scale(x):
    return lax.rsqrt(jnp.mean(x * x, axis=-1, keepdims=True) + EPS)


def _inproj_body(x_ref, g_ref, sc_ref, sh_ref, *refs):
    w_refs, (wa_ref, o_ref, a_ref) = refs[:-3], refs[-3:]
    x = x_ref[...]
    h = x * _rms_scale(x) * g_ref[...]
    h = h * (1.0 + sc_ref[0]) + sh_ref[0]
    hb = h.astype(BF16)
    col = 0
    for w_ref in w_refs:
        for n in range(w_ref.shape[1] // PROJ_TN):
            cols = slice(n * PROJ_TN, (n + 1) * PROJ_TN)
            o_ref[:, col:col + PROJ_TN] = jnp.dot(hb, w_ref[:, cols], preferred_element_type=F32).astype(BF16)
            col += PROJ_TN
    a_ref[...] = jnp.dot(hb, wa_ref[...], preferred_element_type=F32)


def _in_projection(x2d, gain, sc, sh, pieces, w_a, rows_per_mod):
    rows, d = x2d.shape
    n_main = sum(width for _, width, _ in pieces)
    tiles_per_mod = rows_per_mod // PROJ_TM
    mod_map = lambda i: (i // tiles_per_mod, 0, 0)
    piece_specs = [pl.BlockSpec((d, width), functools.partial(lambda i, b: (0, b), b=block),
                                pipeline_mode=pl.Buffered(1)) for _, width, block in pieces]
    return pl.pallas_call(
        _inproj_body,
        grid=(rows // PROJ_TM,),
        in_specs=[
            pl.BlockSpec((PROJ_TM, d), lambda i: (i, 0)),
            _resident((1, d)),
            pl.BlockSpec((1, 1, d), mod_map),
            pl.BlockSpec((1, 1, d), mod_map),
            *piece_specs,
            _resident((d, 128)),
        ],
        out_specs=[
            pl.BlockSpec((PROJ_TM, n_main), lambda i: (i, 0)),
            pl.BlockSpec((PROJ_TM, 128), lambda i: (i, 0)),
        ],
        out_shape=[
            jax.ShapeDtypeStruct((rows, n_main), BF16),
            jax.ShapeDtypeStruct((rows, 128), F32),
        ],
        compiler_params=_cparams(("arbitrary",)),
        name="in_projection",
    )(x2d, gain, sc, sh, *[w for w, _, _ in pieces], w_a)


def _log_sigmoid(z):
    return jnp.minimum(z, 0.0) - jnp.log1p(jnp.exp(-jnp.abs(z)))


def _gla_cumulative_decay(a, w2, ba, tri):
    z = jnp.dot(a.astype(BF16), w2, preferred_element_type=F32) + ba
    g = _log_sigmoid(z) * (1.0 / GLA_TAU)
    g_hi, g_lo = _bf16_terms(g)
    return jnp.dot(tri, g_hi, preferred_element_type=F32) + jnp.dot(tri, g_lo, preferred_element_type=F32)


def _gla_prep(q, k, G, reverse):
    nc = SUPER // GLA_CHUNK
    G = G.reshape(nc, GLA_CHUNK, GLA_DK)
    end_row = 0 if reverse else GLA_CHUNK - 1
    mid_row = GLA_CHUNK - 1 - GLA_CHUNK // 2 if reverse else GLA_CHUNK // 2
    g_end = G[:, end_row:end_row + 1, :]
    g_mid = G[:, mid_row:mid_row + 1, :]
    k4 = k.astype(F32).reshape(nc, GLA_CHUNK, GLA_DK)
    dec = jnp.broadcast_to(jnp.exp(g_end), (nc, 8, GLA_DK)).reshape(nc * 8, GLA_DK)
    flat = lambda t: t.reshape(SUPER, GLA_DK).astype(BF16)
    if q is None:
        return None, None, None, flat(k4 * jnp.exp(g_end - G)), dec
    q4 = q.astype(F32).reshape(nc, GLA_CHUNK, GLA_DK) * (GLA_DK ** -0.5)
    qg = q4 * jnp.exp(G - g_mid)
    kg = k4 * jnp.exp(g_mid - G)
    qe = qg * jnp.exp(g_mid)
    kd = kg * jnp.exp(g_end - g_mid)
    return flat(qg), flat(kg), flat(qe), flat(kd), dec


def _gla_apply(qg, kg, qe, kd, dec, v, mask, st_ref, reverse):
    nc = SUPER // GLA_CHUNK
    o = None
    if qg is not None:
        att = lax.dot_general(qg, kg, (((1,), (1,)), ((), ())), preferred_element_type=F32)
        att = jnp.where(mask, att, 0.0).astype(BF16)
        o = jnp.dot(att, v, preferred_element_type=F32)
    outs = [None] * nc
    order = range(nc - 1, -1, -1) if reverse else range(nc)
    for c in order:
        rows = slice(c * GLA_CHUNK, (c + 1) * GLA_CHUNK)
        st = st_ref[...]
        if qg is not None:
            inter = lax.dot_general(qe[rows], st.astype(BF16), (((1,), (1,)), ((), ())),
                                    preferred_element_type=F32)
            outs[c] = o[rows] + inter
        upd = lax.dot_general(v[rows], kd[rows], (((0,), (0,)), ((), ())), preferred_element_type=F32)
        st_ref[...] = st * dec[8 * c:8 * c + 1, :] + upd
    if qg is None:
        return None
    return jnp.concatenate(outs, axis=0)


def _gla_body(q_ref, k_ref, v_ref, r_ref, a_ref, kc_ref, vc_ref, ac_ref,
              w2f_ref, baf_ref, w2b_ref, bab_ref, gn_ref, y_ref, o_acc, st, ops_a, ops_b, dec_a, dec_b,
              *, n_ctx):
    n_sup = q_ref.shape[0] // SUPER
    row = lax.broadcasted_iota(I32, (SUPER, SUPER), 0)
    col = lax.broadcasted_iota(I32, (SUPER, SUPER), 1)
    same_chunk = (row >> 6) == (col >> 6)
    mask_f = same_chunk & (col <= row)
    mask_b = same_chunk & (col >= row)
    tri_f = jnp.where(mask_f, 1.0, 0.0).astype(BF16)
    tri_b = jnp.where(mask_b, 1.0, 0.0).astype(BF16)
    heads = range(GLA_HPS)
    kcol = [slice(h * GLA_DK, (h + 1) * GLA_DK) for h in heads]
    vcol = [slice(h * GLA_DV, (h + 1) * GLA_DV) for h in heads]
    dirs = ((False, w2f_ref, baf_ref, tri_f, mask_f), (True, w2b_ref, bab_ref, tri_b, mask_b))

    st[...] = jnp.zeros_like(st)
    n_csup = n_ctx // SUPER
    for s in range(n_csup):
        for d, (reverse, w2_ref, ba_ref, tri, mask) in enumerate(dirs):
            sc = n_csup - 1 - s if reverse else s
            rows = slice(sc * SUPER, (sc + 1) * SUPER)
            G = _gla_cumulative_decay(ac_ref[rows, :], w2_ref[...], ba_ref[...], tri)
            for h in heads:
                _, _, _, kd, dec = _gla_prep(None, kc_ref[rows, kcol[h]], G[:, kcol[h]], reverse)
                _gla_apply(None, None, None, kd, dec, vc_ref[rows, vcol[h]], mask, st.at[d, h], reverse)

    o_acc[...] = jnp.zeros_like(o_acc)

    def rows_of(i, reverse):
        sc = n_sup - 1 - i if reverse else i
        return pl.ds(pl.multiple_of(sc * SUPER, SUPER), SUPER)

    def prepare(i, ops, decs, d):
        reverse, w2_ref, ba_ref, tri, mask = dirs[d]
        rows = rows_of(i, reverse)
        G = _gla_cumulative_decay(a_ref[rows, :], w2_ref[...], ba_ref[...], tri)
        for h in heads:
            vals = _gla_prep(q_ref[rows, kcol[h]], k_ref[rows, kcol[h]], G[:, kcol[h]], reverse)
            for j in range(4):
                ops[GLA_HPS * d + h, j] = vals[j]
            decs[GLA_HPS * d + h] = vals[4]

    def apply(i, ops, decs, d):
        reverse, w2_ref, ba_ref, tri, mask = dirs[d]
        rows = rows_of(i, reverse)
        for h in heads:
            ci = GLA_HPS * d + h
            out = _gla_apply(ops[ci, 0], ops[ci, 1], ops[ci, 2], ops[ci, 3], decs[ci],
                             v_ref[rows, vcol[h]], mask, st.at[d, h], reverse)
            o_acc[rows, vcol[h]] += out

    for d in range(2):
        prepare(0, ops_a, dec_a, d)

    def step(j, carry):
        i = 2 * j
        for d in range(2):
            prepare(i + 1, ops_b, dec_b, d)
            apply(i, ops_a, dec_a, d)
        nxt = jnp.minimum(i + 2, n_sup - 1)
        for d in range(2):
            prepare(nxt, ops_a, dec_a, d)
            apply(i + 1, ops_b, dec_b, d)
        return carry

    lax.fori_loop(0, n_sup // 2, step, 0)

    for h in heads:
        o = o_acc[:, vcol[h]]
        o = o * _rms_scale(o) * gn_ref[:, vcol[h]]
        y_ref[:, vcol[h]] = (o * _silu(r_ref[:, vcol[h]].astype(F32))).astype(BF16)


def _gla(u_lat, a_lat, u_ctx, a_ctx, w2f, baf, w2b, bab, gla_norm, batch, seq, n_ctx):
    groups = GLA_HEADS // GLA_HPS
    kw, vw = GLA_HPS * GLA_DK, GLA_HPS * GLA_DV
    kb = GLA_KEY_WIDTH // kw
    vb = 2 * GLA_KEY_WIDTH // vw
    rb = vb + groups
    assert GLA_KEY_WIDTH % kw == 0 and (2 * GLA_KEY_WIDTH) % vw == 0 and GLA_WIDTH % kw == 0
    ckb = GLA_WIDTH // kw
    return pl.pallas_call(
        functools.partial(_gla_body, n_ctx=n_ctx),
        grid=(batch, groups),
        in_specs=[
            pl.BlockSpec((seq, kw), lambda b, h: (b, h)),
            pl.BlockSpec((seq, kw), lambda b, h: (b, kb + h)),
            pl.BlockSpec((seq, vw), lambda b, h: (b, vb + h)),
            pl.BlockSpec((seq, vw), lambda b, h: (b, rb + h)),
            pl.BlockSpec((seq, 128), lambda b, h: (b, 0)),
            pl.BlockSpec((n_ctx, kw), lambda b, h: (b, ckb + h)),
            pl.BlockSpec((n_ctx, vw), lambda b, h: (b, h)),
            pl.BlockSpec((n_ctx, 128), lambda b, h: (b, 0)),
            pl.BlockSpec((128, kw), lambda b, h: (0, h)),
            pl.BlockSpec((1, kw), lambda b, h: (0, h)),
            pl.BlockSpec((128, kw), lambda b, h: (0, h)),
            pl.BlockSpec((1, kw), lambda b, h: (0, h)),
            pl.BlockSpec((1, vw), lambda b, h: (0, h)),
        ],
        out_specs=pl.BlockSpec((seq, vw), lambda b, h: (b, h)),
        out_shape=jax.ShapeDtypeStruct((batch * seq, GLA_WIDTH), BF16),
        scratch_shapes=[
            pltpu.VMEM((seq, vw), F32),
            pltpu.VMEM((2, GLA_HPS, GLA_DV, GLA_DK), F32),
            pltpu.VMEM((2 * GLA_HPS, 4, SUPER, GLA_DK), BF16),
            pltpu.VMEM((2 * GLA_HPS, 4, SUPER, GLA_DK), BF16),
            pltpu.VMEM((2 * GLA_HPS, 8 * (SUPER // GLA_CHUNK), GLA_DK), F32),
            pltpu.VMEM((2 * GLA_HPS, 8 * (SUPER // GLA_CHUNK), GLA_DK), F32),
        ],
        compiler_params=_cparams(("arbitrary", "arbitrary")),
        name="gla",
    )(u_lat, u_lat, u_lat, u_lat, a_lat, u_ctx, u_ctx, a_ctx, w2f, baf, w2b, bab, gla_norm)


def _col_window_matrices():
    t = np.arange(SUPER)
    r, c = t // GRID_W, t % GRID_W
    mats = []
    for w in POOL_WINDOWS:
        lo = np.maximum(c - w // 2, 0)[:, None]
        hi = np.minimum(c + w // 2, GRID_W)[:, None]
        m = (r[:, None] == r[None, :]) & (c[None, :] >= lo) & (c[None, :] < hi)
        mats.append(m.astype(np.float32))
    return jnp.asarray(np.stack(mats), dtype=BF16)


def _pool_body(p_ref, cw_ref, wp_ref, ps_ref, y_ref, pad_ref):
    seq = p_ref.shape[0]
    n_rows = seq // GRID_W
    zeros = jnp.zeros((POOL_PAD, POOL_GROUP), F32)
    pad_ref[0:POOL_PAD, :] = zeros
    pad_ref[POOL_PAD + seq:POOL_PAD + seq + POOL_PAD, :] = zeros
    t = lax.broadcasted_iota(I32, (seq, POOL_GROUP), 0)
    r = t >> 6
    c = t & (GRID_W - 1)
    for gi, w in enumerate(POOL_WINDOWS):
        cols = slice(gi * POOL_GROUP, (gi + 1) * POOL_GROUP)
        cw = cw_ref[gi]
        for j in range(seq // SUPER):
            rows = slice(j * SUPER, (j + 1) * SUPER)
            pad_ref[POOL_PAD + j * SUPER:POOL_PAD + (j + 1) * SUPER, :] = jnp.dot(
                cw, p_ref[rows, cols], preferred_element_type=F32)
        total = None
        for d in range(-(w // 2), w // 2):
            start = POOL_PAD + d * GRID_W
            part = pad_ref[start:start + seq, :]
            total = part if total is None else total + part
        cnt_r = jnp.minimum(r + w // 2, n_rows) - jnp.maximum(r - w // 2, 0)
        cnt_c = jnp.minimum(c + w // 2, GRID_W) - jnp.maximum(c - w // 2, 0)
        mean = total / (cnt_r * cnt_c).astype(F32)
        diff = (mean - p_ref[:, cols].astype(F32)).astype(BF16)
        y = jnp.dot(diff, wp_ref[gi], preferred_element_type=F32) * ps_ref[:, cols]
        y_ref[:, cols] = y.astype(BF16)


def _pool_mixer(u_lat, col_mats, w_pool, pool_scale, batch, seq):
    pb = (u_lat.shape[1] - POOL_WIDTH) // POOL_WIDTH
    ng = len(POOL_WINDOWS)
    return pl.pallas_call(
        _pool_body,
        grid=(batch,),
        in_specs=[
            pl.BlockSpec((seq, POOL_WIDTH), lambda b: (b, pb)),
            _resident((ng, SUPER, SUPER)),
            _resident((ng, POOL_GROUP, POOL_GROUP)),
            _resident((1, POOL_WIDTH)),
        ],
        out_specs=pl.BlockSpec((seq, POOL_WIDTH), lambda b: (b, 0)),
        out_shape=jax.ShapeDtypeStruct((batch * seq, POOL_WIDTH), BF16),
        scratch_shapes=[pltpu.VMEM((seq + 2 * POOL_PAD, POOL_GROUP), F32)],
        compiler_params=_cparams(("arbitrary",)),
        name="pool_mixer",
    )(u_lat, col_mats, w_pool, pool_scale)


def _first_index(hit, iota, size, axis):
    return jnp.min(jnp.where(hit, iota, size), axis=axis, keepdims=True)


def _mix_body(yg_ref, yp_ref, x_ref, gt1_ref, sc2_ref, sh2_ref, gpost_ref, gpre_ref, wout_ref,
              wr_ref, rb_ref, upper_ref,
              x1_ref, hp_ref, eidx_ref, pos_ref, wts_ref, cnt_ref, run_ref, wrow_ref, y_scr):
    tm = x_ref.shape[0]
    neg_inf = jnp.float32(-jnp.inf)
    step = pl.program_id(0)

    @pl.when(step == 0)
    def _():
        run_ref[...] = jnp.zeros_like(run_ref)
        wrow_ref[...] = jnp.zeros_like(wrow_ref)
        y_scr[...] = jnp.zeros_like(y_scr)

    y = y_scr[...]
    y_new = jnp.dot(yg_ref[...], wout_ref[0:GLA_WIDTH, :], preferred_element_type=F32)
    y_scr[...] = y_new + jnp.dot(yp_ref[...], wout_ref[GLA_WIDTH:, :], preferred_element_type=F32)
    x1 = x_ref[...] + gt1_ref[0] * (y * _rms_scale(y) * gpost_ref[...])
    x1_ref[...] = x1
    h = x1 * _rms_scale(x1) * gpre_ref[...]
    h = h * (1.0 + sc2_ref[0]) + sh2_ref[0]
    _store_row_tiles(hp_ref, _pack_halves(h[:, :HALF], h[:, HALF:]))

    h_hi, h_lo = _bf16_terms(h)
    both = jnp.dot(h_hi, wr_ref[...], preferred_element_type=F32)
    lt = both[:, :128] + both[:, 128:] + jnp.dot(h_lo, wr_ref[:, :128], preferred_element_type=F32)
    logits = lt.T[0:N_EXPERTS, :]
    scores = jax.nn.sigmoid(logits)
    sel = scores + rb_ref[...]
    shape3 = (N_GROUPS, GROUP_SIZE, tm)
    sel3 = sel.reshape(shape3)
    i_in = lax.broadcasted_iota(I32, shape3, 1).astype(F32)
    m1 = jnp.max(sel3, axis=1, keepdims=True)
    f1 = _first_index(sel3 == m1, i_in, float(GROUP_SIZE), 1)
    m2 = jnp.max(jnp.where(i_in == f1, neg_inf, sel3), axis=1, keepdims=True)
    grp = jnp.broadcast_to(m1 + m2, shape3).reshape(N_EXPERTS, tm)
    i_e = lax.broadcasted_iota(I32, (N_EXPERTS, tm), 0)
    i_grp = (i_e >> 3).astype(F32)
    i_e = i_e.astype(F32)
    allowed = jnp.zeros((N_EXPERTS, tm), F32)
    for _ in range(TOPK_GROUPS):
        m = jnp.max(grp, axis=0, keepdims=True)
        pick = i_grp == _first_index(grp == m, i_grp, float(N_GROUPS), 0)
        allowed = jnp.where(pick, 1.0, allowed)
        grp = jnp.where(pick, neg_inf, grp)
    cand = jnp.where(allowed > 0.0, sel, neg_inf)
    onehot = jnp.zeros((N_EXPERTS, tm), F32)
    picks, wts = [], []
    for k in range(TOP_K):
        m = jnp.max(cand, axis=0, keepdims=True)
        f = _first_index(cand == m, i_e, float(N_EXPERTS), 0)
        pick = i_e == f
        picks.append(pick)
        eidx_ref[k:k + 1, :] = f.astype(I32)
        wts.append(jnp.sum(jnp.where(pick, scores, 0.0), axis=0, keepdims=True))
        onehot = jnp.where(pick, 1.0, onehot)
        cand = jnp.where(pick, neg_inf, cand)
    w_sum = wts[0]
    for k in range(1, TOP_K):
        w_sum = w_sum + wts[k]
    for k in range(TOP_K):
        wrow_ref[k:k + 1, :] = wts[k] / w_sum * ROUTED_SCALE
    wts_ref[...] = wrow_ref[...].T

    before = jnp.dot(onehot.astype(BF16), upper_ref[...], preferred_element_type=F32)
    before = before + run_ref[:, 0:1]
    for k in range(TOP_K):
        pos_ref[k:k + 1, :] = jnp.sum(jnp.where(picks[k], before, 0.0), axis=0, keepdims=True).astype(I32)
    counted = jnp.where(step > 0, 1.0, 0.0)
    run_ref[...] = run_ref[...] + counted * jnp.sum(onehot, axis=1, keepdims=True)
    cnt_ref[...] = run_ref[...].astype(I32)


def _mix_and_route(y_gla, y_pool, x2d, gt1, sc2, sh2, g_post, g_pre, w_out, w_router2, router_bias, seq,
                   part, n_parts):
    d = x2d.shape[1]
    rows = x2d.shape[0] // n_parts
    tiles = rows // MIX_TM
    first = part * tiles
    tiles_per_b = seq // MIX_TM
    proj = lambda i: (first + jnp.minimum(i, tiles - 1), 0)
    bmap = lambda i: ((first + jnp.maximum(i - 1, 0)) // tiles_per_b, 0, 0)
    xmap = lambda i: (first + jnp.maximum(i - 1, 0), 0)
    rmap = lambda i: (jnp.maximum(i - 1, 0), 0)
    tmap = lambda i: (0, jnp.maximum(i - 1, 0))
    upper = jnp.asarray(np.triu(np.ones((MIX_TM, MIX_TM), np.float32), 1), dtype=BF16)
    return pl.pallas_call(
        _mix_body,
        grid=(tiles + 1,),
        in_specs=[
            pl.BlockSpec((MIX_TM, GLA_WIDTH), proj),
            pl.BlockSpec((MIX_TM, POOL_WIDTH), proj),
            pl.BlockSpec((MIX_TM, d), xmap),
            pl.BlockSpec((1, 1, d), bmap),
            pl.BlockSpec((1, 1, d), bmap),
            pl.BlockSpec((1, 1, d), bmap),
            _resident((1, d)),
            _resident((1, d)),
            _resident((d, d)),
            _resident((d, 256)),
            _resident((N_EXPERTS, 1)),
            _resident((MIX_TM, MIX_TM)),
        ],
        out_specs=[
            pl.BlockSpec((MIX_TM, d), rmap),
            pl.BlockSpec((MIX_TM * ROW_SUBLANES, 128), rmap),
            pl.BlockSpec((TOP_K, MIX_TM), tmap),
            pl.BlockSpec((TOP_K, MIX_TM), tmap),
            pl.BlockSpec((MIX_TM, 128), rmap),
            pl.BlockSpec((N_EXPERTS, 128), lambda i: (0, 0)),
        ],
        out_shape=[
            jax.ShapeDtypeStruct((rows, d), F32),
            jax.ShapeDtypeStruct((rows * ROW_SUBLANES, 128), U32),
            jax.ShapeDtypeStruct((TOP_K, rows), I32),
            jax.ShapeDtypeStruct((TOP_K, rows), I32),
            jax.ShapeDtypeStruct((rows, 128), F32),
            jax.ShapeDtypeStruct((N_EXPERTS, 128), I32),
        ],
        scratch_shapes=[pltpu.VMEM((N_EXPERTS, 128), F32), pltpu.VMEM((128, MIX_TM), F32),
                        pltpu.VMEM((MIX_TM, d), F32)],
        compiler_params=_cparams(("arbitrary",)),
        name="mix_and_route",
    )(y_gla, y_pool, x2d, gt1, sc2, sh2, g_post, g_pre, w_out, w_router2, router_bias, upper)


def _swiglu(lo, hi, wg_ref, wu_ref, wd_ref):
    g = jnp.dot(lo, wg_ref[:HALF, :], preferred_element_type=F32)
    g = g + jnp.dot(hi, wg_ref[HALF:, :], preferred_element_type=F32)
    u = jnp.dot(lo, wu_ref[:HALF, :], preferred_element_type=F32)
    u = u + jnp.dot(hi, wu_ref[HALF:, :], preferred_element_type=F32)
    act = (_silu(g) * u).astype(BF16)
    return jnp.dot(act, wd_ref[...], preferred_element_type=F32)


def _shared_body(hp_ref, wg_ref, wu_ref, wd_ref, o_ref):
    lo, hi = _unpack_halves(_load_row_tiles(hp_ref))
    o_ref[...] = _swiglu(lo.astype(BF16), hi.astype(BF16), wg_ref, wu_ref, wd_ref).astype(BF16)


def _shared_expert(h_packed, w_sg, w_su, w_sd):
    d, ds = w_sg.shape
    rows = h_packed.shape[0] // ROW_SUBLANES
    return pl.pallas_call(
        _shared_body,
        grid=(rows // SHARED_TM,),
        in_specs=[
            pl.BlockSpec((SHARED_TM * ROW_SUBLANES, 128), lambda i: (i, 0)),
            _resident((d, ds)),
            _resident((d, ds)),
            _resident((ds, d)),
        ],
        out_specs=pl.BlockSpec((SHARED_TM, d), lambda i: (i, 0)),
        out_shape=jax.ShapeDtypeStruct((rows, d), BF16),
        compiler_params=_cparams(("arbitrary",)),
        name="shared_expert",
    )(h_packed, w_sg, w_su, w_sd)


def _offsets_body(ps_ref, e_ref, p_ref, d_ref):
    e = e_ref[...]
    d = p_ref[...]
    for x in range(N_EXPERTS):
        d = d + jnp.where(e == x, ps_ref[x], 0)
    d_ref[...] = d


def _route_offsets(eidx_t, pos_t, pstarts):
    k, rows = eidx_t.shape
    spec = pl.BlockSpec((k, OFFS_TN), lambda i, ps: (0, i))
    grid_spec = pltpu.PrefetchScalarGridSpec(
        num_scalar_prefetch=1, grid=(rows // OFFS_TN,), in_specs=[spec, spec], out_specs=spec)
    return pl.pallas_call(
        _offsets_body,
        grid_spec=grid_spec,
        out_shape=jax.ShapeDtypeStruct((k, rows), I32),
        compiler_params=_cparams(("arbitrary",)),
        name="route_offsets",
    )(pstarts, eidx_t, pos_t)


def _expert_body(be_ref, nu_ref, valid_ref, nxt_ref, slot_ref, x_ref, wg_hbm, wu_hbm, wd_hbm, y_ref,
                 wg_f, wu_f, wd_f, wg_b, wu_b, wd_b, sems):
    i = pl.program_id(0)
    valid = valid_ref[i]
    expert = be_ref[i]
    slot = slot_ref[i]

    def weight_copies(e, s):
        return [pltpu.make_async_copy(src.at[e], dst.at[s], sems.at[s, j])
                for j, (src, dst) in enumerate(((wg_hbm, wg_f), (wu_hbm, wu_f), (wd_hbm, wd_f)))]

    @pl.when(i == 0)
    def _():
        for cp in weight_copies(expert, slot):
            cp.start()

    @pl.when((i == 0) | (expert != be_ref[jnp.maximum(i - 1, 0)]))
    def _():
        for cp in weight_copies(expert, slot):
            cp.wait()

        @pl.when(nxt_ref[i] != expert)
        def _():
            for cp in weight_copies(nxt_ref[i], 1 - slot):
                cp.start()

        wg_b[...] = wg_f[slot].astype(BF16)
        wu_b[...] = wu_f[slot].astype(BF16)
        wd_b[...] = wd_f[slot].astype(BF16)

    def compute(n_rows):
        tiles = pl.ds(0, n_rows * ROW_SUBLANES)
        lo, hi = _unpack_halves(_load_row_tiles(x_ref.at[tiles]))
        y = _swiglu(lo.astype(BF16), hi.astype(BF16), wg_b, wu_b, wd_b)
        _store_row_tiles(y_ref.at[tiles], _pack_halves(y[:, :HALF], y[:, HALF:]))

    quarter = MOE_TILE // 4
    for n in range(1, 5):
        @pl.when((valid > (n - 1) * quarter) & (valid <= n * quarter))
        def _(n=n):
            compute(n * quarter)


def _experts(xs, block_expert, n_used, valid, next_expert, slot, w_eg, w_eu, w_ed):
    n_rows = xs.shape[0] // ROW_SUBLANES
    n_blocks = n_rows // MOE_TILE
    d, de = w_eg.shape[1], w_eg.shape[2]
    row_map = lambda i, be, nu, *_: (jnp.minimum(i, nu[0] - 1), 0)
    grid_spec = pltpu.PrefetchScalarGridSpec(
        num_scalar_prefetch=5,
        grid=(n_blocks,),
        in_specs=[
            pl.BlockSpec((MOE_TILE * ROW_SUBLANES, 128), row_map),
            pl.BlockSpec(memory_space=pl.ANY),
            pl.BlockSpec(memory_space=pl.ANY),
            pl.BlockSpec(memory_space=pl.ANY),
        ],
        out_specs=pl.BlockSpec((MOE_TILE * ROW_SUBLANES, 128), row_map),
        scratch_shapes=[
            pltpu.VMEM((2, d, de), F32),
            pltpu.VMEM((2, d, de), F32),
            pltpu.VMEM((2, de, d), F32),
            pltpu.VMEM((d, de), BF16),
            pltpu.VMEM((d, de), BF16),
            pltpu.VMEM((de, d), BF16),
            pltpu.SemaphoreType.DMA((2, 3)),
        ],
    )
    return pl.pallas_call(
        _expert_body,
        grid_spec=grid_spec,
        out_shape=jax.ShapeDtypeStruct((n_rows * ROW_SUBLANES, 128), U32),
        compiler_params=_cparams(("arbitrary",)),
        name="experts",
    )(block_expert, n_used, valid, next_expert, slot, xs, w_eg, w_eu, w_ed)


def _sc_mesh():
    return plsc.VectorSubcoreMesh(core_axis_name="c", subcore_axis_name="s")


def _sc_worker():
    return lax.axis_index("s") * SC_CORES + lax.axis_index("c")


def _sc_gather_rows(table, idx):
    n_idx = idx.shape[0]
    per_worker = n_idx // SC_WORKERS
    n_chunks = per_worker // SC_CHUNK
    assert per_worker * SC_WORKERS == n_idx and n_chunks * SC_CHUNK == per_worker and n_chunks % 2 == 0
    row_shape = table.shape[1:]

    @functools.partial(
        pl.kernel, mesh=_sc_mesh(),
        out_type=jax.ShapeDtypeStruct((n_idx,) + row_shape, table.dtype),
        scratch_types=[
            pltpu.VMEM((per_worker,), I32),
            pltpu.VMEM((SC_CHUNK,) + row_shape, table.dtype),
            pltpu.VMEM((SC_CHUNK,) + row_shape, table.dtype),
        ] + [pltpu.SemaphoreType.DMA] * 4,
    )
    def gather(table_hbm, idx_hbm, out_hbm, idx_v, buf0, buf1, g0, g1, w0, w1):
        bufs, gsem, wsem = (buf0, buf1), (g0, g1), (w0, w1)
        base = _sc_worker() * per_worker
        pltpu.sync_copy(idx_hbm.at[pl.ds(base, per_worker)], idx_v)

        def fetch(j, b):
            return pltpu.make_async_copy(table_hbm.at[idx_v.at[pl.ds(j * SC_CHUNK, SC_CHUNK)]], bufs[b], gsem[b])

        def flush(j, b):
            return pltpu.make_async_copy(bufs[b], out_hbm.at[pl.ds(base + j * SC_CHUNK, SC_CHUNK)], wsem[b])

        fetch(0, 0).start()
        fetch(0, 0).wait()
        fetch(1, 1).start()
        flush(0, 0).start()

        @pl.loop(1, n_chunks - 1, step=2)
        def _(j):
            for off in range(2):
                jj, b = j + off, (1 + off) % 2
                fetch(jj, b).wait()
                flush(jj - 1, 1 - b).wait()
                fetch(jj + 1, 1 - b).start()
                flush(jj, b).start()

        fetch(n_chunks - 1, 1).wait()
        flush(n_chunks - 1, 1).start()
        flush(n_chunks - 2, 0).wait()
        flush(n_chunks - 1, 1).wait()

    return gather(table, idx)


def _sc_scatter_rows(rows, dest_t, n_out):
    n_rows = rows.shape[0]
    per_worker = n_rows // SC_WORKERS
    n_chunks = per_worker // SC_CHUNK
    assert per_worker * SC_WORKERS == n_rows and n_chunks * SC_CHUNK == per_worker
    row_shape = rows.shape[1:]
    idx_w = dest_t.reshape(TOP_K, SC_WORKERS, n_chunks, SC_CHUNK).transpose(1, 2, 0, 3)
    idx_w = idx_w.reshape(SC_WORKERS, n_chunks * TOP_K, SC_CHUNK)

    @functools.partial(
        pl.kernel, mesh=_sc_mesh(),
        out_type=jax.ShapeDtypeStruct((n_out,) + row_shape, rows.dtype),
        scratch_types=[
            pltpu.VMEM((n_chunks * TOP_K, SC_CHUNK), I32),
            pltpu.VMEM((SC_CHUNK,) + row_shape, rows.dtype),
            pltpu.VMEM((SC_CHUNK,) + row_shape, rows.dtype),
        ] + [pltpu.SemaphoreType.DMA] * 4,
    )
    def scatter(rows_hbm, idx_hbm, out_hbm, idx_v, buf0, buf1, r0, r1, s0, s1):
        bufs, rsem, ssem = (buf0, buf1), (r0, r1), (s0, s1)
        worker = _sc_worker()
        base = worker * per_worker
        pltpu.sync_copy(idx_hbm.at[worker], idx_v)

        def fetch(j, b):
            return pltpu.make_async_copy(rows_hbm.at[pl.ds(base + j * SC_CHUNK, SC_CHUNK)], bufs[b], rsem[b])

        def send(j, k, b):
            return pltpu.make_async_copy(bufs[b], out_hbm.at[idx_v.at[j * TOP_K + k]], ssem[b])

        fetch(0, 0).start()
        for j in range(n_chunks):
            b = j % 2
            fetch(j, b).wait()
            if j + 1 < n_chunks:
                if j >= 1:
                    for k in range(TOP_K):
                        send(j - 1, k, 1 - b).wait()
                fetch(j + 1, 1 - b).start()
            for k in range(TOP_K):
                send(j, k, b).start()
        for j in range(max(n_chunks - 2, 0), n_chunks):
            for k in range(TOP_K):
                send(j, k, j % 2).wait()

    return scatter(rows, idx_w)


def _combine_body(*refs):
    yk_refs = refs[:TOP_K]
    w_ref, shr_ref, x1_ref, gt2_ref, gpost_ref = refs[TOP_K:TOP_K + 5]
    o_ref = refs[-1]
    tt = x1_ref.shape[0]
    w = w_ref[...]
    ssq = jnp.zeros((tt, 1), F32)
    for c in range(HALF // 128):
        c_lo = slice(c * 128, (c + 1) * 128)
        c_hi = slice(HALF + c * 128, HALF + (c + 1) * 128)
        y_lo = shr_ref[:, c_lo].astype(F32)
        y_hi = shr_ref[:, c_hi].astype(F32)
        for k in range(TOP_K):
            lo, hi = _unpack_halves(yk_refs[k][pl.ds(c, tt, stride=ROW_SUBLANES), :])
            y_lo = y_lo + w[:, k:k + 1] * lo
            y_hi = y_hi + w[:, k:k + 1] * hi
        ssq = ssq + jnp.sum(y_lo * y_lo, axis=-1, keepdims=True) + jnp.sum(y_hi * y_hi, axis=-1, keepdims=True)
        o_ref[:, c_lo] = y_lo
        o_ref[:, c_hi] = y_hi
    scale = lax.rsqrt(ssq / D_MODEL + EPS)
    o_ref[...] = x1_ref[...] + gt2_ref[0] * (o_ref[...] * scale * gpost_ref[...])


def _combine(yu, wts, shared, x1, gt2, g_post, seq, part, n_parts, prev_out):
    rows, d = x1.shape
    tiles = rows // COMB_TT
    first = part * tiles
    tiles_per_b = seq // COMB_TT
    loc = lambda i: (i, 0)
    yk_specs = [pl.BlockSpec((COMB_TT * ROW_SUBLANES, 128), functools.partial(lambda i, k: (k * tiles + i, 0), k=k))
                for k in range(TOP_K)]
    in_specs = yk_specs + [
        pl.BlockSpec((COMB_TT, 128), loc),
        pl.BlockSpec((COMB_TT, d), loc),
        pl.BlockSpec((COMB_TT, d), loc),
        pl.BlockSpec((1, 1, d), lambda i: ((first + i) // tiles_per_b, 0, 0)),
        _resident((1, d)),
    ]
    args = [yu] * TOP_K + [wts, shared, x1, gt2, g_post]
    aliases = {}
    if prev_out is not None:
        aliases = {len(args): 0}
        in_specs.append(pl.BlockSpec(memory_space=pl.ANY))
        args.append(prev_out)
    return pl.pallas_call(
        _combine_body,
        grid=(tiles,),
        in_specs=in_specs,
        out_specs=pl.BlockSpec((COMB_TT, d), lambda i: (first + i, 0)),
        out_shape=jax.ShapeDtypeStruct((rows * n_parts, d), F32),
        input_output_aliases=aliases,
        compiler_params=_cparams(("arbitrary",)),
        name="combine",
    )(*args)


def kernel(x, c, ctx, c_ctx, w_mod, b_mod, norm_mix_pre, norm_mix_post, norm_ffn_pre, norm_ffn_post, w_in, w_a2_fwd, b_a_fwd, w_a2_bwd, b_a_bwd, gla_norm, w_pool, pool_scale, w_out, w_router, router_bias, w_exp_gate, w_exp_up, w_exp_down, w_sh_gate, w_sh_up, w_sh_down):
    batch, seq, d = x.shape
    n_ctx = ctx.shape[1]
    assert w_mod.shape[0] == 1 and d == D_MODEL
    assert seq % (2 * SUPER) == 0 and n_ctx % SUPER == 0 and seq % PROJ_TM == 0 and (batch * n_ctx) % PROJ_TM == 0
    rows = batch * seq

    mod_rows = 16
    c_all = jnp.concatenate([c, c_ctx[None, :], jnp.zeros((mod_rows - batch - 1, d), F32)], axis=0)
    mod_all = _modulation(c_all, w_mod[0], b_mod[0][None, :])
    sh1, sc1, gt1, sh2, sc2, gt2 = [m.reshape(batch, 1, d) for m in jnp.split(mod_all[:batch], 6, axis=-1)]
    csh1 = mod_all[batch, 0:d].reshape(1, 1, d)
    csc1 = mod_all[batch, d:2 * d].reshape(1, 1, d)

    kw, gw = GLA_KEY_WIDTH, GLA_WIDTH
    a0 = 2 * kw + 2 * gw
    w_in0 = w_in[0]
    w_bf = w_in0.astype(BF16)
    w_a = jnp.pad(w_bf[:, a0:a0 + 2 * GLA_RANK], ((0, 0), (0, 128 - 2 * GLA_RANK)))
    lat_pieces = [(w_bf, a0, 0), (w_bf[:, a0 + 2 * GLA_RANK:], POOL_WIDTH, 0)]
    ctx_pieces = [(w_bf, gw, 2 * kw // gw), (w_bf, kw, 1)]
    w2f = jnp.pad(w_a2_fwd[0], ((0, 128 - GLA_RANK), (0, 0))).astype(BF16)
    w2b = jnp.pad(w_a2_bwd[0], ((GLA_RANK, 128 - 2 * GLA_RANK), (0, 0))).astype(BF16)
    g_mix_pre = norm_mix_pre[0][None, :]
    w_router2 = jnp.concatenate(_bf16_terms(jnp.pad(w_router[0], ((0, 0), (0, 128 - N_EXPERTS)))), axis=1)

    u_ctx, a_ctx = _in_projection(ctx.reshape(batch * n_ctx, d), g_mix_pre, csc1, csh1, ctx_pieces, w_a,
                                  batch * n_ctx)
    u_lat, a_lat = _in_projection(x.reshape(rows, d), g_mix_pre, sc1, sh1, lat_pieces, w_a, seq)

    y_gla = _gla(u_lat, a_lat, u_ctx, a_ctx, w2f, b_a_fwd[0][None, :], w2b, b_a_bwd[0][None, :],
                 gla_norm[0][None, :], batch, seq, n_ctx)
    y_pool = _pool_mixer(u_lat, _col_window_matrices(), w_pool[0].astype(BF16), pool_scale[0][None, :],
                         batch, seq)

    w_out_bf = w_out[0].astype(BF16)
    shared_w = (w_sh_gate[0].astype(BF16), w_sh_up[0].astype(BF16), w_sh_down[0].astype(BF16))
    x2d = x.reshape(rows, d)
    part_rows = rows // MOE_PARTS
    n_blocks = part_rows * TOP_K // MOE_TILE + N_EXPERTS
    e_ids = jnp.arange(N_EXPERTS, dtype=I32)

    mixed = [_mix_and_route(y_gla, y_pool, x2d, gt1, sc2, sh2, norm_mix_post[0][None, :],
                            norm_ffn_pre[0][None, :], w_out_bf, w_router2, router_bias[0][:, None], seq,
                            p, MOE_PARTS) for p in range(MOE_PARTS)]
    out = None
    for p, (x1, h_packed, eidx_t, pos_t, wts, counts) in enumerate(mixed):
        shared = _shared_expert(h_packed, *shared_w)
        counts = counts[:, 0]
        padded = (counts + MOE_TILE - 1) // MOE_TILE * MOE_TILE
        pends = jnp.cumsum(padded)
        pstarts = pends - padded
        dest_t = _route_offsets(eidx_t, pos_t, pstarts.astype(I32))
        n_used = (pends[-1] // MOE_TILE).astype(I32)
        blk = jnp.minimum(jnp.arange(n_blocks, dtype=I32), n_used - 1)
        block_expert = jnp.sum((blk * MOE_TILE)[:, None] >= pends[None, :], axis=1).astype(I32)
        block_expert = jnp.minimum(block_expert, N_EXPERTS - 1)
        is_block_expert = block_expert[:, None] == e_ids[None, :]
        per_block = lambda v: jnp.sum(jnp.where(is_block_expert, v[None, :], 0), axis=1).astype(I32)
        valid = jnp.clip(per_block(pstarts + counts) - blk * MOE_TILE, 0, MOE_TILE)
        valid = jnp.where(jnp.arange(n_blocks, dtype=I32) < n_used, valid, 0).astype(I32)
        has_rows = padded > 0
        later = jnp.where((e_ids[None, :] > e_ids[:, None]) & has_rows[None, :], e_ids[None, :], N_EXPERTS)
        next_e = jnp.min(later, axis=1)
        next_e = jnp.where(next_e == N_EXPERTS, e_ids, next_e)
        slot_e = (jnp.cumsum(has_rows.astype(I32)) - has_rows.astype(I32)) % 2

        xs = _sc_scatter_rows(h_packed.reshape(-1, ROW_SUBLANES, 128), dest_t, n_blocks * MOE_TILE)
        ys = _experts(xs.reshape(-1, 128), block_expert, n_used.reshape(1), valid, per_block(next_e),
                      per_block(slot_e), w_exp_gate[0], w_exp_up[0], w_exp_down[0])
        yu = _sc_gather_rows(ys.reshape(-1, ROW_SUBLANES, 128), dest_t.reshape(-1))
        out = _combine(yu.reshape(-1, 128), wts, shared, x1, gt2, norm_ffn_post[0][None, :], seq,
                       p, MOE_PARTS, out)
    return out.reshape(batch, seq, d)
```

```python
import functools

import numpy as np
import jax
import jax.numpy as jnp
from jax import lax
from jax.experimental import pallas as pl
from jax.experimental.pallas import tpu as pltpu
from jax.experimental.pallas import tpu_sc as plsc

F32 = jnp.float32
BF16 = jnp.bfloat16
I32 = jnp.int32
U32 = jnp.uint32

D_MODEL = 2048
GRID_W = 64
GLA_HEADS = 4
GLA_DK = 128
GLA_DV = 256
GLA_KEY_WIDTH = GLA_HEADS * GLA_DK
GLA_WIDTH = GLA_HEADS * GLA_DV
GLA_RANK = 16
GLA_TAU = 16.0
GLA_CHUNK = 64
POOL_WIDTH = 1024
POOL_WINDOWS = (2, 4, 8, 16)
POOL_GROUP = 256
N_EXPERTS = 64
TOP_K = 8
N_GROUPS = 8
GROUP_SIZE = N_EXPERTS // N_GROUPS
TOPK_GROUPS = 4
D_EXPERT = 512
D_SHARED = 512
ROUTED_SCALE = 2.5
EPS = 1e-6

HALF = D_MODEL // 2
SUPER = 4 * GLA_CHUNK
GLA_HPS = 4
POOL_PAD = 8 * GRID_W
VMEM_LIMIT = 56 * 1024 * 1024

MOD_TN = 1024
PROJ_TM = 512
PROJ_TN = 512
MIX_TM = 512
MOE_TILE = 512
COMB_TT = 256
MOE_PARTS = 2
ROW_SUBLANES = 8
SC_CORES = 2
SC_WORKERS = 32
SC_CHUNK = 32
OFFS_TN = 2048


def _cparams(sem):
    return pltpu.CompilerParams(dimension_semantics=sem, vmem_limit_bytes=VMEM_LIMIT)


def _resident(shape):
    nd = len(shape)
    return pl.BlockSpec(shape, lambda *_: (0,) * nd, pipeline_mode=pl.Buffered(1))


def _silu(v):
    return v * jax.nn.sigmoid(v)


def _pack_halves(lo, hi):
    lo_b = lax.bitcast_convert_type(lo.astype(BF16).astype(F32), U32)
    hi_b = lax.bitcast_convert_type(hi.astype(BF16).astype(F32), U32)
    return (hi_b & jnp.uint32(0xFFFF0000)) | (lo_b >> 16)


def _unpack_halves(p):
    lo = lax.bitcast_convert_type(p << 16, F32)
    hi = lax.bitcast_convert_type(p & jnp.uint32(0xFFFF0000), F32)
    return lo, hi


def _bf16_terms(x):
    hi = lax.bitcast_convert_type(lax.bitcast_convert_type(x, U32) & jnp.uint32(0xFFFF0000), F32)
    return hi.astype(BF16), (x - hi).astype(BF16)


def _store_row_tiles(ref, packed):
    n = packed.shape[0]
    for c in range(HALF // 128):
        ref[pl.ds(c, n, stride=ROW_SUBLANES), :] = packed[:, c * 128:(c + 1) * 128]


def _load_row_tiles(ref):
    n = ref.shape[0] // ROW_SUBLANES
    return jnp.concatenate([ref[pl.ds(c, n, stride=ROW_SUBLANES), :] for c in range(HALF // 128)], axis=1)


def _mod_body(c_ref, w_ref, b_ref, o_ref):
    s_hi, s_lo = _bf16_terms(_silu(c_ref[...]))
    w_hi, w_lo = _bf16_terms(w_ref[...])
    acc = jnp.dot(s_hi, w_hi, preferred_element_type=F32)
    acc = acc + jnp.dot(s_lo, w_hi, preferred_element_type=F32)
    acc = acc + jnp.dot(s_hi, w_lo, preferred_element_type=F32)
    o_ref[...] = acc + b_ref[...]


def _modulation(c_all, w_mod, b_mod):
    rows, d = c_all.shape
    n = w_mod.shape[1]
    return pl.pallas_call(
        _mod_body,
        grid=(n // MOD_TN,),
        in_specs=[
            pl.BlockSpec((rows, d), lambda j: (0, 0)),
            pl.BlockSpec((d, MOD_TN), lambda j: (0, j)),
            pl.BlockSpec((1, MOD_TN), lambda j: (0, j)),
        ],
        out_specs=pl.BlockSpec((rows, MOD_TN), lambda j: (0, j)),
        out_shape=jax.ShapeDtypeStruct((rows, n), F32),
        compiler_params=_cparams(("arbitrary",)),
        name="modulation",
    )(c_all, w_mod, b_mod)


def _rms_scale(x):
    return lax.rsqrt(jnp.mean(x * x, axis=-1, keepdims=True) + EPS)


def _inproj_body(x_ref, g_ref, sc_ref, sh_ref, *refs):
    w_refs, (wa_ref, o_ref, a_ref) = refs[:-3], refs[-3:]
    x = x_ref[...]
    h = x * _rms_scale(x) * g_ref[...]
    h = h * (1.0 + sc_ref[0]) + sh_ref[0]
    hb = h.astype(BF16)
    col = 0
    for w_ref in w_refs:
        for n in range(w_ref.shape[1] // PROJ_TN):
            cols = slice(n * PROJ_TN, (n + 1) * PROJ_TN)
            o_ref[:, col:col + PROJ_TN] = jnp.dot(hb, w_ref[:, cols], preferred_element_type=F32).astype(BF16)
            col += PROJ_TN
    a_ref[...] = jnp.dot(hb, wa_ref[...], preferred_element_type=F32)


def _in_projection(x2d, gain, sc, sh, pieces, w_a, rows_per_mod):
    rows, d = x2d.shape
    n_main = sum(width for _, width, _ in pieces)
    tiles_per_mod = rows_per_mod // PROJ_TM
    mod_map = lambda i: (i // tiles_per_mod, 0, 0)
    piece_specs = [pl.BlockSpec((d, width), functools.partial(lambda i, b: (0, b), b=block),
                                pipeline_mode=pl.Buffered(1)) for _, width, block in pieces]
    return pl.pallas_call(
        _inproj_body,
        grid=(rows // PROJ_TM,),
        in_specs=[
            pl.BlockSpec((PROJ_TM, d), lambda i: (i, 0)),
            _resident((1, d)),
            pl.BlockSpec((1, 1, d), mod_map),
            pl.BlockSpec((1, 1, d), mod_map),
            *piece_specs,
            _resident((d, 128)),
        ],
        out_specs=[
            pl.BlockSpec((PROJ_TM, n_main), lambda i: (i, 0)),
            pl.BlockSpec((PROJ_TM, 128), lambda i: (i, 0)),
        ],
        out_shape=[
            jax.ShapeDtypeStruct((rows, n_main), BF16),
            jax.ShapeDtypeStruct((rows, 128), F32),
        ],
        compiler_params=_cparams(("arbitrary",)),
        name="in_projection",
    )(x2d, gain, sc, sh, *[w for w, _, _ in pieces], w_a)


def _log_sigmoid(z):
    return jnp.minimum(z, 0.0) - jnp.log1p(jnp.exp(-jnp.abs(z)))


def _gla_cumulative_decay(a, w2, ba, tri):
    z = jnp.dot(a.astype(BF16), w2, preferred_element_type=F32) + ba
    g = _log_sigmoid(z) * (1.0 / GLA_TAU)
    g_hi, g_lo = _bf16_terms(g)
    return jnp.dot(tri, g_hi, preferred_element_type=F32) + jnp.dot(tri, g_lo, preferred_element_type=F32)


def _gla_prep(q, k, G, reverse):
    nc = SUPER // GLA_CHUNK
    G = G.reshape(nc, GLA_CHUNK, GLA_DK)
    end_row = 0 if reverse else GLA_CHUNK - 1
    mid_row = GLA_CHUNK - 1 - GLA_CHUNK // 2 if reverse else GLA_CHUNK // 2
    g_end = G[:, end_row:end_row + 1, :]
    g_mid = G[:, mid_row:mid_row + 1, :]
    k4 = k.astype(F32).reshape(nc, GLA_CHUNK, GLA_DK)
    dec = jnp.broadcast_to(jnp.exp(g_end), (nc, 8, GLA_DK)).reshape(nc * 8, GLA_DK)
    flat = lambda t: t.reshape(SUPER, GLA_DK).astype(BF16)
    if q is None:
        return None, None, None, flat(k4 * jnp.exp(g_end - G)), dec
    q4 = q.astype(F32).reshape(nc, GLA_CHUNK, GLA_DK) * (GLA_DK ** -0.5)
    qg = q4 * jnp.exp(G - g_mid)
    kg = k4 * jnp.exp(g_mid - G)
    qe = qg * jnp.exp(g_mid)
    kd = kg * jnp.exp(g_end - g_mid)
    return flat(qg), flat(kg), flat(qe), flat(kd), dec


def _gla_apply(qg, kg, qe, kd, dec, v, mask, st_ref, reverse):
    nc = SUPER // GLA_CHUNK
    o = None
    if qg is not None:
        att = lax.dot_general(qg, kg, (((1,), (1,)), ((), ())), preferred_element_type=F32)
        att = jnp.where(mask, att, 0.0).astype(BF16)
        o = jnp.dot(att, v, preferred_element_type=F32)
    outs = [None] * nc
    order = range(nc - 1, -1, -1) if reverse else range(nc)
    for c in order:
        rows = slice(c * GLA_CHUNK, (c + 1) * GLA_CHUNK)
        st = st_ref[...]
        if qg is not None:
            inter = lax.dot_general(qe[rows], st.astype(BF16), (((1,), (1,)), ((), ())),
                                    preferred_element_type=F32)
            outs[c] = o[rows] + inter
        upd = lax.dot_general(v[rows], kd[rows], (((0,), (0,)), ((), ())), preferred_element_type=F32)
        st_ref[...] = st * dec[8 * c:8 * c + 1, :] + upd
    if qg is None:
        return None
    return jnp.concatenate(outs, axis=0)


def _gla_body(q_ref, k_ref, v_ref, r_ref, a_ref, kc_ref, vc_ref, ac_ref,
              w2f_ref, baf_ref, w2b_ref, bab_ref, gn_ref, y_ref, o_acc, st, ops_a, ops_b, dec_a, dec_b,
              *, n_ctx):
    n_sup = q_ref.shape[0] // SUPER
    row = lax.broadcasted_iota(I32, (SUPER, SUPER), 0)
    col = lax.broadcasted_iota(I32, (SUPER, SUPER), 1)
    same_chunk = (row >> 6) == (col >> 6)
    mask_f = same_chunk & (col <= row)
    mask_b = same_chunk & (col >= row)
    tri_f = jnp.where(mask_f, 1.0, 0.0).astype(BF16)
    tri_b = jnp.where(mask_b, 1.0, 0.0).astype(BF16)
    heads = range(GLA_HPS)
    kcol = [slice(h * GLA_DK, (h + 1) * GLA_DK) for h in heads]
    vcol = [slice(h * GLA_DV, (h + 1) * GLA_DV) for h in heads]
    dirs = ((False, w2f_ref, baf_ref, tri_f, mask_f), (True, w2b_ref, bab_ref, tri_b, mask_b))

    st[...] = jnp.zeros_like(st)
    n_csup = n_ctx // SUPER
    for s in range(n_csup):
        for d, (reverse, w2_ref, ba_ref, tri, mask) in enumerate(dirs):
            sc = n_csup - 1 - s if reverse else s
            rows = slice(sc * SUPER, (sc + 1) * SUPER)
            G = _gla_cumulative_decay(ac_ref[rows, :], w2_ref[...], ba_ref[...], tri)
            for h in heads:
                _, _, _, kd, dec = _gla_prep(None, kc_ref[rows, kcol[h]], G[:, kcol[h]], reverse)
                _gla_apply(None, None, None, kd, dec, vc_ref[rows, vcol[h]], mask, st.at[d, h], reverse)

    o_acc[...] = jnp.zeros_like(o_acc)

    def rows_of(i, reverse):
        sc = n_sup - 1 - i if reverse else i
        return pl.ds(pl.multiple_of(sc * SUPER, SUPER), SUPER)

    def prepare(i, ops, decs, d):
        reverse, w2_ref, ba_ref, tri, mask = dirs[d]
        rows = rows_of(i, reverse)
        G = _gla_cumulative_decay(a_ref[rows, :], w2_ref[...], ba_ref[...], tri)
        for h in heads:
            vals = _gla_prep(q_ref[rows, kcol[h]], k_ref[rows, kcol[h]], G[:, kcol[h]], reverse)
            for j in range(4):
                ops[GLA_HPS * d + h, j] = vals[j]
            decs[GLA_HPS * d + h] = vals[4]

    def apply(i, ops, decs, d):
        reverse, w2_ref, ba_ref, tri, mask = dirs[d]
        rows = rows_of(i, reverse)
        for h in heads:
            ci = GLA_HPS * d + h
            out = _gla_apply(ops[ci, 0], ops[ci, 1], ops[ci, 2], ops[ci, 3], decs[ci],
                             v_ref[rows, vcol[h]], mask, st.at[d, h], reverse)
            o_acc[rows, vcol[h]] += out

    for d in range(2):
        prepare(0, ops_a, dec_a, d)

    def step(j, carry):
        i = 2 * j
        for d in range(2):
            prepare(i + 1, ops_b, dec_b, d)
            apply(i, ops_a, dec_a, d)
        nxt = jnp.minimum(i + 2, n_sup - 1)
        for d in range(2):
            prepare(nxt, ops_a, dec_a, d)
            apply(i + 1, ops_b, dec_b, d)
        return carry

    lax.fori_loop(0, n_sup // 2, step, 0)

    for h in heads:
        o = o_acc[:, vcol[h]]
        o = o * _rms_scale(o) * gn_ref[:, vcol[h]]
        y_ref[:, vcol[h]] = (o * _silu(r_ref[:, vcol[h]].astype(F32))).astype(BF16)


def _gla(u_lat, a_lat, u_ctx, a_ctx, w2f, baf, w2b, bab, gla_norm, batch, seq, n_ctx):
    groups = GLA_HEADS // GLA_HPS
    kw, vw = GLA_HPS * GLA_DK, GLA_HPS * GLA_DV
    kb = GLA_KEY_WIDTH // kw
    vb = 2 * GLA_KEY_WIDTH // vw
    rb = vb + groups
    assert GLA_KEY_WIDTH % kw == 0 and (2 * GLA_KEY_WIDTH) % vw == 0 and GLA_WIDTH % kw == 0
    ckb = GLA_WIDTH // kw
    return pl.pallas_call(
        functools.partial(_gla_body, n_ctx=n_ctx),
        grid=(batch, groups),
        in_specs=[
            pl.BlockSpec((seq, kw), lambda b, h: (b, h)),
            pl.BlockSpec((seq, kw), lambda b, h: (b, kb + h)),
            pl.BlockSpec((seq, vw), lambda b, h: (b, vb + h)),
            pl.BlockSpec((seq, vw), lambda b, h: (b, rb + h)),
            pl.BlockSpec((seq, 128), lambda b, h: (b, 0)),
            pl.BlockSpec((n_ctx, kw), lambda b, h: (b, ckb + h)),
            pl.BlockSpec((n_ctx, vw), lambda b, h: (b, h)),
            pl.BlockSpec((n_ctx, 128), lambda b, h: (b, 0)),
            pl.BlockSpec((128, kw), lambda b, h: (0, h)),
            pl.BlockSpec((1, kw), lambda b, h: (0, h)),
            pl.BlockSpec((128, kw), lambda b, h: (0, h)),
            pl.BlockSpec((1, kw), lambda b, h: (0, h)),
            pl.BlockSpec((1, vw), lambda b, h: (0, h)),
        ],
        out_specs=pl.BlockSpec((seq, vw), lambda b, h: (b, h)),
        out_shape=jax.ShapeDtypeStruct((batch * seq, GLA_WIDTH), BF16),
        scratch_shapes=[
            pltpu.VMEM((seq, vw), F32),
            pltpu.VMEM((2, GLA_HPS, GLA_DV, GLA_DK), F32),
            pltpu.VMEM((2 * GLA_HPS, 4, SUPER, GLA_DK), BF16),
            pltpu.VMEM((2 * GLA_HPS, 4, SUPER, GLA_DK), BF16),
            pltpu.VMEM((2 * GLA_HPS, 8 * (SUPER // GLA_CHUNK), GLA_DK), F32),
            pltpu.VMEM((2 * GLA_HPS, 8 * (SUPER // GLA_CHUNK), GLA_DK), F32),
        ],
        compiler_params=_cparams(("arbitrary", "arbitrary")),
        name="gla",
    )(u_lat, u_lat, u_lat, u_lat, a_lat, u_ctx, u_ctx, a_ctx, w2f, baf, w2b, bab, gla_norm)


def _col_window_matrices():
    t = np.arange(SUPER)
    r, c = t // GRID_W, t % GRID_W
    mats = []
    for w in POOL_WINDOWS:
        lo = np.maximum(c - w // 2, 0)[:, None]
        hi = np.minimum(c + w // 2, GRID_W)[:, None]
        m = (r[:, None] == r[None, :]) & (c[None, :] >= lo) & (c[None, :] < hi)
        mats.append(m.astype(np.float32))
    return jnp.asarray(np.stack(mats), dtype=BF16)


def _pool_body(p_ref, cw_ref, wp_ref, ps_ref, y_ref, pad_ref):
    seq = p_ref.shape[0]
    n_rows = seq // GRID_W
    zeros = jnp.zeros((POOL_PAD, POOL_GROUP), F32)
    pad_ref[0:POOL_PAD, :] = zeros
    pad_ref[POOL_PAD + seq:POOL_PAD + seq + POOL_PAD, :] = zeros
    t = lax.broadcasted_iota(I32, (seq, POOL_GROUP), 0)
    r = t >> 6
    c = t & (GRID_W - 1)
    for gi, w in enumerate(POOL_WINDOWS):
        cols = slice(gi * POOL_GROUP, (gi + 1) * POOL_GROUP)
        cw = cw_ref[gi]
        for j in range(seq // SUPER):
            rows = slice(j * SUPER, (j + 1) * SUPER)
            pad_ref[POOL_PAD + j * SUPER:POOL_PAD + (j + 1) * SUPER, :] = jnp.dot(
                cw, p_ref[rows, cols], preferred_element_type=F32)
        total = None
        for d in range(-(w // 2), w // 2):
            start = POOL_PAD + d * GRID_W
            part = pad_ref[start:start + seq, :]
            total = part if total is None else total + part
        cnt_r = jnp.minimum(r + w // 2, n_rows) - jnp.maximum(r - w // 2, 0)
        cnt_c = jnp.minimum(c + w // 2, GRID_W) - jnp.maximum(c - w // 2, 0)
        mean = total / (cnt_r * cnt_c).astype(F32)
        diff = (mean - p_ref[:, cols].astype(F32)).astype(BF16)
        y = jnp.dot(diff, wp_ref[gi], preferred_element_type=F32) * ps_ref[:, cols]
        y_ref[:, cols] = y.astype(BF16)


def _pool_mixer(u_lat, col_mats, w_pool, pool_scale, batch, seq):
    pb = (u_lat.shape[1] - POOL_WIDTH) // POOL_WIDTH
    ng = len(POOL_WINDOWS)
    return pl.pallas_call(
        _pool_body,
        grid=(batch,),
        in_specs=[
            pl.BlockSpec((seq, POOL_WIDTH), lambda b: (b, pb)),
            _resident((ng, SUPER, SUPER)),
            _resident((ng, POOL_GROUP, POOL_GROUP)),
            _resident((1, POOL_WIDTH)),
        ],
        out_specs=pl.BlockSpec((seq, POOL_WIDTH), lambda b: (b, 0)),
        out_shape=jax.ShapeDtypeStruct((batch * seq, POOL_WIDTH), BF16),
        scratch_shapes=[pltpu.VMEM((seq + 2 * POOL_PAD, POOL_GROUP), F32)],
        compiler_params=_cparams(("arbitrary",)),
        name="pool_mixer",
    )(u_lat, col_mats, w_pool, pool_scale)


def _first_index(hit, iota, size, axis):
    return jnp.min(jnp.where(hit, iota, size), axis=axis, keepdims=True)


def _mix_body(yg_ref, yp_ref, x_ref, gt1_ref, sc2_ref, sh2_ref, gpost_ref, gpre_ref, wout_ref,
              wr_ref, rb_ref, upper_ref,
              x1_ref, hp_ref, eidx_ref, pos_ref, wts_ref, cnt_ref, run_ref, wrow_ref, y_scr):
    tm = x_ref.shape[0]
    neg_inf = jnp.float32(-jnp.inf)
    step = pl.program_id(0)

    @pl.when(step == 0)
    def _():
        run_ref[...] = jnp.zeros_like(run_ref)
        wrow_ref[...] = jnp.zeros_like(wrow_ref)
        y_scr[...] = jnp.zeros_like(y_scr)

    y = y_scr[...]
    y_new = jnp.dot(yg_ref[...], wout_ref[0:GLA_WIDTH, :], preferred_element_type=F32)
    y_scr[...] = y_new + jnp.dot(yp_ref[...], wout_ref[GLA_WIDTH:, :], preferred_element_type=F32)
    x1 = x_ref[...] + gt1_ref[0] * (y * _rms_scale(y) * gpost_ref[...])
    x1_ref[...] = x1
    h = x1 * _rms_scale(x1) * gpre_ref[...]
    h = h * (1.0 + sc2_ref[0]) + sh2_ref[0]
    _store_row_tiles(hp_ref, _pack_halves(h[:, :HALF], h[:, HALF:]))

    h_hi, h_lo = _bf16_terms(h)
    both = jnp.dot(h_hi, wr_ref[...], preferred_element_type=F32)
    lt = both[:, :128] + both[:, 128:] + jnp.dot(h_lo, wr_ref[:, :128], preferred_element_type=F32)
    logits = lt.T[0:N_EXPERTS, :]
    scores = jax.nn.sigmoid(logits)
    sel = scores + rb_ref[...]
    shape3 = (N_GROUPS, GROUP_SIZE, tm)
    sel3 = sel.reshape(shape3)
    i_in = lax.broadcasted_iota(I32, shape3, 1).astype(F32)
    m1 = jnp.max(sel3, axis=1, keepdims=True)
    f1 = _first_index(sel3 == m1, i_in, float(GROUP_SIZE), 1)
    m2 = jnp.max(jnp.where(i_in == f1, neg_inf, sel3), axis=1, keepdims=True)
    grp = jnp.broadcast_to(m1 + m2, shape3).reshape(N_EXPERTS, tm)
    i_e = lax.broadcasted_iota(I32, (N_EXPERTS, tm), 0)
    i_grp = (i_e >> 3).astype(F32)
    i_e = i_e.astype(F32)
    allowed = jnp.zeros((N_EXPERTS, tm), F32)
    for _ in range(TOPK_GROUPS):
        m = jnp.max(grp, axis=0, keepdims=True)
        pick = i_grp == _first_index(grp == m, i_grp, float(N_GROUPS), 0)
        allowed = jnp.where(pick, 1.0, allowed)
        grp = jnp.where(pick, neg_inf, grp)
    cand = jnp.where(allowed > 0.0, sel, neg_inf)
    onehot = jnp.zeros((N_EXPERTS, tm), F32)
    picks, wts = [], []
    for k in range(TOP_K):
        m = jnp.max(cand, axis=0, keepdims=True)
        f = _first_index(cand == m, i_e, float(N_EXPERTS), 0)
        pick = i_e == f
        picks.append(pick)
        eidx_ref[k:k + 1, :] = f.astype(I32)
        wts.append(jnp.sum(jnp.where(pick, scores, 0.0), axis=0, keepdims=True))
        onehot = jnp.where(pick, 1.0, onehot)
        cand = jnp.where(pick, neg_inf, cand)
    w_sum = wts[0]
    for k in range(1, TOP_K):
        w_sum = w_sum + wts[k]
    for k in range(TOP_K):
        wrow_ref[k:k + 1, :] = wts[k] / w_sum * ROUTED_SCALE
    wts_ref[...] = wrow_ref[...].T

    before = jnp.dot(onehot.astype(BF16), upper_ref[...], preferred_element_type=F32)
    before = before + run_ref[:, 0:1]
    for k in range(TOP_K):
        pos_ref[k:k + 1, :] = jnp.sum(jnp.where(picks[k], before, 0.0), axis=0, keepdims=True).astype(I32)
    counted = jnp.where(step > 0, 1.0, 0.0)
    run_ref[...] = run_ref[...] + counted * jnp.sum(onehot, axis=1, keepdims=True)
    cnt_ref[...] = run_ref[...].astype(I32)


def _mix_and_route(y_gla, y_pool, x2d, gt1, sc2, sh2, g_post, g_pre, w_out, w_router2, router_bias, seq,
                   part, n_parts):
    d = x2d.shape[1]
    rows = x2d.shape[0] // n_parts
    tiles = rows // MIX_TM
    first = part * tiles
    tiles_per_b = seq // MIX_TM
    proj = lambda i: (first + jnp.minimum(i, tiles - 1), 0)
    bmap = lambda i: ((first + jnp.maximum(i - 1, 0)) // tiles_per_b, 0, 0)
    xmap = lambda i: (first + jnp.maximum(i - 1, 0), 0)
    rmap = lambda i: (jnp.maximum(i - 1, 0), 0)
    tmap = lambda i: (0, jnp.maximum(i - 1, 0))
    upper = jnp.asarray(np.triu(np.ones((MIX_TM, MIX_TM), np.float32), 1), dtype=BF16)
    return pl.pallas_call(
        _mix_body,
        grid=(tiles + 1,),
        in_specs=[
            pl.BlockSpec((MIX_TM, GLA_WIDTH), proj),
            pl.BlockSpec((MIX_TM, POOL_WIDTH), proj),
            pl.BlockSpec((MIX_TM, d), xmap),
            pl.BlockSpec((1, 1, d), bmap),
            pl.BlockSpec((1, 1, d), bmap),
            pl.BlockSpec((1, 1, d), bmap),
            _resident((1, d)),
            _resident((1, d)),
            _resident((d, d)),
            _resident((d, 256)),
            _resident((N_EXPERTS, 1)),
            _resident((MIX_TM, MIX_TM)),
        ],
        out_specs=[
            pl.BlockSpec((MIX_TM, d), rmap),
            pl.BlockSpec((MIX_TM * ROW_SUBLANES, 128), rmap),
            pl.BlockSpec((TOP_K, MIX_TM), tmap),
            pl.BlockSpec((TOP_K, MIX_TM), tmap),
            pl.BlockSpec((MIX_TM, 128), rmap),
            pl.BlockSpec((N_EXPERTS, 128), lambda i: (0, 0)),
        ],
        out_shape=[
            jax.ShapeDtypeStruct((rows, d), F32),
            jax.ShapeDtypeStruct((rows * ROW_SUBLANES, 128), U32),
            jax.ShapeDtypeStruct((TOP_K, rows), I32),
            jax.ShapeDtypeStruct((TOP_K, rows), I32),
            jax.ShapeDtypeStruct((rows, 128), F32),
            jax.ShapeDtypeStruct((N_EXPERTS, 128), I32),
        ],
        scratch_shapes=[pltpu.VMEM((N_EXPERTS, 128), F32), pltpu.VMEM((128, MIX_TM), F32),
                        pltpu.VMEM((MIX_TM, d), F32)],
        compiler_params=_cparams(("arbitrary",)),
        name="mix_and_route",
    )(y_gla, y_pool, x2d, gt1, sc2, sh2, g_post, g_pre, w_out, w_router2, router_bias, upper)


def _swiglu(lo, hi, wg_ref, wu_ref, wd_ref):
    g = jnp.dot(lo, wg_ref[:HALF, :], preferred_element_type=F32)
    g = g + jnp.dot(hi, wg_ref[HALF:, :], preferred_element_type=F32)
    u = jnp.dot(lo, wu_ref[:HALF, :], preferred_element_type=F32)
    u = u + jnp.dot(hi, wu_ref[HALF:, :], preferred_element_type=F32)
    act = (_silu(g) * u).astype(BF16)
    return jnp.dot(act, wd_ref[...], preferred_element_type=F32)


def _offsets_body(ps_ref, e_ref, p_ref, d_ref):
    e = e_ref[...]
    d = p_ref[...]
    for x in range(N_EXPERTS):
        d = d + jnp.where(e == x, ps_ref[x], 0)
    d_ref[...] = d


def _route_offsets(eidx_t, pos_t, pstarts):
    k, rows = eidx_t.shape
    spec = pl.BlockSpec((k, OFFS_TN), lambda i, ps: (0, i))
    grid_spec = pltpu.PrefetchScalarGridSpec(
        num_scalar_prefetch=1, grid=(rows // OFFS_TN,), in_specs=[spec, spec], out_specs=spec)
    return pl.pallas_call(
        _offsets_body,
        grid_spec=grid_spec,
        out_shape=jax.ShapeDtypeStruct((k, rows), I32),
        compiler_params=_cparams(("arbitrary",)),
        name="route_offsets",
    )(pstarts, eidx_t, pos_t)


def _expert_body(be_ref, nu_ref, valid_ref, nxt_ref, slot_ref, x_ref, wg_hbm, wu_hbm, wd_hbm, y_ref,
                 wg_f, wu_f, wd_f, wg_b, wu_b, wd_b, sems):
    i = pl.program_id(0)
    valid = valid_ref[i]
    expert = be_ref[i]
    slot = slot_ref[i]

    def weight_copies(e, s):
        return [pltpu.make_async_copy(src.at[e], dst.at[s], sems.at[s, j])
                for j, (src, dst) in enumerate(((wg_hbm, wg_f), (wu_hbm, wu_f), (wd_hbm, wd_f)))]

    @pl.when(i == 0)
    def _():
        for cp in weight_copies(expert, slot):
            cp.start()

    @pl.when((i == 0) | (expert != be_ref[jnp.maximum(i - 1, 0)]))
    def _():
        for cp in weight_copies(expert, slot):
            cp.wait()

        @pl.when(nxt_ref[i] != expert)
        def _():
            for cp in weight_copies(nxt_ref[i], 1 - slot):
                cp.start()

        wg_b[...] = wg_f[slot].astype(BF16)
        wu_b[...] = wu_f[slot].astype(BF16)
        wd_b[...] = wd_f[slot].astype(BF16)

    def compute(n_rows):
        tiles = pl.ds(0, n_rows * ROW_SUBLANES)
        lo, hi = _unpack_halves(_load_row_tiles(x_ref.at[tiles]))
        y = _swiglu(lo.astype(BF16), hi.astype(BF16), wg_b, wu_b, wd_b)
        _store_row_tiles(y_ref.at[tiles], _pack_halves(y[:, :HALF], y[:, HALF:]))

    quarter = MOE_TILE // 4
    for n in range(1, 5):
        @pl.when((valid > (n - 1) * quarter) & (valid <= n * quarter))
        def _(n=n):
            compute(n * quarter)


def _experts(xs, block_expert, n_used, valid, next_expert, slot, w_eg, w_eu, w_ed):
    n_rows = xs.shape[0] // ROW_SUBLANES
    n_blocks = n_rows // MOE_TILE
    d, de = w_eg.shape[1], w_eg.shape[2]
    row_map = lambda i, be, nu, *_: (jnp.minimum(i, nu[0] - 1), 0)
    grid_spec = pltpu.PrefetchScalarGridSpec(
        num_scalar_prefetch=5,
        grid=(n_blocks,),
        in_specs=[
            pl.BlockSpec((MOE_TILE * ROW_SUBLANES, 128), row_map),
            pl.BlockSpec(memory_space=pl.ANY),
            pl.BlockSpec(memory_space=pl.ANY),
            pl.BlockSpec(memory_space=pl.ANY),
        ],
        out_specs=pl.BlockSpec((MOE_TILE * ROW_SUBLANES, 128), row_map),
        scratch_shapes=[
            pltpu.VMEM((2, d, de), F32),
            pltpu.VMEM((2, d, de), F32),
            pltpu.VMEM((2, de, d), F32),
            pltpu.VMEM((d, de), BF16),
            pltpu.VMEM((d, de), BF16),
            pltpu.VMEM((de, d), BF16),
            pltpu.SemaphoreType.DMA((2, 3)),
        ],
    )
    return pl.pallas_call(
        _expert_body,
        grid_spec=grid_spec,
        out_shape=jax.ShapeDtypeStruct((n_rows * ROW_SUBLANES, 128), U32),
        compiler_params=_cparams(("arbitrary",)),
        name="experts",
    )(block_expert, n_used, valid, next_expert, slot, xs, w_eg, w_eu, w_ed)


def _sc_mesh():
    return plsc.VectorSubcoreMesh(core_axis_name="c", subcore_axis_name="s")


def _sc_worker():
    return lax.axis_index("s") * SC_CORES + lax.axis_index("c")


def _sc_gather_rows(table, idx):
    n_idx = idx.shape[0]
    per_worker = n_idx // SC_WORKERS
    n_chunks = per_worker // SC_CHUNK
    assert per_worker * SC_WORKERS == n_idx and n_chunks * SC_CHUNK == per_worker and n_chunks % 2 == 0
    row_shape = table.shape[1:]

    @functools.partial(
        pl.kernel, mesh=_sc_mesh(),
        out_type=jax.ShapeDtypeStruct((n_idx,) + row_shape, table.dtype),
        scratch_types=[
            pltpu.VMEM((per_worker,), I32),
            pltpu.VMEM((SC_CHUNK,) + row_shape, table.dtype),
            pltpu.VMEM((SC_CHUNK,) + row_shape, table.dtype),
        ] + [pltpu.SemaphoreType.DMA] * 4,
    )
    def gather(table_hbm, idx_hbm, out_hbm, idx_v, buf0, buf1, g0, g1, w0, w1):
        bufs, gsem, wsem = (buf0, buf1), (g0, g1), (w0, w1)
        base = _sc_worker() * per_worker
        pltpu.sync_copy(idx_hbm.at[pl.ds(base, per_worker)], idx_v)

        def fetch(j, b):
            return pltpu.make_async_copy(table_hbm.at[idx_v.at[pl.ds(j * SC_CHUNK, SC_CHUNK)]], bufs[b], gsem[b])

        def flush(j, b):
            return pltpu.make_async_copy(bufs[b], out_hbm.at[pl.ds(base + j * SC_CHUNK, SC_CHUNK)], wsem[b])

        fetch(0, 0).start()
        fetch(0, 0).wait()
        fetch(1, 1).start()
        flush(0, 0).start()

        @pl.loop(1, n_chunks - 1, step=2)
        def _(j):
            for off in range(2):
                jj, b = j + off, (1 + off) % 2
                fetch(jj, b).wait()
                flush(jj - 1, 1 - b).wait()
                fetch(jj + 1, 1 - b).start()
                flush(jj, b).start()

        fetch(n_chunks - 1, 1).wait()
        flush(n_chunks - 1, 1).start()
        flush(n_chunks - 2, 0).wait()
        flush(n_chunks - 1, 1).wait()

    return gather(table, idx)


def _sc_scatter_rows(rows, dest_t, n_out):
    n_rows = rows.shape[0]
    per_worker = n_rows // SC_WORKERS
    n_chunks = per_worker // SC_CHUNK
    assert per_worker * SC_WORKERS == n_rows and n_chunks * SC_CHUNK == per_worker
    row_shape = rows.shape[1:]
    idx_w = dest_t.reshape(TOP_K, SC_WORKERS, n_chunks, SC_CHUNK).transpose(1, 2, 0, 3)
    idx_w = idx_w.reshape(SC_WORKERS, n_chunks * TOP_K, SC_CHUNK)

    @functools.partial(
        pl.kernel, mesh=_sc_mesh(),
        out_type=jax.ShapeDtypeStruct((n_out,) + row_shape, rows.dtype),
        scratch_types=[
            pltpu.VMEM((n_chunks * TOP_K, SC_CHUNK), I32),
            pltpu.VMEM((SC_CHUNK,) + row_shape, rows.dtype),
            pltpu.VMEM((SC_CHUNK,) + row_shape, rows.dtype),
        ] + [pltpu.SemaphoreType.DMA] * 4,
    )
    def scatter(rows_hbm, idx_hbm, out_hbm, idx_v, buf0, buf1, r0, r1, s0, s1):
        bufs, rsem, ssem = (buf0, buf1), (r0, r1), (s0, s1)
        worker = _sc_worker()
        base = worker * per_worker
        pltpu.sync_copy(idx_hbm.at[worker], idx_v)

        def fetch(j, b):
            return pltpu.make_async_copy(rows_hbm.at[pl.ds(base + j * SC_CHUNK, SC_CHUNK)], bufs[b], rsem[b])

        def send(j, k, b):
            return pltpu.make_async_copy(bufs[b], out_hbm.at[idx_v.at[j * TOP_K + k]], ssem[b])

        fetch(0, 0).start()
        for j in range(n_chunks):
            b = j % 2
            fetch(j, b).wait()
            if j + 1 < n_chunks:
                if j >= 1:
                    for k in range(TOP_K):
                        send(j - 1, k, 1 - b).wait()
                fetch(j + 1, 1 - b).start()
            for k in range(TOP_K):
                send(j, k, b).start()
        for j in range(max(n_chunks - 2, 0), n_chunks):
            for k in range(TOP_K):
                send(j, k, j % 2).wait()

    return scatter(rows, idx_w)


def _combine_body(*refs):
    yk_refs = refs[:TOP_K]
    w_ref, hp_ref, wg_ref, wu_ref, wd_ref, x1_ref, gt2_ref, gpost_ref = refs[TOP_K:TOP_K + 8]
    o_ref = refs[-1]
    tt = x1_ref.shape[0]
    w = w_ref[...]
    ssq = jnp.zeros((tt, 1), F32)
    lo, hi = _unpack_halves(_load_row_tiles(hp_ref))
    o_ref[...] = _swiglu(lo.astype(BF16), hi.astype(BF16), wg_ref, wu_ref, wd_ref)
    for c in range(HALF // 128):
        c_lo = slice(c * 128, (c + 1) * 128)
        c_hi = slice(HALF + c * 128, HALF + (c + 1) * 128)
        y_lo = o_ref[:, c_lo]
        y_hi = o_ref[:, c_hi]
        for k in range(TOP_K):
            lo, hi = _unpack_halves(yk_refs[k][pl.ds(c, tt, stride=ROW_SUBLANES), :])
            y_lo = y_lo + w[:, k:k + 1] * lo
            y_hi = y_hi + w[:, k:k + 1] * hi
        ssq = ssq + jnp.sum(y_lo * y_lo, axis=-1, keepdims=True) + jnp.sum(y_hi * y_hi, axis=-1, keepdims=True)
        o_ref[:, c_lo] = y_lo
        o_ref[:, c_hi] = y_hi
    scale = lax.rsqrt(ssq / D_MODEL + EPS)
    o_ref[...] = x1_ref[...] + gt2_ref[0] * (o_ref[...] * scale * gpost_ref[...])


def _combine(yu, wts, h_packed, shared_w, x1, gt2, g_post, seq, part, n_parts, prev_out):
    rows, d = x1.shape
    tiles = rows // COMB_TT
    first = part * tiles
    tiles_per_b = seq // COMB_TT
    loc = lambda i: (i, 0)
    yk_specs = [pl.BlockSpec((COMB_TT * ROW_SUBLANES, 128), functools.partial(lambda i, k: (k * tiles + i, 0), k=k))
                for k in range(TOP_K)]
    in_specs = yk_specs + [
        pl.BlockSpec((COMB_TT, 128), loc),
        pl.BlockSpec((COMB_TT * ROW_SUBLANES, 128), loc),
        *[_resident(wt.shape) for wt in shared_w],
        pl.BlockSpec((COMB_TT, d), loc),
        pl.BlockSpec((1, 1, d), lambda i: ((first + i) // tiles_per_b, 0, 0)),
        _resident((1, d)),
    ]
    args = [yu] * TOP_K + [wts, h_packed, *shared_w, x1, gt2, g_post]
    aliases = {}
    if prev_out is not None:
        aliases = {len(args): 0}
        in_specs.append(pl.BlockSpec(memory_space=pl.ANY))
        args.append(prev_out)
    return pl.pallas_call(
        _combine_body,
        grid=(tiles,),
        in_specs=in_specs,
        out_specs=pl.BlockSpec((COMB_TT, d), lambda i: (first + i, 0)),
        out_shape=jax.ShapeDtypeStruct((rows * n_parts, d), F32),
        input_output_aliases=aliases,
        compiler_params=_cparams(("arbitrary",)),
        name="combine",
    )(*args)


def kernel(x, c, ctx, c_ctx, w_mod, b_mod, norm_mix_pre, norm_mix_post, norm_ffn_pre, norm_ffn_post, w_in, w_a2_fwd, b_a_fwd, w_a2_bwd, b_a_bwd, gla_norm, w_pool, pool_scale, w_out, w_router, router_bias, w_exp_gate, w_exp_up, w_exp_down, w_sh_gate, w_sh_up, w_sh_down):
    batch, seq, d = x.shape
    n_ctx = ctx.shape[1]
    assert w_mod.shape[0] == 1 and d == D_MODEL
    assert seq % (2 * SUPER) == 0 and n_ctx % SUPER == 0 and seq % PROJ_TM == 0 and (batch * n_ctx) % PROJ_TM == 0
    rows = batch * seq

    mod_rows = 16
    c_all = jnp.concatenate([c, c_ctx[None, :], jnp.zeros((mod_rows - batch - 1, d), F32)], axis=0)
    mod_all = _modulation(c_all, w_mod[0], b_mod[0][None, :])
    sh1, sc1, gt1, sh2, sc2, gt2 = [m.reshape(batch, 1, d) for m in jnp.split(mod_all[:batch], 6, axis=-1)]
    csh1 = mod_all[batch, 0:d].reshape(1, 1, d)
    csc1 = mod_all[batch, d:2 * d].reshape(1, 1, d)

    kw, gw = GLA_KEY_WIDTH, GLA_WIDTH
    a0 = 2 * kw + 2 * gw
    w_in0 = w_in[0]
    w_bf = w_in0.astype(BF16)
    w_a = jnp.pad(w_bf[:, a0:a0 + 2 * GLA_RANK], ((0, 0), (0, 128 - 2 * GLA_RANK)))
    lat_pieces = [(w_bf, a0, 0), (w_bf[:, a0 + 2 * GLA_RANK:], POOL_WIDTH, 0)]
    ctx_pieces = [(w_bf, gw, 2 * kw // gw), (w_bf, kw, 1)]
    w2f = jnp.pad(w_a2_fwd[0], ((0, 128 - GLA_RANK), (0, 0))).astype(BF16)
    w2b = jnp.pad(w_a2_bwd[0], ((GLA_RANK, 128 - 2 * GLA_RANK), (0, 0))).astype(BF16)
    g_mix_pre = norm_mix_pre[0][None, :]
    w_router2 = jnp.concatenate(_bf16_terms(jnp.pad(w_router[0], ((0, 0), (0, 128 - N_EXPERTS)))), axis=1)

    u_ctx, a_ctx = _in_projection(ctx.reshape(batch * n_ctx, d), g_mix_pre, csc1, csh1, ctx_pieces, w_a,
                                  batch * n_ctx)
    u_lat, a_lat = _in_projection(x.reshape(rows, d), g_mix_pre, sc1, sh1, lat_pieces, w_a, seq)

    y_gla = _gla(u_lat, a_lat, u_ctx, a_ctx, w2f, b_a_fwd[0][None, :], w2b, b_a_bwd[0][None, :],
                 gla_norm[0][None, :], batch, seq, n_ctx)
    y_pool = _pool_mixer(u_lat, _col_window_matrices(), w_pool[0].astype(BF16), pool_scale[0][None, :],
                         batch, seq)

    w_out_bf = w_out[0].astype(BF16)
    shared_w = (w_sh_gate[0].astype(BF16), w_sh_up[0].astype(BF16), w_sh_down[0].astype(BF16))
    x2d = x.reshape(rows, d)
    part_rows = rows // MOE_PARTS
    n_blocks = part_rows * TOP_K // MOE_TILE + N_EXPERTS
    e_ids = jnp.arange(N_EXPERTS, dtype=I32)

    mixed = [_mix_and_route(y_gla, y_pool, x2d, gt1, sc2, sh2, norm_mix_post[0][None, :],
                            norm_ffn_pre[0][None, :], w_out_bf, w_router2, router_bias[0][:, None], seq,
                            p, MOE_PARTS) for p in range(MOE_PARTS)]
    out = None
    for p, (x1, h_packed, eidx_t, pos_t, wts, counts) in enumerate(mixed):
        counts = counts[:, 0]
        padded = (counts + MOE_TILE - 1) // MOE_TILE * MOE_TILE
        pends = jnp.cumsum(padded)
        pstarts = pends - padded
        dest_t = _route_offsets(eidx_t, pos_t, pstarts.astype(I32))
        n_used = (pends[-1] // MOE_TILE).astype(I32)
        blk = jnp.minimum(jnp.arange(n_blocks, dtype=I32), n_used - 1)
        block_expert = jnp.sum((blk * MOE_TILE)[:, None] >= pends[None, :], axis=1).astype(I32)
        block_expert = jnp.minimum(block_expert, N_EXPERTS - 1)
        is_block_expert = block_expert[:, None] == e_ids[None, :]
        per_block = lambda v: jnp.sum(jnp.where(is_block_expert, v[None, :], 0), axis=1).astype(I32)
        valid = jnp.clip(per_block(pstarts + counts) - blk * MOE_TILE, 0, MOE_TILE)
        valid = jnp.where(jnp.arange(n_blocks, dtype=I32) < n_used, valid, 0).astype(I32)
        has_rows = padded > 0
        later = jnp.where((e_ids[None, :] > e_ids[:, None]) & has_rows[None, :], e_ids[None, :], N_EXPERTS)
        next_e = jnp.min(later, axis=1)
        next_e = jnp.where(next_e == N_EXPERTS, e_ids, next_e)
        slot_e = (jnp.cumsum(has_rows.astype(I32)) - has_rows.astype(I32)) % 2

        xs = _sc_scatter_rows(h_packed.reshape(-1, ROW_SUBLANES, 128), dest_t, n_blocks * MOE_TILE)
        ys = _experts(xs.reshape(-1, 128), block_expert, n_used.reshape(1), valid, per_block(next_e),
                      per_block(slot_e), w_exp_gate[0], w_exp_up[0], w_exp_down[0])
        yu = _sc_gather_rows(ys.reshape(-1, ROW_SUBLANES, 128), dest_t.reshape(-1))
        out = _combine(yu.reshape(-1, 128), wts, h_packed, shared_w, x1, gt2, norm_ffn_post[0][None, :],
                       seq, p, MOE_PARTS, out)
    return out.reshape(batch, seq, d)
```

```python
import functools

import numpy as np
import jax
import jax.numpy as jnp
from jax import lax
from jax.experimental import pallas as pl
from jax.experimental.pallas import tpu as pltpu
from jax.experimental.pallas import tpu_sc as plsc

F32 = jnp.float32
BF16 = jnp.bfloat16
I32 = jnp.int32
U32 = jnp.uint32

D_MODEL = 2048
GRID_W = 64
GLA_HEADS = 4
GLA_DK = 128
GLA_DV = 256
GLA_KEY_WIDTH = GLA_HEADS * GLA_DK
GLA_WIDTH = GLA_HEADS * GLA_DV
GLA_RANK = 16
GLA_TAU = 16.0
GLA_CHUNK = 64
POOL_WIDTH = 1024
POOL_WINDOWS = (2, 4, 8, 16)
POOL_GROUP = 256
N_EXPERTS = 64
TOP_K = 8
N_GROUPS = 8
GROUP_SIZE = N_EXPERTS // N_GROUPS
TOPK_GROUPS = 4
ROUTED_SCALE = 2.5
EPS = 1e-6

HALF = D_MODEL // 2
SUPER = 4 * GLA_CHUNK
GLA_HPS = 4
POOL_PAD = 8 * GRID_W
VMEM_LIMIT = 56 * 1024 * 1024

MOD_TN = 1024
PROJ_TM = 512
PROJ_TN = 512
MIX_TM = 512
MOE_TILE = 512
SHARED_TM = 1024
COMB_TT = 256
MOE_PARTS = 2
LANES = 128
ROW_SUBLANES = 8
SC_CORES = 2
SC_WORKERS = 32
SC_CHUNK = 32
OFFS_TN = 2048


def _cparams(sem):
    return pltpu.CompilerParams(dimension_semantics=sem, vmem_limit_bytes=VMEM_LIMIT)


def _resident(shape):
    nd = len(shape)
    return pl.BlockSpec(shape, lambda *_: (0,) * nd, pipeline_mode=pl.Buffered(1))


def _silu(v):
    return v * jax.nn.sigmoid(v)


def _pack_halves(lo, hi):
    lo_b = lax.bitcast_convert_type(lo.astype(BF16).astype(F32), U32)
    hi_b = lax.bitcast_convert_type(hi.astype(BF16).astype(F32), U32)
    return (hi_b & jnp.uint32(0xFFFF0000)) | (lo_b >> 16)


def _unpack_halves(p):
    lo = lax.bitcast_convert_type(p << 16, F32)
    hi = lax.bitcast_convert_type(p & jnp.uint32(0xFFFF0000), F32)
    return lo, hi


def _bf16_terms(x):
    hi = lax.bitcast_convert_type(lax.bitcast_convert_type(x, U32) & jnp.uint32(0xFFFF0000), F32)
    return hi.astype(BF16), (x - hi).astype(BF16)


def _store_row_tiles(ref, packed):
    n = packed.shape[0]
    for c in range(HALF // LANES):
        ref[pl.ds(c, n, stride=ROW_SUBLANES), :] = packed[:, c * LANES:(c + 1) * LANES]


def _load_row_tiles(ref):
    n = ref.shape[0] // ROW_SUBLANES
    return jnp.concatenate([ref[pl.ds(c, n, stride=ROW_SUBLANES), :] for c in range(HALF // LANES)], axis=1)


def _mod_body(c_ref, w_ref, b_ref, o_ref):
    s_hi, s_lo = _bf16_terms(_silu(c_ref[...]))
    w_hi, w_lo = _bf16_terms(w_ref[...])
    acc = jnp.dot(s_hi, w_hi, preferred_element_type=F32)
    acc = acc + jnp.dot(s_lo, w_hi, preferred_element_type=F32)
    acc = acc + jnp.dot(s_hi, w_lo, preferred_element_type=F32)
    o_ref[...] = acc + b_ref[...]


def _modulation(c_all, w_mod, b_mod):
    rows, d = c_all.shape
    n = w_mod.shape[1]
    return pl.pallas_call(
        _mod_body,
        grid=(n // MOD_TN,),
        in_specs=[
            pl.BlockSpec((rows, d), lambda j: (0, 0)),
            pl.BlockSpec((d, MOD_TN), lambda j: (0, j)),
            pl.BlockSpec((1, MOD_TN), lambda j: (0, j)),
        ],
        out_specs=pl.BlockSpec((rows, MOD_TN), lambda j: (0, j)),
        out_shape=jax.ShapeDtypeStruct((rows, n), F32),
        compiler_params=_cparams(("arbitrary",)),
        name="modulation",
    )(c_all, w_mod, b_mod)


def _rms_scale(x):
    return lax.rsqrt(jnp.mean(x * x, axis=-1, keepdims=True) + EPS)


def _inproj_body(x_ref, g_ref, sc_ref, sh_ref, *refs):
    w_refs, (wa_ref, o_ref, a_ref) = refs[:-3], refs[-3:]
    x = x_ref[...]
    h = x * _rms_scale(x) * g_ref[...]
    h = h * (1.0 + sc_ref[0]) + sh_ref[0]
    hb = h.astype(BF16)
    col = 0
    for w_ref in w_refs:
        for n in range(w_ref.shape[1] // PROJ_TN):
            cols = slice(n * PROJ_TN, (n + 1) * PROJ_TN)
            o_ref[:, col:col + PROJ_TN] = jnp.dot(hb, w_ref[:, cols], preferred_element_type=F32).astype(BF16)
            col += PROJ_TN
    a_ref[...] = jnp.dot(hb, wa_ref[...], preferred_element_type=F32)


def _in_projection(x2d, gain, sc, sh, pieces, w_a, rows_per_mod):
    rows, d = x2d.shape
    n_main = sum(width for _, width, _ in pieces)
    tiles_per_mod = rows_per_mod // PROJ_TM
    mod_map = lambda i: (i // tiles_per_mod, 0, 0)
    piece_specs = [pl.BlockSpec((d, width), functools.partial(lambda i, b: (0, b), b=block),
                                pipeline_mode=pl.Buffered(1)) for _, width, block in pieces]
    return pl.pallas_call(
        _inproj_body,
        grid=(rows // PROJ_TM,),
        in_specs=[
            pl.BlockSpec((PROJ_TM, d), lambda i: (i, 0)),
            _resident((1, d)),
            pl.BlockSpec((1, 1, d), mod_map),
            pl.BlockSpec((1, 1, d), mod_map),
            *piece_specs,
            _resident((d, LANES)),
        ],
        out_specs=[
            pl.BlockSpec((PROJ_TM, n_main), lambda i: (i, 0)),
            pl.BlockSpec((PROJ_TM, LANES), lambda i: (i, 0)),
        ],
        out_shape=[
            jax.ShapeDtypeStruct((rows, n_main), BF16),
            jax.ShapeDtypeStruct((rows, LANES), F32),
        ],
        compiler_params=_cparams(("arbitrary",)),
        name="in_projection",
    )(x2d, gain, sc, sh, *[w for w, _, _ in pieces], w_a)


def _log_sigmoid(z):
    return jnp.minimum(z, 0.0) - jnp.log1p(jnp.exp(-jnp.abs(z)))


def _gla_cumulative_decay(a, w2, ba, tri):
    z = jnp.dot(a.astype(BF16), w2, preferred_element_type=F32) + ba
    g = _log_sigmoid(z) * (1.0 / GLA_TAU)
    g_hi, g_lo = _bf16_terms(g)
    return jnp.dot(tri, g_hi, preferred_element_type=F32) + jnp.dot(tri, g_lo, preferred_element_type=F32)


def _gla_prep(q, k, G, reverse):
    nc = SUPER // GLA_CHUNK
    G = G.reshape(nc, GLA_CHUNK, GLA_DK)
    end_row = 0 if reverse else GLA_CHUNK - 1
    mid_row = GLA_CHUNK - 1 - GLA_CHUNK // 2 if reverse else GLA_CHUNK // 2
    g_end = G[:, end_row:end_row + 1, :]
    g_mid = G[:, mid_row:mid_row + 1, :]
    k4 = k.astype(F32).reshape(nc, GLA_CHUNK, GLA_DK)
    dec = jnp.broadcast_to(jnp.exp(g_end), (nc, ROW_SUBLANES, GLA_DK)).reshape(nc * ROW_SUBLANES, GLA_DK)
    flat = lambda t: t.reshape(SUPER, GLA_DK).astype(BF16)
    if q is None:
        return None, None, None, flat(k4 * jnp.exp(g_end - G)), dec
    q4 = q.astype(F32).reshape(nc, GLA_CHUNK, GLA_DK) * (GLA_DK ** -0.5)
    qg = q4 * jnp.exp(G - g_mid)
    kg = k4 * jnp.exp(g_mid - G)
    qe = qg * jnp.exp(g_mid)
    kd = kg * jnp.exp(g_end - g_mid)
    return flat(qg), flat(kg), flat(qe), flat(kd), dec


def _gla_apply(qg, kg, qe, kd, dec, v, mask, st_ref, reverse):
    nc = SUPER // GLA_CHUNK
    o = None
    if qg is not None:
        att = lax.dot_general(qg, kg, (((1,), (1,)), ((), ())), preferred_element_type=F32)
        att = jnp.where(mask, att, 0.0).astype(BF16)
        o = jnp.dot(att, v, preferred_element_type=F32)
    outs = [None] * nc
    order = range(nc - 1, -1, -1) if reverse else range(nc)
    for c in order:
        rows = slice(c * GLA_CHUNK, (c + 1) * GLA_CHUNK)
        st = st_ref[...]
        if qg is not None:
            inter = lax.dot_general(qe[rows], st.astype(BF16), (((1,), (1,)), ((), ())),
                                    preferred_element_type=F32)
            outs[c] = o[rows] + inter
        upd = lax.dot_general(v[rows], kd[rows], (((0,), (0,)), ((), ())), preferred_element_type=F32)
        st_ref[...] = st * dec[ROW_SUBLANES * c:ROW_SUBLANES * c + 1, :] + upd
    if qg is None:
        return None
    return jnp.concatenate(outs, axis=0)


def _gla_body(q_ref, k_ref, v_ref, r_ref, a_ref, kc_ref, vc_ref, ac_ref,
              w2f_ref, baf_ref, w2b_ref, bab_ref, gn_ref, y_ref, o_acc, st, ops_a, ops_b, dec_a, dec_b,
              *, n_ctx):
    n_sup = q_ref.shape[0] // SUPER
    row = lax.broadcasted_iota(I32, (SUPER, SUPER), 0)
    col = lax.broadcasted_iota(I32, (SUPER, SUPER), 1)
    same_chunk = (row // GLA_CHUNK) == (col // GLA_CHUNK)
    mask_f = same_chunk & (col <= row)
    mask_b = same_chunk & (col >= row)
    tri_f = jnp.where(mask_f, 1.0, 0.0).astype(BF16)
    tri_b = jnp.where(mask_b, 1.0, 0.0).astype(BF16)
    heads = range(GLA_HPS)
    kcol = [slice(h * GLA_DK, (h + 1) * GLA_DK) for h in heads]
    vcol = [slice(h * GLA_DV, (h + 1) * GLA_DV) for h in heads]
    dirs = ((False, w2f_ref, baf_ref, tri_f, mask_f), (True, w2b_ref, bab_ref, tri_b, mask_b))

    st[...] = jnp.zeros_like(st)
    n_csup = n_ctx // SUPER
    for s in range(n_csup):
        for d, (reverse, w2_ref, ba_ref, tri, mask) in enumerate(dirs):
            sc = n_csup - 1 - s if reverse else s
            rows = slice(sc * SUPER, (sc + 1) * SUPER)
            G = _gla_cumulative_decay(ac_ref[rows, :], w2_ref[...], ba_ref[...], tri)
            for h in heads:
                _, _, _, kd, dec = _gla_prep(None, kc_ref[rows, kcol[h]], G[:, kcol[h]], reverse)
                _gla_apply(None, None, None, kd, dec, vc_ref[rows, vcol[h]], mask, st.at[d, h], reverse)

    o_acc[...] = jnp.zeros_like(o_acc)

    def rows_of(i, reverse):
        sc = n_sup - 1 - i if reverse else i
        return pl.ds(pl.multiple_of(sc * SUPER, SUPER), SUPER)

    def prepare(i, ops, decs, d):
        reverse, w2_ref, ba_ref, tri, mask = dirs[d]
        rows = rows_of(i, reverse)
        G = _gla_cumulative_decay(a_ref[rows, :], w2_ref[...], ba_ref[...], tri)
        for h in heads:
            vals = _gla_prep(q_ref[rows, kcol[h]], k_ref[rows, kcol[h]], G[:, kcol[h]], reverse)
            for j in range(4):
                ops[GLA_HPS * d + h, j] = vals[j]
            decs[GLA_HPS * d + h] = vals[4]

    def apply(i, ops, decs, d):
        reverse, w2_ref, ba_ref, tri, mask = dirs[d]
        rows = rows_of(i, reverse)
        for h in heads:
            ci = GLA_HPS * d + h
            out = _gla_apply(ops[ci, 0], ops[ci, 1], ops[ci, 2], ops[ci, 3], decs[ci],
                             v_ref[rows, vcol[h]], mask, st.at[d, h], reverse)
            o_acc[rows, vcol[h]] += out

    for d in range(2):
        prepare(0, ops_a, dec_a, d)

    def step(j, carry):
        i = 2 * j
        for d in range(2):
            prepare(i + 1, ops_b, dec_b, d)
            apply(i, ops_a, dec_a, d)
        nxt = jnp.minimum(i + 2, n_sup - 1)
        for d in range(2):
            prepare(nxt, ops_a, dec_a, d)
            apply(i + 1, ops_b, dec_b, d)
        return carry

    lax.fori_loop(0, n_sup // 2, step, 0)

    for h in heads:
        o = o_acc[:, vcol[h]]
        o = o * _rms_scale(o) * gn_ref[:, vcol[h]]
        y_ref[:, vcol[h]] = (o * _silu(r_ref[:, vcol[h]].astype(F32))).astype(BF16)


def _gla(u_lat, a_lat, u_ctx, a_ctx, w2f, baf, w2b, bab, gla_norm, batch, seq, n_ctx):
    groups = GLA_HEADS // GLA_HPS
    kw, vw = GLA_HPS * GLA_DK, GLA_HPS * GLA_DV
    kb = GLA_KEY_WIDTH // kw
    vb = 2 * GLA_KEY_WIDTH // vw
    rb = vb + groups
    assert GLA_KEY_WIDTH % kw == 0 and (2 * GLA_KEY_WIDTH) % vw == 0 and GLA_WIDTH % kw == 0
    ckb = GLA_WIDTH // kw
    return pl.pallas_call(
        functools.partial(_gla_body, n_ctx=n_ctx),
        grid=(batch, groups),
        in_specs=[
            pl.BlockSpec((seq, kw), lambda b, h: (b, h)),
            pl.BlockSpec((seq, kw), lambda b, h: (b, kb + h)),
            pl.BlockSpec((seq, vw), lambda b, h: (b, vb + h)),
            pl.BlockSpec((seq, vw), lambda b, h: (b, rb + h)),
            pl.BlockSpec((seq, LANES), lambda b, h: (b, 0)),
            pl.BlockSpec((n_ctx, kw), lambda b, h: (b, ckb + h)),
            pl.BlockSpec((n_ctx, vw), lambda b, h: (b, h)),
            pl.BlockSpec((n_ctx, LANES), lambda b, h: (b, 0)),
            pl.BlockSpec((LANES, kw), lambda b, h: (0, h)),
            pl.BlockSpec((1, kw), lambda b, h: (0, h)),
            pl.BlockSpec((LANES, kw), lambda b, h: (0, h)),
            pl.BlockSpec((1, kw), lambda b, h: (0, h)),
            pl.BlockSpec((1, vw), lambda b, h: (0, h)),
        ],
        out_specs=pl.BlockSpec((seq, vw), lambda b, h: (b, h)),
        out_shape=jax.ShapeDtypeStruct((batch * seq, GLA_WIDTH), BF16),
        scratch_shapes=[
            pltpu.VMEM((seq, vw), F32),
            pltpu.VMEM((2, GLA_HPS, GLA_DV, GLA_DK), F32),
            pltpu.VMEM((2 * GLA_HPS, 4, SUPER, GLA_DK), BF16),
            pltpu.VMEM((2 * GLA_HPS, 4, SUPER, GLA_DK), BF16),
            pltpu.VMEM((2 * GLA_HPS, ROW_SUBLANES * (SUPER // GLA_CHUNK), GLA_DK), F32),
            pltpu.VMEM((2 * GLA_HPS, ROW_SUBLANES * (SUPER // GLA_CHUNK), GLA_DK), F32),
        ],
        compiler_params=_cparams(("arbitrary", "arbitrary")),
        name="gla",
    )(u_lat, u_lat, u_lat, u_lat, a_lat, u_ctx, u_ctx, a_ctx, w2f, baf, w2b, bab, gla_norm)


def _col_window_matrices():
    t = np.arange(SUPER)
    r, c = t // GRID_W, t % GRID_W
    mats = []
    for w in POOL_WINDOWS:
        lo = np.maximum(c - w // 2, 0)[:, None]
        hi = np.minimum(c + w // 2, GRID_W)[:, None]
        m = (r[:, None] == r[None, :]) & (c[None, :] >= lo) & (c[None, :] < hi)
        mats.append(m.astype(np.float32))
    return jnp.asarray(np.stack(mats), dtype=BF16)


def _pool_body(p_ref, cw_ref, wp_ref, ps_ref, y_ref, pad_ref):
    seq = p_ref.shape[0]
    n_rows = seq // GRID_W
    zeros = jnp.zeros((POOL_PAD, POOL_GROUP), F32)
    pad_ref[0:POOL_PAD, :] = zeros
    pad_ref[POOL_PAD + seq:POOL_PAD + seq + POOL_PAD, :] = zeros
    t = lax.broadcasted_iota(I32, (seq, POOL_GROUP), 0)
    r = t // GRID_W
    c = t % GRID_W
    for gi, w in enumerate(POOL_WINDOWS):
        cols = slice(gi * POOL_GROUP, (gi + 1) * POOL_GROUP)
        cw = cw_ref[gi]
        for j in range(seq // SUPER):
            rows = slice(j * SUPER, (j + 1) * SUPER)
            pad_ref[POOL_PAD + j * SUPER:POOL_PAD + (j + 1) * SUPER, :] = jnp.dot(
                cw, p_ref[rows, cols], preferred_element_type=F32)
        total = None
        for d in range(-(w // 2), w // 2):
            start = POOL_PAD + d * GRID_W
            part = pad_ref[start:start + seq, :]
            total = part if total is None else total + part
        cnt_r = jnp.minimum(r + w // 2, n_rows) - jnp.maximum(r - w // 2, 0)
        cnt_c = jnp.minimum(c + w // 2, GRID_W) - jnp.maximum(c - w // 2, 0)
        mean = total / (cnt_r * cnt_c).astype(F32)
        diff = (mean - p_ref[:, cols].astype(F32)).astype(BF16)
        y = jnp.dot(diff, wp_ref[gi], preferred_element_type=F32) * ps_ref[:, cols]
        y_ref[:, cols] = y.astype(BF16)


def _pool_mixer(u_lat, col_mats, w_pool, pool_scale, batch, seq):
    pb = (u_lat.shape[1] - POOL_WIDTH) // POOL_WIDTH
    ng = len(POOL_WINDOWS)
    return pl.pallas_call(
        _pool_body,
        grid=(batch,),
        in_specs=[
            pl.BlockSpec((seq, POOL_WIDTH), lambda b: (b, pb)),
            _resident((ng, SUPER, SUPER)),
            _resident((ng, POOL_GROUP, POOL_GROUP)),
            _resident((1, POOL_WIDTH)),
        ],
        out_specs=pl.BlockSpec((seq, POOL_WIDTH), lambda b: (b, 0)),
        out_shape=jax.ShapeDtypeStruct((batch * seq, POOL_WIDTH), BF16),
        scratch_shapes=[pltpu.VMEM((seq + 2 * POOL_PAD, POOL_GROUP), F32)],
        compiler_params=_cparams(("arbitrary",)),
        name="pool_mixer",
    )(u_lat, col_mats, w_pool, pool_scale)


def _first_index(hit, iota, size, axis):
    return jnp.min(jnp.where(hit, iota, size), axis=axis, keepdims=True)


def _mix_body(yg_ref, yp_ref, x_ref, gt1_ref, sc2_ref, sh2_ref, gpost_ref, gpre_ref, wout_ref,
              wr_ref, rb_ref, upper_ref,
              x1_ref, hp_ref, eidx_ref, pos_ref, wts_ref, cnt_ref, run_ref, wrow_ref, y_scr):
    tm = x_ref.shape[0]
    neg_inf = jnp.float32(-jnp.inf)
    step = pl.program_id(0)

    @pl.when(step == 0)
    def _():
        run_ref[...] = jnp.zeros_like(run_ref)
        wrow_ref[...] = jnp.zeros_like(wrow_ref)
        y_scr[...] = jnp.zeros_like(y_scr)

    y = y_scr[...]
    y_new = jnp.dot(yg_ref[...], wout_ref[0:GLA_WIDTH, :], preferred_element_type=F32)
    y_scr[...] = y_new + jnp.dot(yp_ref[...], wout_ref[GLA_WIDTH:, :], preferred_element_type=F32)
    x1 = x_ref[...] + gt1_ref[0] * (y * _rms_scale(y) * gpost_ref[...])
    x1_ref[...] = x1
    h = x1 * _rms_scale(x1) * gpre_ref[...]
    h = h * (1.0 + sc2_ref[0]) + sh2_ref[0]
    _store_row_tiles(hp_ref, _pack_halves(h[:, :HALF], h[:, HALF:]))

    h_hi, h_lo = _bf16_terms(h)
    both = jnp.dot(h_hi, wr_ref[...], preferred_element_type=F32)
    lt = both[:, :LANES] + both[:, LANES:] + jnp.dot(h_lo, wr_ref[:, :LANES], preferred_element_type=F32)
    logits = lt.T[0:N_EXPERTS, :]
    scores = jax.nn.sigmoid(logits)
    sel = scores + rb_ref[...]
    shape3 = (N_GROUPS, GROUP_SIZE, tm)
    sel3 = sel.reshape(shape3)
    i_in = lax.broadcasted_iota(I32, shape3, 1).astype(F32)
    m1 = jnp.max(sel3, axis=1, keepdims=True)
    f1 = _first_index(sel3 == m1, i_in, float(GROUP_SIZE), 1)
    m2 = jnp.max(jnp.where(i_in == f1, neg_inf, sel3), axis=1, keepdims=True)
    grp = jnp.broadcast_to(m1 + m2, shape3).reshape(N_EXPERTS, tm)
    i_e = lax.broadcasted_iota(I32, (N_EXPERTS, tm), 0)
    i_grp = (i_e // GROUP_SIZE).astype(F32)
    i_e = i_e.astype(F32)
    allowed = jnp.zeros((N_EXPERTS, tm), F32)
    for _ in range(TOPK_GROUPS):
        m = jnp.max(grp, axis=0, keepdims=True)
        pick = i_grp == _first_index(grp == m, i_grp, float(N_GROUPS), 0)
        allowed = jnp.where(pick, 1.0, allowed)
        grp = jnp.where(pick, neg_inf, grp)
    cand = jnp.where(allowed > 0.0, sel, neg_inf)
    onehot = jnp.zeros((N_EXPERTS, tm), F32)
    picks, wts = [], []
    for k in range(TOP_K):
        m = jnp.max(cand, axis=0, keepdims=True)
        f = _first_index(cand == m, i_e, float(N_EXPERTS), 0)
        pick = i_e == f
        picks.append(pick)
        eidx_ref[k:k + 1, :] = f.astype(I32)
        wts.append(jnp.sum(jnp.where(pick, scores, 0.0), axis=0, keepdims=True))
        onehot = jnp.where(pick, 1.0, onehot)
        cand = jnp.where(pick, neg_inf, cand)
    w_sum = wts[0]
    for k in range(1, TOP_K):
        w_sum = w_sum + wts[k]
    for k in range(TOP_K):
        wrow_ref[k:k + 1, :] = wts[k] / w_sum * ROUTED_SCALE
    wts_ref[...] = wrow_ref[...].T

    before = jnp.dot(onehot.astype(BF16), upper_ref[...], preferred_element_type=F32)
    before = before + run_ref[:, 0:1]
    for k in range(TOP_K):
        pos_ref[k:k + 1, :] = jnp.sum(jnp.where(picks[k], before, 0.0), axis=0, keepdims=True).astype(I32)
    counted = jnp.where(step > 0, 1.0, 0.0)
    run_ref[...] = run_ref[...] + counted * jnp.sum(onehot, axis=1, keepdims=True)
    cnt_ref[...] = run_ref[...].astype(I32)


def _mix_and_route(y_gla, y_pool, x2d, gt1, sc2, sh2, g_post, g_pre, w_out, w_router2, router_bias, seq,
                   part, n_parts):
    d = x2d.shape[1]
    rows = x2d.shape[0] // n_parts
    tiles = rows // MIX_TM
    first = part * tiles
    tiles_per_b = seq // MIX_TM
    proj = lambda i: (first + jnp.minimum(i, tiles - 1), 0)
    bmap = lambda i: ((first + jnp.maximum(i - 1, 0)) // tiles_per_b, 0, 0)
    xmap = lambda i: (first + jnp.maximum(i - 1, 0), 0)
    rmap = lambda i: (jnp.maximum(i - 1, 0), 0)
    tmap = lambda i: (0, jnp.maximum(i - 1, 0))
    upper = jnp.asarray(np.triu(np.ones((MIX_TM, MIX_TM), np.float32), 1), dtype=BF16)
    return pl.pallas_call(
        _mix_body,
        grid=(tiles + 1,),
        in_specs=[
            pl.BlockSpec((MIX_TM, GLA_WIDTH), proj),
            pl.BlockSpec((MIX_TM, POOL_WIDTH), proj),
            pl.BlockSpec((MIX_TM, d), xmap),
            pl.BlockSpec((1, 1, d), bmap),
            pl.BlockSpec((1, 1, d), bmap),
            pl.BlockSpec((1, 1, d), bmap),
            _resident((1, d)),
            _resident((1, d)),
            _resident((d, d)),
            _resident((d, 256)),
            _resident((N_EXPERTS, 1)),
            _resident((MIX_TM, MIX_TM)),
        ],
        out_specs=[
            pl.BlockSpec((MIX_TM, d), rmap),
            pl.BlockSpec((MIX_TM * ROW_SUBLANES, LANES), rmap),
            pl.BlockSpec((TOP_K, MIX_TM), tmap),
            pl.BlockSpec((TOP_K, MIX_TM), tmap),
            pl.BlockSpec((MIX_TM, LANES), rmap),
            pl.BlockSpec((N_EXPERTS, LANES), lambda i: (0, 0)),
        ],
        out_shape=[
            jax.ShapeDtypeStruct((rows, d), F32),
            jax.ShapeDtypeStruct((rows * ROW_SUBLANES, LANES), U32),
            jax.ShapeDtypeStruct((TOP_K, rows), I32),
            jax.ShapeDtypeStruct((TOP_K, rows), I32),
            jax.ShapeDtypeStruct((rows, LANES), F32),
            jax.ShapeDtypeStruct((N_EXPERTS, LANES), I32),
        ],
        scratch_shapes=[pltpu.VMEM((N_EXPERTS, LANES), F32), pltpu.VMEM((LANES, MIX_TM), F32),
                        pltpu.VMEM((MIX_TM, d), F32)],
        compiler_params=_cparams(("arbitrary",)),
        name="mix_and_route",
    )(y_gla, y_pool, x2d, gt1, sc2, sh2, g_post, g_pre, w_out, w_router2, router_bias, upper)


def _swiglu(lo, hi, wg_ref, wu_ref, wd_ref):
    g = jnp.dot(lo, wg_ref[:HALF, :], preferred_element_type=F32)
    g = g + jnp.dot(hi, wg_ref[HALF:, :], preferred_element_type=F32)
    u = jnp.dot(lo, wu_ref[:HALF, :], preferred_element_type=F32)
    u = u + jnp.dot(hi, wu_ref[HALF:, :], preferred_element_type=F32)
    act = (_silu(g) * u).astype(BF16)
    return jnp.dot(act, wd_ref[...], preferred_element_type=F32)


def _shared_body(hp_ref, wg_ref, wu_ref, wd_ref, o_ref):
    lo, hi = _unpack_halves(_load_row_tiles(hp_ref))
    o_ref[...] = _swiglu(lo.astype(BF16), hi.astype(BF16), wg_ref, wu_ref, wd_ref).astype(BF16)


def _shared_expert(h_packed, w_sg, w_su, w_sd):
    d, ds = w_sg.shape
    rows = h_packed.shape[0] // ROW_SUBLANES
    return pl.pallas_call(
        _shared_body,
        grid=(rows // SHARED_TM,),
        in_specs=[
            pl.BlockSpec((SHARED_TM * ROW_SUBLANES, LANES), lambda i: (i, 0)),
            _resident((d, ds)),
            _resident((d, ds)),
            _resident((ds, d)),
        ],
        out_specs=pl.BlockSpec((SHARED_TM, d), lambda i: (i, 0)),
        out_shape=jax.ShapeDtypeStruct((rows, d), BF16),
        compiler_params=_cparams(("arbitrary",)),
        name="shared_expert",
    )(h_packed, w_sg, w_su, w_sd)


def _offsets_body(ps_ref, e_ref, p_ref, d_ref):
    e = e_ref[...]
    d = p_ref[...]
    for x in range(N_EXPERTS):
        d = d + jnp.where(e == x, ps_ref[x], 0)
    d_ref[...] = d


def _route_offsets(eidx_t, pos_t, pstarts):
    k, rows = eidx_t.shape
    spec = pl.BlockSpec((k, OFFS_TN), lambda i, ps: (0, i))
    grid_spec = pltpu.PrefetchScalarGridSpec(
        num_scalar_prefetch=1, grid=(rows // OFFS_TN,), in_specs=[spec, spec], out_specs=spec)
    return pl.pallas_call(
        _offsets_body,
        grid_spec=grid_spec,
        out_shape=jax.ShapeDtypeStruct((k, rows), I32),
        compiler_params=_cparams(("arbitrary",)),
        name="route_offsets",
    )(pstarts, eidx_t, pos_t)


def _expert_body(be_ref, valid_ref, nxt_ref, slot_ref, src_ref, inblk_ref, outblk_ref, *refs):
    x_refs = refs[:MOE_PARTS]
    wg_hbm, wu_hbm, wd_hbm, y_ref, wg_f, wu_f, wd_f, wg_b, wu_b, wd_b, sems = refs[MOE_PARTS:]
    i = pl.program_id(0)
    valid = valid_ref[i]
    expert = be_ref[i]
    slot = slot_ref[i]
    source = src_ref[i]

    def weight_copies(e, s):
        return [pltpu.make_async_copy(src.at[e], dst.at[s], sems.at[s, j])
                for j, (src, dst) in enumerate(((wg_hbm, wg_f), (wu_hbm, wu_f), (wd_hbm, wd_f)))]

    @pl.when(i == 0)
    def _():
        for cp in weight_copies(expert, slot):
            cp.start()

    @pl.when((i == 0) | (expert != be_ref[jnp.maximum(i - 1, 0)]))
    def _():
        for cp in weight_copies(expert, slot):
            cp.wait()

        @pl.when(nxt_ref[i] != expert)
        def _():
            for cp in weight_copies(nxt_ref[i], 1 - slot):
                cp.start()

        wg_b[...] = wg_f[slot].astype(BF16)
        wu_b[...] = wu_f[slot].astype(BF16)
        wd_b[...] = wd_f[slot].astype(BF16)

    def compute(x_ref, n_rows):
        tiles = pl.ds(0, n_rows * ROW_SUBLANES)
        lo, hi = _unpack_halves(_load_row_tiles(x_ref.at[tiles]))
        y = _swiglu(lo.astype(BF16), hi.astype(BF16), wg_b, wu_b, wd_b)
        _store_row_tiles(y_ref.at[tiles], _pack_halves(y[:, :HALF], y[:, HALF:]))

    quarter = MOE_TILE // 4
    for s, x_ref in enumerate(x_refs):
        for n in range(1, 5):
            @pl.when((source == s) & (valid > (n - 1) * quarter) & (valid <= n * quarter))
            def _(x_ref=x_ref, n=n):
                compute(x_ref, n * quarter)


def _experts(xs_parts, block_expert, valid, next_expert, slot, source, in_block, out_block, w_eg, w_eu, w_ed):
    rows_per_part = xs_parts[0].shape[0] // ROW_SUBLANES
    n_blocks = MOE_PARTS * rows_per_part // MOE_TILE
    d, de = w_eg.shape[1], w_eg.shape[2]
    x_specs = [pl.BlockSpec((MOE_TILE * ROW_SUBLANES, LANES),
                            functools.partial(lambda i, *pf, p: (pf[5][p * n_blocks + i], 0), p=p))
               for p in range(MOE_PARTS)]
    grid_spec = pltpu.PrefetchScalarGridSpec(
        num_scalar_prefetch=7,
        grid=(n_blocks,),
        in_specs=x_specs + [pl.BlockSpec(memory_space=pl.ANY)] * 3,
        out_specs=pl.BlockSpec((MOE_TILE * ROW_SUBLANES, LANES), lambda i, *pf: (pf[6][i], 0)),
        scratch_shapes=[
            pltpu.VMEM((2, d, de), F32),
            pltpu.VMEM((2, d, de), F32),
            pltpu.VMEM((2, de, d), F32),
            pltpu.VMEM((d, de), BF16),
            pltpu.VMEM((d, de), BF16),
            pltpu.VMEM((de, d), BF16),
            pltpu.SemaphoreType.DMA((2, 3)),
        ],
    )
    return pl.pallas_call(
        _expert_body,
        grid_spec=grid_spec,
        out_shape=jax.ShapeDtypeStruct((MOE_PARTS * rows_per_part * ROW_SUBLANES, LANES), U32),
        compiler_params=_cparams(("arbitrary",)),
        name="experts",
    )(block_expert, valid, next_expert, slot, source, in_block, out_block, *xs_parts, w_eg, w_eu, w_ed)


def _sc_mesh():
    return plsc.VectorSubcoreMesh(core_axis_name="c", subcore_axis_name="s")


def _sc_worker():
    return lax.axis_index("s") * SC_CORES + lax.axis_index("c")


def _sc_gather_rows(table, idx):
    n_idx = idx.shape[0]
    per_worker = n_idx // SC_WORKERS
    n_chunks = per_worker // SC_CHUNK
    assert per_worker * SC_WORKERS == n_idx and n_chunks * SC_CHUNK == per_worker and n_chunks % 2 == 0
    row_shape = table.shape[1:]

    @functools.partial(
        pl.kernel, mesh=_sc_mesh(),
        out_type=jax.ShapeDtypeStruct((n_idx,) + row_shape, table.dtype),
        scratch_types=[
            pltpu.VMEM((per_worker,), I32),
            pltpu.VMEM((SC_CHUNK,) + row_shape, table.dtype),
            pltpu.VMEM((SC_CHUNK,) + row_shape, table.dtype),
        ] + [pltpu.SemaphoreType.DMA] * 4,
    )
    def gather(table_hbm, idx_hbm, out_hbm, idx_v, buf0, buf1, g0, g1, w0, w1):
        bufs, gsem, wsem = (buf0, buf1), (g0, g1), (w0, w1)
        base = _sc_worker() * per_worker
        pltpu.sync_copy(idx_hbm.at[pl.ds(base, per_worker)], idx_v)

        def fetch(j, b):
            return pltpu.make_async_copy(table_hbm.at[idx_v.at[pl.ds(j * SC_CHUNK, SC_CHUNK)]], bufs[b], gsem[b])

        def flush(j, b):
            return pltpu.make_async_copy(bufs[b], out_hbm.at[pl.ds(base + j * SC_CHUNK, SC_CHUNK)], wsem[b])

        fetch(0, 0).start()
        fetch(0, 0).wait()
        fetch(1, 1).start()
        flush(0, 0).start()

        @pl.loop(1, n_chunks - 1, step=2)
        def _(j):
            for off in range(2):
                jj, b = j + off, (1 + off) % 2
                fetch(jj, b).wait()
                flush(jj - 1, 1 - b).wait()
                fetch(jj + 1, 1 - b).start()
                flush(jj, b).start()

        fetch(n_chunks - 1, 1).wait()
        flush(n_chunks - 1, 1).start()
        flush(n_chunks - 2, 0).wait()
        flush(n_chunks - 1, 1).wait()

    return gather(table, idx)


def _sc_scatter_rows(rows, dest_t, n_out):
    n_rows = rows.shape[0]
    per_worker = n_rows // SC_WORKERS
    n_chunks = per_worker // SC_CHUNK
    assert per_worker * SC_WORKERS == n_rows and n_chunks * SC_CHUNK == per_worker
    row_shape = rows.shape[1:]
    idx_w = dest_t.reshape(TOP_K, SC_WORKERS, n_chunks, SC_CHUNK).transpose(1, 2, 0, 3)
    idx_w = idx_w.reshape(SC_WORKERS, n_chunks * TOP_K, SC_CHUNK)

    @functools.partial(
        pl.kernel, mesh=_sc_mesh(),
        out_type=jax.ShapeDtypeStruct((n_out,) + row_shape, rows.dtype),
        scratch_types=[
            pltpu.VMEM((n_chunks * TOP_K, SC_CHUNK), I32),
            pltpu.VMEM((SC_CHUNK,) + row_shape, rows.dtype),
            pltpu.VMEM((SC_CHUNK,) + row_shape, rows.dtype),
        ] + [pltpu.SemaphoreType.DMA] * 4,
    )
    def scatter(rows_hbm, idx_hbm, out_hbm, idx_v, buf0, buf1, r0, r1, s0, s1):
        bufs, rsem, ssem = (buf0, buf1), (r0, r1), (s0, s1)
        worker = _sc_worker()
        base = worker * per_worker
        pltpu.sync_copy(idx_hbm.at[worker], idx_v)

        def fetch(j, b):
            return pltpu.make_async_copy(rows_hbm.at[pl.ds(base + j * SC_CHUNK, SC_CHUNK)], bufs[b], rsem[b])

        def send(j, k, b):
            return pltpu.make_async_copy(bufs[b], out_hbm.at[idx_v.at[j * TOP_K + k]], ssem[b])

        fetch(0, 0).start()
        for j in range(n_chunks):
            b = j % 2
            fetch(j, b).wait()
            if j + 1 < n_chunks:
                if j >= 1:
                    for k in range(TOP_K):
                        send(j - 1, k, 1 - b).wait()
                fetch(j + 1, 1 - b).start()
            for k in range(TOP_K):
                send(j, k, b).start()
        for j in range(max(n_chunks - 2, 0), n_chunks):
            for k in range(TOP_K):
                send(j, k, j % 2).wait()

    return scatter(rows, idx_w)


def _combine_body(*refs):
    yk_refs = refs[:TOP_K]
    w_ref, shr_ref, x1_ref, gt2_ref, gpost_ref = refs[TOP_K:TOP_K + 5]
    o_ref = refs[-1]
    tt = x1_ref.shape[0]
    w = w_ref[...]
    ssq = jnp.zeros((tt, 1), F32)
    for c in range(HALF // LANES):
        c_lo = slice(c * LANES, (c + 1) * LANES)
        c_hi = slice(HALF + c * LANES, HALF + (c + 1) * LANES)
        y_lo = shr_ref[:, c_lo].astype(F32)
        y_hi = shr_ref[:, c_hi].astype(F32)
        for k in range(TOP_K):
            lo, hi = _unpack_halves(yk_refs[k][pl.ds(c, tt, stride=ROW_SUBLANES), :])
            y_lo = y_lo + w[:, k:k + 1] * lo
            y_hi = y_hi + w[:, k:k + 1] * hi
        ssq = ssq + jnp.sum(y_lo * y_lo, axis=-1, keepdims=True) + jnp.sum(y_hi * y_hi, axis=-1, keepdims=True)
        o_ref[:, c_lo] = y_lo
        o_ref[:, c_hi] = y_hi
    scale = lax.rsqrt(ssq / D_MODEL + EPS)
    o_ref[...] = x1_ref[...] + gt2_ref[0] * (o_ref[...] * scale * gpost_ref[...])


def _combine(yu, wts, shared, x1, gt2, g_post, seq, part, n_parts, prev_out):
    rows, d = x1.shape
    tiles = rows // COMB_TT
    first = part * tiles
    tiles_per_b = seq // COMB_TT
    loc = lambda i: (i, 0)
    yk_specs = [pl.BlockSpec((COMB_TT * ROW_SUBLANES, LANES), functools.partial(lambda i, k: (k * tiles + i, 0), k=k))
                for k in range(TOP_K)]
    in_specs = yk_specs + [
        pl.BlockSpec((COMB_TT, LANES), loc),
        pl.BlockSpec((COMB_TT, d), loc),
        pl.BlockSpec((COMB_TT, d), loc),
        pl.BlockSpec((1, 1, d), lambda i: ((first + i) // tiles_per_b, 0, 0)),
        _resident((1, d)),
    ]
    args = [yu] * TOP_K + [wts, shared, x1, gt2, g_post]
    aliases = {}
    if prev_out is not None:
        aliases = {len(args): 0}
        in_specs.append(pl.BlockSpec(memory_space=pl.ANY))
        args.append(prev_out)
    return pl.pallas_call(
        _combine_body,
        grid=(tiles,),
        in_specs=in_specs,
        out_specs=pl.BlockSpec((COMB_TT, d), lambda i: (first + i, 0)),
        out_shape=jax.ShapeDtypeStruct((rows * n_parts, d), F32),
        input_output_aliases=aliases,
        compiler_params=_cparams(("arbitrary",)),
        name="combine",
    )(*args)


def kernel(x, c, ctx, c_ctx, w_mod, b_mod, norm_mix_pre, norm_mix_post, norm_ffn_pre, norm_ffn_post, w_in, w_a2_fwd, b_a_fwd, w_a2_bwd, b_a_bwd, gla_norm, w_pool, pool_scale, w_out, w_router, router_bias, w_exp_gate, w_exp_up, w_exp_down, w_sh_gate, w_sh_up, w_sh_down):
    batch, seq, d = x.shape
    n_ctx = ctx.shape[1]
    assert w_mod.shape[0] == 1 and d == D_MODEL
    assert seq % (2 * SUPER) == 0 and n_ctx % SUPER == 0 and seq % PROJ_TM == 0 and (batch * n_ctx) % PROJ_TM == 0
    rows = batch * seq

    mod_rows = 16
    c_all = jnp.concatenate([c, c_ctx[None, :], jnp.zeros((mod_rows - batch - 1, d), F32)], axis=0)
    mod_all = _modulation(c_all, w_mod[0], b_mod[0][None, :])
    sh1, sc1, gt1, sh2, sc2, gt2 = [m.reshape(batch, 1, d) for m in jnp.split(mod_all[:batch], 6, axis=-1)]
    csh1 = mod_all[batch, 0:d].reshape(1, 1, d)
    csc1 = mod_all[batch, d:2 * d].reshape(1, 1, d)

    kw, gw = GLA_KEY_WIDTH, GLA_WIDTH
    a0 = 2 * kw + 2 * gw
    w_in0 = w_in[0]
    w_bf = w_in0.astype(BF16)
    w_a = jnp.pad(w_bf[:, a0:a0 + 2 * GLA_RANK], ((0, 0), (0, LANES - 2 * GLA_RANK)))
    lat_pieces = [(w_bf, a0, 0), (w_bf[:, a0 + 2 * GLA_RANK:], POOL_WIDTH, 0)]
    ctx_pieces = [(w_bf, gw, 2 * kw // gw), (w_bf, kw, 1)]
    w2f = jnp.pad(w_a2_fwd[0], ((0, LANES - GLA_RANK), (0, 0))).astype(BF16)
    w2b = jnp.pad(w_a2_bwd[0], ((GLA_RANK, LANES - 2 * GLA_RANK), (0, 0))).astype(BF16)
    g_mix_pre = norm_mix_pre[0][None, :]
    w_router2 = jnp.concatenate(_bf16_terms(jnp.pad(w_router[0], ((0, 0), (0, LANES - N_EXPERTS)))), axis=1)

    u_ctx, a_ctx = _in_projection(ctx.reshape(batch * n_ctx, d), g_mix_pre, csc1, csh1, ctx_pieces, w_a,
                                  batch * n_ctx)
    u_lat, a_lat = _in_projection(x.reshape(rows, d), g_mix_pre, sc1, sh1, lat_pieces, w_a, seq)

    y_gla = _gla(u_lat, a_lat, u_ctx, a_ctx, w2f, b_a_fwd[0][None, :], w2b, b_a_bwd[0][None, :],
                 gla_norm[0][None, :], batch, seq, n_ctx)
    y_pool = _pool_mixer(u_lat, _col_window_matrices(), w_pool[0].astype(BF16), pool_scale[0][None, :],
                         batch, seq)

    w_out_bf = w_out[0].astype(BF16)
    shared_w = (w_sh_gate[0].astype(BF16), w_sh_up[0].astype(BF16), w_sh_down[0].astype(BF16))
    x2d = x.reshape(rows, d)
    part_rows = rows // MOE_PARTS
    n_blocks = part_rows * TOP_K // MOE_TILE + N_EXPERTS
    e_ids = jnp.arange(N_EXPERTS, dtype=I32)

    mixed = [_mix_and_route(y_gla, y_pool, x2d, gt1, sc2, sh2, norm_mix_post[0][None, :],
                            norm_ffn_pre[0][None, :], w_out_bf, w_router2, router_bias[0][:, None], seq,
                            p, MOE_PARTS) for p in range(MOE_PARTS)]
    xs_parts, dests, counts_p, pstarts_p, blocks_p = [], [], [], [], []
    for x1, h_packed, eidx_t, pos_t, wts, counts in mixed:
        counts = counts[:, 0]
        padded = (counts + MOE_TILE - 1) // MOE_TILE * MOE_TILE
        pstarts = jnp.cumsum(padded) - padded
        dest_t = _route_offsets(eidx_t, pos_t, pstarts.astype(I32))
        xs = _sc_scatter_rows(h_packed.reshape(-1, ROW_SUBLANES, LANES), dest_t, n_blocks * MOE_TILE)
        xs_parts.append(xs.reshape(-1, LANES))
        dests.append(dest_t)
        counts_p.append(counts)
        pstarts_p.append(pstarts // MOE_TILE)
        blocks_p.append(padded // MOE_TILE)

    counts_p, pstarts_p, blocks_p = jnp.stack(counts_p), jnp.stack(pstarts_p), jnp.stack(blocks_p)
    per_expert = jnp.sum(blocks_p, axis=0)
    ends = jnp.cumsum(per_expert)
    n_total = MOE_PARTS * n_blocks
    n_used = ends[-1]
    blk = jnp.minimum(jnp.arange(n_total, dtype=I32), n_used - 1)
    block_expert = jnp.minimum(jnp.sum(blk[:, None] >= ends[None, :], axis=1), N_EXPERTS - 1).astype(I32)
    is_block_expert = block_expert[:, None] == e_ids[None, :]
    per_block = lambda v: jnp.sum(jnp.where(is_block_expert, v[None, :], 0), axis=1).astype(I32)
    offset = blk - per_block(ends - per_expert)
    source = jnp.zeros_like(blk)
    local = jnp.zeros_like(blk)
    valid = jnp.zeros_like(blk)
    before = jnp.zeros_like(blk)
    for p in range(MOE_PARTS):
        mine = (offset >= before) & (offset < before + per_block(blocks_p[p]))
        inner = offset - before
        source = jnp.where(mine, p, source)
        local = jnp.where(mine, per_block(pstarts_p[p]) + inner, local)
        valid = jnp.where(mine, jnp.clip(per_block(counts_p[p]) - inner * MOE_TILE, 0, MOE_TILE), valid)
        before = before + per_block(blocks_p[p])
    valid = jnp.where(jnp.arange(n_total, dtype=I32) < n_used, valid, 0).astype(I32)
    in_block = jnp.concatenate([lax.cummax(jnp.where(source == p, local, 0)) for p in range(MOE_PARTS)])
    out_block = source * n_blocks + local
    has_rows = per_expert > 0
    later = jnp.where((e_ids[None, :] > e_ids[:, None]) & has_rows[None, :], e_ids[None, :], N_EXPERTS)
    next_e = jnp.min(later, axis=1)
    next_e = jnp.where(next_e == N_EXPERTS, e_ids, next_e)
    slot_e = (jnp.cumsum(has_rows.astype(I32)) - has_rows.astype(I32)) % 2

    ys = _experts(xs_parts, block_expert, valid, per_block(next_e), per_block(slot_e), source.astype(I32),
                  in_block.astype(I32), out_block.astype(I32), w_exp_gate[0], w_exp_up[0], w_exp_down[0])
    ys3 = ys.reshape(-1, ROW_SUBLANES, LANES)
    out = None
    for p, (x1, h_packed, eidx_t, pos_t, wts, counts) in enumerate(mixed):
        shared = _shared_expert(h_packed, *shared_w)
        yu = _sc_gather_rows(ys3, (dests[p] + p * n_blocks * MOE_TILE).reshape(-1))
        out = _combine(yu.reshape(-1, LANES), wts, shared, x1, gt2, norm_ffn_post[0][None, :], seq,
                       p, MOE_PARTS, out)
    return out.reshape(batch, seq, d)
```

```python
import functools

import numpy as np
import jax
import jax.numpy as jnp
from jax import lax
from jax.experimental import pallas as pl
from jax.experimental.pallas import tpu as pltpu
from jax.experimental.pallas import tpu_sc as plsc

F32 = jnp.float32
BF16 = jnp.bfloat16
I32 = jnp.int32
U32 = jnp.uint32

D_MODEL = 2048
GRID_W = 64
GLA_HEADS = 4
GLA_DK = 128
GLA_DV = 256
GLA_KEY_WIDTH = GLA_HEADS * GLA_DK
GLA_WIDTH = GLA_HEADS * GLA_DV
GLA_RANK = 16
GLA_TAU = 16.0
GLA_CHUNK = 64
POOL_WIDTH = 1024
POOL_WINDOWS = (2, 4, 8, 16)
POOL_GROUP = 256
N_EXPERTS = 64
TOP_K = 8
N_GROUPS = 8
GROUP_SIZE = N_EXPERTS // N_GROUPS
TOPK_GROUPS = 4
ROUTED_SCALE = 2.5
EPS = 1e-6

HALF = D_MODEL // 2
SUPER = 4 * GLA_CHUNK
GLA_HPS = 4
POOL_PAD = 8 * GRID_W
VMEM_LIMIT = 56 * 1024 * 1024

MOD_TN = 1024
PROJ_TM = 512
PROJ_TN = 512
MIX_TM = 512
MOE_TILE = 512
COMB_TT = 256
MOE_PARTS = 2
LANES = 128
ROW_SUBLANES = 8
SC_CORES = 2
SC_WORKERS = 32
SC_CHUNK = 32
OFFS_TN = 2048


def _cparams(sem):
    return pltpu.CompilerParams(dimension_semantics=sem, vmem_limit_bytes=VMEM_LIMIT)


def _resident(shape):
    nd = len(shape)
    return pl.BlockSpec(shape, lambda *_: (0,) * nd, pipeline_mode=pl.Buffered(1))


def _silu(v):
    return v * jax.nn.sigmoid(v)


def _pack_halves(lo, hi):
    lo_b = lax.bitcast_convert_type(lo.astype(BF16).astype(F32), U32)
    hi_b = lax.bitcast_convert_type(hi.astype(BF16).astype(F32), U32)
    return (hi_b & jnp.uint32(0xFFFF0000)) | (lo_b >> 16)


def _unpack_halves(p):
    lo = lax.bitcast_convert_type(p << 16, F32)
    hi = lax.bitcast_convert_type(p & jnp.uint32(0xFFFF0000), F32)
    return lo, hi


def _bf16_terms(x):
    hi = lax.bitcast_convert_type(lax.bitcast_convert_type(x, U32) & jnp.uint32(0xFFFF0000), F32)
    return hi.astype(BF16), (x - hi).astype(BF16)


def _store_row_tiles(ref, packed):
    n = packed.shape[0]
    for c in range(HALF // LANES):
        ref[pl.ds(c, n, stride=ROW_SUBLANES), :] = packed[:, c * LANES:(c + 1) * LANES]


def _load_row_tiles(ref):
    n = ref.shape[0] // ROW_SUBLANES
    return jnp.concatenate([ref[pl.ds(c, n, stride=ROW_SUBLANES), :] for c in range(HALF // LANES)], axis=1)


def _mod_body(c_ref, w_ref, b_ref, o_ref):
    s_hi, s_lo = _bf16_terms(_silu(c_ref[...]))
    w_hi, w_lo = _bf16_terms(w_ref[...])
    acc = jnp.dot(s_hi, w_hi, preferred_element_type=F32)
    acc = acc + jnp.dot(s_lo, w_hi, preferred_element_type=F32)
    acc = acc + jnp.dot(s_hi, w_lo, preferred_element_type=F32)
    o_ref[...] = acc + b_ref[...]


def _modulation(c_all, w_mod, b_mod):
    rows, d = c_all.shape
    n = w_mod.shape[1]
    return pl.pallas_call(
        _mod_body,
        grid=(n // MOD_TN,),
        in_specs=[
            pl.BlockSpec((rows, d), lambda j: (0, 0)),
            pl.BlockSpec((d, MOD_TN), lambda j: (0, j)),
            pl.BlockSpec((1, MOD_TN), lambda j: (0, j)),
        ],
        out_specs=pl.BlockSpec((rows, MOD_TN), lambda j: (0, j)),
        out_shape=jax.ShapeDtypeStruct((rows, n), F32),
        compiler_params=_cparams(("arbitrary",)),
        name="modulation",
    )(c_all, w_mod, b_mod)


def _rms_scale(x):
    return lax.rsqrt(jnp.mean(x * x, axis=-1, keepdims=True) + EPS)


def _inproj_body(x_ref, g_ref, sc_ref, sh_ref, *refs):
    w_refs, (wa_ref, o_ref, a_ref) = refs[:-3], refs[-3:]
    x = x_ref[...]
    h = x * _rms_scale(x) * g_ref[...]
    h = h * (1.0 + sc_ref[0]) + sh_ref[0]
    hb = h.astype(BF16)
    col = 0
    for w_ref in w_refs:
        for n in range(w_ref.shape[1] // PROJ_TN):
            cols = slice(n * PROJ_TN, (n + 1) * PROJ_TN)
            o_ref[:, col:col + PROJ_TN] = jnp.dot(hb, w_ref[:, cols], preferred_element_type=F32).astype(BF16)
            col += PROJ_TN
    a_ref[...] = jnp.dot(hb, wa_ref[...], preferred_element_type=F32)


def _in_projection(x2d, gain, sc, sh, pieces, w_a, rows_per_mod):
    rows, d = x2d.shape
    n_main = sum(width for _, width, _ in pieces)
    tiles_per_mod = rows_per_mod // PROJ_TM
    mod_map = lambda i: (i // tiles_per_mod, 0, 0)
    piece_specs = [pl.BlockSpec((d, width), functools.partial(lambda i, b: (0, b), b=block),
                                pipeline_mode=pl.Buffered(1)) for _, width, block in pieces]
    return pl.pallas_call(
        _inproj_body,
        grid=(rows // PROJ_TM,),
        in_specs=[
            pl.BlockSpec((PROJ_TM, d), lambda i: (i, 0)),
            _resident((1, d)),
            pl.BlockSpec((1, 1, d), mod_map),
            pl.BlockSpec((1, 1, d), mod_map),
            *piece_specs,
            _resident((d, LANES)),
        ],
        out_specs=[
            pl.BlockSpec((PROJ_TM, n_main), lambda i: (i, 0)),
            pl.BlockSpec((PROJ_TM, LANES), lambda i: (i, 0)),
        ],
        out_shape=[
            jax.ShapeDtypeStruct((rows, n_main), BF16),
            jax.ShapeDtypeStruct((rows, LANES), F32),
        ],
        compiler_params=_cparams(("arbitrary",)),
        name="in_projection",
    )(x2d, gain, sc, sh, *[w for w, _, _ in pieces], w_a)


def _log_sigmoid(z):
    return jnp.minimum(z, 0.0) - jnp.log1p(jnp.exp(-jnp.abs(z)))


def _gla_cumulative_decay(a, w2, ba, tri):
    z = jnp.dot(a.astype(BF16), w2, preferred_element_type=F32) + ba
    g = _log_sigmoid(z) * (1.0 / GLA_TAU)
    g_hi, g_lo = _bf16_terms(g)
    return jnp.dot(tri, g_hi, preferred_element_type=F32) + jnp.dot(tri, g_lo, preferred_element_type=F32)


def _gla_prep(q, k, G, reverse):
    nc = SUPER // GLA_CHUNK
    G = G.reshape(nc, GLA_CHUNK, GLA_DK)
    end_row = 0 if reverse else GLA_CHUNK - 1
    mid_row = GLA_CHUNK - 1 - GLA_CHUNK // 2 if reverse else GLA_CHUNK // 2
    g_end = G[:, end_row:end_row + 1, :]
    g_mid = G[:, mid_row:mid_row + 1, :]
    k4 = k.astype(F32).reshape(nc, GLA_CHUNK, GLA_DK)
    dec = jnp.broadcast_to(jnp.exp(g_end), (nc, ROW_SUBLANES, GLA_DK)).reshape(nc * ROW_SUBLANES, GLA_DK)
    flat = lambda t: t.reshape(SUPER, GLA_DK).astype(BF16)
    if q is None:
        return None, None, None, flat(k4 * jnp.exp(g_end - G)), dec
    q4 = q.astype(F32).reshape(nc, GLA_CHUNK, GLA_DK) * (GLA_DK ** -0.5)
    qg = q4 * jnp.exp(G - g_mid)
    kg = k4 * jnp.exp(g_mid - G)
    qe = qg * jnp.exp(g_mid)
    kd = kg * jnp.exp(g_end - g_mid)
    return flat(qg), flat(kg), flat(qe), flat(kd), dec


def _gla_apply(qg, kg, qe, kd, dec, v, mask, st_ref, reverse):
    nc = SUPER // GLA_CHUNK
    o = None
    if qg is not None:
        att = lax.dot_general(qg, kg, (((1,), (1,)), ((), ())), preferred_element_type=F32)
        att = jnp.where(mask, att, 0.0).astype(BF16)
        o = jnp.dot(att, v, preferred_element_type=F32)
    outs = [None] * nc
    order = range(nc - 1, -1, -1) if reverse else range(nc)
    for c in order:
        rows = slice(c * GLA_CHUNK, (c + 1) * GLA_CHUNK)
        st = st_ref[...]
        if qg is not None:
            inter = lax.dot_general(qe[rows], st.astype(BF16), (((1,), (1,)), ((), ())),
                                    preferred_element_type=F32)
            outs[c] = o[rows] + inter
        upd = lax.dot_general(v[rows], kd[rows], (((0,), (0,)), ((), ())), preferred_element_type=F32)
        st_ref[...] = st * dec[ROW_SUBLANES * c:ROW_SUBLANES * c + 1, :] + upd
    if qg is None:
        return None
    return jnp.concatenate(outs, axis=0)


def _gla_body(q_ref, k_ref, v_ref, r_ref, a_ref, kc_ref, vc_ref, ac_ref,
              w2f_ref, baf_ref, w2b_ref, bab_ref, gn_ref, y_ref, o_acc, st, ops_a, ops_b, dec_a, dec_b,
              *, n_ctx):
    n_sup = q_ref.shape[0] // SUPER
    row = lax.broadcasted_iota(I32, (SUPER, SUPER), 0)
    col = lax.broadcasted_iota(I32, (SUPER, SUPER), 1)
    same_chunk = (row // GLA_CHUNK) == (col // GLA_CHUNK)
    mask_f = same_chunk & (col <= row)
    mask_b = same_chunk & (col >= row)
    tri_f = jnp.where(mask_f, 1.0, 0.0).astype(BF16)
    tri_b = jnp.where(mask_b, 1.0, 0.0).astype(BF16)
    heads = range(GLA_HPS)
    kcol = [slice(h * GLA_DK, (h + 1) * GLA_DK) for h in heads]
    vcol = [slice(h * GLA_DV, (h + 1) * GLA_DV) for h in heads]
    dirs = ((False, w2f_ref, baf_ref, tri_f, mask_f), (True, w2b_ref, bab_ref, tri_b, mask_b))

    st[...] = jnp.zeros_like(st)
    n_csup = n_ctx // SUPER
    for s in range(n_csup):
        for d, (reverse, w2_ref, ba_ref, tri, mask) in enumerate(dirs):
            sc = n_csup - 1 - s if reverse else s
            rows = slice(sc * SUPER, (sc + 1) * SUPER)
            G = _gla_cumulative_decay(ac_ref[rows, :], w2_ref[...], ba_ref[...], tri)
            for h in heads:
                _, _, _, kd, dec = _gla_prep(None, kc_ref[rows, kcol[h]], G[:, kcol[h]], reverse)
                _gla_apply(None, None, None, kd, dec, vc_ref[rows, vcol[h]], mask, st.at[d, h], reverse)

    o_acc[...] = jnp.zeros_like(o_acc)

    def rows_of(i, reverse):
        sc = n_sup - 1 - i if reverse else i
        return pl.ds(pl.multiple_of(sc * SUPER, SUPER), SUPER)

    def prepare(i, ops, decs, d):
        reverse, w2_ref, ba_ref, tri, mask = dirs[d]
        rows = rows_of(i, reverse)
        G = _gla_cumulative_decay(a_ref[rows, :], w2_ref[...], ba_ref[...], tri)
        for h in heads:
            vals = _gla_prep(q_ref[rows, kcol[h]], k_ref[rows, kcol[h]], G[:, kcol[h]], reverse)
            for j in range(4):
                ops[GLA_HPS * d + h, j] = vals[j]
            decs[GLA_HPS * d + h] = vals[4]

    def apply(i, ops, decs, d):
        reverse, w2_ref, ba_ref, tri, mask = dirs[d]
        rows = rows_of(i, reverse)
        for h in heads:
            ci = GLA_HPS * d + h
            out = _gla_apply(ops[ci, 0], ops[ci, 1], ops[ci, 2], ops[ci, 3], decs[ci],
                             v_ref[rows, vcol[h]], mask, st.at[d, h], reverse)
            o_acc[rows, vcol[h]] += out

    for d in range(2):
        prepare(0, ops_a, dec_a, d)

    def step(j, carry):
        i = 2 * j
        for d in range(2):
            prepare(i + 1, ops_b, dec_b, d)
            apply(i, ops_a, dec_a, d)
        nxt = jnp.minimum(i + 2, n_sup - 1)
        for d in range(2):
            prepare(nxt, ops_a, dec_a, d)
            apply(i + 1, ops_b, dec_b, d)
        return carry

    lax.fori_loop(0, n_sup // 2, step, 0)

    for h in heads:
        o = o_acc[:, vcol[h]]
        o = o * _rms_scale(o) * gn_ref[:, vcol[h]]
        y_ref[:, vcol[h]] = (o * _silu(r_ref[:, vcol[h]].astype(F32))).astype(BF16)


def _gla(u_lat, a_lat, u_ctx, a_ctx, w2f, baf, w2b, bab, gla_norm, batch, seq, n_ctx):
    groups = GLA_HEADS // GLA_HPS
    kw, vw = GLA_HPS * GLA_DK, GLA_HPS * GLA_DV
    kb = GLA_KEY_WIDTH // kw
    vb = 2 * GLA_KEY_WIDTH // vw
    rb = vb + groups
    assert GLA_KEY_WIDTH % kw == 0 and (2 * GLA_KEY_WIDTH) % vw == 0 and GLA_WIDTH % kw == 0
    ckb = GLA_WIDTH // kw
    return pl.pallas_call(
        functools.partial(_gla_body, n_ctx=n_ctx),
        grid=(batch, groups),
        in_specs=[
            pl.BlockSpec((seq, kw), lambda b, h: (b, h)),
            pl.BlockSpec((seq, kw), lambda b, h: (b, kb + h)),
            pl.BlockSpec((seq, vw), lambda b, h: (b, vb + h)),
            pl.BlockSpec((seq, vw), lambda b, h: (b, rb + h)),
            pl.BlockSpec((seq, LANES), lambda b, h: (b, 0)),
            pl.BlockSpec((n_ctx, kw), lambda b, h: (b, ckb + h)),
            pl.BlockSpec((n_ctx, vw), lambda b, h: (b, h)),
            pl.BlockSpec((n_ctx, LANES), lambda b, h: (b, 0)),
            pl.BlockSpec((LANES, kw), lambda b, h: (0, h)),
            pl.BlockSpec((1, kw), lambda b, h: (0, h)),
            pl.BlockSpec((LANES, kw), lambda b, h: (0, h)),
            pl.BlockSpec((1, kw), lambda b, h: (0, h)),
            pl.BlockSpec((1, vw), lambda b, h: (0, h)),
        ],
        out_specs=pl.BlockSpec((seq, vw), lambda b, h: (b, h)),
        out_shape=jax.ShapeDtypeStruct((batch * seq, GLA_WIDTH), BF16),
        scratch_shapes=[
            pltpu.VMEM((seq, vw), F32),
            pltpu.VMEM((2, GLA_HPS, GLA_DV, GLA_DK), F32),
            pltpu.VMEM((2 * GLA_HPS, 4, SUPER, GLA_DK), BF16),
            pltpu.VMEM((2 * GLA_HPS, 4, SUPER, GLA_DK), BF16),
            pltpu.VMEM((2 * GLA_HPS, ROW_SUBLANES * (SUPER // GLA_CHUNK), GLA_DK), F32),
            pltpu.VMEM((2 * GLA_HPS, ROW_SUBLANES * (SUPER // GLA_CHUNK), GLA_DK), F32),
        ],
        compiler_params=_cparams(("arbitrary", "arbitrary")),
        name="gla",
    )(u_lat, u_lat, u_lat, u_lat, a_lat, u_ctx, u_ctx, a_ctx, w2f, baf, w2b, bab, gla_norm)


def _col_window_matrices():
    t = np.arange(SUPER)
    r, c = t // GRID_W, t % GRID_W
    mats = []
    for w in POOL_WINDOWS:
        lo = np.maximum(c - w // 2, 0)[:, None]
        hi = np.minimum(c + w // 2, GRID_W)[:, None]
        m = (r[:, None] == r[None, :]) & (c[None, :] >= lo) & (c[None, :] < hi)
        mats.append(m.astype(np.float32))
    return jnp.asarray(np.stack(mats), dtype=BF16)


def _pool_body(p_ref, cw_ref, wp_ref, ps_ref, y_ref, pad_ref):
    seq = p_ref.shape[0]
    n_rows = seq // GRID_W
    zeros = jnp.zeros((POOL_PAD, POOL_GROUP), F32)
    pad_ref[0:POOL_PAD, :] = zeros
    pad_ref[POOL_PAD + seq:POOL_PAD + seq + POOL_PAD, :] = zeros
    t = lax.broadcasted_iota(I32, (seq, POOL_GROUP), 0)
    r = t // GRID_W
    c = t % GRID_W
    for gi, w in enumerate(POOL_WINDOWS):
        cols = slice(gi * POOL_GROUP, (gi + 1) * POOL_GROUP)
        cw = cw_ref[gi]
        for j in range(seq // SUPER):
            rows = slice(j * SUPER, (j + 1) * SUPER)
            pad_ref[POOL_PAD + j * SUPER:POOL_PAD + (j + 1) * SUPER, :] = jnp.dot(
                cw, p_ref[rows, cols], preferred_element_type=F32)
        total = None
        for d in range(-(w // 2), w // 2):
            start = POOL_PAD + d * GRID_W
            part = pad_ref[start:start + seq, :]
            total = part if total is None else total + part
        cnt_r = jnp.minimum(r + w // 2, n_rows) - jnp.maximum(r - w // 2, 0)
        cnt_c = jnp.minimum(c + w // 2, GRID_W) - jnp.maximum(c - w // 2, 0)
        mean = total / (cnt_r * cnt_c).astype(F32)
        diff = (mean - p_ref[:, cols].astype(F32)).astype(BF16)
        y = jnp.dot(diff, wp_ref[gi], preferred_element_type=F32) * ps_ref[:, cols]
        y_ref[:, cols] = y.astype(BF16)


def _pool_mixer(u_lat, col_mats, w_pool, pool_scale, batch, seq):
    pb = (u_lat.shape[1] - POOL_WIDTH) // POOL_WIDTH
    ng = len(POOL_WINDOWS)
    return pl.pallas_call(
        _pool_body,
        grid=(batch,),
        in_specs=[
            pl.BlockSpec((seq, POOL_WIDTH), lambda b: (b, pb)),
            _resident((ng, SUPER, SUPER)),
            _resident((ng, POOL_GROUP, POOL_GROUP)),
            _resident((1, POOL_WIDTH)),
        ],
        out_specs=pl.BlockSpec((seq, POOL_WIDTH), lambda b: (b, 0)),
        out_shape=jax.ShapeDtypeStruct((batch * seq, POOL_WIDTH), BF16),
        scratch_shapes=[pltpu.VMEM((seq + 2 * POOL_PAD, POOL_GROUP), F32)],
        compiler_params=_cparams(("arbitrary",)),
        name="pool_mixer",
    )(u_lat, col_mats, w_pool, pool_scale)


def _first_index(hit, iota, size, axis):
    return jnp.min(jnp.where(hit, iota, size), axis=axis, keepdims=True)


def _mix_body(yg_ref, yp_ref, x_ref, gt1_ref, sc2_ref, sh2_ref, gpost_ref, gpre_ref, wout_ref,
              wr_ref, rb_ref, upper_ref,
              x1_ref, hp_ref, eidx_ref, pos_ref, wts_ref, cnt_ref, run_ref, wrow_ref, y_scr):
    tm = x_ref.shape[0]
    neg_inf = jnp.float32(-jnp.inf)
    step = pl.program_id(0)

    @pl.when(step == 0)
    def _():
        run_ref[...] = jnp.zeros_like(run_ref)
        wrow_ref[...] = jnp.zeros_like(wrow_ref)
        y_scr[...] = jnp.zeros_like(y_scr)

    y = y_scr[...]
    y_new = jnp.dot(yg_ref[...], wout_ref[0:GLA_WIDTH, :], preferred_element_type=F32)
    y_scr[...] = y_new + jnp.dot(yp_ref[...], wout_ref[GLA_WIDTH:, :], preferred_element_type=F32)
    x1 = x_ref[...] + gt1_ref[0] * (y * _rms_scale(y) * gpost_ref[...])
    x1_ref[...] = x1
    h = x1 * _rms_scale(x1) * gpre_ref[...]
    h = h * (1.0 + sc2_ref[0]) + sh2_ref[0]
    _store_row_tiles(hp_ref, _pack_halves(h[:, :HALF], h[:, HALF:]))

    h_hi, h_lo = _bf16_terms(h)
    both = jnp.dot(h_hi, wr_ref[...], preferred_element_type=F32)
    lt = both[:, :LANES] + both[:, LANES:] + jnp.dot(h_lo, wr_ref[:, :LANES], preferred_element_type=F32)
    logits = lt.T[0:N_EXPERTS, :]
    scores = jax.nn.sigmoid(logits)
    sel = scores + rb_ref[...]
    shape3 = (N_GROUPS, GROUP_SIZE, tm)
    sel3 = sel.reshape(shape3)
    i_in = lax.broadcasted_iota(I32, shape3, 1).astype(F32)
    m1 = jnp.max(sel3, axis=1, keepdims=True)
    f1 = _first_index(sel3 == m1, i_in, float(GROUP_SIZE), 1)
    m2 = jnp.max(jnp.where(i_in == f1, neg_inf, sel3), axis=1, keepdims=True)
    grp = jnp.broadcast_to(m1 + m2, shape3).reshape(N_EXPERTS, tm)
    i_e = lax.broadcasted_iota(I32, (N_EXPERTS, tm), 0)
    i_grp = (i_e // GROUP_SIZE).astype(F32)
    i_e = i_e.astype(F32)
    allowed = jnp.zeros((N_EXPERTS, tm), F32)
    for _ in range(TOPK_GROUPS):
        m = jnp.max(grp, axis=0, keepdims=True)
        pick = i_grp == _first_index(grp == m, i_grp, float(N_GROUPS), 0)
        allowed = jnp.where(pick, 1.0, allowed)
        grp = jnp.where(pick, neg_inf, grp)
    cand = jnp.where(allowed > 0.0, sel, neg_inf)
    onehot = jnp.zeros((N_EXPERTS, tm), F32)
    picks, wts = [], []
    for k in range(TOP_K):
        m = jnp.max(cand, axis=0, keepdims=True)
        f = _first_index(cand == m, i_e, float(N_EXPERTS), 0)
        pick = i_e == f
        picks.append(pick)
        eidx_ref[k:k + 1, :] = f.astype(I32)
        wts.append(jnp.sum(jnp.where(pick, scores, 0.0), axis=0, keepdims=True))
        onehot = jnp.where(pick, 1.0, onehot)
        cand = jnp.where(pick, neg_inf, cand)
    w_sum = wts[0]
    for k in range(1, TOP_K):
        w_sum = w_sum + wts[k]
    for k in range(TOP_K):
        wrow_ref[k:k + 1, :] = wts[k] / w_sum * ROUTED_SCALE
    wts_ref[...] = wrow_ref[...].T

    before = jnp.dot(onehot.astype(BF16), upper_ref[...], preferred_element_type=F32)
    before = before + run_ref[:, 0:1]
    for k in range(TOP_K):
        pos_ref[k:k + 1, :] = jnp.sum(jnp.where(picks[k], before, 0.0), axis=0, keepdims=True).astype(I32)
    counted = jnp.where(step > 0, 1.0, 0.0)
    run_ref[...] = run_ref[...] + counted * jnp.sum(onehot, axis=1, keepdims=True)
    cnt_ref[...] = run_ref[...].astype(I32)


def _mix_and_route(y_gla, y_pool, x2d, gt1, sc2, sh2, g_post, g_pre, w_out, w_router2, router_bias, seq,
                   part, n_parts):
    d = x2d.shape[1]
    rows = x2d.shape[0] // n_parts
    tiles = rows // MIX_TM
    first = part * tiles
    tiles_per_b = seq // MIX_TM
    proj = lambda i: (first + jnp.minimum(i, tiles - 1), 0)
    bmap = lambda i: ((first + jnp.maximum(i - 1, 0)) // tiles_per_b, 0, 0)
    xmap = lambda i: (first + jnp.maximum(i - 1, 0), 0)
    rmap = lambda i: (jnp.maximum(i - 1, 0), 0)
    tmap = lambda i: (0, jnp.maximum(i - 1, 0))
    upper = jnp.asarray(np.triu(np.ones((MIX_TM, MIX_TM), np.float32), 1), dtype=BF16)
    return pl.pallas_call(
        _mix_body,
        grid=(tiles + 1,),
        in_specs=[
            pl.BlockSpec((MIX_TM, GLA_WIDTH), proj),
            pl.BlockSpec((MIX_TM, POOL_WIDTH), proj),
            pl.BlockSpec((MIX_TM, d), xmap),
            pl.BlockSpec((1, 1, d), bmap),
            pl.BlockSpec((1, 1, d), bmap),
            pl.BlockSpec((1, 1, d), bmap),
            _resident((1, d)),
            _resident((1, d)),
            _resident((d, d)),
            _resident((d, 256)),
            _resident((N_EXPERTS, 1)),
            _resident((MIX_TM, MIX_TM)),
        ],
        out_specs=[
            pl.BlockSpec((MIX_TM, d), rmap),
            pl.BlockSpec((MIX_TM * ROW_SUBLANES, LANES), rmap),
            pl.BlockSpec((TOP_K, MIX_TM), tmap),
            pl.BlockSpec((TOP_K, MIX_TM), tmap),
            pl.BlockSpec((MIX_TM, LANES), rmap),
            pl.BlockSpec((N_EXPERTS, LANES), lambda i: (0, 0)),
        ],
        out_shape=[
            jax.ShapeDtypeStruct((rows, d), F32),
            jax.ShapeDtypeStruct((rows * ROW_SUBLANES, LANES), U32),
            jax.ShapeDtypeStruct((TOP_K, rows), I32),
            jax.ShapeDtypeStruct((TOP_K, rows), I32),
            jax.ShapeDtypeStruct((rows, LANES), F32),
            jax.ShapeDtypeStruct((N_EXPERTS, LANES), I32),
        ],
        scratch_shapes=[pltpu.VMEM((N_EXPERTS, LANES), F32), pltpu.VMEM((LANES, MIX_TM), F32),
                        pltpu.VMEM((MIX_TM, d), F32)],
        compiler_params=_cparams(("arbitrary",)),
        name="mix_and_route",
    )(y_gla, y_pool, x2d, gt1, sc2, sh2, g_post, g_pre, w_out, w_router2, router_bias, upper)


def _swiglu(lo, hi, wg_ref, wu_ref, wd_ref):
    g = jnp.dot(lo, wg_ref[:HALF, :], preferred_element_type=F32)
    g = g + jnp.dot(hi, wg_ref[HALF:, :], preferred_element_type=F32)
    u = jnp.dot(lo, wu_ref[:HALF, :], preferred_element_type=F32)
    u = u + jnp.dot(hi, wu_ref[HALF:, :], preferred_element_type=F32)
    act = (_silu(g) * u).astype(BF16)
    return jnp.dot(act, wd_ref[...], preferred_element_type=F32)


def _offsets_body(ps_ref, e_ref, p_ref, d_ref):
    e = e_ref[...]
    d = p_ref[...]
    for x in range(N_EXPERTS):
        d = d + jnp.where(e == x, ps_ref[x], 0)
    d_ref[...] = d


def _route_offsets(eidx_t, pos_t, pstarts):
    k, rows = eidx_t.shape
    spec = pl.BlockSpec((k, OFFS_TN), lambda i, ps: (0, i))
    grid_spec = pltpu.PrefetchScalarGridSpec(
        num_scalar_prefetch=1, grid=(rows // OFFS_TN,), in_specs=[spec, spec], out_specs=spec)
    return pl.pallas_call(
        _offsets_body,
        grid_spec=grid_spec,
        out_shape=jax.ShapeDtypeStruct((k, rows), I32),
        compiler_params=_cparams(("arbitrary",)),
        name="route_offsets",
    )(pstarts, eidx_t, pos_t)


def _expert_body(be_ref, nu_ref, valid_ref, nxt_ref, slot_ref, x_ref, h_ref, wg_hbm, wu_hbm, wd_hbm,
                 sg_hbm, su_hbm, sd_hbm, y_ref, wg_f, wu_f, wd_f, wg_b, wu_b, wd_b, sems, *, n_routed):
    i = pl.program_id(0)
    valid = valid_ref[i]
    expert = be_ref[i]
    slot = slot_ref[i]
    dsts = (wg_f, wu_f, wd_f)

    def start_weights(e, s):
        @pl.when(e < N_EXPERTS)
        def _():
            for j, (src, dst) in enumerate(zip((wg_hbm, wu_hbm, wd_hbm), dsts)):
                pltpu.make_async_copy(src.at[e], dst.at[s], sems.at[s, j]).start()

        @pl.when(e == N_EXPERTS)
        def _():
            for j, (src, dst) in enumerate(zip((sg_hbm, su_hbm, sd_hbm), dsts)):
                pltpu.make_async_copy(src.at[0], dst.at[s], sems.at[s, j]).start()

    def wait_weights(s):
        for j, (src, dst) in enumerate(zip((wg_hbm, wu_hbm, wd_hbm), dsts)):
            pltpu.make_async_copy(src.at[0], dst.at[s], sems.at[s, j]).wait()

    @pl.when(i == 0)
    def _():
        start_weights(expert, slot)

    @pl.when((i == 0) | (expert != be_ref[jnp.maximum(i - 1, 0)]))
    def _():
        wait_weights(slot)

        @pl.when(nxt_ref[i] != expert)
        def _():
            start_weights(nxt_ref[i], 1 - slot)

        wg_b[...] = wg_f[slot].astype(BF16)
        wu_b[...] = wu_f[slot].astype(BF16)
        wd_b[...] = wd_f[slot].astype(BF16)

    def compute(rows_ref, n_rows):
        tiles = pl.ds(0, n_rows * ROW_SUBLANES)
        lo, hi = _unpack_halves(_load_row_tiles(rows_ref.at[tiles]))
        y = _swiglu(lo.astype(BF16), hi.astype(BF16), wg_b, wu_b, wd_b)
        _store_row_tiles(y_ref.at[tiles], _pack_halves(y[:, :HALF], y[:, HALF:]))

    quarter = MOE_TILE // 4
    for n in range(1, 5):
        @pl.when((i < n_routed) & (valid > (n - 1) * quarter) & (valid <= n * quarter))
        def _(n=n):
            compute(x_ref, n * quarter)

    @pl.when(i >= n_routed)
    def _():
        compute(h_ref, MOE_TILE)


def _experts(xs, h_packed, block_expert, n_used, valid, next_expert, slot, w_eg, w_eu, w_ed, w_sg, w_su, w_sd):
    n_rows = xs.shape[0] // ROW_SUBLANES
    n_routed = n_rows // MOE_TILE
    n_shared = h_packed.shape[0] // ROW_SUBLANES // MOE_TILE
    d, de = w_eg.shape[1], w_eg.shape[2]
    block = (MOE_TILE * ROW_SUBLANES, LANES)
    routed_map = lambda i, be, nu, *_: (jnp.minimum(i, nu[0] - 1), 0)
    shared_map = lambda i, *_: (jnp.maximum(i - n_routed, 0), 0)
    out_map = lambda i, be, nu, *_: (jnp.where(i < n_routed, jnp.minimum(i, nu[0] - 1), i), 0)
    grid_spec = pltpu.PrefetchScalarGridSpec(
        num_scalar_prefetch=5,
        grid=(n_routed + n_shared,),
        in_specs=[pl.BlockSpec(block, routed_map), pl.BlockSpec(block, shared_map)]
        + [pl.BlockSpec(memory_space=pl.ANY)] * 6,
        out_specs=pl.BlockSpec(block, out_map),
        scratch_shapes=[
            pltpu.VMEM((2, d, de), F32),
            pltpu.VMEM((2, d, de), F32),
            pltpu.VMEM((2, de, d), F32),
            pltpu.VMEM((d, de), BF16),
            pltpu.VMEM((d, de), BF16),
            pltpu.VMEM((de, d), BF16),
            pltpu.SemaphoreType.DMA((2, 3)),
        ],
    )
    return pl.pallas_call(
        functools.partial(_expert_body, n_routed=n_routed),
        grid_spec=grid_spec,
        out_shape=jax.ShapeDtypeStruct(((n_routed + n_shared) * MOE_TILE * ROW_SUBLANES, LANES), U32),
        compiler_params=_cparams(("arbitrary",)),
        name="experts",
    )(block_expert, n_used, valid, next_expert, slot, xs, h_packed, w_eg, w_eu, w_ed, w_sg, w_su, w_sd)


def _sc_mesh():
    return plsc.VectorSubcoreMesh(core_axis_name="c", subcore_axis_name="s")


def _sc_worker():
    return lax.axis_index("s") * SC_CORES + lax.axis_index("c")


def _sc_gather_rows(table, idx):
    n_idx = idx.shape[0]
    per_worker = n_idx // SC_WORKERS
    n_chunks = per_worker // SC_CHUNK
    assert per_worker * SC_WORKERS == n_idx and n_chunks * SC_CHUNK == per_worker and n_chunks % 2 == 0
    row_shape = table.shape[1:]

    @functools.partial(
        pl.kernel, mesh=_sc_mesh(),
        out_type=jax.ShapeDtypeStruct((n_idx,) + row_shape, table.dtype),
        scratch_types=[
            pltpu.VMEM((per_worker,), I32),
            pltpu.VMEM((SC_CHUNK,) + row_shape, table.dtype),
            pltpu.VMEM((SC_CHUNK,) + row_shape, table.dtype),
        ] + [pltpu.SemaphoreType.DMA] * 4,
    )
    def gather(table_hbm, idx_hbm, out_hbm, idx_v, buf0, buf1, g0, g1, w0, w1):
        bufs, gsem, wsem = (buf0, buf1), (g0, g1), (w0, w1)
        base = _sc_worker() * per_worker
        pltpu.sync_copy(idx_hbm.at[pl.ds(base, per_worker)], idx_v)

        def fetch(j, b):
            return pltpu.make_async_copy(table_hbm.at[idx_v.at[pl.ds(j * SC_CHUNK, SC_CHUNK)]], bufs[b], gsem[b])

        def flush(j, b):
            return pltpu.make_async_copy(bufs[b], out_hbm.at[pl.ds(base + j * SC_CHUNK, SC_CHUNK)], wsem[b])

        fetch(0, 0).start()
        fetch(0, 0).wait()
        fetch(1, 1).start()
        flush(0, 0).start()

        @pl.loop(1, n_chunks - 1, step=2)
        def _(j):
            for off in range(2):
                jj, b = j + off, (1 + off) % 2
                fetch(jj, b).wait()
                flush(jj - 1, 1 - b).wait()
                fetch(jj + 1, 1 - b).start()
                flush(jj, b).start()

        fetch(n_chunks - 1, 1).wait()
        flush(n_chunks - 1, 1).start()
        flush(n_chunks - 2, 0).wait()
        flush(n_chunks - 1, 1).wait()

    return gather(table, idx)


def _sc_scatter_rows(rows, dest_t, n_out):
    n_rows = rows.shape[0]
    per_worker = n_rows // SC_WORKERS
    n_chunks = per_worker // SC_CHUNK
    assert per_worker * SC_WORKERS == n_rows and n_chunks * SC_CHUNK == per_worker
    row_shape = rows.shape[1:]
    idx_w = dest_t.reshape(TOP_K, SC_WORKERS, n_chunks, SC_CHUNK).transpose(1, 2, 0, 3)
    idx_w = idx_w.reshape(SC_WORKERS, n_chunks * TOP_K, SC_CHUNK)

    @functools.partial(
        pl.kernel, mesh=_sc_mesh(),
        out_type=jax.ShapeDtypeStruct((n_out,) + row_shape, rows.dtype),
        scratch_types=[
            pltpu.VMEM((n_chunks * TOP_K, SC_CHUNK), I32),
            pltpu.VMEM((SC_CHUNK,) + row_shape, rows.dtype),
            pltpu.VMEM((SC_CHUNK,) + row_shape, rows.dtype),
        ] + [pltpu.SemaphoreType.DMA] * 4,
    )
    def scatter(rows_hbm, idx_hbm, out_hbm, idx_v, buf0, buf1, r0, r1, s0, s1):
        bufs, rsem, ssem = (buf0, buf1), (r0, r1), (s0, s1)
        worker = _sc_worker()
        base = worker * per_worker
        pltpu.sync_copy(idx_hbm.at[worker], idx_v)

        def fetch(j, b):
            return pltpu.make_async_copy(rows_hbm.at[pl.ds(base + j * SC_CHUNK, SC_CHUNK)], bufs[b], rsem[b])

        def send(j, k, b):
            return pltpu.make_async_copy(bufs[b], out_hbm.at[idx_v.at[j * TOP_K + k]], ssem[b])

        fetch(0, 0).start()
        for j in range(n_chunks):
            b = j % 2
            fetch(j, b).wait()
            if j + 1 < n_chunks:
                if j >= 1:
                    for k in range(TOP_K):
                        send(j - 1, k, 1 - b).wait()
                fetch(j + 1, 1 - b).start()
            for k in range(TOP_K):
                send(j, k, b).start()
        for j in range(max(n_chunks - 2, 0), n_chunks):
            for k in range(TOP_K):
                send(j, k, j % 2).wait()

    return scatter(rows, idx_w)


def _combine_body(*refs):
    yk_refs = refs[:TOP_K]
    w_ref, shr_ref, x1_ref, gt2_ref, gpost_ref = refs[TOP_K:TOP_K + 5]
    o_ref = refs[-1]
    tt = x1_ref.shape[0]
    w = w_ref[...]
    ssq = jnp.zeros((tt, 1), F32)
    for c in range(HALF // LANES):
        c_lo = slice(c * LANES, (c + 1) * LANES)
        c_hi = slice(HALF + c * LANES, HALF + (c + 1) * LANES)
        y_lo, y_hi = _unpack_halves(shr_ref[pl.ds(c, tt, stride=ROW_SUBLANES), :])
        for k in range(TOP_K):
            lo, hi = _unpack_halves(yk_refs[k][pl.ds(c, tt, stride=ROW_SUBLANES), :])
            y_lo = y_lo + w[:, k:k + 1] * lo
            y_hi = y_hi + w[:, k:k + 1] * hi
        ssq = ssq + jnp.sum(y_lo * y_lo, axis=-1, keepdims=True) + jnp.sum(y_hi * y_hi, axis=-1, keepdims=True)
        o_ref[:, c_lo] = y_lo
        o_ref[:, c_hi] = y_hi
    scale = lax.rsqrt(ssq / D_MODEL + EPS)
    o_ref[...] = x1_ref[...] + gt2_ref[0] * (o_ref[...] * scale * gpost_ref[...])


def _combine(yu, wts, ys, shared_block0, x1, gt2, g_post, seq, part, n_parts, prev_out):
    rows, d = x1.shape
    tiles = rows // COMB_TT
    first = part * tiles
    tiles_per_b = seq // COMB_TT
    loc = lambda i: (i, 0)
    yk_specs = [pl.BlockSpec((COMB_TT * ROW_SUBLANES, LANES), functools.partial(lambda i, k: (k * tiles + i, 0), k=k))
                for k in range(TOP_K)]
    in_specs = yk_specs + [
        pl.BlockSpec((COMB_TT, LANES), loc),
        pl.BlockSpec((COMB_TT * ROW_SUBLANES, LANES), lambda i: (shared_block0 + i, 0)),
        pl.BlockSpec((COMB_TT, d), loc),
        pl.BlockSpec((1, 1, d), lambda i: ((first + i) // tiles_per_b, 0, 0)),
        _resident((1, d)),
    ]
    args = [yu] * TOP_K + [wts, ys, x1, gt2, g_post]
    aliases = {}
    if prev_out is not None:
        aliases = {len(args): 0}
        in_specs.append(pl.BlockSpec(memory_space=pl.ANY))
        args.append(prev_out)
    return pl.pallas_call(
        _combine_body,
        grid=(tiles,),
        in_specs=in_specs,
        out_specs=pl.BlockSpec((COMB_TT, d), lambda i: (first + i, 0)),
        out_shape=jax.ShapeDtypeStruct((rows * n_parts, d), F32),
        input_output_aliases=aliases,
        compiler_params=_cparams(("arbitrary",)),
        name="combine",
    )(*args)


def kernel(x, c, ctx, c_ctx, w_mod, b_mod, norm_mix_pre, norm_mix_post, norm_ffn_pre, norm_ffn_post, w_in, w_a2_fwd, b_a_fwd, w_a2_bwd, b_a_bwd, gla_norm, w_pool, pool_scale, w_out, w_router, router_bias, w_exp_gate, w_exp_up, w_exp_down, w_sh_gate, w_sh_up, w_sh_down):
    batch, seq, d = x.shape
    n_ctx = ctx.shape[1]
    assert w_mod.shape[0] == 1 and d == D_MODEL
    assert seq % (2 * SUPER) == 0 and n_ctx % SUPER == 0 and seq % PROJ_TM == 0 and (batch * n_ctx) % PROJ_TM == 0
    rows = batch * seq

    mod_rows = 16
    c_all = jnp.concatenate([c, c_ctx[None, :], jnp.zeros((mod_rows - batch - 1, d), F32)], axis=0)
    mod_all = _modulation(c_all, w_mod[0], b_mod[0][None, :])
    sh1, sc1, gt1, sh2, sc2, gt2 = [m.reshape(batch, 1, d) for m in jnp.split(mod_all[:batch], 6, axis=-1)]
    csh1 = mod_all[batch, 0:d].reshape(1, 1, d)
    csc1 = mod_all[batch, d:2 * d].reshape(1, 1, d)

    kw, gw = GLA_KEY_WIDTH, GLA_WIDTH
    a0 = 2 * kw + 2 * gw
    w_in0 = w_in[0]
    w_bf = w_in0.astype(BF16)
    w_a = jnp.pad(w_bf[:, a0:a0 + 2 * GLA_RANK], ((0, 0), (0, LANES - 2 * GLA_RANK)))
    lat_pieces = [(w_bf, a0, 0), (w_bf[:, a0 + 2 * GLA_RANK:], POOL_WIDTH, 0)]
    ctx_pieces = [(w_bf, gw, 2 * kw // gw), (w_bf, kw, 1)]
    w2f = jnp.pad(w_a2_fwd[0], ((0, LANES - GLA_RANK), (0, 0))).astype(BF16)
    w2b = jnp.pad(w_a2_bwd[0], ((GLA_RANK, LANES - 2 * GLA_RANK), (0, 0))).astype(BF16)
    g_mix_pre = norm_mix_pre[0][None, :]
    w_router2 = jnp.concatenate(_bf16_terms(jnp.pad(w_router[0], ((0, 0), (0, LANES - N_EXPERTS)))), axis=1)

    u_ctx, a_ctx = _in_projection(ctx.reshape(batch * n_ctx, d), g_mix_pre, csc1, csh1, ctx_pieces, w_a,
                                  batch * n_ctx)
    u_lat, a_lat = _in_projection(x.reshape(rows, d), g_mix_pre, sc1, sh1, lat_pieces, w_a, seq)

    y_gla = _gla(u_lat, a_lat, u_ctx, a_ctx, w2f, b_a_fwd[0][None, :], w2b, b_a_bwd[0][None, :],
                 gla_norm[0][None, :], batch, seq, n_ctx)
    y_pool = _pool_mixer(u_lat, _col_window_matrices(), w_pool[0].astype(BF16), pool_scale[0][None, :],
                         batch, seq)

    w_out_bf = w_out[0].astype(BF16)
    x2d = x.reshape(rows, d)
    part_rows = rows // MOE_PARTS
    n_blocks = part_rows * TOP_K // MOE_TILE + N_EXPERTS
    e_ids = jnp.arange(N_EXPERTS, dtype=I32)

    mixed = [_mix_and_route(y_gla, y_pool, x2d, gt1, sc2, sh2, norm_mix_post[0][None, :],
                            norm_ffn_pre[0][None, :], w_out_bf, w_router2, router_bias[0][:, None], seq,
                            p, MOE_PARTS) for p in range(MOE_PARTS)]
    out = None
    for p, (x1, h_packed, eidx_t, pos_t, wts, counts) in enumerate(mixed):
        counts = counts[:, 0]
        padded = (counts + MOE_TILE - 1) // MOE_TILE * MOE_TILE
        pends = jnp.cumsum(padded)
        pstarts = pends - padded
        dest_t = _route_offsets(eidx_t, pos_t, pstarts.astype(I32))
        n_used = (pends[-1] // MOE_TILE).astype(I32)
        blk = jnp.minimum(jnp.arange(n_blocks, dtype=I32), n_used - 1)
        n_shared = part_rows // MOE_TILE
        block_expert = jnp.sum((blk * MOE_TILE)[:, None] >= pends[None, :], axis=1).astype(I32)
        block_expert = jnp.minimum(block_expert, N_EXPERTS - 1)
        is_block_expert = block_expert[:, None] == e_ids[None, :]
        per_block = lambda v: jnp.sum(jnp.where(is_block_expert, v[None, :], 0), axis=1).astype(I32)
        valid = jnp.clip(per_block(pstarts + counts) - blk * MOE_TILE, 0, MOE_TILE)
        valid = jnp.where(jnp.arange(n_blocks, dtype=I32) < n_used, valid, 0).astype(I32)
        has_rows = padded > 0
        later = jnp.where((e_ids[None, :] > e_ids[:, None]) & has_rows[None, :], e_ids[None, :], N_EXPERTS)
        next_e = jnp.min(later, axis=1)
        next_e = jnp.where(next_e == N_EXPERTS, e_ids, next_e)
        next_e = jnp.where(next_e == e_ids, N_EXPERTS, next_e)
        slot_e = (jnp.cumsum(has_rows.astype(I32)) - has_rows.astype(I32)) % 2
        shared_tab = lambda v: jnp.full((n_shared,), v, I32)
        block_expert = jnp.concatenate([block_expert, shared_tab(N_EXPERTS)])
        valid = jnp.concatenate([valid, shared_tab(MOE_TILE)])
        next_b = jnp.concatenate([per_block(next_e), shared_tab(N_EXPERTS)])
        slot_b = jnp.concatenate([per_block(slot_e), shared_tab(jnp.sum(has_rows.astype(I32)) % 2)])

        xs = _sc_scatter_rows(h_packed.reshape(-1, ROW_SUBLANES, LANES), dest_t, n_blocks * MOE_TILE)
        ys = _experts(xs.reshape(-1, LANES), h_packed, block_expert, n_used.reshape(1), valid, next_b, slot_b,
                      w_exp_gate[0], w_exp_up[0], w_exp_down[0], w_sh_gate, w_sh_up, w_sh_down)
        yu = _sc_gather_rows(ys.reshape(-1, ROW_SUBLANES, LANES), dest_t.reshape(-1))
        out = _combine(yu.reshape(-1, LANES), wts, ys, n_blocks * MOE_TILE // COMB_TT, x1, gt2,
                       norm_ffn_post[0][None, :], seq, p, MOE_PARTS, out)
    return out.reshape(batch, seq, d)
```

```python
import functools

import numpy as np
import jax
import jax.numpy as jnp
from jax import lax
from jax.experimental import pallas as pl
from jax.experimental.pallas import tpu as pltpu
from jax.experimental.pallas import tpu_sc as plsc

F32 = jnp.float32
BF16 = jnp.bfloat16
I32 = jnp.int32
U32 = jnp.uint32

D_MODEL = 2048
GRID_W = 64
GLA_HEADS = 4
GLA_DK = 128
GLA_DV = 256
GLA_KEY_WIDTH = GLA_HEADS * GLA_DK
GLA_WIDTH = GLA_HEADS * GLA_DV
GLA_RANK = 16
GLA_TAU = 16.0
GLA_CHUNK = 64
POOL_WIDTH = 1024
POOL_WINDOWS = (2, 4, 8, 16)
POOL_GROUP = 256
N_EXPERTS = 64
TOP_K = 8
N_GROUPS = 8
GROUP_SIZE = N_EXPERTS // N_GROUPS
TOPK_GROUPS = 4
ROUTED_SCALE = 2.5
EPS = 1e-6

HALF = D_MODEL // 2
SUPER = 4 * GLA_CHUNK
GLA_HPS = 4
POOL_PAD = 8 * GRID_W
VMEM_LIMIT = 56 * 1024 * 1024

MOD_TN = 1024
PROJ_TM = 512
PROJ_TN = 512
MIX_TM = 512
MOE_TILE = 512
X_RING = 3
SHARED_TM = 1024
COMB_TT = 256
MOE_PARTS = 2
LANES = 128
ROW_SUBLANES = 8
SC_CORES = 2
SC_WORKERS = 32
SC_CHUNK = 32
OFFS_TN = 2048


def _cparams(sem):
    return pltpu.CompilerParams(dimension_semantics=sem, vmem_limit_bytes=VMEM_LIMIT)


def _resident(shape):
    nd = len(shape)
    return pl.BlockSpec(shape, lambda *_: (0,) * nd, pipeline_mode=pl.Buffered(1))


def _silu(v):
    return v * jax.nn.sigmoid(v)


def _pack_halves(lo, hi):
    lo_b = lax.bitcast_convert_type(lo.astype(BF16).astype(F32), U32)
    hi_b = lax.bitcast_convert_type(hi.astype(BF16).astype(F32), U32)
    return (hi_b & jnp.uint32(0xFFFF0000)) | (lo_b >> 16)


def _unpack_halves(p):
    lo = lax.bitcast_convert_type(p << 16, F32)
    hi = lax.bitcast_convert_type(p & jnp.uint32(0xFFFF0000), F32)
    return lo, hi


def _bf16_terms(x):
    hi = lax.bitcast_convert_type(lax.bitcast_convert_type(x, U32) & jnp.uint32(0xFFFF0000), F32)
    return hi.astype(BF16), (x - hi).astype(BF16)


def _store_row_tiles(ref, packed):
    n = packed.shape[0]
    for c in range(HALF // LANES):
        ref[pl.ds(c, n, stride=ROW_SUBLANES), :] = packed[:, c * LANES:(c + 1) * LANES]


def _load_row_tiles(ref):
    n = ref.shape[0] // ROW_SUBLANES
    return jnp.concatenate([ref[pl.ds(c, n, stride=ROW_SUBLANES), :] for c in range(HALF // LANES)], axis=1)


def _mod_body(c_ref, w_ref, b_ref, o_ref):
    s_hi, s_lo = _bf16_terms(_silu(c_ref[...]))
    w_hi, w_lo = _bf16_terms(w_ref[...])
    acc = jnp.dot(s_hi, w_hi, preferred_element_type=F32)
    acc = acc + jnp.dot(s_lo, w_hi, preferred_element_type=F32)
    acc = acc + jnp.dot(s_hi, w_lo, preferred_element_type=F32)
    o_ref[...] = acc + b_ref[...]


def _modulation(c_all, w_mod, b_mod):
    rows, d = c_all.shape
    n = w_mod.shape[1]
    return pl.pallas_call(
        _mod_body,
        grid=(n // MOD_TN,),
        in_specs=[
            pl.BlockSpec((rows, d), lambda j: (0, 0)),
            pl.BlockSpec((d, MOD_TN), lambda j: (0, j)),
            pl.BlockSpec((1, MOD_TN), lambda j: (0, j)),
        ],
        out_specs=pl.BlockSpec((rows, MOD_TN), lambda j: (0, j)),
        out_shape=jax.ShapeDtypeStruct((rows, n), F32),
        compiler_params=_cparams(("arbitrary",)),
        name="modulation",
    )(c_all, w_mod, b_mod)


def _rms_scale(x):
    return lax.rsqrt(jnp.mean(x * x, axis=-1, keepdims=True) + EPS)


def _inproj_body(x_ref, g_ref, sc_ref, sh_ref, *refs):
    w_refs, (wa_ref, o_ref, a_ref) = refs[:-3], refs[-3:]
    x = x_ref[...]
    h = x * _rms_scale(x) * g_ref[...]
    h = h * (1.0 + sc_ref[0]) + sh_ref[0]
    hb = h.astype(BF16)
    col = 0
    for w_ref in w_refs:
        for n in range(w_ref.shape[1] // PROJ_TN):
            cols = slice(n * PROJ_TN, (n + 1) * PROJ_TN)
            o_ref[:, col:col + PROJ_TN] = jnp.dot(hb, w_ref[:, cols], preferred_element_type=F32).astype(BF16)
            col += PROJ_TN
    a_ref[...] = jnp.dot(hb, wa_ref[...], preferred_element_type=F32)


def _in_projection(x2d, gain, sc, sh, pieces, w_a, rows_per_mod):
    rows, d = x2d.shape
    n_main = sum(width for _, width, _ in pieces)
    tiles_per_mod = rows_per_mod // PROJ_TM
    mod_map = lambda i: (i // tiles_per_mod, 0, 0)
    piece_specs = [pl.BlockSpec((d, width), functools.partial(lambda i, b: (0, b), b=block),
                                pipeline_mode=pl.Buffered(1)) for _, width, block in pieces]
    return pl.pallas_call(
        _inproj_body,
        grid=(rows // PROJ_TM,),
        in_specs=[
            pl.BlockSpec((PROJ_TM, d), lambda i: (i, 0)),
            _resident((1, d)),
            pl.BlockSpec((1, 1, d), mod_map),
            pl.BlockSpec((1, 1, d), mod_map),
            *piece_specs,
            _resident((d, LANES)),
        ],
        out_specs=[
            pl.BlockSpec((PROJ_TM, n_main), lambda i: (i, 0)),
            pl.BlockSpec((PROJ_TM, LANES), lambda i: (i, 0)),
        ],
        out_shape=[
            jax.ShapeDtypeStruct((rows, n_main), BF16),
            jax.ShapeDtypeStruct((rows, LANES), F32),
        ],
        compiler_params=_cparams(("arbitrary",)),
        name="in_projection",
    )(x2d, gain, sc, sh, *[w for w, _, _ in pieces], w_a)


def _log_sigmoid(z):
    return jnp.minimum(z, 0.0) - jnp.log1p(jnp.exp(-jnp.abs(z)))


def _gla_cumulative_decay(a, w2, ba, tri):
    z = jnp.dot(a.astype(BF16), w2, preferred_element_type=F32) + ba
    g = _log_sigmoid(z) * (1.0 / GLA_TAU)
    g_hi, g_lo = _bf16_terms(g)
    return jnp.dot(tri, g_hi, preferred_element_type=F32) + jnp.dot(tri, g_lo, preferred_element_type=F32)


def _gla_prep(q, k, G, reverse):
    nc = SUPER // GLA_CHUNK
    G = G.reshape(nc, GLA_CHUNK, GLA_DK)
    end_row = 0 if reverse else GLA_CHUNK - 1
    mid_row = GLA_CHUNK - 1 - GLA_CHUNK // 2 if reverse else GLA_CHUNK // 2
    g_end = G[:, end_row:end_row + 1, :]
    g_mid = G[:, mid_row:mid_row + 1, :]
    k4 = k.astype(F32).reshape(nc, GLA_CHUNK, GLA_DK)
    dec = jnp.broadcast_to(jnp.exp(g_end), (nc, ROW_SUBLANES, GLA_DK)).reshape(nc * ROW_SUBLANES, GLA_DK)
    flat = lambda t: t.reshape(SUPER, GLA_DK).astype(BF16)
    if q is None:
        return None, None, None, flat(k4 * jnp.exp(g_end - G)), dec
    q4 = q.astype(F32).reshape(nc, GLA_CHUNK, GLA_DK) * (GLA_DK ** -0.5)
    qg = q4 * jnp.exp(G - g_mid)
    kg = k4 * jnp.exp(g_mid - G)
    qe = qg * jnp.exp(g_mid)
    kd = kg * jnp.exp(g_end - g_mid)
    return flat(qg), flat(kg), flat(qe), flat(kd), dec


def _gla_apply(qg, kg, qe, kd, dec, v, mask, st_ref, reverse):
    nc = SUPER // GLA_CHUNK
    o = None
    if qg is not None:
        att = lax.dot_general(qg, kg, (((1,), (1,)), ((), ())), preferred_element_type=F32)
        att = jnp.where(mask, att, 0.0).astype(BF16)
        o = jnp.dot(att, v, preferred_element_type=F32)
    outs = [None] * nc
    order = range(nc - 1, -1, -1) if reverse else range(nc)
    for c in order:
        rows = slice(c * GLA_CHUNK, (c + 1) * GLA_CHUNK)
        st = st_ref[...]
        if qg is not None:
            inter = lax.dot_general(qe[rows], st.astype(BF16), (((1,), (1,)), ((), ())),
                                    preferred_element_type=F32)
            outs[c] = o[rows] + inter
        upd = lax.dot_general(v[rows], kd[rows], (((0,), (0,)), ((), ())), preferred_element_type=F32)
        st_ref[...] = st * dec[ROW_SUBLANES * c:ROW_SUBLANES * c + 1, :] + upd
    if qg is None:
        return None
    return jnp.concatenate(outs, axis=0)


def _gla_body(q_ref, k_ref, v_ref, r_ref, a_ref, kc_ref, vc_ref, ac_ref,
              w2f_ref, baf_ref, w2b_ref, bab_ref, gn_ref, y_ref, o_acc, st, ops_a, ops_b, dec_a, dec_b,
              *, n_ctx):
    n_sup = q_ref.shape[0] // SUPER
    row = lax.broadcasted_iota(I32, (SUPER, SUPER), 0)
    col = lax.broadcasted_iota(I32, (SUPER, SUPER), 1)
    same_chunk = (row // GLA_CHUNK) == (col // GLA_CHUNK)
    mask_f = same_chunk & (col <= row)
    mask_b = same_chunk & (col >= row)
    tri_f = jnp.where(mask_f, 1.0, 0.0).astype(BF16)
    tri_b = jnp.where(mask_b, 1.0, 0.0).astype(BF16)
    heads = range(GLA_HPS)
    kcol = [slice(h * GLA_DK, (h + 1) * GLA_DK) for h in heads]
    vcol = [slice(h * GLA_DV, (h + 1) * GLA_DV) for h in heads]
    dirs = ((False, w2f_ref, baf_ref, tri_f, mask_f), (True, w2b_ref, bab_ref, tri_b, mask_b))

    st[...] = jnp.zeros_like(st)
    n_csup = n_ctx // SUPER
    for s in range(n_csup):
        for d, (reverse, w2_ref, ba_ref, tri, mask) in enumerate(dirs):
            sc = n_csup - 1 - s if reverse else s
            rows = slice(sc * SUPER, (sc + 1) * SUPER)
            G = _gla_cumulative_decay(ac_ref[rows, :], w2_ref[...], ba_ref[...], tri)
            for h in heads:
                _, _, _, kd, dec = _gla_prep(None, kc_ref[rows, kcol[h]], G[:, kcol[h]], reverse)
                _gla_apply(None, None, None, kd, dec, vc_ref[rows, vcol[h]], mask, st.at[d, h], reverse)

    o_acc[...] = jnp.zeros_like(o_acc)

    def rows_of(i, reverse):
        sc = n_sup - 1 - i if reverse else i
        return pl.ds(pl.multiple_of(sc * SUPER, SUPER), SUPER)

    def prepare(i, ops, decs, d):
        reverse, w2_ref, ba_ref, tri, mask = dirs[d]
        rows = rows_of(i, reverse)
        G = _gla_cumulative_decay(a_ref[rows, :], w2_ref[...], ba_ref[...], tri)
        for h in heads:
            vals = _gla_prep(q_ref[rows, kcol[h]], k_ref[rows, kcol[h]], G[:, kcol[h]], reverse)
            for j in range(4):
                ops[GLA_HPS * d + h, j] = vals[j]
            decs[GLA_HPS * d + h] = vals[4]

    def apply(i, ops, decs, d):
        reverse, w2_ref, ba_ref, tri, mask = dirs[d]
        rows = rows_of(i, reverse)
        for h in heads:
            ci = GLA_HPS * d + h
            out = _gla_apply(ops[ci, 0], ops[ci, 1], ops[ci, 2], ops[ci, 3], decs[ci],
                             v_ref[rows, vcol[h]], mask, st.at[d, h], reverse)
            o_acc[rows, vcol[h]] += out

    for d in range(2):
        prepare(0, ops_a, dec_a, d)

    def step(j, carry):
        i = 2 * j
        for d in range(2):
            prepare(i + 1, ops_b, dec_b, d)
            apply(i, ops_a, dec_a, d)
        nxt = jnp.minimum(i + 2, n_sup - 1)
        for d in range(2):
            prepare(nxt, ops_a, dec_a, d)
            apply(i + 1, ops_b, dec_b, d)
        return carry

    lax.fori_loop(0, n_sup // 2, step, 0)

    for h in heads:
        o = o_acc[:, vcol[h]]
        o = o * _rms_scale(o) * gn_ref[:, vcol[h]]
        y_ref[:, vcol[h]] = (o * _silu(r_ref[:, vcol[h]].astype(F32))).astype(BF16)


def _gla(u_lat, a_lat, u_ctx, a_ctx, w2f, baf, w2b, bab, gla_norm, batch, seq, n_ctx):
    groups = GLA_HEADS // GLA_HPS
    kw, vw = GLA_HPS * GLA_DK, GLA_HPS * GLA_DV
    kb = GLA_KEY_WIDTH // kw
    vb = 2 * GLA_KEY_WIDTH // vw
    rb = vb + groups
    assert GLA_KEY_WIDTH % kw == 0 and (2 * GLA_KEY_WIDTH) % vw == 0 and GLA_WIDTH % kw == 0
    ckb = GLA_WIDTH // kw
    return pl.pallas_call(
        functools.partial(_gla_body, n_ctx=n_ctx),
        grid=(batch, groups),
        in_specs=[
            pl.BlockSpec((seq, kw), lambda b, h: (b, h)),
            pl.BlockSpec((seq, kw), lambda b, h: (b, kb + h)),
            pl.BlockSpec((seq, vw), lambda b, h: (b, vb + h)),
            pl.BlockSpec((seq, vw), lambda b, h: (b, rb + h)),
            pl.BlockSpec((seq, LANES), lambda b, h: (b, 0)),
            pl.BlockSpec((n_ctx, kw), lambda b, h: (b, ckb + h)),
            pl.BlockSpec((n_ctx, vw), lambda b, h: (b, h)),
            pl.BlockSpec((n_ctx, LANES), lambda b, h: (b, 0)),
            pl.BlockSpec((LANES, kw), lambda b, h: (0, h)),
            pl.BlockSpec((1, kw), lambda b, h: (0, h)),
            pl.BlockSpec((LANES, kw), lambda b, h: (0, h)),
            pl.BlockSpec((1, kw), lambda b, h: (0, h)),
            pl.BlockSpec((1, vw), lambda b, h: (0, h)),
        ],
        out_specs=pl.BlockSpec((seq, vw), lambda b, h: (b, h)),
        out_shape=jax.ShapeDtypeStruct((batch * seq, GLA_WIDTH), BF16),
        scratch_shapes=[
            pltpu.VMEM((seq, vw), F32),
            pltpu.VMEM((2, GLA_HPS, GLA_DV, GLA_DK), F32),
            pltpu.VMEM((2 * GLA_HPS, 4, SUPER, GLA_DK), BF16),
            pltpu.VMEM((2 * GLA_HPS, 4, SUPER, GLA_DK), BF16),
            pltpu.VMEM((2 * GLA_HPS, ROW_SUBLANES * (SUPER // GLA_CHUNK), GLA_DK), F32),
            pltpu.VMEM((2 * GLA_HPS, ROW_SUBLANES * (SUPER // GLA_CHUNK), GLA_DK), F32),
        ],
        compiler_params=_cparams(("arbitrary", "arbitrary")),
        name="gla",
    )(u_lat, u_lat, u_lat, u_lat, a_lat, u_ctx, u_ctx, a_ctx, w2f, baf, w2b, bab, gla_norm)


def _col_window_matrices():
    t = np.arange(SUPER)
    r, c = t // GRID_W, t % GRID_W
    mats = []
    for w in POOL_WINDOWS:
        lo = np.maximum(c - w // 2, 0)[:, None]
        hi = np.minimum(c + w // 2, GRID_W)[:, None]
        m = (r[:, None] == r[None, :]) & (c[None, :] >= lo) & (c[None, :] < hi)
        mats.append(m.astype(np.float32))
    return jnp.asarray(np.stack(mats), dtype=BF16)


def _pool_body(p_ref, cw_ref, wp_ref, ps_ref, y_ref, pad_ref):
    seq = p_ref.shape[0]
    n_rows = seq // GRID_W
    zeros = jnp.zeros((POOL_PAD, POOL_GROUP), F32)
    pad_ref[0:POOL_PAD, :] = zeros
    pad_ref[POOL_PAD + seq:POOL_PAD + seq + POOL_PAD, :] = zeros
    t = lax.broadcasted_iota(I32, (seq, POOL_GROUP), 0)
    r = t // GRID_W
    c = t % GRID_W
    for gi, w in enumerate(POOL_WINDOWS):
        cols = slice(gi * POOL_GROUP, (gi + 1) * POOL_GROUP)
        cw = cw_ref[gi]
        for j in range(seq // SUPER):
            rows = slice(j * SUPER, (j + 1) * SUPER)
            pad_ref[POOL_PAD + j * SUPER:POOL_PAD + (j + 1) * SUPER, :] = jnp.dot(
                cw, p_ref[rows, cols], preferred_element_type=F32)
        total = None
        for d in range(-(w // 2), w // 2):
            start = POOL_PAD + d * GRID_W
            part = pad_ref[start:start + seq, :]
            total = part if total is None else total + part
        cnt_r = jnp.minimum(r + w // 2, n_rows) - jnp.maximum(r - w // 2, 0)
        cnt_c = jnp.minimum(c + w // 2, GRID_W) - jnp.maximum(c - w // 2, 0)
        mean = total / (cnt_r * cnt_c).astype(F32)
        diff = (mean - p_ref[:, cols].astype(F32)).astype(BF16)
        y = jnp.dot(diff, wp_ref[gi], preferred_element_type=F32) * ps_ref[:, cols]
        y_ref[:, cols] = y.astype(BF16)


def _pool_mixer(u_lat, col_mats, w_pool, pool_scale, batch, seq):
    pb = (u_lat.shape[1] - POOL_WIDTH) // POOL_WIDTH
    ng = len(POOL_WINDOWS)
    return pl.pallas_call(
        _pool_body,
        grid=(batch,),
        in_specs=[
            pl.BlockSpec((seq, POOL_WIDTH), lambda b: (b, pb)),
            _resident((ng, SUPER, SUPER)),
            _resident((ng, POOL_GROUP, POOL_GROUP)),
            _resident((1, POOL_WIDTH)),
        ],
        out_specs=pl.BlockSpec((seq, POOL_WIDTH), lambda b: (b, 0)),
        out_shape=jax.ShapeDtypeStruct((batch * seq, POOL_WIDTH), BF16),
        scratch_shapes=[pltpu.VMEM((seq + 2 * POOL_PAD, POOL_GROUP), F32)],
        compiler_params=_cparams(("arbitrary",)),
        name="pool_mixer",
    )(u_lat, col_mats, w_pool, pool_scale)


def _first_index(hit, iota, size, axis):
    return jnp.min(jnp.where(hit, iota, size), axis=axis, keepdims=True)


def _mix_body(yg_ref, yp_ref, x_ref, gt1_ref, sc2_ref, sh2_ref, gpost_ref, gpre_ref, wout_ref,
              wr_ref, rb_ref, upper_ref,
              x1_ref, hp_ref, eidx_ref, pos_ref, wts_ref, cnt_ref, run_ref, wrow_ref, y_scr):
    tm = x_ref.shape[0]
    neg_inf = jnp.float32(-jnp.inf)
    step = pl.program_id(0)

    @pl.when(step == 0)
    def _():
        run_ref[...] = jnp.zeros_like(run_ref)
        wrow_ref[...] = jnp.zeros_like(wrow_ref)
        y_scr[...] = jnp.zeros_like(y_scr)

    y = y_scr[...]
    y_new = jnp.dot(yg_ref[...], wout_ref[0:GLA_WIDTH, :], preferred_element_type=F32)
    y_scr[...] = y_new + jnp.dot(yp_ref[...], wout_ref[GLA_WIDTH:, :], preferred_element_type=F32)
    x1 = x_ref[...] + gt1_ref[0] * (y * _rms_scale(y) * gpost_ref[...])
    x1_ref[...] = x1
    h = x1 * _rms_scale(x1) * gpre_ref[...]
    h = h * (1.0 + sc2_ref[0]) + sh2_ref[0]
    _store_row_tiles(hp_ref, _pack_halves(h[:, :HALF], h[:, HALF:]))

    h_hi, h_lo = _bf16_terms(h)
    both = jnp.dot(h_hi, wr_ref[...], preferred_element_type=F32)
    lt = both[:, :LANES] + both[:, LANES:] + jnp.dot(h_lo, wr_ref[:, :LANES], preferred_element_type=F32)
    logits = lt.T[0:N_EXPERTS, :]
    scores = jax.nn.sigmoid(logits)
    sel = scores + rb_ref[...]
    shape3 = (N_GROUPS, GROUP_SIZE, tm)
    sel3 = sel.reshape(shape3)
    i_in = lax.broadcasted_iota(I32, shape3, 1).astype(F32)
    m1 = jnp.max(sel3, axis=1, keepdims=True)
    f1 = _first_index(sel3 == m1, i_in, float(GROUP_SIZE), 1)
    m2 = jnp.max(jnp.where(i_in == f1, neg_inf, sel3), axis=1, keepdims=True)
    grp = jnp.broadcast_to(m1 + m2, shape3).reshape(N_EXPERTS, tm)
    i_e = lax.broadcasted_iota(I32, (N_EXPERTS, tm), 0)
    i_grp = (i_e // GROUP_SIZE).astype(F32)
    i_e = i_e.astype(F32)
    allowed = jnp.zeros((N_EXPERTS, tm), F32)
    for _ in range(TOPK_GROUPS):
        m = jnp.max(grp, axis=0, keepdims=True)
        pick = i_grp == _first_index(grp == m, i_grp, float(N_GROUPS), 0)
        allowed = jnp.where(pick, 1.0, allowed)
        grp = jnp.where(pick, neg_inf, grp)
    cand = jnp.where(allowed > 0.0, sel, neg_inf)
    onehot = jnp.zeros((N_EXPERTS, tm), F32)
    picks, wts = [], []
    for k in range(TOP_K):
        m = jnp.max(cand, axis=0, keepdims=True)
        f = _first_index(cand == m, i_e, float(N_EXPERTS), 0)
        pick = i_e == f
        picks.append(pick)
        eidx_ref[k:k + 1, :] = f.astype(I32)
        wts.append(jnp.sum(jnp.where(pick, scores, 0.0), axis=0, keepdims=True))
        onehot = jnp.where(pick, 1.0, onehot)
        cand = jnp.where(pick, neg_inf, cand)
    w_sum = wts[0]
    for k in range(1, TOP_K):
        w_sum = w_sum + wts[k]
    for k in range(TOP_K):
        wrow_ref[k:k + 1, :] = wts[k] / w_sum * ROUTED_SCALE
    wts_ref[...] = wrow_ref[...].T

    before = jnp.dot(onehot.astype(BF16), upper_ref[...], preferred_element_type=F32)
    before = before + run_ref[:, 0:1]
    for k in range(TOP_K):
        pos_ref[k:k + 1, :] = jnp.sum(jnp.where(picks[k], before, 0.0), axis=0, keepdims=True).astype(I32)
    counted = jnp.where(step > 0, 1.0, 0.0)
    run_ref[...] = run_ref[...] + counted * jnp.sum(onehot, axis=1, keepdims=True)
    cnt_ref[...] = run_ref[...].astype(I32)


def _mix_and_route(y_gla, y_pool, x2d, gt1, sc2, sh2, g_post, g_pre, w_out, w_router2, router_bias, seq,
                   part, n_parts):
    d = x2d.shape[1]
    rows = x2d.shape[0] // n_parts
    tiles = rows // MIX_TM
    first = part * tiles
    tiles_per_b = seq // MIX_TM
    proj = lambda i: (first + jnp.minimum(i, tiles - 1), 0)
    bmap = lambda i: ((first + jnp.maximum(i - 1, 0)) // tiles_per_b, 0, 0)
    xmap = lambda i: (first + jnp.maximum(i - 1, 0), 0)
    rmap = lambda i: (jnp.maximum(i - 1, 0), 0)
    tmap = lambda i: (0, jnp.maximum(i - 1, 0))
    upper = jnp.asarray(np.triu(np.ones((MIX_TM, MIX_TM), np.float32), 1), dtype=BF16)
    return pl.pallas_call(
        _mix_body,
        grid=(tiles + 1,),
        in_specs=[
            pl.BlockSpec((MIX_TM, GLA_WIDTH), proj),
            pl.BlockSpec((MIX_TM, POOL_WIDTH), proj),
            pl.BlockSpec((MIX_TM, d), xmap),
            pl.BlockSpec((1, 1, d), bmap),
            pl.BlockSpec((1, 1, d), bmap),
            pl.BlockSpec((1, 1, d), bmap),
            _resident((1, d)),
            _resident((1, d)),
            _resident((d, d)),
            _resident((d, 256)),
            _resident((N_EXPERTS, 1)),
            _resident((MIX_TM, MIX_TM)),
        ],
        out_specs=[
            pl.BlockSpec((MIX_TM, d), rmap),
            pl.BlockSpec((MIX_TM * ROW_SUBLANES, LANES), rmap),
            pl.BlockSpec((TOP_K, MIX_TM), tmap),
            pl.BlockSpec((TOP_K, MIX_TM), tmap),
            pl.BlockSpec((MIX_TM, LANES), rmap),
            pl.BlockSpec((N_EXPERTS, LANES), lambda i: (0, 0)),
        ],
        out_shape=[
            jax.ShapeDtypeStruct((rows, d), F32),
            jax.ShapeDtypeStruct((rows * ROW_SUBLANES, LANES), U32),
            jax.ShapeDtypeStruct((TOP_K, rows), I32),
            jax.ShapeDtypeStruct((TOP_K, rows), I32),
            jax.ShapeDtypeStruct((rows, LANES), F32),
            jax.ShapeDtypeStruct((N_EXPERTS, LANES), I32),
        ],
        scratch_shapes=[pltpu.VMEM((N_EXPERTS, LANES), F32), pltpu.VMEM((LANES, MIX_TM), F32),
                        pltpu.VMEM((MIX_TM, d), F32)],
        compiler_params=_cparams(("arbitrary",)),
        name="mix_and_route",
    )(y_gla, y_pool, x2d, gt1, sc2, sh2, g_post, g_pre, w_out, w_router2, router_bias, upper)


def _swiglu(lo, hi, wg_ref, wu_ref, wd_ref):
    g = jnp.dot(lo, wg_ref[:HALF, :], preferred_element_type=F32)
    g = g + jnp.dot(hi, wg_ref[HALF:, :], preferred_element_type=F32)
    u = jnp.dot(lo, wu_ref[:HALF, :], preferred_element_type=F32)
    u = u + jnp.dot(hi, wu_ref[HALF:, :], preferred_element_type=F32)
    act = (_silu(g) * u).astype(BF16)
    return jnp.dot(act, wd_ref[...], preferred_element_type=F32)


def _shared_body(hp_ref, wg_ref, wu_ref, wd_ref, o_ref):
    lo, hi = _unpack_halves(_load_row_tiles(hp_ref))
    o_ref[...] = _swiglu(lo.astype(BF16), hi.astype(BF16), wg_ref, wu_ref, wd_ref).astype(BF16)


def _shared_expert(h_packed, w_sg, w_su, w_sd):
    d, ds = w_sg.shape
    rows = h_packed.shape[0] // ROW_SUBLANES
    return pl.pallas_call(
        _shared_body,
        grid=(rows // SHARED_TM,),
        in_specs=[
            pl.BlockSpec((SHARED_TM * ROW_SUBLANES, LANES), lambda i: (i, 0)),
            _resident((d, ds)),
            _resident((d, ds)),
            _resident((ds, d)),
        ],
        out_specs=pl.BlockSpec((SHARED_TM, d), lambda i: (i, 0)),
        out_shape=jax.ShapeDtypeStruct((rows, d), BF16),
        compiler_params=_cparams(("arbitrary",)),
        name="shared_expert",
    )(h_packed, w_sg, w_su, w_sd)


def _offsets_body(ps_ref, e_ref, p_ref, d_ref):
    e = e_ref[...]
    d = p_ref[...]
    for x in range(N_EXPERTS):
        d = d + jnp.where(e == x, ps_ref[x], 0)
    d_ref[...] = d


def _route_offsets(eidx_t, pos_t, pstarts):
    k, rows = eidx_t.shape
    spec = pl.BlockSpec((k, OFFS_TN), lambda i, ps: (0, i))
    grid_spec = pltpu.PrefetchScalarGridSpec(
        num_scalar_prefetch=1, grid=(rows // OFFS_TN,), in_specs=[spec, spec], out_specs=spec)
    return pl.pallas_call(
        _offsets_body,
        grid_spec=grid_spec,
        out_shape=jax.ShapeDtypeStruct((k, rows), I32),
        compiler_params=_cparams(("arbitrary",)),
        name="route_offsets",
    )(pstarts, eidx_t, pos_t)


def _expert_body(be_ref, nu_ref, valid_ref, nxt_ref, slot_ref, xs_hbm, wg_hbm, wu_hbm, wd_hbm, y_ref,
                 wg_f, wu_f, wd_f, wg_b, wu_b, wd_b, sems, xbuf, xsems):
    i = pl.program_id(0)
    valid = valid_ref[i]
    expert = be_ref[i]
    slot = slot_ref[i]
    n_used = nu_ref[0]
    block_tiles = MOE_TILE * ROW_SUBLANES

    def row_copy(step):
        ring = step % X_RING
        src = xs_hbm.at[pl.ds(pl.multiple_of(step * block_tiles, block_tiles), block_tiles)]
        return pltpu.make_async_copy(src, xbuf.at[ring], xsems.at[ring])

    @pl.when(i == 0)
    def _():
        for s in range(X_RING - 1):
            @pl.when(s < n_used)
            def _(s=s):
                row_copy(s).start()

    @pl.when(i + (X_RING - 1) < n_used)
    def _():
        row_copy(i + (X_RING - 1)).start()

    @pl.when(i < n_used)
    def _():
        row_copy(i).wait()

    x_ref = xbuf.at[i % X_RING]

    def weight_copies(e, s):
        return [pltpu.make_async_copy(src.at[e], dst.at[s], sems.at[s, j])
                for j, (src, dst) in enumerate(((wg_hbm, wg_f), (wu_hbm, wu_f), (wd_hbm, wd_f)))]

    @pl.when(i == 0)
    def _():
        for cp in weight_copies(expert, slot):
            cp.start()

    @pl.when((i == 0) | (expert != be_ref[jnp.maximum(i - 1, 0)]))
    def _():
        for cp in weight_copies(expert, slot):
            cp.wait()

        @pl.when(nxt_ref[i] != expert)
        def _():
            for cp in weight_copies(nxt_ref[i], 1 - slot):
                cp.start()

        wg_b[...] = wg_f[slot].astype(BF16)
        wu_b[...] = wu_f[slot].astype(BF16)
        wd_b[...] = wd_f[slot].astype(BF16)

    def compute(n_rows):
        tiles = pl.ds(0, n_rows * ROW_SUBLANES)
        lo, hi = _unpack_halves(_load_row_tiles(x_ref.at[tiles]))
        y = _swiglu(lo.astype(BF16), hi.astype(BF16), wg_b, wu_b, wd_b)
        _store_row_tiles(y_ref.at[tiles], _pack_halves(y[:, :HALF], y[:, HALF:]))

    quarter = MOE_TILE // 4
    for n in range(1, 5):
        @pl.when((valid > (n - 1) * quarter) & (valid <= n * quarter))
        def _(n=n):
            compute(n * quarter)


def _experts(xs, block_expert, n_used, valid, next_expert, slot, w_eg, w_eu, w_ed):
    n_rows = xs.shape[0] // ROW_SUBLANES
    n_blocks = n_rows // MOE_TILE
    d, de = w_eg.shape[1], w_eg.shape[2]
    row_map = lambda i, be, nu, *_: (jnp.minimum(i, nu[0] - 1), 0)
    grid_spec = pltpu.PrefetchScalarGridSpec(
        num_scalar_prefetch=5,
        grid=(n_blocks,),
        in_specs=[pl.BlockSpec(memory_space=pl.ANY)] * 4,
        out_specs=pl.BlockSpec((MOE_TILE * ROW_SUBLANES, LANES), row_map),
        scratch_shapes=[
            pltpu.VMEM((2, d, de), F32),
            pltpu.VMEM((2, d, de), F32),
            pltpu.VMEM((2, de, d), F32),
            pltpu.VMEM((d, de), BF16),
            pltpu.VMEM((d, de), BF16),
            pltpu.VMEM((de, d), BF16),
            pltpu.SemaphoreType.DMA((2, 3)),
            pltpu.VMEM((X_RING, MOE_TILE * ROW_SUBLANES, LANES), U32),
            pltpu.SemaphoreType.DMA((X_RING,)),
        ],
    )
    return pl.pallas_call(
        _expert_body,
        grid_spec=grid_spec,
        out_shape=jax.ShapeDtypeStruct((n_rows * ROW_SUBLANES, LANES), U32),
        compiler_params=_cparams(("arbitrary",)),
        name="experts",
    )(block_expert, n_used, valid, next_expert, slot, xs, w_eg, w_eu, w_ed)


def _sc_mesh():
    return plsc.VectorSubcoreMesh(core_axis_name="c", subcore_axis_name="s")


def _sc_worker():
    return lax.axis_index("s") * SC_CORES + lax.axis_index("c")


def _sc_gather_rows(table, idx):
    n_idx = idx.shape[0]
    per_worker = n_idx // SC_WORKERS
    n_chunks = per_worker // SC_CHUNK
    assert per_worker * SC_WORKERS == n_idx and n_chunks * SC_CHUNK == per_worker and n_chunks % 2 == 0
    row_shape = table.shape[1:]

    @functools.partial(
        pl.kernel, mesh=_sc_mesh(),
        out_type=jax.ShapeDtypeStruct((n_idx,) + row_shape, table.dtype),
        scratch_types=[
            pltpu.VMEM((per_worker,), I32),
            pltpu.VMEM((SC_CHUNK,) + row_shape, table.dtype),
            pltpu.VMEM((SC_CHUNK,) + row_shape, table.dtype),
        ] + [pltpu.SemaphoreType.DMA] * 4,
    )
    def gather(table_hbm, idx_hbm, out_hbm, idx_v, buf0, buf1, g0, g1, w0, w1):
        bufs, gsem, wsem = (buf0, buf1), (g0, g1), (w0, w1)
        base = _sc_worker() * per_worker
        pltpu.sync_copy(idx_hbm.at[pl.ds(base, per_worker)], idx_v)

        def fetch(j, b):
            return pltpu.make_async_copy(table_hbm.at[idx_v.at[pl.ds(j * SC_CHUNK, SC_CHUNK)]], bufs[b], gsem[b])

        def flush(j, b):
            return pltpu.make_async_copy(bufs[b], out_hbm.at[pl.ds(base + j * SC_CHUNK, SC_CHUNK)], wsem[b])

        fetch(0, 0).start()
        fetch(0, 0).wait()
        fetch(1, 1).start()
        flush(0, 0).start()

        @pl.loop(1, n_chunks - 1, step=2)
        def _(j):
            for off in range(2):
                jj, b = j + off, (1 + off) % 2
                fetch(jj, b).wait()
                flush(jj - 1, 1 - b).wait()
                fetch(jj + 1, 1 - b).start()
                flush(jj, b).start()

        fetch(n_chunks - 1, 1).wait()
        flush(n_chunks - 1, 1).start()
        flush(n_chunks - 2, 0).wait()
        flush(n_chunks - 1, 1).wait()

    return gather(table, idx)


def _sc_scatter_rows(rows, dest_t, n_out):
    n_rows = rows.shape[0]
    per_worker = n_rows // SC_WORKERS
    n_chunks = per_worker // SC_CHUNK
    assert per_worker * SC_WORKERS == n_rows and n_chunks * SC_CHUNK == per_worker
    row_shape = rows.shape[1:]
    idx_w = dest_t.reshape(TOP_K, SC_WORKERS, n_chunks, SC_CHUNK).transpose(1, 2, 0, 3)
    idx_w = idx_w.reshape(SC_WORKERS, n_chunks * TOP_K, SC_CHUNK)

    @functools.partial(
        pl.kernel, mesh=_sc_mesh(),
        out_type=jax.ShapeDtypeStruct((n_out,) + row_shape, rows.dtype),
        scratch_types=[
            pltpu.VMEM((n_chunks * TOP_K, SC_CHUNK), I32),
            pltpu.VMEM((SC_CHUNK,) + row_shape, rows.dtype),
            pltpu.VMEM((SC_CHUNK,) + row_shape, rows.dtype),
        ] + [pltpu.SemaphoreType.DMA] * 4,
    )
    def scatter(rows_hbm, idx_hbm, out_hbm, idx_v, buf0, buf1, r0, r1, s0, s1):
        bufs, rsem, ssem = (buf0, buf1), (r0, r1), (s0, s1)
        worker = _sc_worker()
        base = worker * per_worker
        pltpu.sync_copy(idx_hbm.at[worker], idx_v)

        def fetch(j, b):
            return pltpu.make_async_copy(rows_hbm.at[pl.ds(base + j * SC_CHUNK, SC_CHUNK)], bufs[b], rsem[b])

        def send(j, k, b):
            return pltpu.make_async_copy(bufs[b], out_hbm.at[idx_v.at[j * TOP_K + k]], ssem[b])

        fetch(0, 0).start()
        for j in range(n_chunks):
            b = j % 2
            fetch(j, b).wait()
            if j + 1 < n_chunks:
                if j >= 1:
                    for k in range(TOP_K):
                        send(j - 1, k, 1 - b).wait()
                fetch(j + 1, 1 - b).start()
            for k in range(TOP_K):
                send(j, k, b).start()
        for j in range(max(n_chunks - 2, 0), n_chunks):
            for k in range(TOP_K):
                send(j, k, j % 2).wait()

    return scatter(rows, idx_w)


def _combine_body(*refs):
    yk_refs = refs[:TOP_K]
    w_ref, shr_ref, x1_ref, gt2_ref, gpost_ref = refs[TOP_K:TOP_K + 5]
    o_ref = refs[-1]
    tt = x1_ref.shape[0]
    w = w_ref[...]
    ssq = jnp.zeros((tt, 1), F32)
    for c in range(HALF // LANES):
        c_lo = slice(c * LANES, (c + 1) * LANES)
        c_hi = slice(HALF + c * LANES, HALF + (c + 1) * LANES)
        y_lo = shr_ref[:, c_lo].astype(F32)
        y_hi = shr_ref[:, c_hi].astype(F32)
        for k in range(TOP_K):
            lo, hi = _unpack_halves(yk_refs[k][pl.ds(c, tt, stride=ROW_SUBLANES), :])
            y_lo = y_lo + w[:, k:k + 1] * lo
            y_hi = y_hi + w[:, k:k + 1] * hi
        ssq = ssq + jnp.sum(y_lo * y_lo, axis=-1, keepdims=True) + jnp.sum(y_hi * y_hi, axis=-1, keepdims=True)
        o_ref[:, c_lo] = y_lo
        o_ref[:, c_hi] = y_hi
    scale = lax.rsqrt(ssq / D_MODEL + EPS)
    o_ref[...] = x1_ref[...] + gt2_ref[0] * (o_ref[...] * scale * gpost_ref[...])


def _combine(yu, wts, shared, x1, gt2, g_post, seq, part, n_parts, prev_out):
    rows, d = x1.shape
    tiles = rows // COMB_TT
    first = part * tiles
    tiles_per_b = seq // COMB_TT
    loc = lambda i: (i, 0)
    yk_specs = [pl.BlockSpec((COMB_TT * ROW_SUBLANES, LANES), functools.partial(lambda i, k: (k * tiles + i, 0), k=k))
                for k in range(TOP_K)]
    in_specs = yk_specs + [
        pl.BlockSpec((COMB_TT, LANES), loc),
        pl.BlockSpec((COMB_TT, d), loc),
        pl.BlockSpec((COMB_TT, d), loc),
        pl.BlockSpec((1, 1, d), lambda i: ((first + i) // tiles_per_b, 0, 0)),
        _resident((1, d)),
    ]
    args = [yu] * TOP_K + [wts, shared, x1, gt2, g_post]
    aliases = {}
    if prev_out is not None:
        aliases = {len(args): 0}
        in_specs.append(pl.BlockSpec(memory_space=pl.ANY))
        args.append(prev_out)
    return pl.pallas_call(
        _combine_body,
        grid=(tiles,),
        in_specs=in_specs,
        out_specs=pl.BlockSpec((COMB_TT, d), lambda i: (first + i, 0)),
        out_shape=jax.ShapeDtypeStruct((rows * n_parts, d), F32),
        input_output_aliases=aliases,
        compiler_params=_cparams(("arbitrary",)),
        name="combine",
    )(*args)


def kernel(x, c, ctx, c_ctx, w_mod, b_mod, norm_mix_pre, norm_mix_post, norm_ffn_pre, norm_ffn_post, w_in, w_a2_fwd, b_a_fwd, w_a2_bwd, b_a_bwd, gla_norm, w_pool, pool_scale, w_out, w_router, router_bias, w_exp_gate, w_exp_up, w_exp_down, w_sh_gate, w_sh_up, w_sh_down):
    batch, seq, d = x.shape
    n_ctx = ctx.shape[1]
    assert w_mod.shape[0] == 1 and d == D_MODEL
    assert seq % (2 * SUPER) == 0 and n_ctx % SUPER == 0 and seq % PROJ_TM == 0 and (batch * n_ctx) % PROJ_TM == 0
    rows = batch * seq

    mod_rows = 16
    c_all = jnp.concatenate([c, c_ctx[None, :], jnp.zeros((mod_rows - batch - 1, d), F32)], axis=0)
    mod_all = _modulation(c_all, w_mod[0], b_mod[0][None, :])
    sh1, sc1, gt1, sh2, sc2, gt2 = [m.reshape(batch, 1, d) for m in jnp.split(mod_all[:batch], 6, axis=-1)]
    csh1 = mod_all[batch, 0:d].reshape(1, 1, d)
    csc1 = mod_all[batch, d:2 * d].reshape(1, 1, d)

    kw, gw = GLA_KEY_WIDTH, GLA_WIDTH
    a0 = 2 * kw + 2 * gw
    w_in0 = w_in[0]
    w_bf = w_in0.astype(BF16)
    w_a = jnp.pad(w_bf[:, a0:a0 + 2 * GLA_RANK], ((0, 0), (0, LANES - 2 * GLA_RANK)))
    lat_pieces = [(w_bf, a0, 0), (w_bf[:, a0 + 2 * GLA_RANK:], POOL_WIDTH, 0)]
    ctx_pieces = [(w_bf, gw, 2 * kw // gw), (w_bf, kw, 1)]
    w2f = jnp.pad(w_a2_fwd[0], ((0, LANES - GLA_RANK), (0, 0))).astype(BF16)
    w2b = jnp.pad(w_a2_bwd[0], ((GLA_RANK, LANES - 2 * GLA_RANK), (0, 0))).astype(BF16)
    g_mix_pre = norm_mix_pre[0][None, :]
    w_router2 = jnp.concatenate(_bf16_terms(jnp.pad(w_router[0], ((0, 0), (0, LANES - N_EXPERTS)))), axis=1)

    u_ctx, a_ctx = _in_projection(ctx.reshape(batch * n_ctx, d), g_mix_pre, csc1, csh1, ctx_pieces, w_a,
                                  batch * n_ctx)
    u_lat, a_lat = _in_projection(x.reshape(rows, d), g_mix_pre, sc1, sh1, lat_pieces, w_a, seq)

    y_gla = _gla(u_lat, a_lat, u_ctx, a_ctx, w2f, b_a_fwd[0][None, :], w2b, b_a_bwd[0][None, :],
                 gla_norm[0][None, :], batch, seq, n_ctx)
    y_pool = _pool_mixer(u_lat, _col_window_matrices(), w_pool[0].astype(BF16), pool_scale[0][None, :],
                         batch, seq)

    w_out_bf = w_out[0].astype(BF16)
    shared_w = (w_sh_gate[0].astype(BF16), w_sh_up[0].astype(BF16), w_sh_down[0].astype(BF16))
    x2d = x.reshape(rows, d)
    part_rows = rows // MOE_PARTS
    n_blocks = part_rows * TOP_K // MOE_TILE + N_EXPERTS
    e_ids = jnp.arange(N_EXPERTS, dtype=I32)

    mixed = [_mix_and_route(y_gla, y_pool, x2d, gt1, sc2, sh2, norm_mix_post[0][None, :],
                            norm_ffn_pre[0][None, :], w_out_bf, w_router2, router_bias[0][:, None], seq,
                            p, MOE_PARTS) for p in range(MOE_PARTS)]
    out = None
    for p, (x1, h_packed, eidx_t, pos_t, wts, counts) in enumerate(mixed):
        shared = _shared_expert(h_packed, *shared_w)
        counts = counts[:, 0]
        padded = (counts + MOE_TILE - 1) // MOE_TILE * MOE_TILE
        pends = jnp.cumsum(padded)
        pstarts = pends - padded
        dest_t = _route_offsets(eidx_t, pos_t, pstarts.astype(I32))
        n_used = (pends[-1] // MOE_TILE).astype(I32)
        blk = jnp.minimum(jnp.arange(n_blocks, dtype=I32), n_used - 1)
        block_expert = jnp.sum((blk * MOE_TILE)[:, None] >= pends[None, :], axis=1).astype(I32)
        block_expert = jnp.minimum(block_expert, N_EXPERTS - 1)
        is_block_expert = block_expert[:, None] == e_ids[None, :]
        per_block = lambda v: jnp.sum(jnp.where(is_block_expert, v[None, :], 0), axis=1).astype(I32)
        valid = jnp.clip(per_block(pstarts + counts) - blk * MOE_TILE, 0, MOE_TILE)
        valid = jnp.where(jnp.arange(n_blocks, dtype=I32) < n_used, valid, 0).astype(I32)
        has_rows = padded > 0
        later = jnp.where((e_ids[None, :] > e_ids[:, None]) & has_rows[None, :], e_ids[None, :], N_EXPERTS)
        next_e = jnp.min(later, axis=1)
        next_e = jnp.where(next_e == N_EXPERTS, e_ids, next_e)
        slot_e = (jnp.cumsum(has_rows.astype(I32)) - has_rows.astype(I32)) % 2

        xs = _sc_scatter_rows(h_packed.reshape(-1, ROW_SUBLANES, LANES), dest_t, n_blocks * MOE_TILE)
        ys = _experts(xs.reshape(-1, LANES), block_expert, n_used.reshape(1), valid, per_block(next_e),
                      per_block(slot_e), w_exp_gate[0], w_exp_up[0], w_exp_down[0])
        yu = _sc_gather_rows(ys.reshape(-1, ROW_SUBLANES, LANES), dest_t.reshape(-1))
        out = _combine(yu.reshape(-1, LANES), wts, shared, x1, gt2, norm_ffn_post[0][None, :], seq,
                       p, MOE_PARTS, out)
    return out.reshape(batch, seq, d)
```

```python
import functools

import numpy as np
import jax
import jax.numpy as jnp
from jax import lax
from jax.experimental import pallas as pl
from jax.experimental.pallas import tpu as pltpu
from jax.experimental.pallas import tpu_sc as plsc

F32 = jnp.float32
BF16 = jnp.bfloat16
I32 = jnp.int32
U32 = jnp.uint32

D_MODEL = 2048
GRID_W = 64
GLA_HEADS = 4
GLA_DK = 128
GLA_DV = 256
GLA_KEY_WIDTH = GLA_HEADS * GLA_DK
GLA_WIDTH = GLA_HEADS * GLA_DV
GLA_RANK = 16
GLA_TAU = 16.0
GLA_CHUNK = 64
POOL_WIDTH = 1024
POOL_WINDOWS = (2, 4, 8, 16)
POOL_GROUP = 256
N_EXPERTS = 64
TOP_K = 8
N_GROUPS = 8
GROUP_SIZE = N_EXPERTS // N_GROUPS
TOPK_GROUPS = 4
ROUTED_SCALE = 2.5
EPS = 1e-6

HALF = D_MODEL // 2
SUPER = 4 * GLA_CHUNK
GLA_HPS = 4
POOL_PAD = 8 * GRID_W
VMEM_LIMIT = 56 * 1024 * 1024

MOD_TN = 1024
PROJ_TM = 512
PROJ_TN = 512
MIX_TM = 512
MOE_TILE = 512
X_RING = 3
Y_RING = 3
SHARED_TM = 1024
COMB_TT = 256
MOE_PARTS = 2
LANES = 128
ROW_SUBLANES = 8
SC_CORES = 2
SC_WORKERS = 32
SC_CHUNK = 32
OFFS_TN = 2048


def _cparams(sem):
    return pltpu.CompilerParams(dimension_semantics=sem, vmem_limit_bytes=VMEM_LIMIT)


def _resident(shape):
    nd = len(shape)
    return pl.BlockSpec(shape, lambda *_: (0,) * nd, pipeline_mode=pl.Buffered(1))


def _silu(v):
    return v * jax.nn.sigmoid(v)


def _pack_halves(lo, hi):
    lo_b = lax.bitcast_convert_type(lo.astype(BF16).astype(F32), U32)
    hi_b = lax.bitcast_convert_type(hi.astype(BF16).astype(F32), U32)
    return (hi_b & jnp.uint32(0xFFFF0000)) | (lo_b >> 16)


def _unpack_halves(p):
    lo = lax.bitcast_convert_type(p << 16, F32)
    hi = lax.bitcast_convert_type(p & jnp.uint32(0xFFFF0000), F32)
    return lo, hi


def _bf16_terms(x):
    hi = lax.bitcast_convert_type(lax.bitcast_convert_type(x, U32) & jnp.uint32(0xFFFF0000), F32)
    return hi.astype(BF16), (x - hi).astype(BF16)


def _store_row_tiles(ref, packed):
    n = packed.shape[0]
    for c in range(HALF // LANES):
        ref[pl.ds(c, n, stride=ROW_SUBLANES), :] = packed[:, c * LANES:(c + 1) * LANES]


def _load_row_tiles(ref):
    n = ref.shape[0] // ROW_SUBLANES
    return jnp.concatenate([ref[pl.ds(c, n, stride=ROW_SUBLANES), :] for c in range(HALF // LANES)], axis=1)


def _mod_body(c_ref, w_ref, b_ref, o_ref):
    s_hi, s_lo = _bf16_terms(_silu(c_ref[...]))
    w_hi, w_lo = _bf16_terms(w_ref[...])
    acc = jnp.dot(s_hi, w_hi, preferred_element_type=F32)
    acc = acc + jnp.dot(s_lo, w_hi, preferred_element_type=F32)
    acc = acc + jnp.dot(s_hi, w_lo, preferred_element_type=F32)
    o_ref[...] = acc + b_ref[...]


def _modulation(c_all, w_mod, b_mod):
    rows, d = c_all.shape
    n = w_mod.shape[1]
    return pl.pallas_call(
        _mod_body,
        grid=(n // MOD_TN,),
        in_specs=[
            pl.BlockSpec((rows, d), lambda j: (0, 0)),
            pl.BlockSpec((d, MOD_TN), lambda j: (0, j)),
            pl.BlockSpec((1, MOD_TN), lambda j: (0, j)),
        ],
        out_specs=pl.BlockSpec((rows, MOD_TN), lambda j: (0, j)),
        out_shape=jax.ShapeDtypeStruct((rows, n), F32),
        compiler_params=_cparams(("arbitrary",)),
        name="modulation",
    )(c_all, w_mod, b_mod)


def _rms_scale(x):
    return lax.rsqrt(jnp.mean(x * x, axis=-1, keepdims=True) + EPS)


def _inproj_body(x_ref, g_ref, sc_ref, sh_ref, *refs):
    w_refs, (wa_ref, o_ref, a_ref) = refs[:-3], refs[-3:]
    x = x_ref[...]
    h = x * _rms_scale(x) * g_ref[...]
    h = h * (1.0 + sc_ref[0]) + sh_ref[0]
    hb = h.astype(BF16)
    col = 0
    for w_ref in w_refs:
        for n in range(w_ref.shape[1] // PROJ_TN):
            cols = slice(n * PROJ_TN, (n + 1) * PROJ_TN)
            o_ref[:, col:col + PROJ_TN] = jnp.dot(hb, w_ref[:, cols], preferred_element_type=F32).astype(BF16)
            col += PROJ_TN
    a_ref[...] = jnp.dot(hb, wa_ref[...], preferred_element_type=F32)


def _in_projection(x2d, gain, sc, sh, pieces, w_a, rows_per_mod):
    rows, d = x2d.shape
    n_main = sum(width for _, width, _ in pieces)
    tiles_per_mod = rows_per_mod // PROJ_TM
    mod_map = lambda i: (i // tiles_per_mod, 0, 0)
    piece_specs = [pl.BlockSpec((d, width), functools.partial(lambda i, b: (0, b), b=block),
                                pipeline_mode=pl.Buffered(1)) for _, width, block in pieces]
    return pl.pallas_call(
        _inproj_body,
        grid=(rows // PROJ_TM,),
        in_specs=[
            pl.BlockSpec((PROJ_TM, d), lambda i: (i, 0)),
            _resident((1, d)),
            pl.BlockSpec((1, 1, d), mod_map),
            pl.BlockSpec((1, 1, d), mod_map),
            *piece_specs,
            _resident((d, LANES)),
        ],
        out_specs=[
            pl.BlockSpec((PROJ_TM, n_main), lambda i: (i, 0)),
            pl.BlockSpec((PROJ_TM, LANES), lambda i: (i, 0)),
        ],
        out_shape=[
            jax.ShapeDtypeStruct((rows, n_main), BF16),
            jax.ShapeDtypeStruct((rows, LANES), F32),
        ],
        compiler_params=_cparams(("arbitrary",)),
        name="in_projection",
    )(x2d, gain, sc, sh, *[w for w, _, _ in pieces], w_a)


def _log_sigmoid(z):
    return jnp.minimum(z, 0.0) - jnp.log1p(jnp.exp(-jnp.abs(z)))


def _gla_cumulative_decay(a, w2, ba, tri):
    z = jnp.dot(a.astype(BF16), w2, preferred_element_type=F32) + ba
    g = _log_sigmoid(z) * (1.0 / GLA_TAU)
    g_hi, g_lo = _bf16_terms(g)
    return jnp.dot(tri, g_hi, preferred_element_type=F32) + jnp.dot(tri, g_lo, preferred_element_type=F32)


def _gla_prep(q, k, G, reverse):
    nc = SUPER // GLA_CHUNK
    G = G.reshape(nc, GLA_CHUNK, GLA_DK)
    end_row = 0 if reverse else GLA_CHUNK - 1
    mid_row = GLA_CHUNK - 1 - GLA_CHUNK // 2 if reverse else GLA_CHUNK // 2
    g_end = G[:, end_row:end_row + 1, :]
    g_mid = G[:, mid_row:mid_row + 1, :]
    k4 = k.astype(F32).reshape(nc, GLA_CHUNK, GLA_DK)
    dec = jnp.broadcast_to(jnp.exp(g_end), (nc, ROW_SUBLANES, GLA_DK)).reshape(nc * ROW_SUBLANES, GLA_DK)
    flat = lambda t: t.reshape(SUPER, GLA_DK).astype(BF16)
    if q is None:
        return None, None, None, flat(k4 * jnp.exp(g_end - G)), dec
    q4 = q.astype(F32).reshape(nc, GLA_CHUNK, GLA_DK) * (GLA_DK ** -0.5)
    qg = q4 * jnp.exp(G - g_mid)
    kg = k4 * jnp.exp(g_mid - G)
    qe = qg * jnp.exp(g_mid)
    kd = kg * jnp.exp(g_end - g_mid)
    return flat(qg), flat(kg), flat(qe), flat(kd), dec


def _gla_apply(qg, kg, qe, kd, dec, v, mask, st_ref, reverse):
    nc = SUPER // GLA_CHUNK
    o = None
    if qg is not None:
        att = lax.dot_general(qg, kg, (((1,), (1,)), ((), ())), preferred_element_type=F32)
        att = jnp.where(mask, att, 0.0).astype(BF16)
        o = jnp.dot(att, v, preferred_element_type=F32)
    outs = [None] * nc
    order = range(nc - 1, -1, -1) if reverse else range(nc)
    for c in order:
        rows = slice(c * GLA_CHUNK, (c + 1) * GLA_CHUNK)
        st = st_ref[...]
        if qg is not None:
            inter = lax.dot_general(qe[rows], st.astype(BF16), (((1,), (1,)), ((), ())),
                                    preferred_element_type=F32)
            outs[c] = o[rows] + inter
        upd = lax.dot_general(v[rows], kd[rows], (((0,), (0,)), ((), ())), preferred_element_type=F32)
        st_ref[...] = st * dec[ROW_SUBLANES * c:ROW_SUBLANES * c + 1, :] + upd
    if qg is None:
        return None
    return jnp.concatenate(outs, axis=0)


def _gla_body(q_ref, k_ref, v_ref, r_ref, a_ref, kc_ref, vc_ref, ac_ref,
              w2f_ref, baf_ref, w2b_ref, bab_ref, gn_ref, y_ref, o_acc, st, ops_a, ops_b, dec_a, dec_b,
              *, n_ctx):
    n_sup = q_ref.shape[0] // SUPER
    row = lax.broadcasted_iota(I32, (SUPER, SUPER), 0)
    col = lax.broadcasted_iota(I32, (SUPER, SUPER), 1)
    same_chunk = (row // GLA_CHUNK) == (col // GLA_CHUNK)
    mask_f = same_chunk & (col <= row)
    mask_b = same_chunk & (col >= row)
    tri_f = jnp.where(mask_f, 1.0, 0.0).astype(BF16)
    tri_b = jnp.where(mask_b, 1.0, 0.0).astype(BF16)
    heads = range(GLA_HPS)
    kcol = [slice(h * GLA_DK, (h + 1) * GLA_DK) for h in heads]
    vcol = [slice(h * GLA_DV, (h + 1) * GLA_DV) for h in heads]
    dirs = ((False, w2f_ref, baf_ref, tri_f, mask_f), (True, w2b_ref, bab_ref, tri_b, mask_b))

    st[...] = jnp.zeros_like(st)
    n_csup = n_ctx // SUPER
    for s in range(n_csup):
        for d, (reverse, w2_ref, ba_ref, tri, mask) in enumerate(dirs):
            sc = n_csup - 1 - s if reverse else s
            rows = slice(sc * SUPER, (sc + 1) * SUPER)
            G = _gla_cumulative_decay(ac_ref[rows, :], w2_ref[...], ba_ref[...], tri)
            for h in heads:
                _, _, _, kd, dec = _gla_prep(None, kc_ref[rows, kcol[h]], G[:, kcol[h]], reverse)
                _gla_apply(None, None, None, kd, dec, vc_ref[rows, vcol[h]], mask, st.at[d, h], reverse)

    o_acc[...] = jnp.zeros_like(o_acc)

    def rows_of(i, reverse):
        sc = n_sup - 1 - i if reverse else i
        return pl.ds(pl.multiple_of(sc * SUPER, SUPER), SUPER)

    def prepare(i, ops, decs, d):
        reverse, w2_ref, ba_ref, tri, mask = dirs[d]
        rows = rows_of(i, reverse)
        G = _gla_cumulative_decay(a_ref[rows, :], w2_ref[...], ba_ref[...], tri)
        for h in heads:
            vals = _gla_prep(q_ref[rows, kcol[h]], k_ref[rows, kcol[h]], G[:, kcol[h]], reverse)
            for j in range(4):
                ops[GLA_HPS * d + h, j] = vals[j]
            decs[GLA_HPS * d + h] = vals[4]

    def apply(i, ops, decs, d):
        reverse, w2_ref, ba_ref, tri, mask = dirs[d]
        rows = rows_of(i, reverse)
        for h in heads:
            ci = GLA_HPS * d + h
            out = _gla_apply(ops[ci, 0], ops[ci, 1], ops[ci, 2], ops[ci, 3], decs[ci],
                             v_ref[rows, vcol[h]], mask, st.at[d, h], reverse)
            o_acc[rows, vcol[h]] += out

    for d in range(2):
        prepare(0, ops_a, dec_a, d)

    def step(j, carry):
        i = 2 * j
        for d in range(2):
            prepare(i + 1, ops_b, dec_b, d)
            apply(i, ops_a, dec_a, d)
        nxt = jnp.minimum(i + 2, n_sup - 1)
        for d in range(2):
            prepare(nxt, ops_a, dec_a, d)
            apply(i + 1, ops_b, dec_b, d)
        return carry

    lax.fori_loop(0, n_sup // 2, step, 0)

    for h in heads:
        o = o_acc[:, vcol[h]]
        o = o * _rms_scale(o) * gn_ref[:, vcol[h]]
        y_ref[:, vcol[h]] = (o * _silu(r_ref[:, vcol[h]].astype(F32))).astype(BF16)


def _gla(u_lat, a_lat, u_ctx, a_ctx, w2f, baf, w2b, bab, gla_norm, batch, seq, n_ctx):
    groups = GLA_HEADS // GLA_HPS
    kw, vw = GLA_HPS * GLA_DK, GLA_HPS * GLA_DV
    kb = GLA_KEY_WIDTH // kw
    vb = 2 * GLA_KEY_WIDTH // vw
    rb = vb + groups
    assert GLA_KEY_WIDTH % kw == 0 and (2 * GLA_KEY_WIDTH) % vw == 0 and GLA_WIDTH % kw == 0
    ckb = GLA_WIDTH // kw
    return pl.pallas_call(
        functools.partial(_gla_body, n_ctx=n_ctx),
        grid=(batch, groups),
        in_specs=[
            pl.BlockSpec((seq, kw), lambda b, h: (b, h)),
            pl.BlockSpec((seq, kw), lambda b, h: (b, kb + h)),
            pl.BlockSpec((seq, vw), lambda b, h: (b, vb + h)),
            pl.BlockSpec((seq, vw), lambda b, h: (b, rb + h)),
            pl.BlockSpec((seq, LANES), lambda b, h: (b, 0)),
            pl.BlockSpec((n_ctx, kw), lambda b, h: (b, ckb + h)),
            pl.BlockSpec((n_ctx, vw), lambda b, h: (b, h)),
            pl.BlockSpec((n_ctx, LANES), lambda b, h: (b, 0)),
            pl.BlockSpec((LANES, kw), lambda b, h: (0, h)),
            pl.BlockSpec((1, kw), lambda b, h: (0, h)),
            pl.BlockSpec((LANES, kw), lambda b, h: (0, h)),
            pl.BlockSpec((1, kw), lambda b, h: (0, h)),
            pl.BlockSpec((1, vw), lambda b, h: (0, h)),
        ],
        out_specs=pl.BlockSpec((seq, vw), lambda b, h: (b, h)),
        out_shape=jax.ShapeDtypeStruct((batch * seq, GLA_WIDTH), BF16),
        scratch_shapes=[
            pltpu.VMEM((seq, vw), F32),
            pltpu.VMEM((2, GLA_HPS, GLA_DV, GLA_DK), F32),
            pltpu.VMEM((2 * GLA_HPS, 4, SUPER, GLA_DK), BF16),
            pltpu.VMEM((2 * GLA_HPS, 4, SUPER, GLA_DK), BF16),
            pltpu.VMEM((2 * GLA_HPS, ROW_SUBLANES * (SUPER // GLA_CHUNK), GLA_DK), F32),
            pltpu.VMEM((2 * GLA_HPS, ROW_SUBLANES * (SUPER // GLA_CHUNK), GLA_DK), F32),
        ],
        compiler_params=_cparams(("arbitrary", "arbitrary")),
        name="gla",
    )(u_lat, u_lat, u_lat, u_lat, a_lat, u_ctx, u_ctx, a_ctx, w2f, baf, w2b, bab, gla_norm)


def _col_window_matrices():
    t = np.arange(SUPER)
    r, c = t // GRID_W, t % GRID_W
    mats = []
    for w in POOL_WINDOWS:
        lo = np.maximum(c - w // 2, 0)[:, None]
        hi = np.minimum(c + w // 2, GRID_W)[:, None]
        m = (r[:, None] == r[None, :]) & (c[None, :] >= lo) & (c[None, :] < hi)
        mats.append(m.astype(np.float32))
    return jnp.asarray(np.stack(mats), dtype=BF16)


def _pool_body(p_ref, cw_ref, wp_ref, ps_ref, y_ref, pad_ref):
    seq = p_ref.shape[0]
    n_rows = seq // GRID_W
    zeros = jnp.zeros((POOL_PAD, POOL_GROUP), F32)
    pad_ref[0:POOL_PAD, :] = zeros
    pad_ref[POOL_PAD + seq:POOL_PAD + seq + POOL_PAD, :] = zeros
    t = lax.broadcasted_iota(I32, (seq, POOL_GROUP), 0)
    r = t // GRID_W
    c = t % GRID_W
    for gi, w in enumerate(POOL_WINDOWS):
        cols = slice(gi * POOL_GROUP, (gi + 1) * POOL_GROUP)
        cw = cw_ref[gi]
        for j in range(seq // SUPER):
            rows = slice(j * SUPER, (j + 1) * SUPER)
            pad_ref[POOL_PAD + j * SUPER:POOL_PAD + (j + 1) * SUPER, :] = jnp.dot(
                cw, p_ref[rows, cols], preferred_element_type=F32)
        total = None
        for d in range(-(w // 2), w // 2):
            start = POOL_PAD + d * GRID_W
            part = pad_ref[start:start + seq, :]
            total = part if total is None else total + part
        cnt_r = jnp.minimum(r + w // 2, n_rows) - jnp.maximum(r - w // 2, 0)
        cnt_c = jnp.minimum(c + w // 2, GRID_W) - jnp.maximum(c - w // 2, 0)
        mean = total / (cnt_r * cnt_c).astype(F32)
        diff = (mean - p_ref[:, cols].astype(F32)).astype(BF16)
        y = jnp.dot(diff, wp_ref[gi], preferred_element_type=F32) * ps_ref[:, cols]
        y_ref[:, cols] = y.astype(BF16)


def _pool_mixer(u_lat, col_mats, w_pool, pool_scale, batch, seq):
    pb = (u_lat.shape[1] - POOL_WIDTH) // POOL_WIDTH
    ng = len(POOL_WINDOWS)
    return pl.pallas_call(
        _pool_body,
        grid=(batch,),
        in_specs=[
            pl.BlockSpec((seq, POOL_WIDTH), lambda b: (b, pb)),
            _resident((ng, SUPER, SUPER)),
            _resident((ng, POOL_GROUP, POOL_GROUP)),
            _resident((1, POOL_WIDTH)),
        ],
        out_specs=pl.BlockSpec((seq, POOL_WIDTH), lambda b: (b, 0)),
        out_shape=jax.ShapeDtypeStruct((batch * seq, POOL_WIDTH), BF16),
        scratch_shapes=[pltpu.VMEM((seq + 2 * POOL_PAD, POOL_GROUP), F32)],
        compiler_params=_cparams(("arbitrary",)),
        name="pool_mixer",
    )(u_lat, col_mats, w_pool, pool_scale)


def _first_index(hit, iota, size, axis):
    return jnp.min(jnp.where(hit, iota, size), axis=axis, keepdims=True)


def _mix_body(yg_ref, yp_ref, x_ref, gt1_ref, sc2_ref, sh2_ref, gpost_ref, gpre_ref, wout_ref,
              wr_ref, rb_ref, upper_ref,
              x1_ref, hp_ref, eidx_ref, pos_ref, wts_ref, cnt_ref, run_ref, wrow_ref, y_scr):
    tm = x_ref.shape[0]
    neg_inf = jnp.float32(-jnp.inf)
    step = pl.program_id(0)

    @pl.when(step == 0)
    def _():
        run_ref[...] = jnp.zeros_like(run_ref)
        wrow_ref[...] = jnp.zeros_like(wrow_ref)
        y_scr[...] = jnp.zeros_like(y_scr)

    y = y_scr[...]
    y_new = jnp.dot(yg_ref[...], wout_ref[0:GLA_WIDTH, :], preferred_element_type=F32)
    y_scr[...] = y_new + jnp.dot(yp_ref[...], wout_ref[GLA_WIDTH:, :], preferred_element_type=F32)
    x1 = x_ref[...] + gt1_ref[0] * (y * _rms_scale(y) * gpost_ref[...])
    x1_ref[...] = x1
    h = x1 * _rms_scale(x1) * gpre_ref[...]
    h = h * (1.0 + sc2_ref[0]) + sh2_ref[0]
    _store_row_tiles(hp_ref, _pack_halves(h[:, :HALF], h[:, HALF:]))

    h_hi, h_lo = _bf16_terms(h)
    both = jnp.dot(h_hi, wr_ref[...], preferred_element_type=F32)
    lt = both[:, :LANES] + both[:, LANES:] + jnp.dot(h_lo, wr_ref[:, :LANES], preferred_element_type=F32)
    logits = lt.T[0:N_EXPERTS, :]
    scores = jax.nn.sigmoid(logits)
    sel = scores + rb_ref[...]
    shape3 = (N_GROUPS, GROUP_SIZE, tm)
    sel3 = sel.reshape(shape3)
    i_in = lax.broadcasted_iota(I32, shape3, 1).astype(F32)
    m1 = jnp.max(sel3, axis=1, keepdims=True)
    f1 = _first_index(sel3 == m1, i_in, float(GROUP_SIZE), 1)
    m2 = jnp.max(jnp.where(i_in == f1, neg_inf, sel3), axis=1, keepdims=True)
    grp = jnp.broadcast_to(m1 + m2, shape3).reshape(N_EXPERTS, tm)
    i_e = lax.broadcasted_iota(I32, (N_EXPERTS, tm), 0)
    i_grp = (i_e // GROUP_SIZE).astype(F32)
    i_e = i_e.astype(F32)
    allowed = jnp.zeros((N_EXPERTS, tm), F32)
    for _ in range(TOPK_GROUPS):
        m = jnp.max(grp, axis=0, keepdims=True)
        pick = i_grp == _first_index(grp == m, i_grp, float(N_GROUPS), 0)
        allowed = jnp.where(pick, 1.0, allowed)
        grp = jnp.where(pick, neg_inf, grp)
    cand = jnp.where(allowed > 0.0, sel, neg_inf)
    onehot = jnp.zeros((N_EXPERTS, tm), F32)
    picks, wts = [], []
    for k in range(TOP_K):
        m = jnp.max(cand, axis=0, keepdims=True)
        f = _first_index(cand == m, i_e, float(N_EXPERTS), 0)
        pick = i_e == f
        picks.append(pick)
        eidx_ref[k:k + 1, :] = f.astype(I32)
        wts.append(jnp.sum(jnp.where(pick, scores, 0.0), axis=0, keepdims=True))
        onehot = jnp.where(pick, 1.0, onehot)
        cand = jnp.where(pick, neg_inf, cand)
    w_sum = wts[0]
    for k in range(1, TOP_K):
        w_sum = w_sum + wts[k]
    for k in range(TOP_K):
        wrow_ref[k:k + 1, :] = wts[k] / w_sum * ROUTED_SCALE
    wts_ref[...] = wrow_ref[...].T

    before = jnp.dot(onehot.astype(BF16), upper_ref[...], preferred_element_type=F32)
    before = before + run_ref[:, 0:1]
    for k in range(TOP_K):
        pos_ref[k:k + 1, :] = jnp.sum(jnp.where(picks[k], before, 0.0), axis=0, keepdims=True).astype(I32)
    counted = jnp.where(step > 0, 1.0, 0.0)
    run_ref[...] = run_ref[...] + counted * jnp.sum(onehot, axis=1, keepdims=True)
    cnt_ref[...] = run_ref[...].astype(I32)


def _mix_and_route(y_gla, y_pool, x2d, gt1, sc2, sh2, g_post, g_pre, w_out, w_router2, router_bias, seq,
                   part, n_parts):
    d = x2d.shape[1]
    rows = x2d.shape[0] // n_parts
    tiles = rows // MIX_TM
    first = part * tiles
    tiles_per_b = seq // MIX_TM
    proj = lambda i: (first + jnp.minimum(i, tiles - 1), 0)
    bmap = lambda i: ((first + jnp.maximum(i - 1, 0)) // tiles_per_b, 0, 0)
    xmap = lambda i: (first + jnp.maximum(i - 1, 0), 0)
    rmap = lambda i: (jnp.maximum(i - 1, 0), 0)
    tmap = lambda i: (0, jnp.maximum(i - 1, 0))
    upper = jnp.asarray(np.triu(np.ones((MIX_TM, MIX_TM), np.float32), 1), dtype=BF16)
    return pl.pallas_call(
        _mix_body,
        grid=(tiles + 1,),
        in_specs=[
            pl.BlockSpec((MIX_TM, GLA_WIDTH), proj),
            pl.BlockSpec((MIX_TM, POOL_WIDTH), proj),
            pl.BlockSpec((MIX_TM, d), xmap),
            pl.BlockSpec((1, 1, d), bmap),
            pl.BlockSpec((1, 1, d), bmap),
            pl.BlockSpec((1, 1, d), bmap),
            _resident((1, d)),
            _resident((1, d)),
            _resident((d, d)),
            _resident((d, 256)),
            _resident((N_EXPERTS, 1)),
            _resident((MIX_TM, MIX_TM)),
        ],
        out_specs=[
            pl.BlockSpec((MIX_TM, d), rmap),
            pl.BlockSpec((MIX_TM * ROW_SUBLANES, LANES), rmap),
            pl.BlockSpec((TOP_K, MIX_TM), tmap),
            pl.BlockSpec((TOP_K, MIX_TM), tmap),
            pl.BlockSpec((MIX_TM, LANES), rmap),
            pl.BlockSpec((N_EXPERTS, LANES), lambda i: (0, 0)),
        ],
        out_shape=[
            jax.ShapeDtypeStruct((rows, d), F32),
            jax.ShapeDtypeStruct((rows * ROW_SUBLANES, LANES), U32),
            jax.ShapeDtypeStruct((TOP_K, rows), I32),
            jax.ShapeDtypeStruct((TOP_K, rows), I32),
            jax.ShapeDtypeStruct((rows, LANES), F32),
            jax.ShapeDtypeStruct((N_EXPERTS, LANES), I32),
        ],
        scratch_shapes=[pltpu.VMEM((N_EXPERTS, LANES), F32), pltpu.VMEM((LANES, MIX_TM), F32),
                        pltpu.VMEM((MIX_TM, d), F32)],
        compiler_params=_cparams(("arbitrary",)),
        name="mix_and_route",
    )(y_gla, y_pool, x2d, gt1, sc2, sh2, g_post, g_pre, w_out, w_router2, router_bias, upper)


def _swiglu(lo, hi, wg_ref, wu_ref, wd_ref):
    g = jnp.dot(lo, wg_ref[:HALF, :], preferred_element_type=F32)
    g = g + jnp.dot(hi, wg_ref[HALF:, :], preferred_element_type=F32)
    u = jnp.dot(lo, wu_ref[:HALF, :], preferred_element_type=F32)
    u = u + jnp.dot(hi, wu_ref[HALF:, :], preferred_element_type=F32)
    act = (_silu(g) * u).astype(BF16)
    return jnp.dot(act, wd_ref[...], preferred_element_type=F32)


def _shared_body(hp_ref, wg_ref, wu_ref, wd_ref, o_ref):
    lo, hi = _unpack_halves(_load_row_tiles(hp_ref))
    o_ref[...] = _swiglu(lo.astype(BF16), hi.astype(BF16), wg_ref, wu_ref, wd_ref).astype(BF16)


def _shared_expert(h_packed, w_sg, w_su, w_sd):
    d, ds = w_sg.shape
    rows = h_packed.shape[0] // ROW_SUBLANES
    return pl.pallas_call(
        _shared_body,
        grid=(rows // SHARED_TM,),
        in_specs=[
            pl.BlockSpec((SHARED_TM * ROW_SUBLANES, LANES), lambda i: (i, 0)),
            _resident((d, ds)),
            _resident((d, ds)),
            _resident((ds, d)),
        ],
        out_specs=pl.BlockSpec((SHARED_TM, d), lambda i: (i, 0)),
        out_shape=jax.ShapeDtypeStruct((rows, d), BF16),
        compiler_params=_cparams(("arbitrary",)),
        name="shared_expert",
    )(h_packed, w_sg, w_su, w_sd)


def _offsets_body(ps_ref, e_ref, p_ref, d_ref):
    e = e_ref[...]
    d = p_ref[...]
    for x in range(N_EXPERTS):
        d = d + jnp.where(e == x, ps_ref[x], 0)
    d_ref[...] = d


def _route_offsets(eidx_t, pos_t, pstarts):
    k, rows = eidx_t.shape
    spec = pl.BlockSpec((k, OFFS_TN), lambda i, ps: (0, i))
    grid_spec = pltpu.PrefetchScalarGridSpec(
        num_scalar_prefetch=1, grid=(rows // OFFS_TN,), in_specs=[spec, spec], out_specs=spec)
    return pl.pallas_call(
        _offsets_body,
        grid_spec=grid_spec,
        out_shape=jax.ShapeDtypeStruct((k, rows), I32),
        compiler_params=_cparams(("arbitrary",)),
        name="route_offsets",
    )(pstarts, eidx_t, pos_t)


def _expert_body(be_ref, nu_ref, valid_ref, nxt_ref, slot_ref, xs_hbm, wg_hbm, wu_hbm, wd_hbm, y_ref,
                 wg_f, wu_f, wd_f, wg_b, wu_b, wd_b, sems, xbuf, xsems):
    i = pl.program_id(0)
    valid = valid_ref[i]
    expert = be_ref[i]
    slot = slot_ref[i]
    n_used = nu_ref[0]
    block_tiles = MOE_TILE * ROW_SUBLANES

    def row_copy(step):
        ring = step % X_RING
        src = xs_hbm.at[pl.ds(pl.multiple_of(step * block_tiles, block_tiles), block_tiles)]
        return pltpu.make_async_copy(src, xbuf.at[ring], xsems.at[ring])

    @pl.when(i == 0)
    def _():
        for s in range(X_RING - 1):
            @pl.when(s < n_used)
            def _(s=s):
                row_copy(s).start()

    @pl.when(i + (X_RING - 1) < n_used)
    def _():
        row_copy(i + (X_RING - 1)).start()

    @pl.when(i < n_used)
    def _():
        row_copy(i).wait()

    x_ref = xbuf.at[i % X_RING]

    def weight_copies(e, s):
        return [pltpu.make_async_copy(src.at[e], dst.at[s], sems.at[s, j])
                for j, (src, dst) in enumerate(((wg_hbm, wg_f), (wu_hbm, wu_f), (wd_hbm, wd_f)))]

    @pl.when(i == 0)
    def _():
        for cp in weight_copies(expert, slot):
            cp.start()

    @pl.when((i == 0) | (expert != be_ref[jnp.maximum(i - 1, 0)]))
    def _():
        for cp in weight_copies(expert, slot):
            cp.wait()

        @pl.when(nxt_ref[i] != expert)
        def _():
            for cp in weight_copies(nxt_ref[i], 1 - slot):
                cp.start()

        wg_b[...] = wg_f[slot].astype(BF16)
        wu_b[...] = wu_f[slot].astype(BF16)
        wd_b[...] = wd_f[slot].astype(BF16)

    def compute(n_rows):
        tiles = pl.ds(0, n_rows * ROW_SUBLANES)
        lo, hi = _unpack_halves(_load_row_tiles(x_ref.at[tiles]))
        y = _swiglu(lo.astype(BF16), hi.astype(BF16), wg_b, wu_b, wd_b)
        _store_row_tiles(y_ref.at[tiles], _pack_halves(y[:, :HALF], y[:, HALF:]))

    quarter = MOE_TILE // 4
    for n in range(1, 5):
        @pl.when((valid > (n - 1) * quarter) & (valid <= n * quarter))
        def _(n=n):
            compute(n * quarter)


def _experts(xs, block_expert, n_used, valid, next_expert, slot, w_eg, w_eu, w_ed):
    n_rows = xs.shape[0] // ROW_SUBLANES
    n_blocks = n_rows // MOE_TILE
    d, de = w_eg.shape[1], w_eg.shape[2]
    row_map = lambda i, be, nu, *_: (jnp.minimum(i, nu[0] - 1), 0)
    grid_spec = pltpu.PrefetchScalarGridSpec(
        num_scalar_prefetch=5,
        grid=(n_blocks,),
        in_specs=[pl.BlockSpec(memory_space=pl.ANY)] * 4,
        out_specs=pl.BlockSpec((MOE_TILE * ROW_SUBLANES, LANES), row_map),
        scratch_shapes=[
            pltpu.VMEM((2, d, de), F32),
            pltpu.VMEM((2, d, de), F32),
            pltpu.VMEM((2, de, d), F32),
            pltpu.VMEM((d, de), BF16),
            pltpu.VMEM((d, de), BF16),
            pltpu.VMEM((de, d), BF16),
            pltpu.SemaphoreType.DMA((2, 3)),
            pltpu.VMEM((X_RING, MOE_TILE * ROW_SUBLANES, LANES), U32),
            pltpu.SemaphoreType.DMA((X_RING,)),
        ],
    )
    return pl.pallas_call(
        _expert_body,
        grid_spec=grid_spec,
        out_shape=jax.ShapeDtypeStruct((n_rows * ROW_SUBLANES, LANES), U32),
        compiler_params=_cparams(("arbitrary",)),
        name="experts",
    )(block_expert, n_used, valid, next_expert, slot, xs, w_eg, w_eu, w_ed)


def _sc_mesh():
    return plsc.VectorSubcoreMesh(core_axis_name="c", subcore_axis_name="s")


def _sc_worker():
    return lax.axis_index("s") * SC_CORES + lax.axis_index("c")


def _sc_gather_rows(table, idx):
    n_idx = idx.shape[0]
    per_worker = n_idx // SC_WORKERS
    n_chunks = per_worker // SC_CHUNK
    assert per_worker * SC_WORKERS == n_idx and n_chunks * SC_CHUNK == per_worker and n_chunks % 2 == 0
    row_shape = table.shape[1:]

    @functools.partial(
        pl.kernel, mesh=_sc_mesh(),
        out_type=jax.ShapeDtypeStruct((n_idx,) + row_shape, table.dtype),
        scratch_types=[
            pltpu.VMEM((per_worker,), I32),
            pltpu.VMEM((SC_CHUNK,) + row_shape, table.dtype),
            pltpu.VMEM((SC_CHUNK,) + row_shape, table.dtype),
        ] + [pltpu.SemaphoreType.DMA] * 4,
    )
    def gather(table_hbm, idx_hbm, out_hbm, idx_v, buf0, buf1, g0, g1, w0, w1):
        bufs, gsem, wsem = (buf0, buf1), (g0, g1), (w0, w1)
        base = _sc_worker() * per_worker
        pltpu.sync_copy(idx_hbm.at[pl.ds(base, per_worker)], idx_v)

        def fetch(j, b):
            return pltpu.make_async_copy(table_hbm.at[idx_v.at[pl.ds(j * SC_CHUNK, SC_CHUNK)]], bufs[b], gsem[b])

        def flush(j, b):
            return pltpu.make_async_copy(bufs[b], out_hbm.at[pl.ds(base + j * SC_CHUNK, SC_CHUNK)], wsem[b])

        fetch(0, 0).start()
        fetch(0, 0).wait()
        fetch(1, 1).start()
        flush(0, 0).start()

        @pl.loop(1, n_chunks - 1, step=2)
        def _(j):
            for off in range(2):
                jj, b = j + off, (1 + off) % 2
                fetch(jj, b).wait()
                flush(jj - 1, 1 - b).wait()
                fetch(jj + 1, 1 - b).start()
                flush(jj, b).start()

        fetch(n_chunks - 1, 1).wait()
        flush(n_chunks - 1, 1).start()
        flush(n_chunks - 2, 0).wait()
        flush(n_chunks - 1, 1).wait()

    return gather(table, idx)


def _sc_scatter_rows(rows, dest_t, n_out):
    n_rows = rows.shape[0]
    per_worker = n_rows // SC_WORKERS
    n_chunks = per_worker // SC_CHUNK
    assert per_worker * SC_WORKERS == n_rows and n_chunks * SC_CHUNK == per_worker
    row_shape = rows.shape[1:]
    idx_w = dest_t.reshape(TOP_K, SC_WORKERS, n_chunks, SC_CHUNK).transpose(1, 2, 0, 3)
    idx_w = idx_w.reshape(SC_WORKERS, n_chunks * TOP_K, SC_CHUNK)

    @functools.partial(
        pl.kernel, mesh=_sc_mesh(),
        out_type=jax.ShapeDtypeStruct((n_out,) + row_shape, rows.dtype),
        scratch_types=[
            pltpu.VMEM((n_chunks * TOP_K, SC_CHUNK), I32),
            pltpu.VMEM((SC_CHUNK,) + row_shape, rows.dtype),
            pltpu.VMEM((SC_CHUNK,) + row_shape, rows.dtype),
        ] + [pltpu.SemaphoreType.DMA] * 4,
    )
    def scatter(rows_hbm, idx_hbm, out_hbm, idx_v, buf0, buf1, r0, r1, s0, s1):
        bufs, rsem, ssem = (buf0, buf1), (r0, r1), (s0, s1)
        worker = _sc_worker()
        base = worker * per_worker
        pltpu.sync_copy(idx_hbm.at[worker], idx_v)

        def fetch(j, b):
            return pltpu.make_async_copy(rows_hbm.at[pl.ds(base + j * SC_CHUNK, SC_CHUNK)], bufs[b], rsem[b])

        def send(j, k, b):
            return pltpu.make_async_copy(bufs[b], out_hbm.at[idx_v.at[j * TOP_K + k]], ssem[b])

        fetch(0, 0).start()
        for j in range(n_chunks):
            b = j % 2
            fetch(j, b).wait()
            if j + 1 < n_chunks:
                if j >= 1:
                    for k in range(TOP_K):
                        send(j - 1, k, 1 - b).wait()
                fetch(j + 1, 1 - b).start()
            for k in range(TOP_K):
                send(j, k, b).start()
        for j in range(max(n_chunks - 2, 0), n_chunks):
            for k in range(TOP_K):
                send(j, k, j % 2).wait()

    return scatter(rows, idx_w)


def _combine_body(yu_hbm, w_ref, shr_ref, x1_ref, gt2_ref, gpost_ref, *refs):
    o_ref, ybuf, ysems = refs[-3:]
    tt = x1_ref.shape[0]
    i = pl.program_id(0)
    tiles = pl.num_programs(0)
    block_tiles = tt * ROW_SUBLANES

    def row_copies(step):
        ring = step % Y_RING
        copies = []
        for k in range(TOP_K):
            start = pl.multiple_of((k * tiles + step) * block_tiles, block_tiles)
            copies.append(pltpu.make_async_copy(yu_hbm.at[pl.ds(start, block_tiles)], ybuf.at[ring, k],
                                                ysems.at[ring]))
        return copies

    @pl.when(i == 0)
    def _():
        for s in range(Y_RING - 1):
            @pl.when(s < tiles)
            def _(s=s):
                for cp in row_copies(s):
                    cp.start()

    @pl.when(i + (Y_RING - 1) < tiles)
    def _():
        for cp in row_copies(i + (Y_RING - 1)):
            cp.start()

    for cp in row_copies(i):
        cp.wait()
    yk_refs = [ybuf.at[i % Y_RING, k] for k in range(TOP_K)]
    w = w_ref[...]
    ssq = jnp.zeros((tt, 1), F32)
    for c in range(HALF // LANES):
        c_lo = slice(c * LANES, (c + 1) * LANES)
        c_hi = slice(HALF + c * LANES, HALF + (c + 1) * LANES)
        y_lo = shr_ref[:, c_lo].astype(F32)
        y_hi = shr_ref[:, c_hi].astype(F32)
        for k in range(TOP_K):
            lo, hi = _unpack_halves(yk_refs[k][pl.ds(c, tt, stride=ROW_SUBLANES), :])
            y_lo = y_lo + w[:, k:k + 1] * lo
            y_hi = y_hi + w[:, k:k + 1] * hi
        ssq = ssq + jnp.sum(y_lo * y_lo, axis=-1, keepdims=True) + jnp.sum(y_hi * y_hi, axis=-1, keepdims=True)
        o_ref[:, c_lo] = y_lo
        o_ref[:, c_hi] = y_hi
    scale = lax.rsqrt(ssq / D_MODEL + EPS)
    o_ref[...] = x1_ref[...] + gt2_ref[0] * (o_ref[...] * scale * gpost_ref[...])


def _combine(yu, wts, shared, x1, gt2, g_post, seq, part, n_parts, prev_out):
    rows, d = x1.shape
    tiles = rows // COMB_TT
    first = part * tiles
    tiles_per_b = seq // COMB_TT
    loc = lambda i: (i, 0)
    in_specs = [
        pl.BlockSpec(memory_space=pl.ANY),
        pl.BlockSpec((COMB_TT, LANES), loc),
        pl.BlockSpec((COMB_TT, d), loc),
        pl.BlockSpec((COMB_TT, d), loc),
        pl.BlockSpec((1, 1, d), lambda i: ((first + i) // tiles_per_b, 0, 0)),
        _resident((1, d)),
    ]
    args = [yu, wts, shared, x1, gt2, g_post]
    aliases = {}
    if prev_out is not None:
        aliases = {len(args): 0}
        in_specs.append(pl.BlockSpec(memory_space=pl.ANY))
        args.append(prev_out)
    return pl.pallas_call(
        _combine_body,
        grid=(tiles,),
        in_specs=in_specs,
        out_specs=pl.BlockSpec((COMB_TT, d), lambda i: (first + i, 0)),
        out_shape=jax.ShapeDtypeStruct((rows * n_parts, d), F32),
        scratch_shapes=[pltpu.VMEM((Y_RING, TOP_K, COMB_TT * ROW_SUBLANES, LANES), U32),
                        pltpu.SemaphoreType.DMA((Y_RING,))],
        input_output_aliases=aliases,
        compiler_params=_cparams(("arbitrary",)),
        name="combine",
    )(*args)


def kernel(x, c, ctx, c_ctx, w_mod, b_mod, norm_mix_pre, norm_mix_post, norm_ffn_pre, norm_ffn_post, w_in, w_a2_fwd, b_a_fwd, w_a2_bwd, b_a_bwd, gla_norm, w_pool, pool_scale, w_out, w_router, router_bias, w_exp_gate, w_exp_up, w_exp_down, w_sh_gate, w_sh_up, w_sh_down):
    batch, seq, d = x.shape
    n_ctx = ctx.shape[1]
    assert w_mod.shape[0] == 1 and d == D_MODEL
    assert seq % (2 * SUPER) == 0 and n_ctx % SUPER == 0 and seq % PROJ_TM == 0 and (batch * n_ctx) % PROJ_TM == 0
    rows = batch * seq

    mod_rows = 16
    c_all = jnp.concatenate([c, c_ctx[None, :], jnp.zeros((mod_rows - batch - 1, d), F32)], axis=0)
    mod_all = _modulation(c_all, w_mod[0], b_mod[0][None, :])
    sh1, sc1, gt1, sh2, sc2, gt2 = [m.reshape(batch, 1, d) for m in jnp.split(mod_all[:batch], 6, axis=-1)]
    csh1 = mod_all[batch, 0:d].reshape(1, 1, d)
    csc1 = mod_all[batch, d:2 * d].reshape(1, 1, d)

    kw, gw = GLA_KEY_WIDTH, GLA_WIDTH
    a0 = 2 * kw + 2 * gw
    w_in0 = w_in[0]
    w_bf = w_in0.astype(BF16)
    w_a = jnp.pad(w_bf[:, a0:a0 + 2 * GLA_RANK], ((0, 0), (0, LANES - 2 * GLA_RANK)))
    lat_pieces = [(w_bf, a0, 0), (w_bf[:, a0 + 2 * GLA_RANK:], POOL_WIDTH, 0)]
    ctx_pieces = [(w_bf, gw, 2 * kw // gw), (w_bf, kw, 1)]
    w2f = jnp.pad(w_a2_fwd[0], ((0, LANES - GLA_RANK), (0, 0))).astype(BF16)
    w2b = jnp.pad(w_a2_bwd[0], ((GLA_RANK, LANES - 2 * GLA_RANK), (0, 0))).astype(BF16)
    g_mix_pre = norm_mix_pre[0][None, :]
    w_router2 = jnp.concatenate(_bf16_terms(jnp.pad(w_router[0], ((0, 0), (0, LANES - N_EXPERTS)))), axis=1)

    u_ctx, a_ctx = _in_projection(ctx.reshape(batch * n_ctx, d), g_mix_pre, csc1, csh1, ctx_pieces, w_a,
                                  batch * n_ctx)
    u_lat, a_lat = _in_projection(x.reshape(rows, d), g_mix_pre, sc1, sh1, lat_pieces, w_a, seq)

    y_gla = _gla(u_lat, a_lat, u_ctx, a_ctx, w2f, b_a_fwd[0][None, :], w2b, b_a_bwd[0][None, :],
                 gla_norm[0][None, :], batch, seq, n_ctx)
    y_pool = _pool_mixer(u_lat, _col_window_matrices(), w_pool[0].astype(BF16), pool_scale[0][None, :],
                         batch, seq)

    w_out_bf = w_out[0].astype(BF16)
    shared_w = (w_sh_gate[0].astype(BF16), w_sh_up[0].astype(BF16), w_sh_down[0].astype(BF16))
    x2d = x.reshape(rows, d)
    part_rows = rows // MOE_PARTS
    n_blocks = part_rows * TOP_K // MOE_TILE + N_EXPERTS
    e_ids = jnp.arange(N_EXPERTS, dtype=I32)

    mixed = [_mix_and_route(y_gla, y_pool, x2d, gt1, sc2, sh2, norm_mix_post[0][None, :],
                            norm_ffn_pre[0][None, :], w_out_bf, w_router2, router_bias[0][:, None], seq,
                            p, MOE_PARTS) for p in range(MOE_PARTS)]
    out = None
    for p, (x1, h_packed, eidx_t, pos_t, wts, counts) in enumerate(mixed):
        shared = _shared_expert(h_packed, *shared_w)
        counts = counts[:, 0]
        padded = (counts + MOE_TILE - 1) // MOE_TILE * MOE_TILE
        pends = jnp.cumsum(padded)
        pstarts = pends - padded
        dest_t = _route_offsets(eidx_t, pos_t, pstarts.astype(I32))
        n_used = (pends[-1] // MOE_TILE).astype(I32)
        blk = jnp.minimum(jnp.arange(n_blocks, dtype=I32), n_used - 1)
        block_expert = jnp.sum((blk * MOE_TILE)[:, None] >= pends[None, :], axis=1).astype(I32)
        block_expert = jnp.minimum(block_expert, N_EXPERTS - 1)
        is_block_expert = block_expert[:, None] == e_ids[None, :]
        per_block = lambda v: jnp.sum(jnp.where(is_block_expert, v[None, :], 0), axis=1).astype(I32)
        valid = jnp.clip(per_block(pstarts + counts) - blk * MOE_TILE, 0, MOE_TILE)
        valid = jnp.where(jnp.arange(n_blocks, dtype=I32) < n_used, valid, 0).astype(I32)
        has_rows = padded > 0
        later = jnp.where((e_ids[None, :] > e_ids[:, None]) & has_rows[None, :], e_ids[None, :], N_EXPERTS)
        next_e = jnp.min(later, axis=1)
        next_e = jnp.where(next_e == N_EXPERTS, e_ids, next_e)
        slot_e = (jnp.cumsum(has_rows.astype(I32)) - has_rows.astype(I32)) % 2

        xs = _sc_scatter_rows(h_packed.reshape(-1, ROW_SUBLANES, LANES), dest_t, n_blocks * MOE_TILE)
        ys = _experts(xs.reshape(-1, LANES), block_expert, n_used.reshape(1), valid, per_block(next_e),
                      per_block(slot_e), w_exp_gate[0], w_exp_up[0], w_exp_down[0])
        yu = _sc_gather_rows(ys.reshape(-1, ROW_SUBLANES, LANES), dest_t.reshape(-1))
        out = _combine(yu.reshape(-1, LANES), wts, shared, x1, gt2, norm_ffn_post[0][None, :], seq,
                       p, MOE_PARTS, out)
    return out.reshape(batch, seq, d)
```

```python
import functools

import numpy as np
import jax
import jax.numpy as jnp
from jax import lax
from jax.experimental import pallas as pl
from jax.experimental.pallas import tpu as pltpu
from jax.experimental.pallas import tpu_sc as plsc

F32 = jnp.float32
BF16 = jnp.bfloat16
I32 = jnp.int32
U32 = jnp.uint32

D_MODEL = 2048
GRID_W = 64
GLA_HEADS = 4
GLA_DK = 128
GLA_DV = 256
GLA_KEY_WIDTH = GLA_HEADS * GLA_DK
GLA_WIDTH = GLA_HEADS * GLA_DV
GLA_RANK = 16
GLA_TAU = 16.0
GLA_CHUNK = 64
POOL_WIDTH = 1024
POOL_WINDOWS = (2, 4, 8, 16)
POOL_GROUP = 256
N_EXPERTS = 64
TOP_K = 8
N_GROUPS = 8
GROUP_SIZE = N_EXPERTS // N_GROUPS
TOPK_GROUPS = 4
ROUTED_SCALE = 2.5
EPS = 1e-6

HALF = D_MODEL // 2
SUPER = 4 * GLA_CHUNK
GLA_HPS = 4
POOL_PAD = 8 * GRID_W
VMEM_LIMIT = 56 * 1024 * 1024

MOD_TN = 1024
PROJ_TM = 512
PROJ_TN = 512
MIX_TM = 512
MOE_TILE = 512
X_RING = 3
Y_RING = 3
SHARED_TM = 1024
COMB_TT = 256
MOE_PARTS = 2
LANES = 128
ROW_SUBLANES = 8
SC_CORES = 2
SC_WORKERS = 32
SC_CHUNK = 32
OFFS_TN = 2048


def _cparams(sem):
    return pltpu.CompilerParams(dimension_semantics=sem, vmem_limit_bytes=VMEM_LIMIT)


def _resident(shape):
    nd = len(shape)
    return pl.BlockSpec(shape, lambda *_: (0,) * nd, pipeline_mode=pl.Buffered(1))


def _silu(v):
    return v * jax.nn.sigmoid(v)


def _pack_halves(lo, hi):
    lo_b = lax.bitcast_convert_type(lo.astype(BF16).astype(F32), U32)
    hi_b = lax.bitcast_convert_type(hi.astype(BF16).astype(F32), U32)
    return (hi_b & jnp.uint32(0xFFFF0000)) | (lo_b >> 16)


def _unpack_halves(p):
    lo = lax.bitcast_convert_type(p << 16, F32)
    hi = lax.bitcast_convert_type(p & jnp.uint32(0xFFFF0000), F32)
    return lo, hi


def _bf16_terms(x):
    hi = lax.bitcast_convert_type(lax.bitcast_convert_type(x, U32) & jnp.uint32(0xFFFF0000), F32)
    return hi.astype(BF16), (x - hi).astype(BF16)


def _store_row_tiles(ref, packed):
    n = packed.shape[0]
    for c in range(HALF // LANES):
        ref[pl.ds(c, n, stride=ROW_SUBLANES), :] = packed[:, c * LANES:(c + 1) * LANES]


def _load_row_tiles(ref):
    n = ref.shape[0] // ROW_SUBLANES
    return jnp.concatenate([ref[pl.ds(c, n, stride=ROW_SUBLANES), :] for c in range(HALF // LANES)], axis=1)


def _mod_body(c_ref, w_ref, b_ref, o_ref):
    s_hi, s_lo = _bf16_terms(_silu(c_ref[...]))
    w_hi, w_lo = _bf16_terms(w_ref[...])
    acc = jnp.dot(s_hi, w_hi, preferred_element_type=F32)
    acc = acc + jnp.dot(s_lo, w_hi, preferred_element_type=F32)
    acc = acc + jnp.dot(s_hi, w_lo, preferred_element_type=F32)
    o_ref[...] = acc + b_ref[...]


def _modulation(c_all, w_mod, b_mod):
    rows, d = c_all.shape
    n = w_mod.shape[1]
    return pl.pallas_call(
        _mod_body,
        grid=(n // MOD_TN,),
        in_specs=[
            pl.BlockSpec((rows, d), lambda j: (0, 0)),
            pl.BlockSpec((d, MOD_TN), lambda j: (0, j)),
            pl.BlockSpec((1, MOD_TN), lambda j: (0, j)),
        ],
        out_specs=pl.BlockSpec((rows, MOD_TN), lambda j: (0, j)),
        out_shape=jax.ShapeDtypeStruct((rows, n), F32),
        compiler_params=_cparams(("arbitrary",)),
        name="modulation",
    )(c_all, w_mod, b_mod)


def _rms_scale(x):
    return lax.rsqrt(jnp.mean(x * x, axis=-1, keepdims=True) + EPS)


def _inproj_body(x_ref, g_ref, sc_ref, sh_ref, *refs):
    w_refs, (wa_ref, o_ref, a_ref) = refs[:-3], refs[-3:]
    x = x_ref[...]
    h = x * _rms_scale(x) * g_ref[...]
    h = h * (1.0 + sc_ref[0]) + sh_ref[0]
    hb = h.astype(BF16)
    col = 0
    for w_ref in w_refs:
        for n in range(w_ref.shape[1] // PROJ_TN):
            cols = slice(n * PROJ_TN, (n + 1) * PROJ_TN)
            o_ref[:, col:col + PROJ_TN] = jnp.dot(hb, w_ref[:, cols], preferred_element_type=F32).astype(BF16)
            col += PROJ_TN
    a_ref[...] = jnp.dot(hb, wa_ref[...], preferred_element_type=F32)


def _in_projection(x2d, gain, sc, sh, pieces, w_a, rows_per_mod):
    rows, d = x2d.shape
    n_main = sum(width for _, width, _ in pieces)
    tiles_per_mod = rows_per_mod // PROJ_TM
    mod_map = lambda i: (i // tiles_per_mod, 0, 0)
    piece_specs = [pl.BlockSpec((d, width), functools.partial(lambda i, b: (0, b), b=block),
                                pipeline_mode=pl.Buffered(1)) for _, width, block in pieces]
    return pl.pallas_call(
        _inproj_body,
        grid=(rows // PROJ_TM,),
        in_specs=[
            pl.BlockSpec((PROJ_TM, d), lambda i: (i, 0)),
            _resident((1, d)),
            pl.BlockSpec((1, 1, d), mod_map),
            pl.BlockSpec((1, 1, d), mod_map),
            *piece_specs,
            _resident((d, LANES)),
        ],
        out_specs=[
            pl.BlockSpec((PROJ_TM, n_main), lambda i: (i, 0)),
            pl.BlockSpec((PROJ_TM, LANES), lambda i: (i, 0)),
        ],
        out_shape=[
            jax.ShapeDtypeStruct((rows, n_main), BF16),
            jax.ShapeDtypeStruct((rows, LANES), F32),
        ],
        compiler_params=_cparams(("arbitrary",)),
        name="in_projection",
    )(x2d, gain, sc, sh, *[w for w, _, _ in pieces], w_a)


def _log_sigmoid(z):
    return jnp.minimum(z, 0.0) - jnp.log1p(jnp.exp(-jnp.abs(z)))


def _gla_cumulative_decay(a, w2, ba, tri):
    z = jnp.dot(a.astype(BF16), w2, preferred_element_type=F32) + ba
    g = _log_sigmoid(z) * (1.0 / GLA_TAU)
    g_hi, g_lo = _bf16_terms(g)
    return jnp.dot(tri, g_hi, preferred_element_type=F32) + jnp.dot(tri, g_lo, preferred_element_type=F32)


def _gla_prep(q, k, G, reverse):
    nc = SUPER // GLA_CHUNK
    G = G.reshape(nc, GLA_CHUNK, GLA_DK)
    end_row = 0 if reverse else GLA_CHUNK - 1
    mid_row = GLA_CHUNK - 1 - GLA_CHUNK // 2 if reverse else GLA_CHUNK // 2
    g_end = G[:, end_row:end_row + 1, :]
    g_mid = G[:, mid_row:mid_row + 1, :]
    k4 = k.astype(F32).reshape(nc, GLA_CHUNK, GLA_DK)
    dec = jnp.broadcast_to(jnp.exp(g_end), (nc, ROW_SUBLANES, GLA_DK)).reshape(nc * ROW_SUBLANES, GLA_DK)
    flat = lambda t: t.reshape(SUPER, GLA_DK).astype(BF16)
    if q is None:
        return None, None, None, flat(k4 * jnp.exp(g_end - G)), dec
    q4 = q.astype(F32).reshape(nc, GLA_CHUNK, GLA_DK) * (GLA_DK ** -0.5)
    qg = q4 * jnp.exp(G - g_mid)
    kg = k4 * jnp.exp(g_mid - G)
    qe = qg * jnp.exp(g_mid)
    kd = kg * jnp.exp(g_end - g_mid)
    return flat(qg), flat(kg), flat(qe), flat(kd), dec


def _gla_apply(qg, kg, qe, kd, dec, v, mask, st_ref, reverse):
    nc = SUPER // GLA_CHUNK
    o = None
    if qg is not None:
        att = lax.dot_general(qg, kg, (((1,), (1,)), ((), ())), preferred_element_type=F32)
        att = jnp.where(mask, att, 0.0).astype(BF16)
        o = jnp.dot(att, v, preferred_element_type=F32)
    outs = [None] * nc
    order = range(nc - 1, -1, -1) if reverse else range(nc)
    for c in order:
        rows = slice(c * GLA_CHUNK, (c + 1) * GLA_CHUNK)
        st = st_ref[...]
        if qg is not None:
            inter = lax.dot_general(qe[rows], st.astype(BF16), (((1,), (1,)), ((), ())),
                                    preferred_element_type=F32)
            outs[c] = o[rows] + inter
        upd = lax.dot_general(v[rows], kd[rows], (((0,), (0,)), ((), ())), preferred_element_type=F32)
        st_ref[...] = st * dec[ROW_SUBLANES * c:ROW_SUBLANES * c + 1, :] + upd
    if qg is None:
        return None
    return jnp.concatenate(outs, axis=0)


def _gla_body(q_ref, k_ref, v_ref, r_ref, a_ref, kc_ref, vc_ref, ac_ref,
              w2f_ref, baf_ref, w2b_ref, bab_ref, gn_ref, y_ref, o_acc, st, ops_a, ops_b, dec_a, dec_b,
              *, n_ctx):
    n_sup = q_ref.shape[0] // SUPER
    row = lax.broadcasted_iota(I32, (SUPER, SUPER), 0)
    col = lax.broadcasted_iota(I32, (SUPER, SUPER), 1)
    same_chunk = (row // GLA_CHUNK) == (col // GLA_CHUNK)
    mask_f = same_chunk & (col <= row)
    mask_b = same_chunk & (col >= row)
    tri_f = jnp.where(mask_f, 1.0, 0.0).astype(BF16)
    tri_b = jnp.where(mask_b, 1.0, 0.0).astype(BF16)
    heads = range(GLA_HPS)
    kcol = [slice(h * GLA_DK, (h + 1) * GLA_DK) for h in heads]
    vcol = [slice(h * GLA_DV, (h + 1) * GLA_DV) for h in heads]
    dirs = ((False, w2f_ref, baf_ref, tri_f, mask_f), (True, w2b_ref, bab_ref, tri_b, mask_b))

    st[...] = jnp.zeros_like(st)
    n_csup = n_ctx // SUPER
    for s in range(n_csup):
        for d, (reverse, w2_ref, ba_ref, tri, mask) in enumerate(dirs):
            sc = n_csup - 1 - s if reverse else s
            rows = slice(sc * SUPER, (sc + 1) * SUPER)
            G = _gla_cumulative_decay(ac_ref[rows, :], w2_ref[...], ba_ref[...], tri)
            for h in heads:
                _, _, _, kd, dec = _gla_prep(None, kc_ref[rows, kcol[h]], G[:, kcol[h]], reverse)
                _gla_apply(None, None, None, kd, dec, vc_ref[rows, vcol[h]], mask, st.at[d, h], reverse)

    o_acc[...] = jnp.zeros_like(o_acc)

    def rows_of(i, reverse):
        sc = n_sup - 1 - i if reverse else i
        return pl.ds(pl.multiple_of(sc * SUPER, SUPER), SUPER)

    def prepare(i, ops, decs, d):
        reverse, w2_ref, ba_ref, tri, mask = dirs[d]
        rows = rows_of(i, reverse)
        G = _gla_cumulative_decay(a_ref[rows, :], w2_ref[...], ba_ref[...], tri)
        for h in heads:
            vals = _gla_prep(q_ref[rows, kcol[h]], k_ref[rows, kcol[h]], G[:, kcol[h]], reverse)
            for j in range(4):
                ops[GLA_HPS * d + h, j] = vals[j]
            decs[GLA_HPS * d + h] = vals[4]

    def apply(i, ops, decs, d):
        reverse, w2_ref, ba_ref, tri, mask = dirs[d]
        rows = rows_of(i, reverse)
        for h in heads:
            ci = GLA_HPS * d + h
            out = _gla_apply(ops[ci, 0], ops[ci, 1], ops[ci, 2], ops[ci, 3], decs[ci],
                             v_ref[rows, vcol[h]], mask, st.at[d, h], reverse)
            o_acc[rows, vcol[h]] += out

    for d in range(2):
        prepare(0, ops_a, dec_a, d)

    def step(j, carry):
        i = 2 * j
        for d in range(2):
            prepare(i + 1, ops_b, dec_b, d)
            apply(i, ops_a, dec_a, d)
        nxt = jnp.minimum(i + 2, n_sup - 1)
        for d in range(2):
            prepare(nxt, ops_a, dec_a, d)
            apply(i + 1, ops_b, dec_b, d)
        return carry

    lax.fori_loop(0, n_sup // 2, step, 0)

    for h in heads:
        o = o_acc[:, vcol[h]]
        o = o * _rms_scale(o) * gn_ref[:, vcol[h]]
        y_ref[:, vcol[h]] = (o * _silu(r_ref[:, vcol[h]].astype(F32))).astype(BF16)


def _gla(u_lat, a_lat, u_ctx, a_ctx, w2f, baf, w2b, bab, gla_norm, batch, seq, n_ctx):
    groups = GLA_HEADS // GLA_HPS
    kw, vw = GLA_HPS * GLA_DK, GLA_HPS * GLA_DV
    kb = GLA_KEY_WIDTH // kw
    vb = 2 * GLA_KEY_WIDTH // vw
    rb = vb + groups
    assert GLA_KEY_WIDTH % kw == 0 and (2 * GLA_KEY_WIDTH) % vw == 0 and GLA_WIDTH % kw == 0
    ckb = GLA_WIDTH // kw
    return pl.pallas_call(
        functools.partial(_gla_body, n_ctx=n_ctx),
        grid=(batch, groups),
        in_specs=[
            pl.BlockSpec((seq, kw), lambda b, h: (b, h)),
            pl.BlockSpec((seq, kw), lambda b, h: (b, kb + h)),
            pl.BlockSpec((seq, vw), lambda b, h: (b, vb + h)),
            pl.BlockSpec((seq, vw), lambda b, h: (b, rb + h)),
            pl.BlockSpec((seq, LANES), lambda b, h: (b, 0)),
            pl.BlockSpec((n_ctx, kw), lambda b, h: (b, ckb + h)),
            pl.BlockSpec((n_ctx, vw), lambda b, h: (b, h)),
            pl.BlockSpec((n_ctx, LANES), lambda b, h: (b, 0)),
            pl.BlockSpec((LANES, kw), lambda b, h: (0, h)),
            pl.BlockSpec((1, kw), lambda b, h: (0, h)),
            pl.BlockSpec((LANES, kw), lambda b, h: (0, h)),
            pl.BlockSpec((1, kw), lambda b, h: (0, h)),
            pl.BlockSpec((1, vw), lambda b, h: (0, h)),
        ],
        out_specs=pl.BlockSpec((seq, vw), lambda b, h: (b, h)),
        out_shape=jax.ShapeDtypeStruct((batch * seq, GLA_WIDTH), BF16),
        scratch_shapes=[
            pltpu.VMEM((seq, vw), F32),
            pltpu.VMEM((2, GLA_HPS, GLA_DV, GLA_DK), F32),
            pltpu.VMEM((2 * GLA_HPS, 4, SUPER, GLA_DK), BF16),
            pltpu.VMEM((2 * GLA_HPS, 4, SUPER, GLA_DK), BF16),
            pltpu.VMEM((2 * GLA_HPS, ROW_SUBLANES * (SUPER // GLA_CHUNK), GLA_DK), F32),
            pltpu.VMEM((2 * GLA_HPS, ROW_SUBLANES * (SUPER // GLA_CHUNK), GLA_DK), F32),
        ],
        compiler_params=_cparams(("arbitrary", "arbitrary")),
        name="gla",
    )(u_lat, u_lat, u_lat, u_lat, a_lat, u_ctx, u_ctx, a_ctx, w2f, baf, w2b, bab, gla_norm)


def _col_window_matrices():
    t = np.arange(SUPER)
    r, c = t // GRID_W, t % GRID_W
    mats = []
    for w in POOL_WINDOWS:
        lo = np.maximum(c - w // 2, 0)[:, None]
        hi = np.minimum(c + w // 2, GRID_W)[:, None]
        m = (r[:, None] == r[None, :]) & (c[None, :] >= lo) & (c[None, :] < hi)
        mats.append(m.astype(np.float32))
    return jnp.asarray(np.stack(mats), dtype=BF16)


def _pool_body(p_ref, cw_ref, wp_ref, ps_ref, y_ref, pad_ref):
    seq = p_ref.shape[0]
    n_rows = seq // GRID_W
    zeros = jnp.zeros((POOL_PAD, POOL_GROUP), F32)
    pad_ref[0:POOL_PAD, :] = zeros
    pad_ref[POOL_PAD + seq:POOL_PAD + seq + POOL_PAD, :] = zeros
    t = lax.broadcasted_iota(I32, (seq, POOL_GROUP), 0)
    r = t // GRID_W
    c = t % GRID_W
    for gi, w in enumerate(POOL_WINDOWS):
        cols = slice(gi * POOL_GROUP, (gi + 1) * POOL_GROUP)
        cw = cw_ref[gi]
        for j in range(seq // SUPER):
            rows = slice(j * SUPER, (j + 1) * SUPER)
            pad_ref[POOL_PAD + j * SUPER:POOL_PAD + (j + 1) * SUPER, :] = jnp.dot(
                cw, p_ref[rows, cols], preferred_element_type=F32)
        total = None
        for d in range(-(w // 2), w // 2):
            start = POOL_PAD + d * GRID_W
            part = pad_ref[start:start + seq, :]
            total = part if total is None else total + part
        cnt_r = jnp.minimum(r + w // 2, n_rows) - jnp.maximum(r - w // 2, 0)
        cnt_c = jnp.minimum(c + w // 2, GRID_W) - jnp.maximum(c - w // 2, 0)
        mean = total / (cnt_r * cnt_c).astype(F32)
        diff = (mean - p_ref[:, cols].astype(F32)).astype(BF16)
        y = jnp.dot(diff, wp_ref[gi], preferred_element_type=F32) * ps_ref[:, cols]
        y_ref[:, cols] = y.astype(BF16)


def _pool_mixer(u_lat, col_mats, w_pool, pool_scale, batch, seq):
    pb = (u_lat.shape[1] - POOL_WIDTH) // POOL_WIDTH
    ng = len(POOL_WINDOWS)
    return pl.pallas_call(
        _pool_body,
        grid=(batch,),
        in_specs=[
            pl.BlockSpec((seq, POOL_WIDTH), lambda b: (b, pb)),
            _resident((ng, SUPER, SUPER)),
            _resident((ng, POOL_GROUP, POOL_GROUP)),
            _resident((1, POOL_WIDTH)),
        ],
        out_specs=pl.BlockSpec((seq, POOL_WIDTH), lambda b: (b, 0)),
        out_shape=jax.ShapeDtypeStruct((batch * seq, POOL_WIDTH), BF16),
        scratch_shapes=[pltpu.VMEM((seq + 2 * POOL_PAD, POOL_GROUP), F32)],
        compiler_params=_cparams(("arbitrary",)),
        name="pool_mixer",
    )(u_lat, col_mats, w_pool, pool_scale)


def _first_index(hit, iota, size, axis):
    return jnp.min(jnp.where(hit, iota, size), axis=axis, keepdims=True)


def _mix_body(yg_ref, yp_ref, x_ref, gt1_ref, sc2_ref, sh2_ref, gpost_ref, gpre_ref, wout_ref,
              wr_ref, rb_ref, upper_ref,
              x1_ref, hp_ref, eidx_ref, pos_ref, wts_ref, cnt_ref, run_ref, wrow_ref, y_scr):
    tm = x_ref.shape[0]
    neg_inf = jnp.float32(-jnp.inf)
    step = pl.program_id(0)

    @pl.when(step == 0)
    def _():
        run_ref[...] = jnp.zeros_like(run_ref)
        wrow_ref[...] = jnp.zeros_like(wrow_ref)
        y_scr[...] = jnp.zeros_like(y_scr)

    y = y_scr[...]
    y_new = jnp.dot(yg_ref[...], wout_ref[0:GLA_WIDTH, :], preferred_element_type=F32)
    y_scr[...] = y_new + jnp.dot(yp_ref[...], wout_ref[GLA_WIDTH:, :], preferred_element_type=F32)
    x1 = x_ref[...] + gt1_ref[0] * (y * _rms_scale(y) * gpost_ref[...])
    x1_ref[...] = x1
    h = x1 * _rms_scale(x1) * gpre_ref[...]
    h = h * (1.0 + sc2_ref[0]) + sh2_ref[0]
    _store_row_tiles(hp_ref, _pack_halves(h[:, :HALF], h[:, HALF:]))

    h_hi, h_lo = _bf16_terms(h)
    both = jnp.dot(h_hi, wr_ref[...], preferred_element_type=F32)
    lt = both[:, :LANES] + both[:, LANES:] + jnp.dot(h_lo, wr_ref[:, :LANES], preferred_element_type=F32)
    logits = lt.T[0:N_EXPERTS, :]
    scores = jax.nn.sigmoid(logits)
    sel = scores + rb_ref[...]
    shape3 = (N_GROUPS, GROUP_SIZE, tm)
    sel3 = sel.reshape(shape3)
    i_in = lax.broadcasted_iota(I32, shape3, 1).astype(F32)
    m1 = jnp.max(sel3, axis=1, keepdims=True)
    f1 = _first_index(sel3 == m1, i_in, float(GROUP_SIZE), 1)
    m2 = jnp.max(jnp.where(i_in == f1, neg_inf, sel3), axis=1, keepdims=True)
    grp = jnp.broadcast_to(m1 + m2, shape3).reshape(N_EXPERTS, tm)
    i_e = lax.broadcasted_iota(I32, (N_EXPERTS, tm), 0)
    i_grp = (i_e // GROUP_SIZE).astype(F32)
    i_e = i_e.astype(F32)
    allowed = jnp.zeros((N_EXPERTS, tm), F32)
    for _ in range(TOPK_GROUPS):
        m = jnp.max(grp, axis=0, keepdims=True)
        pick = i_grp == _first_index(grp == m, i_grp, float(N_GROUPS), 0)
        allowed = jnp.where(pick, 1.0, allowed)
        grp = jnp.where(pick, neg_inf, grp)
    cand = jnp.where(allowed > 0.0, sel, neg_inf)
    onehot = jnp.zeros((N_EXPERTS, tm), F32)
    picks, wts = [], []
    for k in range(TOP_K):
        m = jnp.max(cand, axis=0, keepdims=True)
        f = _first_index(cand == m, i_e, float(N_EXPERTS), 0)
        pick = i_e == f
        picks.append(pick)
        eidx_ref[k:k + 1, :] = f.astype(I32)
        wts.append(jnp.sum(jnp.where(pick, scores, 0.0), axis=0, keepdims=True))
        onehot = jnp.where(pick, 1.0, onehot)
        cand = jnp.where(pick, neg_inf, cand)
    w_sum = wts[0]
    for k in range(1, TOP_K):
        w_sum = w_sum + wts[k]
    for k in range(TOP_K):
        wrow_ref[k:k + 1, :] = wts[k] / w_sum * ROUTED_SCALE
    wts_ref[...] = wrow_ref[...].T

    before = jnp.dot(onehot.astype(BF16), upper_ref[...], preferred_element_type=F32)
    before = before + run_ref[:, 0:1]
    for k in range(TOP_K):
        pos_ref[k:k + 1, :] = jnp.sum(jnp.where(picks[k], before, 0.0), axis=0, keepdims=True).astype(I32)
    counted = jnp.where(step > 0, 1.0, 0.0)
    run_ref[...] = run_ref[...] + counted * jnp.sum(onehot, axis=1, keepdims=True)
    cnt_ref[...] = run_ref[...].astype(I32)


def _mix_and_route(y_gla, y_pool, x2d, gt1, sc2, sh2, g_post, g_pre, w_out, w_router2, router_bias, seq,
                   part, n_parts):
    d = x2d.shape[1]
    rows = x2d.shape[0] // n_parts
    tiles = rows // MIX_TM
    first = part * tiles
    tiles_per_b = seq // MIX_TM
    proj = lambda i: (first + jnp.minimum(i, tiles - 1), 0)
    bmap = lambda i: ((first + jnp.maximum(i - 1, 0)) // tiles_per_b, 0, 0)
    xmap = lambda i: (first + jnp.maximum(i - 1, 0), 0)
    rmap = lambda i: (jnp.maximum(i - 1, 0), 0)
    tmap = lambda i: (0, jnp.maximum(i - 1, 0))
    upper = jnp.asarray(np.triu(np.ones((MIX_TM, MIX_TM), np.float32), 1), dtype=BF16)
    return pl.pallas_call(
        _mix_body,
        grid=(tiles + 1,),
        in_specs=[
            pl.BlockSpec((MIX_TM, GLA_WIDTH), proj),
            pl.BlockSpec((MIX_TM, POOL_WIDTH), proj),
            pl.BlockSpec((MIX_TM, d), xmap),
            pl.BlockSpec((1, 1, d), bmap),
            pl.BlockSpec((1, 1, d), bmap),
            pl.BlockSpec((1, 1, d), bmap),
            _resident((1, d)),
            _resident((1, d)),
            _resident((d, d)),
            _resident((d, 256)),
            _resident((N_EXPERTS, 1)),
            _resident((MIX_TM, MIX_TM)),
        ],
        out_specs=[
            pl.BlockSpec((MIX_TM, d), rmap),
            pl.BlockSpec((MIX_TM * ROW_SUBLANES, LANES), rmap),
            pl.BlockSpec((TOP_K, MIX_TM), tmap),
            pl.BlockSpec((TOP_K, MIX_TM), tmap),
            pl.BlockSpec((MIX_TM, LANES), rmap),
            pl.BlockSpec((N_EXPERTS, LANES), lambda i: (0, 0)),
        ],
        out_shape=[
            jax.ShapeDtypeStruct((rows, d), F32),
            jax.ShapeDtypeStruct((rows * ROW_SUBLANES, LANES), U32),
            jax.ShapeDtypeStruct((TOP_K, rows), I32),
            jax.ShapeDtypeStruct((TOP_K, rows), I32),
            jax.ShapeDtypeStruct((rows, LANES), F32),
            jax.ShapeDtypeStruct((N_EXPERTS, LANES), I32),
        ],
        scratch_shapes=[pltpu.VMEM((N_EXPERTS, LANES), F32), pltpu.VMEM((LANES, MIX_TM), F32),
                        pltpu.VMEM((MIX_TM, d), F32)],
        compiler_params=_cparams(("arbitrary",)),
        name="mix_and_route",
    )(y_gla, y_pool, x2d, gt1, sc2, sh2, g_post, g_pre, w_out, w_router2, router_bias, upper)


def _swiglu(lo, hi, wg_ref, wu_ref, wd_ref):
    g = jnp.dot(lo, wg_ref[:HALF, :], preferred_element_type=F32)
    g = g + jnp.dot(hi, wg_ref[HALF:, :], preferred_element_type=F32)
    u = jnp.dot(lo, wu_ref[:HALF, :], preferred_element_type=F32)
    u = u + jnp.dot(hi, wu_ref[HALF:, :], preferred_element_type=F32)
    act = (_silu(g) * u).astype(BF16)
    return jnp.dot(act, wd_ref[...], preferred_element_type=F32)


def _shared_body(hp_ref, wg_ref, wu_ref, wd_ref, o_ref):
    lo, hi = _unpack_halves(_load_row_tiles(hp_ref))
    o_ref[...] = _swiglu(lo.astype(BF16), hi.astype(BF16), wg_ref, wu_ref, wd_ref).astype(BF16)


def _shared_expert(h_packed, w_sg, w_su, w_sd):
    d, ds = w_sg.shape
    rows = h_packed.shape[0] // ROW_SUBLANES
    return pl.pallas_call(
        _shared_body,
        grid=(rows // SHARED_TM,),
        in_specs=[
            pl.BlockSpec((SHARED_TM * ROW_SUBLANES, LANES), lambda i: (i, 0)),
            _resident((d, ds)),
            _resident((d, ds)),
            _resident((ds, d)),
        ],
        out_specs=pl.BlockSpec((SHARED_TM, d), lambda i: (i, 0)),
        out_shape=jax.ShapeDtypeStruct((rows, d), BF16),
        compiler_params=_cparams(("arbitrary",)),
        name="shared_expert",
    )(h_packed, w_sg, w_su, w_sd)


def _offsets_body(ps_ref, e_ref, p_ref, d_ref):
    e = e_ref[...]
    d = p_ref[...]
    for x in range(N_EXPERTS):
        d = d + jnp.where(e == x, ps_ref[x], 0)
    d_ref[...] = d


def _route_offsets(eidx_t, pos_t, pstarts):
    k, rows = eidx_t.shape
    spec = pl.BlockSpec((k, OFFS_TN), lambda i, ps: (0, i))
    grid_spec = pltpu.PrefetchScalarGridSpec(
        num_scalar_prefetch=1, grid=(rows // OFFS_TN,), in_specs=[spec, spec], out_specs=spec)
    return pl.pallas_call(
        _offsets_body,
        grid_spec=grid_spec,
        out_shape=jax.ShapeDtypeStruct((k, rows), I32),
        compiler_params=_cparams(("arbitrary",)),
        name="route_offsets",
    )(pstarts, eidx_t, pos_t)


def _swiglu_f32w(lo, hi, wg_ref, wu_ref, wd_ref):
    kc = 512
    g = u = None
    for half, x in enumerate((lo, hi)):
        for c in range(HALF // kc):
            xc = x[:, c * kc:(c + 1) * kc]
            rows = pl.ds(half * HALF + c * kc, kc)
            gc = jnp.dot(xc, wg_ref[rows, :].astype(BF16), preferred_element_type=F32)
            uc = jnp.dot(xc, wu_ref[rows, :].astype(BF16), preferred_element_type=F32)
            g = gc if g is None else g + gc
            u = uc if u is None else u + uc
    act = (_silu(g) * u).astype(BF16)
    return jnp.dot(act, wd_ref[...].astype(BF16), preferred_element_type=F32)


def _expert_body(be_ref, nu_ref, valid_ref, nxt_ref, slot_ref, xs_hbm, wg_hbm, wu_hbm, wd_hbm, y_ref,
                 wg_f, wu_f, wd_f, sems, xbuf, xsems):
    i = pl.program_id(0)
    valid = valid_ref[i]
    expert = be_ref[i]
    slot = slot_ref[i]
    n_used = nu_ref[0]
    block_tiles = MOE_TILE * ROW_SUBLANES

    def row_copy(step):
        ring = step % X_RING
        src = xs_hbm.at[pl.ds(pl.multiple_of(step * block_tiles, block_tiles), block_tiles)]
        return pltpu.make_async_copy(src, xbuf.at[ring], xsems.at[ring])

    @pl.when(i == 0)
    def _():
        for s in range(X_RING - 1):
            @pl.when(s < n_used)
            def _(s=s):
                row_copy(s).start()

    @pl.when(i + (X_RING - 1) < n_used)
    def _():
        row_copy(i + (X_RING - 1)).start()

    @pl.when(i < n_used)
    def _():
        row_copy(i).wait()

    x_ref = xbuf.at[i % X_RING]

    def weight_copies(e, s):
        return [pltpu.make_async_copy(src.at[e], dst.at[s], sems.at[s, j])
                for j, (src, dst) in enumerate(((wg_hbm, wg_f), (wu_hbm, wu_f), (wd_hbm, wd_f)))]

    @pl.when(i == 0)
    def _():
        for cp in weight_copies(expert, slot):
            cp.start()

    @pl.when((i == 0) | (expert != be_ref[jnp.maximum(i - 1, 0)]))
    def _():
        for cp in weight_copies(expert, slot):
            cp.wait()

        @pl.when(nxt_ref[i] != expert)
        def _():
            for cp in weight_copies(nxt_ref[i], 1 - slot):
                cp.start()

    wg_ref, wu_ref, wd_ref = wg_f.at[slot], wu_f.at[slot], wd_f.at[slot]

    def compute(n_rows):
        tiles = pl.ds(0, n_rows * ROW_SUBLANES)
        lo, hi = _unpack_halves(_load_row_tiles(x_ref.at[tiles]))
        y = _swiglu_f32w(lo.astype(BF16), hi.astype(BF16), wg_ref, wu_ref, wd_ref)
        _store_row_tiles(y_ref.at[tiles], _pack_halves(y[:, :HALF], y[:, HALF:]))

    quarter = MOE_TILE // 4
    for n in range(1, 5):
        @pl.when((valid > (n - 1) * quarter) & (valid <= n * quarter))
        def _(n=n):
            compute(n * quarter)


def _experts(xs, block_expert, n_used, valid, next_expert, slot, w_eg, w_eu, w_ed):
    n_rows = xs.shape[0] // ROW_SUBLANES
    n_blocks = n_rows // MOE_TILE
    d, de = w_eg.shape[1], w_eg.shape[2]
    row_map = lambda i, be, nu, *_: (jnp.minimum(i, nu[0] - 1), 0)
    grid_spec = pltpu.PrefetchScalarGridSpec(
        num_scalar_prefetch=5,
        grid=(n_blocks,),
        in_specs=[pl.BlockSpec(memory_space=pl.ANY)] * 4,
        out_specs=pl.BlockSpec((MOE_TILE * ROW_SUBLANES, LANES), row_map),
        scratch_shapes=[
            pltpu.VMEM((2, d, de), F32),
            pltpu.VMEM((2, d, de), F32),
            pltpu.VMEM((2, de, d), F32),
            pltpu.SemaphoreType.DMA((2, 3)),
            pltpu.VMEM((X_RING, MOE_TILE * ROW_SUBLANES, LANES), U32),
            pltpu.SemaphoreType.DMA((X_RING,)),
        ],
    )
    return pl.pallas_call(
        _expert_body,
        grid_spec=grid_spec,
        out_shape=jax.ShapeDtypeStruct((n_rows * ROW_SUBLANES, LANES), U32),
        compiler_params=_cparams(("arbitrary",)),
        name="experts",
    )(block_expert, n_used, valid, next_expert, slot, xs, w_eg, w_eu, w_ed)


def _sc_mesh():
    return plsc.VectorSubcoreMesh(core_axis_name="c", subcore_axis_name="s")


def _sc_worker():
    return lax.axis_index("s") * SC_CORES + lax.axis_index("c")


def _sc_gather_rows(table, idx):
    n_idx = idx.shape[0]
    per_worker = n_idx // SC_WORKERS
    n_chunks = per_worker // SC_CHUNK
    assert per_worker * SC_WORKERS == n_idx and n_chunks * SC_CHUNK == per_worker and n_chunks % 2 == 0
    row_shape = table.shape[1:]

    @functools.partial(
        pl.kernel, mesh=_sc_mesh(),
        out_type=jax.ShapeDtypeStruct((n_idx,) + row_shape, table.dtype),
        scratch_types=[
            pltpu.VMEM((per_worker,), I32),
            pltpu.VMEM((SC_CHUNK,) + row_shape, table.dtype),
            pltpu.VMEM((SC_CHUNK,) + row_shape, table.dtype),
        ] + [pltpu.SemaphoreType.DMA] * 4,
    )
    def gather(table_hbm, idx_hbm, out_hbm, idx_v, buf0, buf1, g0, g1, w0, w1):
        bufs, gsem, wsem = (buf0, buf1), (g0, g1), (w0, w1)
        base = _sc_worker() * per_worker
        pltpu.sync_copy(idx_hbm.at[pl.ds(base, per_worker)], idx_v)

        def fetch(j, b):
            return pltpu.make_async_copy(table_hbm.at[idx_v.at[pl.ds(j * SC_CHUNK, SC_CHUNK)]], bufs[b], gsem[b])

        def flush(j, b):
            return pltpu.make_async_copy(bufs[b], out_hbm.at[pl.ds(base + j * SC_CHUNK, SC_CHUNK)], wsem[b])

        fetch(0, 0).start()
        fetch(0, 0).wait()
        fetch(1, 1).start()
        flush(0, 0).start()

        @pl.loop(1, n_chunks - 1, step=2)
        def _(j):
            for off in range(2):
                jj, b = j + off, (1 + off) % 2
                fetch(jj, b).wait()
                flush(jj - 1, 1 - b).wait()
                fetch(jj + 1, 1 - b).start()
                flush(jj, b).start()

        fetch(n_chunks - 1, 1).wait()
        flush(n_chunks - 1, 1).start()
        flush(n_chunks - 2, 0).wait()
        flush(n_chunks - 1, 1).wait()

    return gather(table, idx)


def _sc_scatter_rows(rows, dest_t, n_out):
    n_rows = rows.shape[0]
    per_worker = n_rows // SC_WORKERS
    n_chunks = per_worker // SC_CHUNK
    assert per_worker * SC_WORKERS == n_rows and n_chunks * SC_CHUNK == per_worker
    row_shape = rows.shape[1:]
    idx_w = dest_t.reshape(TOP_K, SC_WORKERS, n_chunks, SC_CHUNK).transpose(1, 2, 0, 3)
    idx_w = idx_w.reshape(SC_WORKERS, n_chunks * TOP_K, SC_CHUNK)

    @functools.partial(
        pl.kernel, mesh=_sc_mesh(),
        out_type=jax.ShapeDtypeStruct((n_out,) + row_shape, rows.dtype),
        scratch_types=[
            pltpu.VMEM((n_chunks * TOP_K, SC_CHUNK), I32),
            pltpu.VMEM((SC_CHUNK,) + row_shape, rows.dtype),
            pltpu.VMEM((SC_CHUNK,) + row_shape, rows.dtype),
        ] + [pltpu.SemaphoreType.DMA] * 4,
    )
    def scatter(rows_hbm, idx_hbm, out_hbm, idx_v, buf0, buf1, r0, r1, s0, s1):
        bufs, rsem, ssem = (buf0, buf1), (r0, r1), (s0, s1)
        worker = _sc_worker()
        base = worker * per_worker
        pltpu.sync_copy(idx_hbm.at[worker], idx_v)

        def fetch(j, b):
            return pltpu.make_async_copy(rows_hbm.at[pl.ds(base + j * SC_CHUNK, SC_CHUNK)], bufs[b], rsem[b])

        def send(j, k, b):
            return pltpu.make_async_copy(bufs[b], out_hbm.at[idx_v.at[j * TOP_K + k]], ssem[b])

        fetch(0, 0).start()
        for j in range(n_chunks):
            b = j % 2
            fetch(j, b).wait()
            if j + 1 < n_chunks:
                if j >= 1:
                    for k in range(TOP_K):
                        send(j - 1, k, 1 - b).wait()
                fetch(j + 1, 1 - b).start()
            for k in range(TOP_K):
                send(j, k, b).start()
        for j in range(max(n_chunks - 2, 0), n_chunks):
            for k in range(TOP_K):
                send(j, k, j % 2).wait()

    return scatter(rows, idx_w)


def _combine_body(yu_hbm, w_ref, shr_ref, x1_ref, gt2_ref, gpost_ref, *refs):
    o_ref, ybuf, ysems = refs[-3:]
    tt = x1_ref.shape[0]
    i = pl.program_id(0)
    tiles = pl.num_programs(0)
    block_tiles = tt * ROW_SUBLANES

    def row_copies(step):
        ring = step % Y_RING
        copies = []
        for k in range(TOP_K):
            start = pl.multiple_of((k * tiles + step) * block_tiles, block_tiles)
            copies.append(pltpu.make_async_copy(yu_hbm.at[pl.ds(start, block_tiles)], ybuf.at[ring, k],
                                                ysems.at[ring]))
        return copies

    @pl.when(i == 0)
    def _():
        for s in range(Y_RING - 1):
            @pl.when(s < tiles)
            def _(s=s):
                for cp in row_copies(s):
                    cp.start()

    @pl.when(i + (Y_RING - 1) < tiles)
    def _():
        for cp in row_copies(i + (Y_RING - 1)):
            cp.start()

    for cp in row_copies(i):
        cp.wait()
    yk_refs = [ybuf.at[i % Y_RING, k] for k in range(TOP_K)]
    w = w_ref[...]
    ssq = jnp.zeros((tt, 1), F32)
    for c in range(HALF // LANES):
        c_lo = slice(c * LANES, (c + 1) * LANES)
        c_hi = slice(HALF + c * LANES, HALF + (c + 1) * LANES)
        y_lo = shr_ref[:, c_lo].astype(F32)
        y_hi = shr_ref[:, c_hi].astype(F32)
        for k in range(TOP_K):
            lo, hi = _unpack_halves(yk_refs[k][pl.ds(c, tt, stride=ROW_SUBLANES), :])
            y_lo = y_lo + w[:, k:k + 1] * lo
            y_hi = y_hi + w[:, k:k + 1] * hi
        ssq = ssq + jnp.sum(y_lo * y_lo, axis=-1, keepdims=True) + jnp.sum(y_hi * y_hi, axis=-1, keepdims=True)
        o_ref[:, c_lo] = y_lo
        o_ref[:, c_hi] = y_hi
    scale = lax.rsqrt(ssq / D_MODEL + EPS)
    o_ref[...] = x1_ref[...] + gt2_ref[0] * (o_ref[...] * scale * gpost_ref[...])


def _combine(yu, wts, shared, x1, gt2, g_post, seq, part, n_parts, prev_out):
    rows, d = x1.shape
    tiles = rows // COMB_TT
    first = part * tiles
    tiles_per_b = seq // COMB_TT
    loc = lambda i: (i, 0)
    in_specs = [
        pl.BlockSpec(memory_space=pl.ANY),
        pl.BlockSpec((COMB_TT, LANES), loc),
        pl.BlockSpec((COMB_TT, d), loc),
        pl.BlockSpec((COMB_TT, d), loc),
        pl.BlockSpec((1, 1, d), lambda i: ((first + i) // tiles_per_b, 0, 0)),
        _resident((1, d)),
    ]
    args = [yu, wts, shared, x1, gt2, g_post]
    aliases = {}
    if prev_out is not None:
        aliases = {len(args): 0}
        in_specs.append(pl.BlockSpec(memory_space=pl.ANY))
        args.append(prev_out)
    return pl.pallas_call(
        _combine_body,
        grid=(tiles,),
        in_specs=in_specs,
        out_specs=pl.BlockSpec((COMB_TT, d), lambda i: (first + i, 0)),
        out_shape=jax.ShapeDtypeStruct((rows * n_parts, d), F32),
        scratch_shapes=[pltpu.VMEM((Y_RING, TOP_K, COMB_TT * ROW_SUBLANES, LANES), U32),
                        pltpu.SemaphoreType.DMA((Y_RING,))],
        input_output_aliases=aliases,
        compiler_params=_cparams(("arbitrary",)),
        name="combine",
    )(*args)


def kernel(x, c, ctx, c_ctx, w_mod, b_mod, norm_mix_pre, norm_mix_post, norm_ffn_pre, norm_ffn_post, w_in, w_a2_fwd, b_a_fwd, w_a2_bwd, b_a_bwd, gla_norm, w_pool, pool_scale, w_out, w_router, router_bias, w_exp_gate, w_exp_up, w_exp_down, w_sh_gate, w_sh_up, w_sh_down):
    batch, seq, d = x.shape
    n_ctx = ctx.shape[1]
    assert w_mod.shape[0] == 1 and d == D_MODEL
    assert seq % (2 * SUPER) == 0 and n_ctx % SUPER == 0 and seq % PROJ_TM == 0 and (batch * n_ctx) % PROJ_TM == 0
    rows = batch * seq

    mod_rows = 16
    c_all = jnp.concatenate([c, c_ctx[None, :], jnp.zeros((mod_rows - batch - 1, d), F32)], axis=0)
    mod_all = _modulation(c_all, w_mod[0], b_mod[0][None, :])
    sh1, sc1, gt1, sh2, sc2, gt2 = [m.reshape(batch, 1, d) for m in jnp.split(mod_all[:batch], 6, axis=-1)]
    csh1 = mod_all[batch, 0:d].reshape(1, 1, d)
    csc1 = mod_all[batch, d:2 * d].reshape(1, 1, d)

    kw, gw = GLA_KEY_WIDTH, GLA_WIDTH
    a0 = 2 * kw + 2 * gw
    w_in0 = w_in[0]
    w_bf = w_in0.astype(BF16)
    w_a = jnp.pad(w_bf[:, a0:a0 + 2 * GLA_RANK], ((0, 0), (0, LANES - 2 * GLA_RANK)))
    lat_pieces = [(w_bf, a0, 0), (w_bf[:, a0 + 2 * GLA_RANK:], POOL_WIDTH, 0)]
    ctx_pieces = [(w_bf, gw, 2 * kw // gw), (w_bf, kw, 1)]
    w2f = jnp.pad(w_a2_fwd[0], ((0, LANES - GLA_RANK), (0, 0))).astype(BF16)
    w2b = jnp.pad(w_a2_bwd[0], ((GLA_RANK, LANES - 2 * GLA_RANK), (0, 0))).astype(BF16)
    g_mix_pre = norm_mix_pre[0][None, :]
    w_router2 = jnp.concatenate(_bf16_terms(jnp.pad(w_router[0], ((0, 0), (0, LANES - N_EXPERTS)))), axis=1)

    u_ctx, a_ctx = _in_projection(ctx.reshape(batch * n_ctx, d), g_mix_pre, csc1, csh1, ctx_pieces, w_a,
                                  batch * n_ctx)
    u_lat, a_lat = _in_projection(x.reshape(rows, d), g_mix_pre, sc1, sh1, lat_pieces, w_a, seq)

    y_gla = _gla(u_lat, a_lat, u_ctx, a_ctx, w2f, b_a_fwd[0][None, :], w2b, b_a_bwd[0][None, :],
                 gla_norm[0][None, :], batch, seq, n_ctx)
    y_pool = _pool_mixer(u_lat, _col_window_matrices(), w_pool[0].astype(BF16), pool_scale[0][None, :],
                         batch, seq)

    w_out_bf = w_out[0].astype(BF16)
    shared_w = (w_sh_gate[0].astype(BF16), w_sh_up[0].astype(BF16), w_sh_down[0].astype(BF16))
    x2d = x.reshape(rows, d)
    part_rows = rows // MOE_PARTS
    n_blocks = part_rows * TOP_K // MOE_TILE + N_EXPERTS
    e_ids = jnp.arange(N_EXPERTS, dtype=I32)

    mixed = [_mix_and_route(y_gla, y_pool, x2d, gt1, sc2, sh2, norm_mix_post[0][None, :],
                            norm_ffn_pre[0][None, :], w_out_bf, w_router2, router_bias[0][:, None], seq,
                            p, MOE_PARTS) for p in range(MOE_PARTS)]
    out = None
    for p, (x1, h_packed, eidx_t, pos_t, wts, counts) in enumerate(mixed):
        shared = _shared_expert(h_packed, *shared_w)
        counts = counts[:, 0]
        padded = (counts + MOE_TILE - 1) // MOE_TILE * MOE_TILE
        pends = jnp.cumsum(padded)
        pstarts = pends - padded
        dest_t = _route_offsets(eidx_t, pos_t, pstarts.astype(I32))
        n_used = (pends[-1] // MOE_TILE).astype(I32)
        blk = jnp.minimum(jnp.arange(n_blocks, dtype=I32), n_used - 1)
        block_expert = jnp.sum((blk * MOE_TILE)[:, None] >= pends[None, :], axis=1).astype(I32)
        block_expert = jnp.minimum(block_expert, N_EXPERTS - 1)
        is_block_expert = block_expert[:, None] == e_ids[None, :]
        per_block = lambda v: jnp.sum(jnp.where(is_block_expert, v[None, :], 0), axis=1).astype(I32)
        valid = jnp.clip(per_block(pstarts + counts) - blk * MOE_TILE, 0, MOE_TILE)
        valid = jnp.where(jnp.arange(n_blocks, dtype=I32) < n_used, valid, 0).astype(I32)
        has_rows = padded > 0
        later = jnp.where((e_ids[None, :] > e_ids[:, None]) & has_rows[None, :], e_ids[None, :], N_EXPERTS)
        next_e = jnp.min(later, axis=1)
        next_e = jnp.where(next_e == N_EXPERTS, e_ids, next_e)
        slot_e = (jnp.cumsum(has_rows.astype(I32)) - has_rows.astype(I32)) % 2

        xs = _sc_scatter_rows(h_packed.reshape(-1, ROW_SUBLANES, LANES), dest_t, n_blocks * MOE_TILE)
        ys = _experts(xs.reshape(-1, LANES), block_expert, n_used.reshape(1), valid, per_block(next_e),
                      per_block(slot_e), w_exp_gate[0], w_exp_up[0], w_exp_down[0])
        yu = _sc_gather_rows(ys.reshape(-1, ROW_SUBLANES, LANES), dest_t.reshape(-1))
        out = _combine(yu.reshape(-1, LANES), wts, shared, x1, gt2, norm_ffn_post[0][None, :], seq,
                       p, MOE_PARTS, out)
    return out.reshape(batch, seq, d)
```

```python
import functools

import numpy as np
import jax
import jax.numpy as jnp
from jax import lax
from jax.experimental import pallas as pl
from jax.experimental.pallas import tpu as pltpu
from jax.experimental.pallas import tpu_sc as plsc

F32 = jnp.float32
BF16 = jnp.bfloat16
I32 = jnp.int32
U32 = jnp.uint32

D_MODEL = 2048
GRID_W = 64
GLA_HEADS = 4
GLA_DK = 128
GLA_DV = 256
GLA_KEY_WIDTH = GLA_HEADS * GLA_DK
GLA_WIDTH = GLA_HEADS * GLA_DV
GLA_RANK = 16
GLA_TAU = 16.0
GLA_CHUNK = 64
POOL_WIDTH = 1024
POOL_WINDOWS = (2, 4, 8, 16)
POOL_GROUP = 256
N_EXPERTS = 64
TOP_K = 8
N_GROUPS = 8
GROUP_SIZE = N_EXPERTS // N_GROUPS
TOPK_GROUPS = 4
ROUTED_SCALE = 2.5
EPS = 1e-6

HALF = D_MODEL // 2
SUPER = 4 * GLA_CHUNK
GLA_HPS = 4
POOL_PAD = 8 * GRID_W
VMEM_LIMIT = 56 * 1024 * 1024

MOD_TN = 1024
PROJ_TM = 512
PROJ_TN = 512
MIX_TM = 512
MOE_TILE = 512
X_RING = 3
W_RING = 3
Y_RING = 3
SHARED_TM = 1024
COMB_TT = 256
MOE_PARTS = 2
LANES = 128
ROW_SUBLANES = 8
SC_CORES = 2
SC_WORKERS = 32
SC_CHUNK = 32
OFFS_TN = 2048


def _cparams(sem):
    return pltpu.CompilerParams(dimension_semantics=sem, vmem_limit_bytes=VMEM_LIMIT)


def _resident(shape):
    nd = len(shape)
    return pl.BlockSpec(shape, lambda *_: (0,) * nd, pipeline_mode=pl.Buffered(1))


def _silu(v):
    return v * jax.nn.sigmoid(v)


def _pack_halves(lo, hi):
    lo_b = lax.bitcast_convert_type(lo.astype(BF16).astype(F32), U32)
    hi_b = lax.bitcast_convert_type(hi.astype(BF16).astype(F32), U32)
    return (hi_b & jnp.uint32(0xFFFF0000)) | (lo_b >> 16)


def _unpack_halves(p):
    lo = lax.bitcast_convert_type(p << 16, F32)
    hi = lax.bitcast_convert_type(p & jnp.uint32(0xFFFF0000), F32)
    return lo, hi


def _bf16_terms(x):
    hi = lax.bitcast_convert_type(lax.bitcast_convert_type(x, U32) & jnp.uint32(0xFFFF0000), F32)
    return hi.astype(BF16), (x - hi).astype(BF16)


def _store_row_tiles(ref, packed):
    n = packed.shape[0]
    for c in range(HALF // LANES):
        ref[pl.ds(c, n, stride=ROW_SUBLANES), :] = packed[:, c * LANES:(c + 1) * LANES]


def _load_row_tiles(ref):
    n = ref.shape[0] // ROW_SUBLANES
    return jnp.concatenate([ref[pl.ds(c, n, stride=ROW_SUBLANES), :] for c in range(HALF // LANES)], axis=1)


def _mod_body(c_ref, w_ref, b_ref, o_ref):
    s_hi, s_lo = _bf16_terms(_silu(c_ref[...]))
    w_hi, w_lo = _bf16_terms(w_ref[...])
    acc = jnp.dot(s_hi, w_hi, preferred_element_type=F32)
    acc = acc + jnp.dot(s_lo, w_hi, preferred_element_type=F32)
    acc = acc + jnp.dot(s_hi, w_lo, preferred_element_type=F32)
    o_ref[...] = acc + b_ref[...]


def _modulation(c_all, w_mod, b_mod):
    rows, d = c_all.shape
    n = w_mod.shape[1]
    return pl.pallas_call(
        _mod_body,
        grid=(n // MOD_TN,),
        in_specs=[
            pl.BlockSpec((rows, d), lambda j: (0, 0)),
            pl.BlockSpec((d, MOD_TN), lambda j: (0, j)),
            pl.BlockSpec((1, MOD_TN), lambda j: (0, j)),
        ],
        out_specs=pl.BlockSpec((rows, MOD_TN), lambda j: (0, j)),
        out_shape=jax.ShapeDtypeStruct((rows, n), F32),
        compiler_params=_cparams(("arbitrary",)),
        name="modulation",
    )(c_all, w_mod, b_mod)


def _rms_scale(x):
    return lax.rsqrt(jnp.mean(x * x, axis=-1, keepdims=True) + EPS)


def _inproj_body(x_ref, g_ref, sc_ref, sh_ref, *refs):
    w_refs, (wa_ref, o_ref, a_ref) = refs[:-3], refs[-3:]
    x = x_ref[...]
    h = x * _rms_scale(x) * g_ref[...]
    h = h * (1.0 + sc_ref[0]) + sh_ref[0]
    hb = h.astype(BF16)
    col = 0
    for w_ref in w_refs:
        for n in range(w_ref.shape[1] // PROJ_TN):
            cols = slice(n * PROJ_TN, (n + 1) * PROJ_TN)
            o_ref[:, col:col + PROJ_TN] = jnp.dot(hb, w_ref[:, cols], preferred_element_type=F32).astype(BF16)
            col += PROJ_TN
    a_ref[...] = jnp.dot(hb, wa_ref[...], preferred_element_type=F32)


def _in_projection(x2d, gain, sc, sh, pieces, w_a, rows_per_mod):
    rows, d = x2d.shape
    n_main = sum(width for _, width, _ in pieces)
    tiles_per_mod = rows_per_mod // PROJ_TM
    mod_map = lambda i: (i // tiles_per_mod, 0, 0)
    piece_specs = [pl.BlockSpec((d, width), functools.partial(lambda i, b: (0, b), b=block),
                                pipeline_mode=pl.Buffered(1)) for _, width, block in pieces]
    return pl.pallas_call(
        _inproj_body,
        grid=(rows // PROJ_TM,),
        in_specs=[
            pl.BlockSpec((PROJ_TM, d), lambda i: (i, 0)),
            _resident((1, d)),
            pl.BlockSpec((1, 1, d), mod_map),
            pl.BlockSpec((1, 1, d), mod_map),
            *piece_specs,
            _resident((d, LANES)),
        ],
        out_specs=[
            pl.BlockSpec((PROJ_TM, n_main), lambda i: (i, 0)),
            pl.BlockSpec((PROJ_TM, LANES), lambda i: (i, 0)),
        ],
        out_shape=[
            jax.ShapeDtypeStruct((rows, n_main), BF16),
            jax.ShapeDtypeStruct((rows, LANES), F32),
        ],
        compiler_params=_cparams(("arbitrary",)),
        name="in_projection",
    )(x2d, gain, sc, sh, *[w for w, _, _ in pieces], w_a)


def _log_sigmoid(z):
    return jnp.minimum(z, 0.0) - jnp.log1p(jnp.exp(-jnp.abs(z)))


def _gla_cumulative_decay(a, w2, ba, tri):
    z = jnp.dot(a.astype(BF16), w2, preferred_element_type=F32) + ba
    g = _log_sigmoid(z) * (1.0 / GLA_TAU)
    g_hi, g_lo = _bf16_terms(g)
    return jnp.dot(tri, g_hi, preferred_element_type=F32) + jnp.dot(tri, g_lo, preferred_element_type=F32)


def _gla_prep(q, k, G, reverse):
    nc = SUPER // GLA_CHUNK
    G = G.reshape(nc, GLA_CHUNK, GLA_DK)
    end_row = 0 if reverse else GLA_CHUNK - 1
    mid_row = GLA_CHUNK - 1 - GLA_CHUNK // 2 if reverse else GLA_CHUNK // 2
    g_end = G[:, end_row:end_row + 1, :]
    g_mid = G[:, mid_row:mid_row + 1, :]
    k4 = k.astype(F32).reshape(nc, GLA_CHUNK, GLA_DK)
    dec = jnp.broadcast_to(jnp.exp(g_end), (nc, ROW_SUBLANES, GLA_DK)).reshape(nc * ROW_SUBLANES, GLA_DK)
    flat = lambda t: t.reshape(SUPER, GLA_DK).astype(BF16)
    if q is None:
        return None, None, None, flat(k4 * jnp.exp(g_end - G)), dec
    q4 = q.astype(F32).reshape(nc, GLA_CHUNK, GLA_DK) * (GLA_DK ** -0.5)
    qg = q4 * jnp.exp(G - g_mid)
    kg = k4 * jnp.exp(g_mid - G)
    qe = qg * jnp.exp(g_mid)
    kd = kg * jnp.exp(g_end - g_mid)
    return flat(qg), flat(kg), flat(qe), flat(kd), dec


def _gla_apply(qg, kg, qe, kd, dec, v, mask, st_ref, reverse):
    nc = SUPER // GLA_CHUNK
    o = None
    if qg is not None:
        att = lax.dot_general(qg, kg, (((1,), (1,)), ((), ())), preferred_element_type=F32)
        att = jnp.where(mask, att, 0.0).astype(BF16)
        o = jnp.dot(att, v, preferred_element_type=F32)
    outs = [None] * nc
    order = range(nc - 1, -1, -1) if reverse else range(nc)
    for c in order:
        rows = slice(c * GLA_CHUNK, (c + 1) * GLA_CHUNK)
        st = st_ref[...]
        if qg is not None:
            inter = lax.dot_general(qe[rows], st.astype(BF16), (((1,), (1,)), ((), ())),
                                    preferred_element_type=F32)
            outs[c] = o[rows] + inter
        upd = lax.dot_general(v[rows], kd[rows], (((0,), (0,)), ((), ())), preferred_element_type=F32)
        st_ref[...] = st * dec[ROW_SUBLANES * c:ROW_SUBLANES * c + 1, :] + upd
    if qg is None:
        return None
    return jnp.concatenate(outs, axis=0)


def _gla_body(q_ref, k_ref, v_ref, r_ref, a_ref, kc_ref, vc_ref, ac_ref,
              w2f_ref, baf_ref, w2b_ref, bab_ref, gn_ref, y_ref, o_acc, st, ops_a, ops_b, dec_a, dec_b,
              *, n_ctx):
    n_sup = q_ref.shape[0] // SUPER
    row = lax.broadcasted_iota(I32, (SUPER, SUPER), 0)
    col = lax.broadcasted_iota(I32, (SUPER, SUPER), 1)
    same_chunk = (row // GLA_CHUNK) == (col // GLA_CHUNK)
    mask_f = same_chunk & (col <= row)
    mask_b = same_chunk & (col >= row)
    tri_f = jnp.where(mask_f, 1.0, 0.0).astype(BF16)
    tri_b = jnp.where(mask_b, 1.0, 0.0).astype(BF16)
    heads = range(GLA_HPS)
    kcol = [slice(h * GLA_DK, (h + 1) * GLA_DK) for h in heads]
    vcol = [slice(h * GLA_DV, (h + 1) * GLA_DV) for h in heads]
    dirs = ((False, w2f_ref, baf_ref, tri_f, mask_f), (True, w2b_ref, bab_ref, tri_b, mask_b))

    st[...] = jnp.zeros_like(st)
    n_csup = n_ctx // SUPER
    for s in range(n_csup):
        for d, (reverse, w2_ref, ba_ref, tri, mask) in enumerate(dirs):
            sc = n_csup - 1 - s if reverse else s
            rows = slice(sc * SUPER, (sc + 1) * SUPER)
            G = _gla_cumulative_decay(ac_ref[rows, :], w2_ref[...], ba_ref[...], tri)
            for h in heads:
                _, _, _, kd, dec = _gla_prep(None, kc_ref[rows, kcol[h]], G[:, kcol[h]], reverse)
                _gla_apply(None, None, None, kd, dec, vc_ref[rows, vcol[h]], mask, st.at[d, h], reverse)

    o_acc[...] = jnp.zeros_like(o_acc)

    def rows_of(i, reverse):
        sc = n_sup - 1 - i if reverse else i
        return pl.ds(pl.multiple_of(sc * SUPER, SUPER), SUPER)

    def prepare(i, ops, decs, d):
        reverse, w2_ref, ba_ref, tri, mask = dirs[d]
        rows = rows_of(i, reverse)
        G = _gla_cumulative_decay(a_ref[rows, :], w2_ref[...], ba_ref[...], tri)
        for h in heads:
            vals = _gla_prep(q_ref[rows, kcol[h]], k_ref[rows, kcol[h]], G[:, kcol[h]], reverse)
            for j in range(4):
                ops[GLA_HPS * d + h, j] = vals[j]
            decs[GLA_HPS * d + h] = vals[4]

    def apply(i, ops, decs, d):
        reverse, w2_ref, ba_ref, tri, mask = dirs[d]
        rows = rows_of(i, reverse)
        for h in heads:
            ci = GLA_HPS * d + h
            out = _gla_apply(ops[ci, 0], ops[ci, 1], ops[ci, 2], ops[ci, 3], decs[ci],
                             v_ref[rows, vcol[h]], mask, st.at[d, h], reverse)
            o_acc[rows, vcol[h]] += out

    for d in range(2):
        prepare(0, ops_a, dec_a, d)

    def step(j, carry):
        i = 2 * j
        for d in range(2):
            prepare(i + 1, ops_b, dec_b, d)
            apply(i, ops_a, dec_a, d)
        nxt = jnp.minimum(i + 2, n_sup - 1)
        for d in range(2):
            prepare(nxt, ops_a, dec_a, d)
            apply(i + 1, ops_b, dec_b, d)
        return carry

    lax.fori_loop(0, n_sup // 2, step, 0)

    for h in heads:
        o = o_acc[:, vcol[h]]
        o = o * _rms_scale(o) * gn_ref[:, vcol[h]]
        y_ref[:, vcol[h]] = (o * _silu(r_ref[:, vcol[h]].astype(F32))).astype(BF16)


def _gla(u_lat, a_lat, u_ctx, a_ctx, w2f, baf, w2b, bab, gla_norm, batch, seq, n_ctx):
    groups = GLA_HEADS // GLA_HPS
    kw, vw = GLA_HPS * GLA_DK, GLA_HPS * GLA_DV
    kb = GLA_KEY_WIDTH // kw
    vb = 2 * GLA_KEY_WIDTH // vw
    rb = vb + groups
    assert GLA_KEY_WIDTH % kw == 0 and (2 * GLA_KEY_WIDTH) % vw == 0 and GLA_WIDTH % kw == 0
    ckb = GLA_WIDTH // kw
    return pl.pallas_call(
        functools.partial(_gla_body, n_ctx=n_ctx),
        grid=(batch, groups),
        in_specs=[
            pl.BlockSpec((seq, kw), lambda b, h: (b, h)),
            pl.BlockSpec((seq, kw), lambda b, h: (b, kb + h)),
            pl.BlockSpec((seq, vw), lambda b, h: (b, vb + h)),
            pl.BlockSpec((seq, vw), lambda b, h: (b, rb + h)),
            pl.BlockSpec((seq, LANES), lambda b, h: (b, 0)),
            pl.BlockSpec((n_ctx, kw), lambda b, h: (b, ckb + h)),
            pl.BlockSpec((n_ctx, vw), lambda b, h: (b, h)),
            pl.BlockSpec((n_ctx, LANES), lambda b, h: (b, 0)),
            pl.BlockSpec((LANES, kw), lambda b, h: (0, h)),
            pl.BlockSpec((1, kw), lambda b, h: (0, h)),
            pl.BlockSpec((LANES, kw), lambda b, h: (0, h)),
            pl.BlockSpec((1, kw), lambda b, h: (0, h)),
            pl.BlockSpec((1, vw), lambda b, h: (0, h)),
        ],
        out_specs=pl.BlockSpec((seq, vw), lambda b, h: (b, h)),
        out_shape=jax.ShapeDtypeStruct((batch * seq, GLA_WIDTH), BF16),
        scratch_shapes=[
            pltpu.VMEM((seq, vw), F32),
            pltpu.VMEM((2, GLA_HPS, GLA_DV, GLA_DK), F32),
            pltpu.VMEM((2 * GLA_HPS, 4, SUPER, GLA_DK), BF16),
            pltpu.VMEM((2 * GLA_HPS, 4, SUPER, GLA_DK), BF16),
            pltpu.VMEM((2 * GLA_HPS, ROW_SUBLANES * (SUPER // GLA_CHUNK), GLA_DK), F32),
            pltpu.VMEM((2 * GLA_HPS, ROW_SUBLANES * (SUPER // GLA_CHUNK), GLA_DK), F32),
        ],
        compiler_params=_cparams(("arbitrary", "arbitrary")),
        name="gla",
    )(u_lat, u_lat, u_lat, u_lat, a_lat, u_ctx, u_ctx, a_ctx, w2f, baf, w2b, bab, gla_norm)


def _col_window_matrices():
    t = np.arange(SUPER)
    r, c = t // GRID_W, t % GRID_W
    mats = []
    for w in POOL_WINDOWS:
        lo = np.maximum(c - w // 2, 0)[:, None]
        hi = np.minimum(c + w // 2, GRID_W)[:, None]
        m = (r[:, None] == r[None, :]) & (c[None, :] >= lo) & (c[None, :] < hi)
        mats.append(m.astype(np.float32))
    return jnp.asarray(np.stack(mats), dtype=BF16)


def _pool_body(p_ref, cw_ref, wp_ref, ps_ref, y_ref, pad_ref):
    seq = p_ref.shape[0]
    n_rows = seq // GRID_W
    zeros = jnp.zeros((POOL_PAD, POOL_GROUP), F32)
    pad_ref[0:POOL_PAD, :] = zeros
    pad_ref[POOL_PAD + seq:POOL_PAD + seq + POOL_PAD, :] = zeros
    t = lax.broadcasted_iota(I32, (seq, POOL_GROUP), 0)
    r = t // GRID_W
    c = t % GRID_W
    for gi, w in enumerate(POOL_WINDOWS):
        cols = slice(gi * POOL_GROUP, (gi + 1) * POOL_GROUP)
        cw = cw_ref[gi]
        for j in range(seq // SUPER):
            rows = slice(j * SUPER, (j + 1) * SUPER)
            pad_ref[POOL_PAD + j * SUPER:POOL_PAD + (j + 1) * SUPER, :] = jnp.dot(
                cw, p_ref[rows, cols], preferred_element_type=F32)
        total = None
        for d in range(-(w // 2), w // 2):
            start = POOL_PAD + d * GRID_W
            part = pad_ref[start:start + seq, :]
            total = part if total is None else total + part
        cnt_r = jnp.minimum(r + w // 2, n_rows) - jnp.maximum(r - w // 2, 0)
        cnt_c = jnp.minimum(c + w // 2, GRID_W) - jnp.maximum(c - w // 2, 0)
        mean = total / (cnt_r * cnt_c).astype(F32)
        diff = (mean - p_ref[:, cols].astype(F32)).astype(BF16)
        y = jnp.dot(diff, wp_ref[gi], preferred_element_type=F32) * ps_ref[:, cols]
        y_ref[:, cols] = y.astype(BF16)


def _pool_mixer(u_lat, col_mats, w_pool, pool_scale, batch, seq):
    pb = (u_lat.shape[1] - POOL_WIDTH) // POOL_WIDTH
    ng = len(POOL_WINDOWS)
    return pl.pallas_call(
        _pool_body,
        grid=(batch,),
        in_specs=[
            pl.BlockSpec((seq, POOL_WIDTH), lambda b: (b, pb)),
            _resident((ng, SUPER, SUPER)),
            _resident((ng, POOL_GROUP, POOL_GROUP)),
            _resident((1, POOL_WIDTH)),
        ],
        out_specs=pl.BlockSpec((seq, POOL_WIDTH), lambda b: (b, 0)),
        out_shape=jax.ShapeDtypeStruct((batch * seq, POOL_WIDTH), BF16),
        scratch_shapes=[pltpu.VMEM((seq + 2 * POOL_PAD, POOL_GROUP), F32)],
        compiler_params=_cparams(("arbitrary",)),
        name="pool_mixer",
    )(u_lat, col_mats, w_pool, pool_scale)


def _first_index(hit, iota, size, axis):
    return jnp.min(jnp.where(hit, iota, size), axis=axis, keepdims=True)


def _mix_body(yg_ref, yp_ref, x_ref, gt1_ref, sc2_ref, sh2_ref, gpost_ref, gpre_ref, wout_ref,
              wr_ref, rb_ref, upper_ref,
              x1_ref, hp_ref, eidx_ref, pos_ref, wts_ref, cnt_ref, run_ref, wrow_ref, y_scr):
    tm = x_ref.shape[0]
    neg_inf = jnp.float32(-jnp.inf)
    step = pl.program_id(0)

    @pl.when(step == 0)
    def _():
        run_ref[...] = jnp.zeros_like(run_ref)
        wrow_ref[...] = jnp.zeros_like(wrow_ref)
        y_scr[...] = jnp.zeros_like(y_scr)

    y = y_scr[...]
    y_new = jnp.dot(yg_ref[...], wout_ref[0:GLA_WIDTH, :], preferred_element_type=F32)
    y_scr[...] = y_new + jnp.dot(yp_ref[...], wout_ref[GLA_WIDTH:, :], preferred_element_type=F32)
    x1 = x_ref[...] + gt1_ref[0] * (y * _rms_scale(y) * gpost_ref[...])
    x1_ref[...] = x1
    h = x1 * _rms_scale(x1) * gpre_ref[...]
    h = h * (1.0 + sc2_ref[0]) + sh2_ref[0]
    _store_row_tiles(hp_ref, _pack_halves(h[:, :HALF], h[:, HALF:]))

    h_hi, h_lo = _bf16_terms(h)
    both = jnp.dot(h_hi, wr_ref[...], preferred_element_type=F32)
    lt = both[:, :LANES] + both[:, LANES:] + jnp.dot(h_lo, wr_ref[:, :LANES], preferred_element_type=F32)
    logits = lt.T[0:N_EXPERTS, :]
    scores = jax.nn.sigmoid(logits)
    sel = scores + rb_ref[...]
    shape3 = (N_GROUPS, GROUP_SIZE, tm)
    sel3 = sel.reshape(shape3)
    i_in = lax.broadcasted_iota(I32, shape3, 1).astype(F32)
    m1 = jnp.max(sel3, axis=1, keepdims=True)
    f1 = _first_index(sel3 == m1, i_in, float(GROUP_SIZE), 1)
    m2 = jnp.max(jnp.where(i_in == f1, neg_inf, sel3), axis=1, keepdims=True)
    grp = jnp.broadcast_to(m1 + m2, shape3).reshape(N_EXPERTS, tm)
    i_e = lax.broadcasted_iota(I32, (N_EXPERTS, tm), 0)
    i_grp = (i_e // GROUP_SIZE).astype(F32)
    i_e = i_e.astype(F32)
    allowed = jnp.zeros((N_EXPERTS, tm), F32)
    for _ in range(TOPK_GROUPS):
        m = jnp.max(grp, axis=0, keepdims=True)
        pick = i_grp == _first_index(grp == m, i_grp, float(N_GROUPS), 0)
        allowed = jnp.where(pick, 1.0, allowed)
        grp = jnp.where(pick, neg_inf, grp)
    cand = jnp.where(allowed > 0.0, sel, neg_inf)
    onehot = jnp.zeros((N_EXPERTS, tm), F32)
    picks, wts = [], []
    for k in range(TOP_K):
        m = jnp.max(cand, axis=0, keepdims=True)
        f = _first_index(cand == m, i_e, float(N_EXPERTS), 0)
        pick = i_e == f
        picks.append(pick)
        eidx_ref[k:k + 1, :] = f.astype(I32)
        wts.append(jnp.sum(jnp.where(pick, scores, 0.0), axis=0, keepdims=True))
        onehot = jnp.where(pick, 1.0, onehot)
        cand = jnp.where(pick, neg_inf, cand)
    w_sum = wts[0]
    for k in range(1, TOP_K):
        w_sum = w_sum + wts[k]
    for k in range(TOP_K):
        wrow_ref[k:k + 1, :] = wts[k] / w_sum * ROUTED_SCALE
    wts_ref[...] = wrow_ref[...].T

    before = jnp.dot(onehot.astype(BF16), upper_ref[...], preferred_element_type=F32)
    before = before + run_ref[:, 0:1]
    for k in range(TOP_K):
        pos_ref[k:k + 1, :] = jnp.sum(jnp.where(picks[k], before, 0.0), axis=0, keepdims=True).astype(I32)
    counted = jnp.where(step > 0, 1.0, 0.0)
    run_ref[...] = run_ref[...] + counted * jnp.sum(onehot, axis=1, keepdims=True)
    cnt_ref[...] = run_ref[...].astype(I32)


def _mix_and_route(y_gla, y_pool, x2d, gt1, sc2, sh2, g_post, g_pre, w_out, w_router2, router_bias, seq,
                   part, n_parts):
    d = x2d.shape[1]
    rows = x2d.shape[0] // n_parts
    tiles = rows // MIX_TM
    first = part * tiles
    tiles_per_b = seq // MIX_TM
    proj = lambda i: (first + jnp.minimum(i, tiles - 1), 0)
    bmap = lambda i: ((first + jnp.maximum(i - 1, 0)) // tiles_per_b, 0, 0)
    xmap = lambda i: (first + jnp.maximum(i - 1, 0), 0)
    rmap = lambda i: (jnp.maximum(i - 1, 0), 0)
    tmap = lambda i: (0, jnp.maximum(i - 1, 0))
    upper = jnp.asarray(np.triu(np.ones((MIX_TM, MIX_TM), np.float32), 1), dtype=BF16)
    return pl.pallas_call(
        _mix_body,
        grid=(tiles + 1,),
        in_specs=[
            pl.BlockSpec((MIX_TM, GLA_WIDTH), proj),
            pl.BlockSpec((MIX_TM, POOL_WIDTH), proj),
            pl.BlockSpec((MIX_TM, d), xmap),
            pl.BlockSpec((1, 1, d), bmap),
            pl.BlockSpec((1, 1, d), bmap),
            pl.BlockSpec((1, 1, d), bmap),
            _resident((1, d)),
            _resident((1, d)),
            _resident((d, d)),
            _resident((d, 256)),
            _resident((N_EXPERTS, 1)),
            _resident((MIX_TM, MIX_TM)),
        ],
        out_specs=[
            pl.BlockSpec((MIX_TM, d), rmap),
            pl.BlockSpec((MIX_TM * ROW_SUBLANES, LANES), rmap),
            pl.BlockSpec((TOP_K, MIX_TM), tmap),
            pl.BlockSpec((TOP_K, MIX_TM), tmap),
            pl.BlockSpec((MIX_TM, LANES), rmap),
            pl.BlockSpec((N_EXPERTS, LANES), lambda i: (0, 0)),
        ],
        out_shape=[
            jax.ShapeDtypeStruct((rows, d), F32),
            jax.ShapeDtypeStruct((rows * ROW_SUBLANES, LANES), U32),
            jax.ShapeDtypeStruct((TOP_K, rows), I32),
            jax.ShapeDtypeStruct((TOP_K, rows), I32),
            jax.ShapeDtypeStruct((rows, LANES), F32),
            jax.ShapeDtypeStruct((N_EXPERTS, LANES), I32),
        ],
        scratch_shapes=[pltpu.VMEM((N_EXPERTS, LANES), F32), pltpu.VMEM((LANES, MIX_TM), F32),
                        pltpu.VMEM((MIX_TM, d), F32)],
        compiler_params=_cparams(("arbitrary",)),
        name="mix_and_route",
    )(y_gla, y_pool, x2d, gt1, sc2, sh2, g_post, g_pre, w_out, w_router2, router_bias, upper)


def _swiglu(lo, hi, wg_ref, wu_ref, wd_ref):
    g = jnp.dot(lo, wg_ref[:HALF, :], preferred_element_type=F32)
    g = g + jnp.dot(hi, wg_ref[HALF:, :], preferred_element_type=F32)
    u = jnp.dot(lo, wu_ref[:HALF, :], preferred_element_type=F32)
    u = u + jnp.dot(hi, wu_ref[HALF:, :], preferred_element_type=F32)
    act = (_silu(g) * u).astype(BF16)
    return jnp.dot(act, wd_ref[...], preferred_element_type=F32)


def _shared_body(hp_ref, wg_ref, wu_ref, wd_ref, o_ref):
    lo, hi = _unpack_halves(_load_row_tiles(hp_ref))
    o_ref[...] = _swiglu(lo.astype(BF16), hi.astype(BF16), wg_ref, wu_ref, wd_ref).astype(BF16)


def _shared_expert(h_packed, w_sg, w_su, w_sd):
    d, ds = w_sg.shape
    rows = h_packed.shape[0] // ROW_SUBLANES
    return pl.pallas_call(
        _shared_body,
        grid=(rows // SHARED_TM,),
        in_specs=[
            pl.BlockSpec((SHARED_TM * ROW_SUBLANES, LANES), lambda i: (i, 0)),
            _resident((d, ds)),
            _resident((d, ds)),
            _resident((ds, d)),
        ],
        out_specs=pl.BlockSpec((SHARED_TM, d), lambda i: (i, 0)),
        out_shape=jax.ShapeDtypeStruct((rows, d), BF16),
        compiler_params=_cparams(("arbitrary",)),
        name="shared_expert",
    )(h_packed, w_sg, w_su, w_sd)


def _offsets_body(ps_ref, e_ref, p_ref, d_ref):
    e = e_ref[...]
    d = p_ref[...]
    for x in range(N_EXPERTS):
        d = d + jnp.where(e == x, ps_ref[x], 0)
    d_ref[...] = d


def _route_offsets(eidx_t, pos_t, pstarts):
    k, rows = eidx_t.shape
    spec = pl.BlockSpec((k, OFFS_TN), lambda i, ps: (0, i))
    grid_spec = pltpu.PrefetchScalarGridSpec(
        num_scalar_prefetch=1, grid=(rows // OFFS_TN,), in_specs=[spec, spec], out_specs=spec)
    return pl.pallas_call(
        _offsets_body,
        grid_spec=grid_spec,
        out_shape=jax.ShapeDtypeStruct((k, rows), I32),
        compiler_params=_cparams(("arbitrary",)),
        name="route_offsets",
    )(pstarts, eidx_t, pos_t)


def _swiglu_f32w(lo, hi, wg_ref, wu_ref, wd_ref):
    kc = 512
    g = u = None
    for half, x in enumerate((lo, hi)):
        for c in range(HALF // kc):
            xc = x[:, c * kc:(c + 1) * kc]
            rows = pl.ds(half * HALF + c * kc, kc)
            gc = jnp.dot(xc, wg_ref[rows, :].astype(BF16), preferred_element_type=F32)
            uc = jnp.dot(xc, wu_ref[rows, :].astype(BF16), preferred_element_type=F32)
            g = gc if g is None else g + gc
            u = uc if u is None else u + uc
    act = (_silu(g) * u).astype(BF16)
    return jnp.dot(act, wd_ref[...].astype(BF16), preferred_element_type=F32)


def _expert_body(be_ref, nu_ref, valid_ref, nxt_ref, nxt2_ref, slot_ref, xs_hbm, wg_hbm, wu_hbm, wd_hbm,
                 y_ref, wg_f, wu_f, wd_f, sems, xbuf, xsems):
    i = pl.program_id(0)
    valid = valid_ref[i]
    expert = be_ref[i]
    slot = slot_ref[i]
    n_used = nu_ref[0]
    block_tiles = MOE_TILE * ROW_SUBLANES

    def row_copy(step):
        ring = step % X_RING
        src = xs_hbm.at[pl.ds(pl.multiple_of(step * block_tiles, block_tiles), block_tiles)]
        return pltpu.make_async_copy(src, xbuf.at[ring], xsems.at[ring])

    @pl.when(i == 0)
    def _():
        for s in range(X_RING - 1):
            @pl.when(s < n_used)
            def _(s=s):
                row_copy(s).start()

    @pl.when(i + (X_RING - 1) < n_used)
    def _():
        row_copy(i + (X_RING - 1)).start()

    @pl.when(i < n_used)
    def _():
        row_copy(i).wait()

    x_ref = xbuf.at[i % X_RING]

    def weight_copies(e, s):
        return [pltpu.make_async_copy(src.at[e], dst.at[s], sems.at[s, j])
                for j, (src, dst) in enumerate(((wg_hbm, wg_f), (wu_hbm, wu_f), (wd_hbm, wd_f)))]

    nxt, nxt2 = nxt_ref[i], nxt2_ref[i]

    @pl.when(i == 0)
    def _():
        for cp in weight_copies(expert, slot):
            cp.start()

        @pl.when(nxt != expert)
        def _():
            for cp in weight_copies(nxt, (slot + 1) % W_RING):
                cp.start()

    @pl.when((i == 0) | (expert != be_ref[jnp.maximum(i - 1, 0)]))
    def _():
        for cp in weight_copies(expert, slot):
            cp.wait()

        @pl.when(nxt2 != nxt)
        def _():
            for cp in weight_copies(nxt2, (slot + W_RING - 1) % W_RING):
                cp.start()

    wg_ref, wu_ref, wd_ref = wg_f.at[slot], wu_f.at[slot], wd_f.at[slot]

    def compute(n_rows):
        tiles = pl.ds(0, n_rows * ROW_SUBLANES)
        lo, hi = _unpack_halves(_load_row_tiles(x_ref.at[tiles]))
        y = _swiglu_f32w(lo.astype(BF16), hi.astype(BF16), wg_ref, wu_ref, wd_ref)
        _store_row_tiles(y_ref.at[tiles], _pack_halves(y[:, :HALF], y[:, HALF:]))

    quarter = MOE_TILE // 4
    for n in range(1, 5):
        @pl.when((valid > (n - 1) * quarter) & (valid <= n * quarter))
        def _(n=n):
            compute(n * quarter)


def _experts(xs, block_expert, n_used, valid, next_expert, next2_expert, slot, w_eg, w_eu, w_ed):
    n_rows = xs.shape[0] // ROW_SUBLANES
    n_blocks = n_rows // MOE_TILE
    d, de = w_eg.shape[1], w_eg.shape[2]
    row_map = lambda i, be, nu, *_: (jnp.minimum(i, nu[0] - 1), 0)
    grid_spec = pltpu.PrefetchScalarGridSpec(
        num_scalar_prefetch=6,
        grid=(n_blocks,),
        in_specs=[pl.BlockSpec(memory_space=pl.ANY)] * 4,
        out_specs=pl.BlockSpec((MOE_TILE * ROW_SUBLANES, LANES), row_map),
        scratch_shapes=[
            pltpu.VMEM((W_RING, d, de), F32),
            pltpu.VMEM((W_RING, d, de), F32),
            pltpu.VMEM((W_RING, de, d), F32),
            pltpu.SemaphoreType.DMA((W_RING, 3)),
            pltpu.VMEM((X_RING, MOE_TILE * ROW_SUBLANES, LANES), U32),
            pltpu.SemaphoreType.DMA((X_RING,)),
        ],
    )
    return pl.pallas_call(
        _expert_body,
        grid_spec=grid_spec,
        out_shape=jax.ShapeDtypeStruct((n_rows * ROW_SUBLANES, LANES), U32),
        compiler_params=_cparams(("arbitrary",)),
        name="experts",
    )(block_expert, n_used, valid, next_expert, next2_expert, slot, xs, w_eg, w_eu, w_ed)


def _sc_mesh():
    return plsc.VectorSubcoreMesh(core_axis_name="c", subcore_axis_name="s")


def _sc_worker():
    return lax.axis_index("s") * SC_CORES + lax.axis_index("c")


def _sc_gather_rows(table, idx):
    n_idx = idx.shape[0]
    per_worker = n_idx // SC_WORKERS
    n_chunks = per_worker // SC_CHUNK
    assert per_worker * SC_WORKERS == n_idx and n_chunks * SC_CHUNK == per_worker and n_chunks % 2 == 0
    row_shape = table.shape[1:]

    @functools.partial(
        pl.kernel, mesh=_sc_mesh(),
        out_type=jax.ShapeDtypeStruct((n_idx,) + row_shape, table.dtype),
        scratch_types=[
            pltpu.VMEM((per_worker,), I32),
            pltpu.VMEM((SC_CHUNK,) + row_shape, table.dtype),
            pltpu.VMEM((SC_CHUNK,) + row_shape, table.dtype),
        ] + [pltpu.SemaphoreType.DMA] * 4,
    )
    def gather(table_hbm, idx_hbm, out_hbm, idx_v, buf0, buf1, g0, g1, w0, w1):
        bufs, gsem, wsem = (buf0, buf1), (g0, g1), (w0, w1)
        base = _sc_worker() * per_worker
        pltpu.sync_copy(idx_hbm.at[pl.ds(base, per_worker)], idx_v)

        def fetch(j, b):
            return pltpu.make_async_copy(table_hbm.at[idx_v.at[pl.ds(j * SC_CHUNK, SC_CHUNK)]], bufs[b], gsem[b])

        def flush(j, b):
            return pltpu.make_async_copy(bufs[b], out_hbm.at[pl.ds(base + j * SC_CHUNK, SC_CHUNK)], wsem[b])

        fetch(0, 0).start()
        fetch(0, 0).wait()
        fetch(1, 1).start()
        flush(0, 0).start()

        @pl.loop(1, n_chunks - 1, step=2)
        def _(j):
            for off in range(2):
                jj, b = j + off, (1 + off) % 2
                fetch(jj, b).wait()
                flush(jj - 1, 1 - b).wait()
                fetch(jj + 1, 1 - b).start()
                flush(jj, b).start()

        fetch(n_chunks - 1, 1).wait()
        flush(n_chunks - 1, 1).start()
        flush(n_chunks - 2, 0).wait()
        flush(n_chunks - 1, 1).wait()

    return gather(table, idx)


def _sc_scatter_rows(rows, dest_t, n_out):
    n_rows = rows.shape[0]
    per_worker = n_rows // SC_WORKERS
    n_chunks = per_worker // SC_CHUNK
    assert per_worker * SC_WORKERS == n_rows and n_chunks * SC_CHUNK == per_worker
    row_shape = rows.shape[1:]
    idx_w = dest_t.reshape(TOP_K, SC_WORKERS, n_chunks, SC_CHUNK).transpose(1, 2, 0, 3)
    idx_w = idx_w.reshape(SC_WORKERS, n_chunks * TOP_K, SC_CHUNK)

    @functools.partial(
        pl.kernel, mesh=_sc_mesh(),
        out_type=jax.ShapeDtypeStruct((n_out,) + row_shape, rows.dtype),
        scratch_types=[
            pltpu.VMEM((n_chunks * TOP_K, SC_CHUNK), I32),
            pltpu.VMEM((SC_CHUNK,) + row_shape, rows.dtype),
            pltpu.VMEM((SC_CHUNK,) + row_shape, rows.dtype),
        ] + [pltpu.SemaphoreType.DMA] * 4,
    )
    def scatter(rows_hbm, idx_hbm, out_hbm, idx_v, buf0, buf1, r0, r1, s0, s1):
        bufs, rsem, ssem = (buf0, buf1), (r0, r1), (s0, s1)
        worker = _sc_worker()
        base = worker * per_worker
        pltpu.sync_copy(idx_hbm.at[worker], idx_v)

        def fetch(j, b):
            return pltpu.make_async_copy(rows_hbm.at[pl.ds(base + j * SC_CHUNK, SC_CHUNK)], bufs[b], rsem[b])

        def send(j, k, b):
            return pltpu.make_async_copy(bufs[b], out_hbm.at[idx_v.at[j * TOP_K + k]], ssem[b])

        fetch(0, 0).start()
        for j in range(n_chunks):
            b = j % 2
            fetch(j, b).wait()
            if j + 1 < n_chunks:
                if j >= 1:
                    for k in range(TOP_K):
                        send(j - 1, k, 1 - b).wait()
                fetch(j + 1, 1 - b).start()
            for k in range(TOP_K):
                send(j, k, b).start()
        for j in range(max(n_chunks - 2, 0), n_chunks):
            for k in range(TOP_K):
                send(j, k, j % 2).wait()

    return scatter(rows, idx_w)


def _combine_body(yu_hbm, w_ref, shr_ref, x1_ref, gt2_ref, gpost_ref, *refs):
    o_ref, ybuf, ysems = refs[-3:]
    tt = x1_ref.shape[0]
    i = pl.program_id(0)
    tiles = pl.num_programs(0)
    block_tiles = tt * ROW_SUBLANES

    def row_copies(step):
        ring = step % Y_RING
        copies = []
        for k in range(TOP_K):
            start = pl.multiple_of((k * tiles + step) * block_tiles, block_tiles)
            copies.append(pltpu.make_async_copy(yu_hbm.at[pl.ds(start, block_tiles)], ybuf.at[ring, k],
                                                ysems.at[ring]))
        return copies

    @pl.when(i == 0)
    def _():
        for s in range(Y_RING - 1):
            @pl.when(s < tiles)
            def _(s=s):
                for cp in row_copies(s):
                    cp.start()

    @pl.when(i + (Y_RING - 1) < tiles)
    def _():
        for cp in row_copies(i + (Y_RING - 1)):
            cp.start()

    for cp in row_copies(i):
        cp.wait()
    yk_refs = [ybuf.at[i % Y_RING, k] for k in range(TOP_K)]
    w = w_ref[...]
    ssq = jnp.zeros((tt, 1), F32)
    for c in range(HALF // LANES):
        c_lo = slice(c * LANES, (c + 1) * LANES)
        c_hi = slice(HALF + c * LANES, HALF + (c + 1) * LANES)
        y_lo = shr_ref[:, c_lo].astype(F32)
        y_hi = shr_ref[:, c_hi].astype(F32)
        for k in range(TOP_K):
            lo, hi = _unpack_halves(yk_refs[k][pl.ds(c, tt, stride=ROW_SUBLANES), :])
            y_lo = y_lo + w[:, k:k + 1] * lo
            y_hi = y_hi + w[:, k:k + 1] * hi
        ssq = ssq + jnp.sum(y_lo * y_lo, axis=-1, keepdims=True) + jnp.sum(y_hi * y_hi, axis=-1, keepdims=True)
        o_ref[:, c_lo] = y_lo
        o_ref[:, c_hi] = y_hi
    scale = lax.rsqrt(ssq / D_MODEL + EPS)
    o_ref[...] = x1_ref[...] + gt2_ref[0] * (o_ref[...] * scale * gpost_ref[...])


def _combine(yu, wts, shared, x1, gt2, g_post, seq, part, n_parts, prev_out):
    rows, d = x1.shape
    tiles = rows // COMB_TT
    first = part * tiles
    tiles_per_b = seq // COMB_TT
    loc = lambda i: (i, 0)
    in_specs = [
        pl.BlockSpec(memory_space=pl.ANY),
        pl.BlockSpec((COMB_TT, LANES), loc),
        pl.BlockSpec((COMB_TT, d), loc),
        pl.BlockSpec((COMB_TT, d), loc),
        pl.BlockSpec((1, 1, d), lambda i: ((first + i) // tiles_per_b, 0, 0)),
        _resident((1, d)),
    ]
    args = [yu, wts, shared, x1, gt2, g_post]
    aliases = {}
    if prev_out is not None:
        aliases = {len(args): 0}
        in_specs.append(pl.BlockSpec(memory_space=pl.ANY))
        args.append(prev_out)
    return pl.pallas_call(
        _combine_body,
        grid=(tiles,),
        in_specs=in_specs,
        out_specs=pl.BlockSpec((COMB_TT, d), lambda i: (first + i, 0)),
        out_shape=jax.ShapeDtypeStruct((rows * n_parts, d), F32),
        scratch_shapes=[pltpu.VMEM((Y_RING, TOP_K, COMB_TT * ROW_SUBLANES, LANES), U32),
                        pltpu.SemaphoreType.DMA((Y_RING,))],
        input_output_aliases=aliases,
        compiler_params=_cparams(("arbitrary",)),
        name="combine",
    )(*args)


def kernel(x, c, ctx, c_ctx, w_mod, b_mod, norm_mix_pre, norm_mix_post, norm_ffn_pre, norm_ffn_post, w_in, w_a2_fwd, b_a_fwd, w_a2_bwd, b_a_bwd, gla_norm, w_pool, pool_scale, w_out, w_router, router_bias, w_exp_gate, w_exp_up, w_exp_down, w_sh_gate, w_sh_up, w_sh_down):
    batch, seq, d = x.shape
    n_ctx = ctx.shape[1]
    assert w_mod.shape[0] == 1 and d == D_MODEL
    assert seq % (2 * SUPER) == 0 and n_ctx % SUPER == 0 and seq % PROJ_TM == 0 and (batch * n_ctx) % PROJ_TM == 0
    rows = batch * seq

    mod_rows = 16
    c_all = jnp.concatenate([c, c_ctx[None, :], jnp.zeros((mod_rows - batch - 1, d), F32)], axis=0)
    mod_all = _modulation(c_all, w_mod[0], b_mod[0][None, :])
    sh1, sc1, gt1, sh2, sc2, gt2 = [m.reshape(batch, 1, d) for m in jnp.split(mod_all[:batch], 6, axis=-1)]
    csh1 = mod_all[batch, 0:d].reshape(1, 1, d)
    csc1 = mod_all[batch, d:2 * d].reshape(1, 1, d)

    kw, gw = GLA_KEY_WIDTH, GLA_WIDTH
    a0 = 2 * kw + 2 * gw
    w_in0 = w_in[0]
    w_bf = w_in0.astype(BF16)
    w_a = jnp.pad(w_bf[:, a0:a0 + 2 * GLA_RANK], ((0, 0), (0, LANES - 2 * GLA_RANK)))
    lat_pieces = [(w_bf, a0, 0), (w_bf[:, a0 + 2 * GLA_RANK:], POOL_WIDTH, 0)]
    ctx_pieces = [(w_bf, gw, 2 * kw // gw), (w_bf, kw, 1)]
    w2f = jnp.pad(w_a2_fwd[0], ((0, LANES - GLA_RANK), (0, 0))).astype(BF16)
    w2b = jnp.pad(w_a2_bwd[0], ((GLA_RANK, LANES - 2 * GLA_RANK), (0, 0))).astype(BF16)
    g_mix_pre = norm_mix_pre[0][None, :]
    w_router2 = jnp.concatenate(_bf16_terms(jnp.pad(w_router[0], ((0, 0), (0, LANES - N_EXPERTS)))), axis=1)

    u_ctx, a_ctx = _in_projection(ctx.reshape(batch * n_ctx, d), g_mix_pre, csc1, csh1, ctx_pieces, w_a,
                                  batch * n_ctx)
    u_lat, a_lat = _in_projection(x.reshape(rows, d), g_mix_pre, sc1, sh1, lat_pieces, w_a, seq)

    y_gla = _gla(u_lat, a_lat, u_ctx, a_ctx, w2f, b_a_fwd[0][None, :], w2b, b_a_bwd[0][None, :],
                 gla_norm[0][None, :], batch, seq, n_ctx)
    y_pool = _pool_mixer(u_lat, _col_window_matrices(), w_pool[0].astype(BF16), pool_scale[0][None, :],
                         batch, seq)

    w_out_bf = w_out[0].astype(BF16)
    shared_w = (w_sh_gate[0].astype(BF16), w_sh_up[0].astype(BF16), w_sh_down[0].astype(BF16))
    x2d = x.reshape(rows, d)
    part_rows = rows // MOE_PARTS
    n_blocks = part_rows * TOP_K // MOE_TILE + N_EXPERTS
    e_ids = jnp.arange(N_EXPERTS, dtype=I32)

    mixed = [_mix_and_route(y_gla, y_pool, x2d, gt1, sc2, sh2, norm_mix_post[0][None, :],
                            norm_ffn_pre[0][None, :], w_out_bf, w_router2, router_bias[0][:, None], seq,
                            p, MOE_PARTS) for p in range(MOE_PARTS)]
    out = None
    for p, (x1, h_packed, eidx_t, pos_t, wts, counts) in enumerate(mixed):
        shared = _shared_expert(h_packed, *shared_w)
        counts = counts[:, 0]
        padded = (counts + MOE_TILE - 1) // MOE_TILE * MOE_TILE
        pends = jnp.cumsum(padded)
        pstarts = pends - padded
        dest_t = _route_offsets(eidx_t, pos_t, pstarts.astype(I32))
        n_used = (pends[-1] // MOE_TILE).astype(I32)
        blk = jnp.minimum(jnp.arange(n_blocks, dtype=I32), n_used - 1)
        block_expert = jnp.sum((blk * MOE_TILE)[:, None] >= pends[None, :], axis=1).astype(I32)
        block_expert = jnp.minimum(block_expert, N_EXPERTS - 1)
        is_block_expert = block_expert[:, None] == e_ids[None, :]
        per_block = lambda v: jnp.sum(jnp.where(is_block_expert, v[None, :], 0), axis=1).astype(I32)
        valid = jnp.clip(per_block(pstarts + counts) - blk * MOE_TILE, 0, MOE_TILE)
        valid = jnp.where(jnp.arange(n_blocks, dtype=I32) < n_used, valid, 0).astype(I32)
        has_rows = padded > 0
        later = jnp.where((e_ids[None, :] > e_ids[:, None]) & has_rows[None, :], e_ids[None, :], N_EXPERTS)
        next_e = jnp.min(later, axis=1)
        next_e = jnp.where(next_e == N_EXPERTS, e_ids, next_e)
        next2_e = jnp.sum(jnp.where(next_e[:, None] == e_ids[None, :], next_e[None, :], 0), axis=1)
        slot_e = (jnp.cumsum(has_rows.astype(I32)) - has_rows.astype(I32)) % W_RING

        xs = _sc_scatter_rows(h_packed.reshape(-1, ROW_SUBLANES, LANES), dest_t, n_blocks * MOE_TILE)
        ys = _experts(xs.reshape(-1, LANES), block_expert, n_used.reshape(1), valid, per_block(next_e),
                      per_block(next2_e), per_block(slot_e), w_exp_gate[0], w_exp_up[0], w_exp_down[0])
        yu = _sc_gather_rows(ys.reshape(-1, ROW_SUBLANES, LANES), dest_t.reshape(-1))
        out = _combine(yu.reshape(-1, LANES), wts, shared, x1, gt2, norm_ffn_post[0][None, :], seq,
                       p, MOE_PARTS, out)
    return out.reshape(batch, seq, d)
```

```python
import functools

import numpy as np
import jax
import jax.numpy as jnp
from jax import lax
from jax.experimental import pallas as pl
from jax.experimental.pallas import tpu as pltpu
from jax.experimental.pallas import tpu_sc as plsc

F32 = jnp.float32
BF16 = jnp.bfloat16
I32 = jnp.int32
U32 = jnp.uint32

D_MODEL = 2048
GRID_W = 64
GLA_HEADS = 4
GLA_DK = 128
GLA_DV = 256
GLA_KEY_WIDTH = GLA_HEADS * GLA_DK
GLA_WIDTH = GLA_HEADS * GLA_DV
GLA_RANK = 16
GLA_TAU = 16.0
GLA_CHUNK = 64
POOL_WIDTH = 1024
POOL_WINDOWS = (2, 4, 8, 16)
POOL_GROUP = 256
N_EXPERTS = 64
TOP_K = 8
N_GROUPS = 8
GROUP_SIZE = N_EXPERTS // N_GROUPS
TOPK_GROUPS = 4
ROUTED_SCALE = 2.5
EPS = 1e-6

HALF = D_MODEL // 2
SUPER = 4 * GLA_CHUNK
GLA_HPS = 4
POOL_PAD = 8 * GRID_W
VMEM_LIMIT = 56 * 1024 * 1024

MOD_TN = 1024
PROJ_TM = 512
PROJ_TN = 512
MIX_TM = 512
MOE_TILE = 512
X_RING = 3
Y_RING = 3
SHARED_TM = 1024
COMB_TT = 256
MOE_PARTS = 2
LANES = 128
ROW_SUBLANES = 8
SC_CORES = 2
SC_WORKERS = 32
SC_CHUNK = 32
OFFS_TN = 2048


def _cparams(sem):
    return pltpu.CompilerParams(dimension_semantics=sem, vmem_limit_bytes=VMEM_LIMIT)


def _resident(shape):
    nd = len(shape)
    return pl.BlockSpec(shape, lambda *_: (0,) * nd, pipeline_mode=pl.Buffered(1))


def _silu(v):
    return v * jax.nn.sigmoid(v)


def _pack_halves(lo, hi):
    lo_b = lax.bitcast_convert_type(lo.astype(BF16).astype(F32), U32)
    hi_b = lax.bitcast_convert_type(hi.astype(BF16).astype(F32), U32)
    return (hi_b & jnp.uint32(0xFFFF0000)) | (lo_b >> 16)


def _unpack_halves(p):
    lo = lax.bitcast_convert_type(p << 16, F32)
    hi = lax.bitcast_convert_type(p & jnp.uint32(0xFFFF0000), F32)
    return lo, hi


def _bf16_terms(x):
    hi = lax.bitcast_convert_type(lax.bitcast_convert_type(x, U32) & jnp.uint32(0xFFFF0000), F32)
    return hi.astype(BF16), (x - hi).astype(BF16)


def _store_row_tiles(ref, packed):
    n = packed.shape[0]
    for c in range(HALF // LANES):
        ref[pl.ds(c, n, stride=ROW_SUBLANES), :] = packed[:, c * LANES:(c + 1) * LANES]


def _load_row_tiles(ref):
    n = ref.shape[0] // ROW_SUBLANES
    return jnp.concatenate([ref[pl.ds(c, n, stride=ROW_SUBLANES), :] for c in range(HALF // LANES)], axis=1)


def _mod_body(c_ref, w_ref, b_ref, o_ref):
    s_hi, s_lo = _bf16_terms(_silu(c_ref[...]))
    w_hi, w_lo = _bf16_terms(w_ref[...])
    acc = jnp.dot(s_hi, w_hi, preferred_element_type=F32)
    acc = acc + jnp.dot(s_lo, w_hi, preferred_element_type=F32)
    acc = acc + jnp.dot(s_hi, w_lo, preferred_element_type=F32)
    o_ref[...] = acc + b_ref[...]


def _modulation(c_all, w_mod, b_mod):
    rows, d = c_all.shape
    n = w_mod.shape[1]
    return pl.pallas_call(
        _mod_body,
        grid=(n // MOD_TN,),
        in_specs=[
            pl.BlockSpec((rows, d), lambda j: (0, 0)),
            pl.BlockSpec((d, MOD_TN), lambda j: (0, j)),
            pl.BlockSpec((1, MOD_TN), lambda j: (0, j)),
        ],
        out_specs=pl.BlockSpec((rows, MOD_TN), lambda j: (0, j)),
        out_shape=jax.ShapeDtypeStruct((rows, n), F32),
        compiler_params=_cparams(("arbitrary",)),
        name="modulation",
    )(c_all, w_mod, b_mod)


def _rms_scale(x):
    return lax.rsqrt(jnp.mean(x * x, axis=-1, keepdims=True) + EPS)


def _inproj_body(x_ref, g_ref, sc_ref, sh_ref, *refs):
    w_refs, (wa_ref, o_ref, a_ref) = refs[:-3], refs[-3:]
    x = x_ref[...]
    h = x * _rms_scale(x) * g_ref[...]
    h = h * (1.0 + sc_ref[0]) + sh_ref[0]
    hb = h.astype(BF16)
    col = 0
    for w_ref in w_refs:
        for n in range(w_ref.shape[1] // PROJ_TN):
            cols = slice(n * PROJ_TN, (n + 1) * PROJ_TN)
            o_ref[:, col:col + PROJ_TN] = jnp.dot(hb, w_ref[:, cols], preferred_element_type=F32).astype(BF16)
            col += PROJ_TN
    a_ref[...] = jnp.dot(hb, wa_ref[...], preferred_element_type=F32)


def _in_projection(x2d, gain, sc, sh, pieces, w_a, rows_per_mod):
    rows, d = x2d.shape
    n_main = sum(width for _, width, _ in pieces)
    tiles_per_mod = rows_per_mod // PROJ_TM
    mod_map = lambda i: (i // tiles_per_mod, 0, 0)
    piece_specs = [pl.BlockSpec((d, width), functools.partial(lambda i, b: (0, b), b=block),
                                pipeline_mode=pl.Buffered(1)) for _, width, block in pieces]
    return pl.pallas_call(
        _inproj_body,
        grid=(rows // PROJ_TM,),
        in_specs=[
            pl.BlockSpec((PROJ_TM, d), lambda i: (i, 0)),
            _resident((1, d)),
            pl.BlockSpec((1, 1, d), mod_map),
            pl.BlockSpec((1, 1, d), mod_map),
            *piece_specs,
            _resident((d, LANES)),
        ],
        out_specs=[
            pl.BlockSpec((PROJ_TM, n_main), lambda i: (i, 0)),
            pl.BlockSpec((PROJ_TM, LANES), lambda i: (i, 0)),
        ],
        out_shape=[
            jax.ShapeDtypeStruct((rows, n_main), BF16),
            jax.ShapeDtypeStruct((rows, LANES), F32),
        ],
        compiler_params=_cparams(("arbitrary",)),
        name="in_projection",
    )(x2d, gain, sc, sh, *[w for w, _, _ in pieces], w_a)


def _log_sigmoid(z):
    return jnp.minimum(z, 0.0) - jnp.log1p(jnp.exp(-jnp.abs(z)))


def _gla_cumulative_decay(a, w2, ba, tri):
    z = jnp.dot(a.astype(BF16), w2, preferred_element_type=F32) + ba
    g = _log_sigmoid(z) * (1.0 / GLA_TAU)
    g_hi, g_lo = _bf16_terms(g)
    return jnp.dot(tri, g_hi, preferred_element_type=F32) + jnp.dot(tri, g_lo, preferred_element_type=F32)


def _gla_prep(q, k, G, reverse):
    nc = SUPER // GLA_CHUNK
    G = G.reshape(nc, GLA_CHUNK, GLA_DK)
    end_row = 0 if reverse else GLA_CHUNK - 1
    mid_row = GLA_CHUNK - 1 - GLA_CHUNK // 2 if reverse else GLA_CHUNK // 2
    g_end = G[:, end_row:end_row + 1, :]
    g_mid = G[:, mid_row:mid_row + 1, :]
    k4 = k.astype(F32).reshape(nc, GLA_CHUNK, GLA_DK)
    dec = jnp.broadcast_to(jnp.exp(g_end), (nc, ROW_SUBLANES, GLA_DK)).reshape(nc * ROW_SUBLANES, GLA_DK)
    flat = lambda t: t.reshape(SUPER, GLA_DK).astype(BF16)
    if q is None:
        return None, None, None, flat(k4 * jnp.exp(g_end - G)), dec
    q4 = q.astype(F32).reshape(nc, GLA_CHUNK, GLA_DK) * (GLA_DK ** -0.5)
    qg = q4 * jnp.exp(G - g_mid)
    kg = k4 * jnp.exp(g_mid - G)
    qe = qg * jnp.exp(g_mid)
    kd = kg * jnp.exp(g_end - g_mid)
    return flat(qg), flat(kg), flat(qe), flat(kd), dec


def _gla_apply(qg, kg, qe, kd, dec, v, mask, st_ref, reverse):
    nc = SUPER // GLA_CHUNK
    o = None
    if qg is not None:
        att = lax.dot_general(qg, kg, (((1,), (1,)), ((), ())), preferred_element_type=F32)
        att = jnp.where(mask, att, 0.0).astype(BF16)
        o = jnp.dot(att, v, preferred_element_type=F32)
    outs = [None] * nc
    order = range(nc - 1, -1, -1) if reverse else range(nc)
    for c in order:
        rows = slice(c * GLA_CHUNK, (c + 1) * GLA_CHUNK)
        st = st_ref[...]
        if qg is not None:
            inter = lax.dot_general(qe[rows], st.astype(BF16), (((1,), (1,)), ((), ())),
                                    preferred_element_type=F32)
            outs[c] = o[rows] + inter
        upd = lax.dot_general(v[rows], kd[rows], (((0,), (0,)), ((), ())), preferred_element_type=F32)
        st_ref[...] = st * dec[ROW_SUBLANES * c:ROW_SUBLANES * c + 1, :] + upd
    if qg is None:
        return None
    return jnp.concatenate(outs, axis=0)


def _gla_body(q_ref, k_ref, v_ref, r_ref, a_ref, kc_ref, vc_ref, ac_ref,
              w2f_ref, baf_ref, w2b_ref, bab_ref, gn_ref, y_ref, o_acc, st, ops_a, ops_b, dec_a, dec_b,
              *, n_ctx):
    n_sup = q_ref.shape[0] // SUPER
    row = lax.broadcasted_iota(I32, (SUPER, SUPER), 0)
    col = lax.broadcasted_iota(I32, (SUPER, SUPER), 1)
    same_chunk = (row // GLA_CHUNK) == (col // GLA_CHUNK)
    mask_f = same_chunk & (col <= row)
    mask_b = same_chunk & (col >= row)
    tri_f = jnp.where(mask_f, 1.0, 0.0).astype(BF16)
    tri_b = jnp.where(mask_b, 1.0, 0.0).astype(BF16)
    heads = range(GLA_HPS)
    kcol = [slice(h * GLA_DK, (h + 1) * GLA_DK) for h in heads]
    vcol = [slice(h * GLA_DV, (h + 1) * GLA_DV) for h in heads]
    dirs = ((False, w2f_ref, baf_ref, tri_f, mask_f), (True, w2b_ref, bab_ref, tri_b, mask_b))

    st[...] = jnp.zeros_like(st)
    n_csup = n_ctx // SUPER
    for s in range(n_csup):
        for d, (reverse, w2_ref, ba_ref, tri, mask) in enumerate(dirs):
            sc = n_csup - 1 - s if reverse else s
            rows = slice(sc * SUPER, (sc + 1) * SUPER)
            G = _gla_cumulative_decay(ac_ref[rows, :], w2_ref[...], ba_ref[...], tri)
            for h in heads:
                _, _, _, kd, dec = _gla_prep(None, kc_ref[rows, kcol[h]], G[:, kcol[h]], reverse)
                _gla_apply(None, None, None, kd, dec, vc_ref[rows, vcol[h]], mask, st.at[d, h], reverse)

    o_acc[...] = jnp.zeros_like(o_acc)

    def rows_of(i, reverse):
        sc = n_sup - 1 - i if reverse else i
        return pl.ds(pl.multiple_of(sc * SUPER, SUPER), SUPER)

    def prepare(i, ops, decs, d):
        reverse, w2_ref, ba_ref, tri, mask = dirs[d]
        rows = rows_of(i, reverse)
        G = _gla_cumulative_decay(a_ref[rows, :], w2_ref[...], ba_ref[...], tri)
        for h in heads:
            vals = _gla_prep(q_ref[rows, kcol[h]], k_ref[rows, kcol[h]], G[:, kcol[h]], reverse)
            for j in range(4):
                ops[GLA_HPS * d + h, j] = vals[j]
            decs[GLA_HPS * d + h] = vals[4]

    def apply(i, ops, decs, d):
        reverse, w2_ref, ba_ref, tri, mask = dirs[d]
        rows = rows_of(i, reverse)
        for h in heads:
            ci = GLA_HPS * d + h
            out = _gla_apply(ops[ci, 0], ops[ci, 1], ops[ci, 2], ops[ci, 3], decs[ci],
                             v_ref[rows, vcol[h]], mask, st.at[d, h], reverse)
            o_acc[rows, vcol[h]] += out

    for d in range(2):
        prepare(0, ops_a, dec_a, d)

    def step(j, carry):
        i = 2 * j
        for d in range(2):
            prepare(i + 1, ops_b, dec_b, d)
            apply(i, ops_a, dec_a, d)
        nxt = jnp.minimum(i + 2, n_sup - 1)
        for d in range(2):
            prepare(nxt, ops_a, dec_a, d)
            apply(i + 1, ops_b, dec_b, d)
        return carry

    lax.fori_loop(0, n_sup // 2, step, 0)

    for h in heads:
        o = o_acc[:, vcol[h]]
        o = o * _rms_scale(o) * gn_ref[:, vcol[h]]
        y_ref[:, vcol[h]] = (o * _silu(r_ref[:, vcol[h]].astype(F32))).astype(BF16)


def _gla(u_lat, a_lat, u_ctx, a_ctx, w2f, baf, w2b, bab, gla_norm, batch, seq, n_ctx):
    groups = GLA_HEADS // GLA_HPS
    kw, vw = GLA_HPS * GLA_DK, GLA_HPS * GLA_DV
    kb = GLA_KEY_WIDTH // kw
    vb = 2 * GLA_KEY_WIDTH // vw
    rb = vb + groups
    assert GLA_KEY_WIDTH % kw == 0 and (2 * GLA_KEY_WIDTH) % vw == 0 and GLA_WIDTH % kw == 0
    ckb = GLA_WIDTH // kw
    return pl.pallas_call(
        functools.partial(_gla_body, n_ctx=n_ctx),
        grid=(batch, groups),
        in_specs=[
            pl.BlockSpec((seq, kw), lambda b, h: (b, h)),
            pl.BlockSpec((seq, kw), lambda b, h: (b, kb + h)),
            pl.BlockSpec((seq, vw), lambda b, h: (b, vb + h)),
            pl.BlockSpec((seq, vw), lambda b, h: (b, rb + h)),
            pl.BlockSpec((seq, LANES), lambda b, h: (b, 0)),
            pl.BlockSpec((n_ctx, kw), lambda b, h: (b, ckb + h)),
            pl.BlockSpec((n_ctx, vw), lambda b, h: (b, h)),
            pl.BlockSpec((n_ctx, LANES), lambda b, h: (b, 0)),
            pl.BlockSpec((LANES, kw), lambda b, h: (0, h)),
            pl.BlockSpec((1, kw), lambda b, h: (0, h)),
            pl.BlockSpec((LANES, kw), lambda b, h: (0, h)),
            pl.BlockSpec((1, kw), lambda b, h: (0, h)),
            pl.BlockSpec((1, vw), lambda b, h: (0, h)),
        ],
        out_specs=pl.BlockSpec((seq, vw), lambda b, h: (b, h)),
        out_shape=jax.ShapeDtypeStruct((batch * seq, GLA_WIDTH), BF16),
        scratch_shapes=[
            pltpu.VMEM((seq, vw), F32),
            pltpu.VMEM((2, GLA_HPS, GLA_DV, GLA_DK), F32),
            pltpu.VMEM((2 * GLA_HPS, 4, SUPER, GLA_DK), BF16),
            pltpu.VMEM((2 * GLA_HPS, 4, SUPER, GLA_DK), BF16),
            pltpu.VMEM((2 * GLA_HPS, ROW_SUBLANES * (SUPER // GLA_CHUNK), GLA_DK), F32),
            pltpu.VMEM((2 * GLA_HPS, ROW_SUBLANES * (SUPER // GLA_CHUNK), GLA_DK), F32),
        ],
        compiler_params=_cparams(("arbitrary", "arbitrary")),
        name="gla",
    )(u_lat, u_lat, u_lat, u_lat, a_lat, u_ctx, u_ctx, a_ctx, w2f, baf, w2b, bab, gla_norm)


def _col_window_matrices():
    t = np.arange(SUPER)
    r, c = t // GRID_W, t % GRID_W
    mats = []
    for w in POOL_WINDOWS:
        lo = np.maximum(c - w // 2, 0)[:, None]
        hi = np.minimum(c + w // 2, GRID_W)[:, None]
        m = (r[:, None] == r[None, :]) & (c[None, :] >= lo) & (c[None, :] < hi)
        mats.append(m.astype(np.float32))
    return jnp.asarray(np.stack(mats), dtype=BF16)


def _pool_body(p_ref, cw_ref, wp_ref, ps_ref, y_ref, pad_ref):
    seq = p_ref.shape[0]
    n_rows = seq // GRID_W
    zeros = jnp.zeros((POOL_PAD, POOL_GROUP), F32)
    pad_ref[0:POOL_PAD, :] = zeros
    pad_ref[POOL_PAD + seq:POOL_PAD + seq + POOL_PAD, :] = zeros
    t = lax.broadcasted_iota(I32, (seq, POOL_GROUP), 0)
    r = t // GRID_W
    c = t % GRID_W
    for gi, w in enumerate(POOL_WINDOWS):
        cols = slice(gi * POOL_GROUP, (gi + 1) * POOL_GROUP)
        cw = cw_ref[gi]
        for j in range(seq // SUPER):
            rows = slice(j * SUPER, (j + 1) * SUPER)
            pad_ref[POOL_PAD + j * SUPER:POOL_PAD + (j + 1) * SUPER, :] = jnp.dot(
                cw, p_ref[rows, cols], preferred_element_type=F32)
        total = None
        for d in range(-(w // 2), w // 2):
            start = POOL_PAD + d * GRID_W
            part = pad_ref[start:start + seq, :]
            total = part if total is None else total + part
        cnt_r = jnp.minimum(r + w // 2, n_rows) - jnp.maximum(r - w // 2, 0)
        cnt_c = jnp.minimum(c + w // 2, GRID_W) - jnp.maximum(c - w // 2, 0)
        mean = total / (cnt_r * cnt_c).astype(F32)
        diff = (mean - p_ref[:, cols].astype(F32)).astype(BF16)
        y = jnp.dot(diff, wp_ref[gi], preferred_element_type=F32) * ps_ref[:, cols]
        y_ref[:, cols] = y.astype(BF16)


def _pool_mixer(u_lat, col_mats, w_pool, pool_scale, batch, seq):
    pb = (u_lat.shape[1] - POOL_WIDTH) // POOL_WIDTH
    ng = len(POOL_WINDOWS)
    return pl.pallas_call(
        _pool_body,
        grid=(batch,),
        in_specs=[
            pl.BlockSpec((seq, POOL_WIDTH), lambda b: (b, pb)),
            _resident((ng, SUPER, SUPER)),
            _resident((ng, POOL_GROUP, POOL_GROUP)),
            _resident((1, POOL_WIDTH)),
        ],
        out_specs=pl.BlockSpec((seq, POOL_WIDTH), lambda b: (b, 0)),
        out_shape=jax.ShapeDtypeStruct((batch * seq, POOL_WIDTH), BF16),
        scratch_shapes=[pltpu.VMEM((seq + 2 * POOL_PAD, POOL_GROUP), F32)],
        compiler_params=_cparams(("arbitrary",)),
        name="pool_mixer",
    )(u_lat, col_mats, w_pool, pool_scale)


def _first_index(hit, iota, size, axis):
    return jnp.min(jnp.where(hit, iota, size), axis=axis, keepdims=True)


def _mix_body(yg_ref, yp_ref, x_ref, gt1_ref, sc2_ref, sh2_ref, gpost_ref, gpre_ref, wout_ref,
              wr_ref, rb_ref, upper_ref,
              x1_ref, hp_ref, eidx_ref, pos_ref, wts_ref, cnt_ref, run_ref, wrow_ref, y_scr):
    tm = x_ref.shape[0]
    neg_inf = jnp.float32(-jnp.inf)
    step = pl.program_id(0)

    @pl.when(step == 0)
    def _():
        run_ref[...] = jnp.zeros_like(run_ref)
        wrow_ref[...] = jnp.zeros_like(wrow_ref)
        y_scr[...] = jnp.zeros_like(y_scr)

    y = y_scr[...]
    y_new = jnp.dot(yg_ref[...], wout_ref[0:GLA_WIDTH, :], preferred_element_type=F32)
    y_scr[...] = y_new + jnp.dot(yp_ref[...], wout_ref[GLA_WIDTH:, :], preferred_element_type=F32)
    x1 = x_ref[...] + gt1_ref[0] * (y * _rms_scale(y) * gpost_ref[...])
    x1_ref[...] = x1
    h = x1 * _rms_scale(x1) * gpre_ref[...]
    h = h * (1.0 + sc2_ref[0]) + sh2_ref[0]
    _store_row_tiles(hp_ref, _pack_halves(h[:, :HALF], h[:, HALF:]))

    h_hi, h_lo = _bf16_terms(h)
    both = jnp.dot(h_hi, wr_ref[...], preferred_element_type=F32)
    lt = both[:, :LANES] + both[:, LANES:] + jnp.dot(h_lo, wr_ref[:, :LANES], preferred_element_type=F32)
    logits = lt.T[0:N_EXPERTS, :]
    scores = jax.nn.sigmoid(logits)
    sel = scores + rb_ref[...]
    shape3 = (N_GROUPS, GROUP_SIZE, tm)
    sel3 = sel.reshape(shape3)
    i_in = lax.broadcasted_iota(I32, shape3, 1).astype(F32)
    m1 = jnp.max(sel3, axis=1, keepdims=True)
    f1 = _first_index(sel3 == m1, i_in, float(GROUP_SIZE), 1)
    m2 = jnp.max(jnp.where(i_in == f1, neg_inf, sel3), axis=1, keepdims=True)
    grp = jnp.broadcast_to(m1 + m2, shape3).reshape(N_EXPERTS, tm)
    i_e = lax.broadcasted_iota(I32, (N_EXPERTS, tm), 0)
    i_grp = (i_e // GROUP_SIZE).astype(F32)
    i_e = i_e.astype(F32)
    allowed = jnp.zeros((N_EXPERTS, tm), F32)
    for _ in range(TOPK_GROUPS):
        m = jnp.max(grp, axis=0, keepdims=True)
        pick = i_grp == _first_index(grp == m, i_grp, float(N_GROUPS), 0)
        allowed = jnp.where(pick, 1.0, allowed)
        grp = jnp.where(pick, neg_inf, grp)
    cand = jnp.where(allowed > 0.0, sel, neg_inf)
    onehot = jnp.zeros((N_EXPERTS, tm), F32)
    picks, wts = [], []
    for k in range(TOP_K):
        m = jnp.max(cand, axis=0, keepdims=True)
        f = _first_index(cand == m, i_e, float(N_EXPERTS), 0)
        pick = i_e == f
        picks.append(pick)
        eidx_ref[k:k + 1, :] = f.astype(I32)
        wts.append(jnp.sum(jnp.where(pick, scores, 0.0), axis=0, keepdims=True))
        onehot = jnp.where(pick, 1.0, onehot)
        cand = jnp.where(pick, neg_inf, cand)
    w_sum = wts[0]
    for k in range(1, TOP_K):
        w_sum = w_sum + wts[k]
    for k in range(TOP_K):
        wrow_ref[k:k + 1, :] = wts[k] / w_sum * ROUTED_SCALE
    wts_ref[...] = wrow_ref[...].T

    before = jnp.dot(onehot.astype(BF16), upper_ref[...], preferred_element_type=F32)
    before = before + run_ref[:, 0:1]
    for k in range(TOP_K):
        pos_ref[k:k + 1, :] = jnp.sum(jnp.where(picks[k], before, 0.0), axis=0, keepdims=True).astype(I32)
    counted = jnp.where(step > 0, 1.0, 0.0)
    run_ref[...] = run_ref[...] + counted * jnp.sum(onehot, axis=1, keepdims=True)
    cnt_ref[...] = run_ref[...].astype(I32)


def _mix_and_route(y_gla, y_pool, x2d, gt1, sc2, sh2, g_post, g_pre, w_out, w_router2, router_bias, seq,
                   part, n_parts):
    d = x2d.shape[1]
    rows = x2d.shape[0] // n_parts
    tiles = rows // MIX_TM
    first = part * tiles
    tiles_per_b = seq // MIX_TM
    proj = lambda i: (first + jnp.minimum(i, tiles - 1), 0)
    bmap = lambda i: ((first + jnp.maximum(i - 1, 0)) // tiles_per_b, 0, 0)
    xmap = lambda i: (first + jnp.maximum(i - 1, 0), 0)
    rmap = lambda i: (jnp.maximum(i - 1, 0), 0)
    tmap = lambda i: (0, jnp.maximum(i - 1, 0))
    upper = jnp.asarray(np.triu(np.ones((MIX_TM, MIX_TM), np.float32), 1), dtype=BF16)
    return pl.pallas_call(
        _mix_body,
        grid=(tiles + 1,),
        in_specs=[
            pl.BlockSpec((MIX_TM, GLA_WIDTH), proj),
            pl.BlockSpec((MIX_TM, POOL_WIDTH), proj),
            pl.BlockSpec((MIX_TM, d), xmap),
            pl.BlockSpec((1, 1, d), bmap),
            pl.BlockSpec((1, 1, d), bmap),
            pl.BlockSpec((1, 1, d), bmap),
            _resident((1, d)),
            _resident((1, d)),
            _resident((d, d)),
            _resident((d, 256)),
            _resident((N_EXPERTS, 1)),
            _resident((MIX_TM, MIX_TM)),
        ],
        out_specs=[
            pl.BlockSpec((MIX_TM, d), rmap),
            pl.BlockSpec((MIX_TM * ROW_SUBLANES, LANES), rmap),
            pl.BlockSpec((TOP_K, MIX_TM), tmap),
            pl.BlockSpec((TOP_K, MIX_TM), tmap),
            pl.BlockSpec((MIX_TM, LANES), rmap),
            pl.BlockSpec((N_EXPERTS, LANES), lambda i: (0, 0)),
        ],
        out_shape=[
            jax.ShapeDtypeStruct((rows, d), F32),
            jax.ShapeDtypeStruct((rows * ROW_SUBLANES, LANES), U32),
            jax.ShapeDtypeStruct((TOP_K, rows), I32),
            jax.ShapeDtypeStruct((TOP_K, rows), I32),
            jax.ShapeDtypeStruct((rows, LANES), F32),
            jax.ShapeDtypeStruct((N_EXPERTS, LANES), I32),
        ],
        scratch_shapes=[pltpu.VMEM((N_EXPERTS, LANES), F32), pltpu.VMEM((LANES, MIX_TM), F32),
                        pltpu.VMEM((MIX_TM, d), F32)],
        compiler_params=_cparams(("arbitrary",)),
        name="mix_and_route",
    )(y_gla, y_pool, x2d, gt1, sc2, sh2, g_post, g_pre, w_out, w_router2, router_bias, upper)


def _swiglu(lo, hi, wg_ref, wu_ref, wd_ref):
    g = jnp.dot(lo, wg_ref[:HALF, :], preferred_element_type=F32)
    g = g + jnp.dot(hi, wg_ref[HALF:, :], preferred_element_type=F32)
    u = jnp.dot(lo, wu_ref[:HALF, :], preferred_element_type=F32)
    u = u + jnp.dot(hi, wu_ref[HALF:, :], preferred_element_type=F32)
    act = (_silu(g) * u).astype(BF16)
    return jnp.dot(act, wd_ref[...], preferred_element_type=F32)


def _shared_body(hp_ref, wg_ref, wu_ref, wd_ref, o_ref):
    lo, hi = _unpack_halves(_load_row_tiles(hp_ref))
    o_ref[...] = _swiglu(lo.astype(BF16), hi.astype(BF16), wg_ref, wu_ref, wd_ref).astype(BF16)


def _shared_expert(h_packed, w_sg, w_su, w_sd):
    d, ds = w_sg.shape
    rows = h_packed.shape[0] // ROW_SUBLANES
    return pl.pallas_call(
        _shared_body,
        grid=(rows // SHARED_TM,),
        in_specs=[
            pl.BlockSpec((SHARED_TM * ROW_SUBLANES, LANES), lambda i: (i, 0)),
            _resident((d, ds)),
            _resident((d, ds)),
            _resident((ds, d)),
        ],
        out_specs=pl.BlockSpec((SHARED_TM, d), lambda i: (i, 0)),
        out_shape=jax.ShapeDtypeStruct((rows, d), BF16),
        compiler_params=_cparams(("arbitrary",)),
        name="shared_expert",
    )(h_packed, w_sg, w_su, w_sd)


def _offsets_body(ps_ref, e_ref, p_ref, d_ref):
    e = e_ref[...]
    d = p_ref[...]
    for x in range(N_EXPERTS):
        d = d + jnp.where(e == x, ps_ref[x], 0)
    d_ref[...] = d


def _route_offsets(eidx_t, pos_t, pstarts):
    k, rows = eidx_t.shape
    spec = pl.BlockSpec((k, OFFS_TN), lambda i, ps: (0, i))
    grid_spec = pltpu.PrefetchScalarGridSpec(
        num_scalar_prefetch=1, grid=(rows // OFFS_TN,), in_specs=[spec, spec], out_specs=spec)
    return pl.pallas_call(
        _offsets_body,
        grid_spec=grid_spec,
        out_shape=jax.ShapeDtypeStruct((k, rows), I32),
        compiler_params=_cparams(("arbitrary",)),
        name="route_offsets",
    )(pstarts, eidx_t, pos_t)


def _swiglu_f32w(lo, hi, wg_ref, wu_ref, wd_ref):
    kc = 512
    g = u = None
    for half, x in enumerate((lo, hi)):
        for c in range(HALF // kc):
            xc = x[:, c * kc:(c + 1) * kc]
            rows = pl.ds(half * HALF + c * kc, kc)
            gc = jnp.dot(xc, wg_ref[rows, :].astype(BF16), preferred_element_type=F32)
            uc = jnp.dot(xc, wu_ref[rows, :].astype(BF16), preferred_element_type=F32)
            g = gc if g is None else g + gc
            u = uc if u is None else u + uc
    act = (_silu(g) * u).astype(BF16)
    return jnp.dot(act, wd_ref[...].astype(BF16), preferred_element_type=F32)


def _expert_body(be_ref, nu_ref, valid_ref, nxt_ref, slot_ref, xs_hbm, wg_hbm, wu_hbm, wd_hbm, y_ref,
                 wg_f, wu_f, wd_f, sems, xbuf, xsems):
    i = pl.program_id(0)
    valid = valid_ref[i]
    expert = be_ref[i]
    slot = slot_ref[i]
    n_used = nu_ref[0]
    block_tiles = MOE_TILE * ROW_SUBLANES

    def row_copy(step):
        ring = step % X_RING
        src = xs_hbm.at[pl.ds(pl.multiple_of(step * block_tiles, block_tiles), block_tiles)]
        return pltpu.make_async_copy(src, xbuf.at[ring], xsems.at[ring])

    @pl.when(i == 0)
    def _():
        for s in range(X_RING - 1):
            @pl.when(s < n_used)
            def _(s=s):
                row_copy(s).start()

    @pl.when(i + (X_RING - 1) < n_used)
    def _():
        row_copy(i + (X_RING - 1)).start()

    @pl.when(i < n_used)
    def _():
        row_copy(i).wait()

    x_ref = xbuf.at[i % X_RING]

    def weight_copies(e, s):
        return [pltpu.make_async_copy(src.at[e], dst.at[s], sems.at[s, j])
                for j, (src, dst) in enumerate(((wg_hbm, wg_f), (wu_hbm, wu_f), (wd_hbm, wd_f)))]

    @pl.when(i == 0)
    def _():
        for cp in weight_copies(expert, slot):
            cp.start()

    @pl.when((i == 0) | (expert != be_ref[jnp.maximum(i - 1, 0)]))
    def _():
        for cp in weight_copies(expert, slot):
            cp.wait()

        @pl.when(nxt_ref[i] != expert)
        def _():
            for cp in weight_copies(nxt_ref[i], 1 - slot):
                cp.start()

    wg_ref, wu_ref, wd_ref = wg_f.at[slot], wu_f.at[slot], wd_f.at[slot]

    def compute(n_rows):
        tiles = pl.ds(0, n_rows * ROW_SUBLANES)
        lo, hi = _unpack_halves(_load_row_tiles(x_ref.at[tiles]))
        y = _swiglu_f32w(lo.astype(BF16), hi.astype(BF16), wg_ref, wu_ref, wd_ref)
        _store_row_tiles(y_ref.at[tiles], _pack_halves(y[:, :HALF], y[:, HALF:]))

    quarter = MOE_TILE // 4
    for n in range(1, 5):
        @pl.when((valid > (n - 1) * quarter) & (valid <= n * quarter))
        def _(n=n):
            compute(n * quarter)


def _experts(xs, block_expert, n_used, valid, next_expert, slot, w_eg, w_eu, w_ed):
    n_rows = xs.shape[0] // ROW_SUBLANES
    n_blocks = n_rows // MOE_TILE
    d, de = w_eg.shape[1], w_eg.shape[2]
    row_map = lambda i, be, nu, *_: (jnp.minimum(i, nu[0] - 1), 0)
    grid_spec = pltpu.PrefetchScalarGridSpec(
        num_scalar_prefetch=5,
        grid=(n_blocks,),
        in_specs=[pl.BlockSpec(memory_space=pl.ANY)] * 4,
        out_specs=pl.BlockSpec((MOE_TILE * ROW_SUBLANES, LANES), row_map),
        scratch_shapes=[
            pltpu.VMEM((2, d, de), w_eg.dtype),
            pltpu.VMEM((2, d, de), w_eu.dtype),
            pltpu.VMEM((2, de, d), w_ed.dtype),
            pltpu.SemaphoreType.DMA((2, 3)),
            pltpu.VMEM((X_RING, MOE_TILE * ROW_SUBLANES, LANES), U32),
            pltpu.SemaphoreType.DMA((X_RING,)),
        ],
    )
    return pl.pallas_call(
        _expert_body,
        grid_spec=grid_spec,
        out_shape=jax.ShapeDtypeStruct((n_rows * ROW_SUBLANES, LANES), U32),
        compiler_params=_cparams(("arbitrary",)),
        name="experts",
    )(block_expert, n_used, valid, next_expert, slot, xs, w_eg, w_eu, w_ed)


def _round_body(w_ref, o_ref):
    o_ref[...] = w_ref[...].astype(BF16)


def _round_weights(w):
    n, a, b = w.shape
    spec = pl.BlockSpec((1, a, b), lambda e: (e, 0, 0))
    return pl.pallas_call(
        _round_body,
        grid=(n,),
        in_specs=[spec],
        out_specs=spec,
        out_shape=jax.ShapeDtypeStruct(w.shape, BF16),
        compiler_params=_cparams(("arbitrary",)),
        name="round_weights",
    )(w)


def _sc_mesh():
    return plsc.VectorSubcoreMesh(core_axis_name="c", subcore_axis_name="s")


def _sc_worker():
    return lax.axis_index("s") * SC_CORES + lax.axis_index("c")


def _sc_gather_rows(table, idx):
    n_idx = idx.shape[0]
    per_worker = n_idx // SC_WORKERS
    n_chunks = per_worker // SC_CHUNK
    assert per_worker * SC_WORKERS == n_idx and n_chunks * SC_CHUNK == per_worker and n_chunks % 2 == 0
    row_shape = table.shape[1:]

    @functools.partial(
        pl.kernel, mesh=_sc_mesh(),
        out_type=jax.ShapeDtypeStruct((n_idx,) + row_shape, table.dtype),
        scratch_types=[
            pltpu.VMEM((per_worker,), I32),
            pltpu.VMEM((SC_CHUNK,) + row_shape, table.dtype),
            pltpu.VMEM((SC_CHUNK,) + row_shape, table.dtype),
        ] + [pltpu.SemaphoreType.DMA] * 4,
    )
    def gather(table_hbm, idx_hbm, out_hbm, idx_v, buf0, buf1, g0, g1, w0, w1):
        bufs, gsem, wsem = (buf0, buf1), (g0, g1), (w0, w1)
        base = _sc_worker() * per_worker
        pltpu.sync_copy(idx_hbm.at[pl.ds(base, per_worker)], idx_v)

        def fetch(j, b):
            return pltpu.make_async_copy(table_hbm.at[idx_v.at[pl.ds(j * SC_CHUNK, SC_CHUNK)]], bufs[b], gsem[b])

        def flush(j, b):
            return pltpu.make_async_copy(bufs[b], out_hbm.at[pl.ds(base + j * SC_CHUNK, SC_CHUNK)], wsem[b])

        fetch(0, 0).start()
        fetch(0, 0).wait()
        fetch(1, 1).start()
        flush(0, 0).start()

        @pl.loop(1, n_chunks - 1, step=2)
        def _(j):
            for off in range(2):
                jj, b = j + off, (1 + off) % 2
                fetch(jj, b).wait()
                flush(jj - 1, 1 - b).wait()
                fetch(jj + 1, 1 - b).start()
                flush(jj, b).start()

        fetch(n_chunks - 1, 1).wait()
        flush(n_chunks - 1, 1).start()
        flush(n_chunks - 2, 0).wait()
        flush(n_chunks - 1, 1).wait()

    return gather(table, idx)


def _sc_scatter_rows(rows, dest_t, n_out):
    n_rows = rows.shape[0]
    per_worker = n_rows // SC_WORKERS
    n_chunks = per_worker // SC_CHUNK
    assert per_worker * SC_WORKERS == n_rows and n_chunks * SC_CHUNK == per_worker
    row_shape = rows.shape[1:]
    idx_w = dest_t.reshape(TOP_K, SC_WORKERS, n_chunks, SC_CHUNK).transpose(1, 2, 0, 3)
    idx_w = idx_w.reshape(SC_WORKERS, n_chunks * TOP_K, SC_CHUNK)

    @functools.partial(
        pl.kernel, mesh=_sc_mesh(),
        out_type=jax.ShapeDtypeStruct((n_out,) + row_shape, rows.dtype),
        scratch_types=[
            pltpu.VMEM((n_chunks * TOP_K, SC_CHUNK), I32),
            pltpu.VMEM((SC_CHUNK,) + row_shape, rows.dtype),
            pltpu.VMEM((SC_CHUNK,) + row_shape, rows.dtype),
        ] + [pltpu.SemaphoreType.DMA] * 4,
    )
    def scatter(rows_hbm, idx_hbm, out_hbm, idx_v, buf0, buf1, r0, r1, s0, s1):
        bufs, rsem, ssem = (buf0, buf1), (r0, r1), (s0, s1)
        worker = _sc_worker()
        base = worker * per_worker
        pltpu.sync_copy(idx_hbm.at[worker], idx_v)

        def fetch(j, b):
            return pltpu.make_async_copy(rows_hbm.at[pl.ds(base + j * SC_CHUNK, SC_CHUNK)], bufs[b], rsem[b])

        def send(j, k, b):
            return pltpu.make_async_copy(bufs[b], out_hbm.at[idx_v.at[j * TOP_K + k]], ssem[b])

        fetch(0, 0).start()
        for j in range(n_chunks):
            b = j % 2
            fetch(j, b).wait()
            if j + 1 < n_chunks:
                if j >= 1:
                    for k in range(TOP_K):
                        send(j - 1, k, 1 - b).wait()
                fetch(j + 1, 1 - b).start()
            for k in range(TOP_K):
                send(j, k, b).start()
        for j in range(max(n_chunks - 2, 0), n_chunks):
            for k in range(TOP_K):
                send(j, k, j % 2).wait()

    return scatter(rows, idx_w)


def _combine_body(yu_hbm, w_ref, shr_ref, x1_ref, gt2_ref, gpost_ref, *refs):
    o_ref, ybuf, ysems = refs[-3:]
    tt = x1_ref.shape[0]
    i = pl.program_id(0)
    tiles = pl.num_programs(0)
    block_tiles = tt * ROW_SUBLANES

    def row_copies(step):
        ring = step % Y_RING
        copies = []
        for k in range(TOP_K):
            start = pl.multiple_of((k * tiles + step) * block_tiles, block_tiles)
            copies.append(pltpu.make_async_copy(yu_hbm.at[pl.ds(start, block_tiles)], ybuf.at[ring, k],
                                                ysems.at[ring]))
        return copies

    @pl.when(i == 0)
    def _():
        for s in range(Y_RING - 1):
            @pl.when(s < tiles)
            def _(s=s):
                for cp in row_copies(s):
                    cp.start()

    @pl.when(i + (Y_RING - 1) < tiles)
    def _():
        for cp in row_copies(i + (Y_RING - 1)):
            cp.start()

    for cp in row_copies(i):
        cp.wait()
    yk_refs = [ybuf.at[i % Y_RING, k] for k in range(TOP_K)]
    w = w_ref[...]
    ssq = jnp.zeros((tt, 1), F32)
    for c in range(HALF // LANES):
        c_lo = slice(c * LANES, (c + 1) * LANES)
        c_hi = slice(HALF + c * LANES, HALF + (c + 1) * LANES)
        y_lo = shr_ref[:, c_lo].astype(F32)
        y_hi = shr_ref[:, c_hi].astype(F32)
        for k in range(TOP_K):
            lo, hi = _unpack_halves(yk_refs[k][pl.ds(c, tt, stride=ROW_SUBLANES), :])
            y_lo = y_lo + w[:, k:k + 1] * lo
            y_hi = y_hi + w[:, k:k + 1] * hi
        ssq = ssq + jnp.sum(y_lo * y_lo, axis=-1, keepdims=True) + jnp.sum(y_hi * y_hi, axis=-1, keepdims=True)
        o_ref[:, c_lo] = y_lo
        o_ref[:, c_hi] = y_hi
    scale = lax.rsqrt(ssq / D_MODEL + EPS)
    o_ref[...] = x1_ref[...] + gt2_ref[0] * (o_ref[...] * scale * gpost_ref[...])


def _combine(yu, wts, shared, x1, gt2, g_post, seq, part, n_parts, prev_out):
    rows, d = x1.shape
    tiles = rows // COMB_TT
    first = part * tiles
    tiles_per_b = seq // COMB_TT
    loc = lambda i: (i, 0)
    in_specs = [
        pl.BlockSpec(memory_space=pl.ANY),
        pl.BlockSpec((COMB_TT, LANES), loc),
        pl.BlockSpec((COMB_TT, d), loc),
        pl.BlockSpec((COMB_TT, d), loc),
        pl.BlockSpec((1, 1, d), lambda i: ((first + i) // tiles_per_b, 0, 0)),
        _resident((1, d)),
    ]
    args = [yu, wts, shared, x1, gt2, g_post]
    aliases = {}
    if prev_out is not None:
        aliases = {len(args): 0}
        in_specs.append(pl.BlockSpec(memory_space=pl.ANY))
        args.append(prev_out)
    return pl.pallas_call(
        _combine_body,
        grid=(tiles,),
        in_specs=in_specs,
        out_specs=pl.BlockSpec((COMB_TT, d), lambda i: (first + i, 0)),
        out_shape=jax.ShapeDtypeStruct((rows * n_parts, d), F32),
        scratch_shapes=[pltpu.VMEM((Y_RING, TOP_K, COMB_TT * ROW_SUBLANES, LANES), U32),
                        pltpu.SemaphoreType.DMA((Y_RING,))],
        input_output_aliases=aliases,
        compiler_params=_cparams(("arbitrary",)),
        name="combine",
    )(*args)


def kernel(x, c, ctx, c_ctx, w_mod, b_mod, norm_mix_pre, norm_mix_post, norm_ffn_pre, norm_ffn_post, w_in, w_a2_fwd, b_a_fwd, w_a2_bwd, b_a_bwd, gla_norm, w_pool, pool_scale, w_out, w_router, router_bias, w_exp_gate, w_exp_up, w_exp_down, w_sh_gate, w_sh_up, w_sh_down):
    batch, seq, d = x.shape
    n_ctx = ctx.shape[1]
    assert w_mod.shape[0] == 1 and d == D_MODEL
    assert seq % (2 * SUPER) == 0 and n_ctx % SUPER == 0 and seq % PROJ_TM == 0 and (batch * n_ctx) % PROJ_TM == 0
    rows = batch * seq

    mod_rows = 16
    c_all = jnp.concatenate([c, c_ctx[None, :], jnp.zeros((mod_rows - batch - 1, d), F32)], axis=0)
    mod_all = _modulation(c_all, w_mod[0], b_mod[0][None, :])
    sh1, sc1, gt1, sh2, sc2, gt2 = [m.reshape(batch, 1, d) for m in jnp.split(mod_all[:batch], 6, axis=-1)]
    csh1 = mod_all[batch, 0:d].reshape(1, 1, d)
    csc1 = mod_all[batch, d:2 * d].reshape(1, 1, d)

    kw, gw = GLA_KEY_WIDTH, GLA_WIDTH
    a0 = 2 * kw + 2 * gw
    w_in0 = w_in[0]
    w_bf = w_in0.astype(BF16)
    w_a = jnp.pad(w_bf[:, a0:a0 + 2 * GLA_RANK], ((0, 0), (0, LANES - 2 * GLA_RANK)))
    lat_pieces = [(w_bf, a0, 0), (w_bf[:, a0 + 2 * GLA_RANK:], POOL_WIDTH, 0)]
    ctx_pieces = [(w_bf, gw, 2 * kw // gw), (w_bf, kw, 1)]
    w2f = jnp.pad(w_a2_fwd[0], ((0, LANES - GLA_RANK), (0, 0))).astype(BF16)
    w2b = jnp.pad(w_a2_bwd[0], ((GLA_RANK, LANES - 2 * GLA_RANK), (0, 0))).astype(BF16)
    g_mix_pre = norm_mix_pre[0][None, :]
    w_router2 = jnp.concatenate(_bf16_terms(jnp.pad(w_router[0], ((0, 0), (0, LANES - N_EXPERTS)))), axis=1)

    u_ctx, a_ctx = _in_projection(ctx.reshape(batch * n_ctx, d), g_mix_pre, csc1, csh1, ctx_pieces, w_a,
                                  batch * n_ctx)
    u_lat, a_lat = _in_projection(x.reshape(rows, d), g_mix_pre, sc1, sh1, lat_pieces, w_a, seq)

    y_gla = _gla(u_lat, a_lat, u_ctx, a_ctx, w2f, b_a_fwd[0][None, :], w2b, b_a_bwd[0][None, :],
                 gla_norm[0][None, :], batch, seq, n_ctx)
    y_pool = _pool_mixer(u_lat, _col_window_matrices(), w_pool[0].astype(BF16), pool_scale[0][None, :],
                         batch, seq)

    w_out_bf = w_out[0].astype(BF16)
    shared_w = (w_sh_gate[0].astype(BF16), w_sh_up[0].astype(BF16), w_sh_down[0].astype(BF16))
    x2d = x.reshape(rows, d)
    part_rows = rows // MOE_PARTS
    n_blocks = part_rows * TOP_K // MOE_TILE + N_EXPERTS
    e_ids = jnp.arange(N_EXPERTS, dtype=I32)
    expert_w = [_round_weights(w[0]) for w in (w_exp_gate, w_exp_up, w_exp_down)]

    mixed = [_mix_and_route(y_gla, y_pool, x2d, gt1, sc2, sh2, norm_mix_post[0][None, :],
                            norm_ffn_pre[0][None, :], w_out_bf, w_router2, router_bias[0][:, None], seq,
                            p, MOE_PARTS) for p in range(MOE_PARTS)]
    out = None
    for p, (x1, h_packed, eidx_t, pos_t, wts, counts) in enumerate(mixed):
        shared = _shared_expert(h_packed, *shared_w)
        counts = counts[:, 0]
        padded = (counts + MOE_TILE - 1) // MOE_TILE * MOE_TILE
        pends = jnp.cumsum(padded)
        pstarts = pends - padded
        dest_t = _route_offsets(eidx_t, pos_t, pstarts.astype(I32))
        n_used = (pends[-1] // MOE_TILE).astype(I32)
        blk = jnp.minimum(jnp.arange(n_blocks, dtype=I32), n_used - 1)
        block_expert = jnp.sum((blk * MOE_TILE)[:, None] >= pends[None, :], axis=1).astype(I32)
        block_expert = jnp.minimum(block_expert, N_EXPERTS - 1)
        is_block_expert = block_expert[:, None] == e_ids[None, :]
        per_block = lambda v: jnp.sum(jnp.where(is_block_expert, v[None, :], 0), axis=1).astype(I32)
        valid = jnp.clip(per_block(pstarts + counts) - blk * MOE_TILE, 0, MOE_TILE)
        valid = jnp.where(jnp.arange(n_blocks, dtype=I32) < n_used, valid, 0).astype(I32)
        has_rows = padded > 0
        later = jnp.where((e_ids[None, :] > e_ids[:, None]) & has_rows[None, :], e_ids[None, :], N_EXPERTS)
        next_e = jnp.min(later, axis=1)
        next_e = jnp.where(next_e == N_EXPERTS, e_ids, next_e)
        slot_e = (jnp.cumsum(has_rows.astype(I32)) - has_rows.astype(I32)) % 2

        xs = _sc_scatter_rows(h_packed.reshape(-1, ROW_SUBLANES, LANES), dest_t, n_blocks * MOE_TILE)
        ys = _experts(xs.reshape(-1, LANES), block_expert, n_used.reshape(1), valid, per_block(next_e),
                      per_block(slot_e), *expert_w)
        yu = _sc_gather_rows(ys.reshape(-1, ROW_SUBLANES, LANES), dest_t.reshape(-1))
        out = _combine(yu.reshape(-1, LANES), wts, shared, x1, gt2, norm_ffn_post[0][None, :], seq,
                       p, MOE_PARTS, out)
    return out.reshape(batch, seq, d)
```
